```python
import jax, jax.numpy as jnp
from jax import lax
import numpy as np

D_MODEL = 1024
BATCH = 4
SEQ = 4096
DEPTH = 1
DEC_BATCH = 32
DEC_SEQ = 4
PAST_LEN = 8192
PAGE_SIZE = 128

HEAD_DIM = 64
N_HEADS = 8
KV_HEADS = 2
HPG = N_HEADS // KV_HEADS
ATTN_WIDTH = N_HEADS * HEAD_DIM
KV_WIDTH = KV_HEADS * HEAD_DIM
ROT_DIM = HEAD_DIM // 4
ROT_HALF = ROT_DIM // 2
ROPE_THETA = 500000.0
CMP_BLOCK = 32
CMP_STRIDE = 16
CMP_HIDDEN = 128
SEL_BLOCK = 64
TOP_N = 16
N_INIT_BLOCKS = 1
N_LOCAL_BLOCKS = 2
WINDOW = 512
QBLK = 128
GMLP_GROUPS = 4
CHUNK = 128
GMLP_GROUP_WIDTH = 128
GMLP_WIDTH = GMLP_GROUPS * GMLP_GROUP_WIDTH
N_BRANCH = 2
IN_WIDTH = ATTN_WIDTH + 6 * KV_WIDTH + 3 * N_HEADS + ATTN_WIDTH + 3 * GMLP_WIDTH + N_BRANCH * D_MODEL
NEG = -1e30
BIG = 1e30
EPS = 1e-6

kernel_name = 'nsa_gmlp_gated_hybrid_step'


def rmsnorm(x, g):
    xf = x.astype(jnp.float32)
    r = lax.rsqrt(jnp.mean(xf * xf, axis=-1, keepdims=True) + EPS)
    return (xf * r).astype(x.dtype) * g


def layernorm(x, g, b):
    xf = x.astype(jnp.float32)
    mu = jnp.mean(xf, axis=-1, keepdims=True)
    var = jnp.mean(jnp.square(xf - mu), axis=-1, keepdims=True)
    return ((xf - mu) * lax.rsqrt(var + EPS)).astype(x.dtype) * g + b


def rope(x, pos):
    inv = jnp.power(jnp.float32(ROPE_THETA), -jnp.arange(0, ROT_DIM, 2, dtype=jnp.float32) / ROT_DIM)
    ang = pos.astype(jnp.float32)[:, None] * inv[None, :]
    cos = jnp.cos(ang)[None, :, None, :].astype(x.dtype)
    sin = jnp.sin(ang)[None, :, None, :].astype(x.dtype)
    x1, x2, xr = x[..., :ROT_HALF], x[..., ROT_HALF:ROT_DIM], x[..., ROT_DIM:]
    return jnp.concatenate([x1 * cos - x2 * sin, x2 * cos + x1 * sin, xr], axis=-1)


def masked_softmax(s, mask):
    s = jnp.where(mask, s.astype(jnp.float32), NEG)
    return jnp.where(mask, jax.nn.softmax(s, axis=-1), 0.0)


def compress(rows, params):
    w1, b1, w2, b2 = params
    B, T, G, dh = rows.shape
    n_cmp = (T - CMP_BLOCK) // CMP_STRIDE + 1
    idx = jnp.arange(n_cmp)[:, None] * CMP_STRIDE + jnp.arange(CMP_BLOCK)[None, :]
    blocks = rows[:, idx].transpose(0, 1, 3, 2, 4).reshape(B, n_cmp, G, CMP_BLOCK * dh)
    return jax.nn.gelu(blocks @ w1 + b1) @ w2 + b2


def cmp_end_positions(n_cmp):
    return jnp.arange(n_cmp) * CMP_STRIDE + CMP_BLOCK - 1


def cmp_sel_overlap(n_cmp, n_sel):
    cs = jnp.arange(n_cmp)[:, None] * CMP_STRIDE
    ss = jnp.arange(n_sel)[None, :] * SEL_BLOCK
    ov = jnp.minimum(cs + CMP_BLOCK, ss + SEL_BLOCK) - jnp.maximum(cs, ss)
    return jnp.clip(ov, 0, None).astype(jnp.float32) / CMP_BLOCK


def nsa_core(q, qpos, kc, vc, cend, overlap, gather_sel, kw, vw, kpos):
    B, Tq, H, dh = q.shape
    q5 = q.reshape(B, Tq, KV_HEADS, HPG, dh) * (dh ** -0.5)
    s = jnp.einsum('bqgjd,bcgd->bgjqc', q5, kc)
    p = masked_softmax(s, (cend[None, :] <= qpos[:, None])[None, None, None])
    o_cmp = jnp.einsum('bgjqc,bcgd->bqgjd', p.astype(vc.dtype), vc).reshape(B, Tq, H, dh)
    imp = jnp.einsum('bgjqc,cs->bgqs', p, overlap)
    n_sel = overlap.shape[1]
    blk = jnp.arange(n_sel)[None, :]
    cur = (qpos // SEL_BLOCK)[:, None]
    valid = blk <= cur
    forced = ((blk <= cur) & (blk > cur - N_LOCAL_BLOCKS)) | (blk < N_INIT_BLOCKS)
    score = jnp.where(forced, BIG, jnp.where(valid, imp, NEG))
    _, idx = lax.top_k(score, min(TOP_N, n_sel))
    kg, vg = gather_sel(idx)
    n_top = idx.shape[-1]
    key_pos = idx[..., None] * SEL_BLOCK + jnp.arange(SEL_BLOCK)
    msel = (key_pos <= qpos[None, None, :, None, None]).reshape(B, KV_HEADS, 1, Tq, n_top * SEL_BLOCK)
    s = jnp.einsum('bqgjd,bgqnkd->bgjqnk', q5, kg).reshape(B, KV_HEADS, HPG, Tq, n_top * SEL_BLOCK)
    p = masked_softmax(s, msel)
    o_sel = jnp.einsum('bgjqm,bgqmd->bqgjd', p.astype(vg.dtype), vg.reshape(B, KV_HEADS, Tq, n_top * SEL_BLOCK, dh)).reshape(B, Tq, H, dh)
    s = jnp.einsum('bqgjd,bkgd->bgjqk', q5, kw)
    mw = (kpos[None, :] <= qpos[:, None]) & (kpos[None, :] > qpos[:, None] - WINDOW) & (kpos[None, :] >= 0)
    p = masked_softmax(s, mw[None, None, None])
    o_win = jnp.einsum('bgjqk,bkgd->bqgjd', p.astype(vw.dtype), vw).reshape(B, Tq, H, dh)
    return o_cmp, o_sel, o_win


def in_project(x, pos, lw):
    B, T, _ = x.shape
    h = rmsnorm(x, lw['norm_g'])
    proj = h @ lw['w_in']
    sizes = [ATTN_WIDTH] + [KV_WIDTH] * 6 + [3 * N_HEADS, ATTN_WIDTH, GMLP_WIDTH, GMLP_WIDTH, GMLP_WIDTH, N_BRANCH * D_MODEL]
    offs = np.cumsum(sizes)[:-1].tolist()
    q, k_cmp, v_cmp, k_sel, v_sel, k_win, v_win, gl, za, u, v, zb, r = jnp.split(proj, offs, axis=-1)
    kvh = lambda t: t.reshape(B, T, KV_HEADS, HEAD_DIM)
    q = rope(q.reshape(B, T, N_HEADS, HEAD_DIM), pos)
    return (q, kvh(k_cmp), kvh(v_cmp), rope(kvh(k_sel), pos), kvh(v_sel), rope(kvh(k_win), pos), kvh(v_win), gl, za, u, v, zb, r)


def spatial_gate(v, w_s, b_s):
    B, T, _ = v.shape
    n_chunk = -(-T // CHUNK)
    v = jnp.pad(v, ((0, 0), (0, n_chunk * CHUNK - T), (0, 0))).reshape(B, n_chunk, CHUNK, GMLP_GROUPS, GMLP_GROUP_WIDTH)
    tri = jnp.tril(jnp.ones((CHUNK, CHUNK), dtype=bool))
    w = jnp.where(tri[None], w_s, 0.0).astype(v.dtype)
    out = jnp.einsum('gts,bcsgd->bctgd', w, v) + b_s.T[None, None, :, :, None].astype(v.dtype)
    return out.reshape(B, n_chunk * CHUNK, GMLP_WIDTH)[:, :T]


def mixer_out(x, o_cmp, o_sel, o_win, gl, za, u, v, zb, r, lw):
    B, T, _ = x.shape
    g = jax.nn.sigmoid(gl.astype(jnp.float32)).astype(x.dtype).reshape(B, T, N_HEADS, 3)
    o = g[..., 0:1] * o_cmp + g[..., 1:2] * o_sel + g[..., 2:3] * o_win
    pa = (o.reshape(B, T, ATTN_WIDTH) * jax.nn.silu(za)) @ lw['w_pa']
    v_n = layernorm(jax.nn.gelu(v), lw['v_norm_g'], lw['v_norm_b'])
    sgu = jax.nn.gelu(u) * spatial_gate(v_n, lw['w_spatial'], lw['b_spatial'])
    pb = (sgu * jax.nn.silu(zb)) @ lw['w_pb']
    ra, rb = jnp.split(r, 2, axis=-1)
    merged = jax.nn.sigmoid(ra) * pa + jax.nn.sigmoid(rb) * pb
    return x + merged @ lw['w_o'], v_n


def prompt_layer(x, lw):
    B, S, _ = x.shape
    pos = jnp.arange(S)
    q, k_cmp, v_cmp, k_sel, v_sel, k_win, v_win, gl, za, u, v, zb, r = in_project(x, pos, lw)
    kc = compress(k_cmp, lw['cmp_k'])
    n_cmp = kc.shape[1]
    cend = cmp_end_positions(n_cmp)
    kc = rope(kc, cend)
    vc = compress(v_cmp, lw['cmp_v'])
    n_sel = S // SEL_BLOCK
    overlap = cmp_sel_overlap(n_cmp, n_sel)
    k_sel_t = k_sel.reshape(B, n_sel, SEL_BLOCK, KV_HEADS, HEAD_DIM).transpose(0, 3, 1, 2, 4)
    v_sel_t = v_sel.reshape(B, n_sel, SEL_BLOCK, KV_HEADS, HEAD_DIM).transpose(0, 3, 1, 2, 4)
    bidx = jnp.arange(B)[:, None, None, None]
    gidx = jnp.arange(KV_HEADS)[None, :, None, None]

    def gather_sel(idx):
        return k_sel_t[bidx, gidx, idx], v_sel_t[bidx, gidx, idx]

    pad = ((0, 0), (WINDOW, 0), (0, 0), (0, 0))
    kw_pad, vw_pad = jnp.pad(k_win, pad), jnp.pad(v_win, pad)
    n_qb = S // QBLK
    q_blocks = q.reshape(B, n_qb, QBLK, N_HEADS, HEAD_DIM).transpose(1, 0, 2, 3, 4)

    def block_fn(args):
        i, qb = args
        start = i * QBLK
        qpos = start + jnp.arange(QBLK)
        kw = lax.dynamic_slice_in_dim(kw_pad, start, WINDOW + QBLK, axis=1)
        vw = lax.dynamic_slice_in_dim(vw_pad, start, WINDOW + QBLK, axis=1)
        kpos = start - WINDOW + jnp.arange(WINDOW + QBLK)
        return nsa_core(qb, qpos, kc, vc, cend, overlap, gather_sel, kw, vw, kpos)

    o_blocks = lax.map(block_fn, (jnp.arange(n_qb), q_blocks))
    o_cmp, o_sel, o_win = [t.transpose(1, 0, 2, 3, 4).reshape(B, S, N_HEADS, HEAD_DIM) for t in o_blocks]
    x_new, _ = mixer_out(x, o_cmp, o_sel, o_win, gl, za, u, v, zb, r, lw)
    wc = min(WINDOW, S)
    return x_new, (k_cmp, v_cmp, k_sel, v_sel, k_win[:, S - wc:], v_win[:, S - wc:])


def sample_layer(x, ck_cmp, cv_cmp, ck_sel, cv_sel, ck_win, cv_win, page_table, lw):
    Bd, Tn, _ = x.shape
    n_pages = page_table.shape[1]
    past = n_pages * PAGE_SIZE
    pos = past + jnp.arange(Tn)
    q, k_cmp, v_cmp, k_sel, v_sel, k_win, v_win, gl, za, u, v, zb, r = in_project(x, pos, lw)

    def gather_pages(pool):
        return pool[page_table].reshape(Bd, past, KV_HEADS, HEAD_DIM)

    kc = compress(jnp.concatenate([gather_pages(ck_cmp), k_cmp], axis=1), lw['cmp_k'])
    n_cmp = kc.shape[1]
    cend = cmp_end_positions(n_cmp)
    kc = rope(kc, cend)
    vc = compress(jnp.concatenate([gather_pages(cv_cmp), v_cmp], axis=1), lw['cmp_v'])
    n_past_blk = past // SEL_BLOCK
    n_tail = -(-Tn // SEL_BLOCK)
    overlap = cmp_sel_overlap(n_cmp, n_past_blk + n_tail)
    bpp = PAGE_SIZE // SEL_BLOCK
    pk = ck_sel.reshape(-1, bpp, SEL_BLOCK, KV_HEADS, HEAD_DIM)
    pv = cv_sel.reshape(-1, bpp, SEL_BLOCK, KV_HEADS, HEAD_DIM)
    tpad = ((0, 0), (0, n_tail * SEL_BLOCK - Tn), (0, 0), (0, 0))
    tk = jnp.pad(k_sel, tpad).reshape(Bd, n_tail, SEL_BLOCK, KV_HEADS, HEAD_DIM).transpose(0, 3, 1, 2, 4)
    tv = jnp.pad(v_sel, tpad).reshape(Bd, n_tail, SEL_BLOCK, KV_HEADS, HEAD_DIM).transpose(0, 3, 1, 2, 4)
    bidx = jnp.arange(Bd)[:, None, None, None]
    gidx = jnp.arange(KV_HEADS)[None, :, None, None]

    def gather_sel(idx):
        phys = page_table[bidx, jnp.minimum(idx // bpp, n_pages - 1)]
        sub = idx % bpp
        ti = jnp.clip(idx - n_past_blk, 0, n_tail - 1)
        is_tail = (idx >= n_past_blk)[..., None, None]
        kg = jnp.where(is_tail, tk[bidx, gidx, ti], pk[phys, sub, :, gidx])
        vg = jnp.where(is_tail, tv[bidx, gidx, ti], pv[phys, sub, :, gidx])
        return kg, vg

    wc = ck_win.shape[1]
    kw = jnp.concatenate([ck_win, k_win], axis=1)
    vw = jnp.concatenate([cv_win, v_win], axis=1)
    kpos = past - wc + jnp.arange(wc + Tn)
    o_cmp, o_sel, o_win = nsa_core(q, pos, kc, vc, cend, overlap, gather_sel, kw, vw, kpos)
    x_new, v_n = mixer_out(x, o_cmp, o_sel, o_win, gl, za, u, v, zb, r, lw)
    return x_new, (k_cmp, v_cmp, k_sel, v_sel, kw[:, kw.shape[1] - wc:], vw[:, vw.shape[1] - wc:], v_n)


def setup_inputs(seed: int = 0) -> dict:
    key = jax.random.key(seed)
    ks = jax.random.split(key, 32)
    f32 = jnp.float32
    nrm = lambda k, shape, scale: scale * jax.random.normal(k, shape, f32)
    n_pages = PAST_LEN // PAGE_SIZE
    n_used = DEC_BATCH * n_pages
    n_phys = n_used + max(1, n_used // 4)
    win_cache = min(WINDOW, PAST_LEN)
    pool_shape = (DEPTH, n_phys, PAGE_SIZE, KV_HEADS, HEAD_DIM)
    win_shape = (DEPTH, DEC_BATCH, win_cache, KV_HEADS, HEAD_DIM)
    perm = jax.random.permutation(ks[8], n_phys)
    page_table = perm[:n_used].reshape(DEC_BATCH, n_pages).astype(jnp.int32)
    L = DEPTH
    return {
        'x_prompt': nrm(ks[0], (BATCH, SEQ, D_MODEL), 1.0),
        'x_sample': nrm(ks[1], (DEC_BATCH, DEC_SEQ, D_MODEL), 1.0),
        'cache_k_cmp': nrm(ks[2], pool_shape, 1.0),
        'cache_v_cmp': nrm(ks[3], pool_shape, 1.0),
        'cache_k_sel': nrm(ks[4], pool_shape, 1.0),
        'cache_v_sel': nrm(ks[5], pool_shape, 1.0),
        'cache_k_win': nrm(ks[6], win_shape, 1.0),
        'cache_v_win': nrm(ks[7], win_shape, 1.0),
        'page_table': page_table,
        'norm_g': 1.0 + nrm(ks[9], (L, D_MODEL), 0.1),
        'w_in': nrm(ks[10], (L, D_MODEL, IN_WIDTH), D_MODEL ** -0.5),
        'cmp_k_w1': nrm(ks[11], (L, CMP_BLOCK * HEAD_DIM, CMP_HIDDEN), (CMP_BLOCK * HEAD_DIM) ** -0.5),
        'cmp_k_b1': nrm(ks[12], (L, CMP_HIDDEN), 0.02),
        'cmp_k_w2': nrm(ks[13], (L, CMP_HIDDEN, HEAD_DIM), CMP_HIDDEN ** -0.5),
        'cmp_k_b2': nrm(ks[14], (L, HEAD_DIM), 0.02),
        'cmp_v_w1': nrm(ks[15], (L, CMP_BLOCK * HEAD_DIM, CMP_HIDDEN), (CMP_BLOCK * HEAD_DIM) ** -0.5),
        'cmp_v_b1': nrm(ks[16], (L, CMP_HIDDEN), 0.02),
        'cmp_v_w2': nrm(ks[17], (L, CMP_HIDDEN, HEAD_DIM), CMP_HIDDEN ** -0.5),
        'cmp_v_b2': nrm(ks[18], (L, HEAD_DIM), 0.02),
        'v_norm_g': 1.0 + nrm(ks[19], (L, GMLP_WIDTH), 0.1),
        'v_norm_b': nrm(ks[20], (L, GMLP_WIDTH), 0.02),
        'w_spatial': nrm(ks[21], (L, GMLP_GROUPS, CHUNK, CHUNK), CHUNK ** -0.5),
        'b_spatial': 1.0 + nrm(ks[22], (L, GMLP_GROUPS, CHUNK), 0.1),
        'w_pa': nrm(ks[23], (L, ATTN_WIDTH, D_MODEL), ATTN_WIDTH ** -0.5),
        'w_pb': nrm(ks[24], (L, GMLP_WIDTH, D_MODEL), GMLP_WIDTH ** -0.5),
        'w_o': nrm(ks[25], (L, D_MODEL, D_MODEL), D_MODEL ** -0.5),
        'final_g': 1.0 + nrm(ks[26], (D_MODEL,), 0.1),
    }


def reference(x_prompt, x_sample, cache_k_cmp, cache_v_cmp, cache_k_sel, cache_v_sel, cache_k_win, cache_v_win, page_table,
              norm_g, w_in, cmp_k_w1, cmp_k_b1, cmp_k_w2, cmp_k_b2, cmp_v_w1, cmp_v_b1, cmp_v_w2, cmp_v_b2,
              v_norm_g, v_norm_b, w_spatial, b_spatial, w_pa, w_pb, w_o, final_g):
    hp, hs = x_prompt, x_sample
    st_p = [[] for _ in range(6)]
    st_s = [[] for _ in range(7)]
    for l in range(DEPTH):
        lw = {
            'norm_g': norm_g[l], 'w_in': w_in[l],
            'cmp_k': (cmp_k_w1[l], cmp_k_b1[l], cmp_k_w2[l], cmp_k_b2[l]),
            'cmp_v': (cmp_v_w1[l], cmp_v_b1[l], cmp_v_w2[l], cmp_v_b2[l]),
            'v_norm_g': v_norm_g[l], 'v_norm_b': v_norm_b[l],
            'w_spatial': w_spatial[l], 'b_spatial': b_spatial[l],
            'w_pa': w_pa[l], 'w_pb': w_pb[l], 'w_o': w_o[l],
        }
        hp, sp = prompt_layer(hp, lw)
        hs, ss = sample_layer(hs, cache_k_cmp[l], cache_v_cmp[l], cache_k_sel[l], cache_v_sel[l],
                              cache_k_win[l], cache_v_win[l], page_table, lw)
        for lst, t in zip(st_p, sp):
            lst.append(t)
        for lst, t in zip(st_s, ss):
            lst.append(t)
    y_prompt = rmsnorm(hp, final_g)
    y_sample = rmsnorm(hs, final_g)
    p_k_cmp, p_v_cmp, p_k_sel, p_v_sel, p_k_win, p_v_win = [jnp.stack(t) for t in st_p]
    s_k_cmp, s_v_cmp, s_k_sel, s_v_sel, s_k_win, s_v_win, s_v_chunk = [jnp.stack(t) for t in st_s]
    return (y_prompt, y_sample, p_k_cmp, p_v_cmp, p_k_sel, p_v_sel, p_k_win, p_v_win,
            s_k_cmp, s_v_cmp, s_k_sel, s_v_sel, s_k_win, s_v_win, s_v_chunk)
```

```python
import functools

import numpy as np
import jax
import jax.numpy as jnp
from jax import lax
from jax.experimental import pallas as pl
from jax.experimental.pallas import tpu as pltpu

F32 = jnp.float32
BF16 = jnp.bfloat16

D_MODEL = 1024
HEAD_DIM = 64
N_HEADS = 8
KV_HEADS = 2
HPG = N_HEADS // KV_HEADS
ATTN_WIDTH = N_HEADS * HEAD_DIM
KV_WIDTH = KV_HEADS * HEAD_DIM
ROT_DIM = HEAD_DIM // 4
ROT_HALF = ROT_DIM // 2
ROPE_THETA = 500000.0
CMP_BLOCK = 32
CMP_STRIDE = 16
CMP_HIDDEN = 128
SEL_BLOCK = 64
TOP_N = 16
N_INIT_BLOCKS = 1
N_LOCAL_BLOCKS = 2
WINDOW = 512
CHUNK = 128
GMLP_GROUPS = 4
GMLP_WIDTH = 512
PAGE_SIZE = 128
NEG = -1e30
BIG = 1e30
EPS = 1e-6

LANES = 128
GL_PAD = LANES - 3 * N_HEADS

OFF_Q = 0
OFF_KV = ATTN_WIDTH
OFF_GL = OFF_KV + 6 * KV_WIDTH
OFF_ZA = OFF_GL + LANES
OFF_U = OFF_ZA + ATTN_WIDTH
OFF_V = OFF_U + GMLP_WIDTH
OFF_ZB = OFF_V + GMLP_WIDTH
OFF_RA = OFF_ZB + GMLP_WIDTH
OFF_RB = OFF_RA + D_MODEL
W_TOT = OFF_RB + D_MODEL

VMEM_LIMIT = 56 * 1024 * 1024

TM_PROMPT = 512
TQ = 128
KT = 512
WIN_KEYS = WINDOW + TQ


def _nt(a, b):
    return lax.dot_general(a, b, (((1,), (1,)), ((), ())), preferred_element_type=F32)


def _dot(a, b):
    return jnp.dot(a, b, preferred_element_type=F32)


def _iota(shape, dim):
    return lax.broadcasted_iota(jnp.int32, shape, dim)


def _rope(slab, cos, s1, s2):
    return slab * cos + pltpu.roll(slab, LANES - ROT_HALF, 1) * s1 + pltpu.roll(slab, ROT_HALF, 1) * s2


def _rope_tables(pos):
    n = pos.shape[0]
    inv = jnp.power(jnp.float32(ROPE_THETA), -jnp.arange(0, ROT_DIM, 2, dtype=F32) / ROT_DIM)
    ang = pos.astype(F32)[:, None] * inv[None, :]
    cos, sin = jnp.cos(ang), jnp.sin(ang)
    rest = HEAD_DIM - ROT_DIM
    c = jnp.concatenate([cos, cos, jnp.ones((n, rest), F32)], axis=1)
    s1 = jnp.concatenate([-sin, jnp.zeros((n, HEAD_DIM - ROT_HALF), F32)], axis=1)
    s2 = jnp.concatenate([jnp.zeros((n, ROT_HALF), F32), sin, jnp.zeros((n, rest), F32)], axis=1)
    return tuple(jnp.tile(t, (1, LANES // HEAD_DIM)) for t in (c, s1, s2))


def _sigmoid(x):
    return 1.0 / (1.0 + jnp.exp(-x))


def _gelu(x):
    return jax.nn.gelu(x, approximate=True)


def _inproj_body(tm, sample, pos_tiles, x_ref, ng_ref, w_ref, cos_ref, s1_ref, s2_ref, wsp_ref, bsp_ref,
                 vng_ref, vnb_ref, wpb_ref, *outs):
    if sample:
        (q_ref, kcmp_ref, vcmp_ref, ksel_ref, vsel_ref, kwin_ref, vwin_ref,
         gat_ref, sza_ref, sra_ref, gpb_ref, vn_ref) = outs
    else:
        (q_ref, kcmp_ref, vcmp_ref, ksel_ref, vsel_ref, kwin_ref, vwin_ref,
         kaug_ref, kwp_ref, vsb_ref, vwb_ref, gat_ref, sza_ref, sra_ref, gpb_ref) = outs

    x = x_ref[...]
    r = lax.rsqrt(jnp.mean(x * x, axis=-1, keepdims=True) + EPS)
    h = ((x * r) * ng_ref[...]).astype(BF16)

    def proj(lo, hi):
        return _dot(h, w_ref[:, lo:hi])

    cos, s1, s2 = cos_ref[...], s1_ref[...], s2_ref[...]
    lane = _iota((tm, LANES), 1)
    low = lane < HEAD_DIM

    q = proj(OFF_Q, OFF_Q + ATTN_WIDTH)
    for pp in range(N_HEADS // 2):
        slab = _rope(q[:, pp * LANES:(pp + 1) * LANES], cos, s1, s2) * (HEAD_DIM ** -0.5)
        if sample:
            q_ref[:, pp * LANES:(pp + 1) * LANES] = slab
        else:
            q_ref[2 * pp] = jnp.where(low, slab, 0.0).astype(BF16)
            q_ref[2 * pp + 1] = jnp.where(low, pltpu.roll(slab, HEAD_DIM, 1), 0.0).astype(BF16)

    kv = proj(OFF_KV, OFF_KV + 6 * KV_WIDTH)
    kcmp_ref[...] = kv[:, 0:LANES]
    vcmp_ref[...] = kv[:, LANES:2 * LANES]
    ksel = _rope(kv[:, 2 * LANES:3 * LANES], cos, s1, s2)
    vsel = kv[:, 3 * LANES:4 * LANES]
    kwin = _rope(kv[:, 4 * LANES:5 * LANES], cos, s1, s2)
    vwin = kv[:, 5 * LANES:6 * LANES]
    ksel_ref[...] = ksel
    vsel_ref[...] = vsel
    kwin_ref[...] = kwin
    vwin_ref[...] = vwin
    if not sample:
        base = (pl.program_id(0) % pos_tiles) * tm
        blk = (base + _iota((tm, LANES), 0)) // SEL_BLOCK
        onehot = jnp.where(lane - HEAD_DIM == blk, 1.0, 0.0)
        kaug_ref[0] = jnp.where(low, ksel, onehot).astype(BF16)
        kaug_ref[1] = jnp.where(low, pltpu.roll(ksel, HEAD_DIM, 1), onehot).astype(BF16)
        kwp_ref[0] = jnp.where(low, kwin, 0.0).astype(BF16)
        kwp_ref[1] = jnp.where(low, pltpu.roll(kwin, HEAD_DIM, 1), 0.0).astype(BF16)
        vsb_ref[...] = vsel.astype(BF16)
        vwb_ref[...] = vwin.astype(BF16)

    gat_ref[...] = _sigmoid(proj(OFF_GL, OFF_GL + LANES))
    za = proj(OFF_ZA, OFF_ZA + ATTN_WIDTH)
    sza_ref[...] = za * _sigmoid(za)
    sra_ref[...] = _sigmoid(proj(OFF_RA, OFF_RA + D_MODEL))

    v = proj(OFF_V, OFF_V + GMLP_WIDTH)
    gv = _gelu(v)
    mu = jnp.mean(gv, axis=-1, keepdims=True)
    var = jnp.mean(jnp.square(gv - mu), axis=-1, keepdims=True)
    vn = ((gv - mu) * lax.rsqrt(var + EPS)) * vng_ref[...] + vnb_ref[...]
    if sample:
        vn_ref[...] = vn
    vnb16 = vn.astype(BF16)
    n_chunk = tm // CHUNK
    tri = _iota((CHUNK, CHUNK), 0) >= _iota((CHUNK, CHUNK), 1)
    mixed = []
    for g in range(GMLP_GROUPS):
        wm = jnp.where(tri, wsp_ref[g], 0.0).astype(BF16)
        cat = jnp.concatenate(
            [vnb16[c * CHUNK:(c + 1) * CHUNK, g * LANES:(g + 1) * LANES] for c in range(n_chunk)], axis=1)
        mixed.append(_dot(wm, cat))
    bsp = bsp_ref[...]
    sg = jnp.concatenate(
        [jnp.concatenate([mixed[g][:, c * LANES:(c + 1) * LANES] for g in range(GMLP_GROUPS)], axis=1) + bsp
         for c in range(n_chunk)], axis=0)
    u = proj(OFF_U, OFF_U + GMLP_WIDTH)
    zb = proj(OFF_ZB, OFF_ZB + GMLP_WIDTH)
    t = (_gelu(u) * sg) * (zb * _sigmoid(zb))
    pb = _dot(t.astype(BF16), wpb_ref[...])
    gpb_ref[...] = _sigmoid(proj(OFF_RB, OFF_RB + D_MODEL)) * pb


def _in_project(x2d, tables, prm, tm, sample):
    T = x2d.shape[0]
    nt = T // tm
    pos_tiles = tables[0].shape[0] // tm
    const = lambda *shape: pl.BlockSpec(shape, lambda i: (0,) * len(shape))
    row = lambda width: pl.BlockSpec((tm, width), lambda i: (i, 0))
    tab = pl.BlockSpec((tm, LANES), lambda i: (i % pos_tiles, 0))
    in_specs = [
        row(D_MODEL), const(1, D_MODEL),
        pl.BlockSpec((D_MODEL, W_TOT), lambda i: (0, 0), pipeline_mode=pl.Buffered(1)),
        tab, tab, tab,
        const(GMLP_GROUPS, CHUNK, CHUNK), const(CHUNK, GMLP_WIDTH), const(1, GMLP_WIDTH), const(1, GMLP_WIDTH),
        const(GMLP_WIDTH, D_MODEL),
    ]
    f32rows = lambda width: jax.ShapeDtypeStruct((T, width), F32)
    kv_shapes = [f32rows(KV_WIDTH)] * 6
    kv_specs = [row(KV_WIDTH)] * 6
    tail_shapes = [f32rows(LANES), f32rows(ATTN_WIDTH), f32rows(D_MODEL), f32rows(D_MODEL)]
    tail_specs = [row(LANES), row(ATTN_WIDTH), row(D_MODEL), row(D_MODEL)]
    if sample:
        out_shape = [f32rows(ATTN_WIDTH)] + kv_shapes + tail_shapes + [f32rows(GMLP_WIDTH)]
        out_specs = [row(ATTN_WIDTH)] + kv_specs + tail_specs + [row(GMLP_WIDTH)]
    else:
        stack = lambda n: pl.BlockSpec((n, tm, LANES), lambda i: (0, i, 0))
        bf = lambda *shape: jax.ShapeDtypeStruct(shape, BF16)
        out_shape = ([bf(N_HEADS, T, LANES)] + kv_shapes
                     + [bf(KV_HEADS, T, LANES), bf(KV_HEADS, T, LANES), bf(T, LANES), bf(T, LANES)] + tail_shapes)
        out_specs = [stack(N_HEADS)] + kv_specs + [stack(KV_HEADS), stack(KV_HEADS), row(LANES), row(LANES)] + tail_specs
    return pl.pallas_call(
        functools.partial(_inproj_body, tm, sample, pos_tiles),
        grid=(nt,),
        in_specs=in_specs,
        out_specs=out_specs,
        out_shape=out_shape,
        compiler_params=pltpu.CompilerParams(dimension_semantics=("arbitrary",), vmem_limit_bytes=VMEM_LIMIT),
        name="in_project_sample" if sample else "in_project_prompt",
    )(x2d, prm["norm_g"], prm["w_in"], *tables, prm["w_sp_s" if sample else "w_sp"],
      prm["b_sp_s" if sample else "b_sp"], prm["v_norm_g"], prm["v_norm_b"], prm["w_pb"])


def _compress_mlp(load_rows, n, w1_ref, b1_ref, w2_ref, b2_ref):
    xcat = jnp.concatenate([load_rows(t) for t in range(CMP_STRIDE)], axis=1).astype(BF16)
    hh = _dot(xcat, w1_ref[...])
    hid = jnp.concatenate(
        [hh[:, 0:LANES] + pltpu.roll(hh[:, LANES:2 * LANES], n - 1, 0),
         hh[:, 2 * LANES:3 * LANES] + pltpu.roll(hh[:, 3 * LANES:4 * LANES], n - 1, 0)], axis=1) + b1_ref[...]
    return _dot(_gelu(hid).astype(BF16), w2_ref[...]) + b2_ref[...]


def _compress_prompt_body(n, kr_ref, vr_ref, w1k, b1k, w2k, b2k, w1v, b1v, w2v, b2v, cos_ref, s1_ref, s2_ref,
                          ovl_ref, kc_ref, vco_ref):
    kc = _compress_mlp(lambda t: kr_ref[0, pl.ds(t, n, stride=CMP_STRIDE), :], n, w1k, b1k, w2k, b2k)
    kc = _rope(kc, cos_ref[...], s1_ref[...], s2_ref[...])
    low = _iota((n, LANES), 1) < HEAD_DIM
    kc_ref[0, 0] = jnp.where(low, kc, 0.0).astype(BF16)
    kc_ref[0, 1] = jnp.where(low, pltpu.roll(kc, HEAD_DIM, 1), 0.0).astype(BF16)
    vc = _compress_mlp(lambda t: vr_ref[0, pl.ds(t, n, stride=CMP_STRIDE), :], n, w1v, b1v, w2v, b2v)
    vco_ref[0, :, 0:LANES] = vc.astype(BF16)
    vco_ref[0, :, LANES:2 * LANES] = ovl_ref[...]


def _compress_prompt(kr, vr, prm, tables, ovl):
    B, S, _ = kr.shape
    n = S // CMP_STRIDE
    const = lambda *shape: pl.BlockSpec(shape, lambda b: (0,) * len(shape))
    rows = pl.BlockSpec((1, S, LANES), lambda b: (b, 0, 0))
    wspecs = [const(CMP_STRIDE * LANES, 4 * LANES), const(1, 2 * LANES), const(2 * LANES, LANES), const(1, LANES)]
    return pl.pallas_call(
        functools.partial(_compress_prompt_body, n),
        grid=(B,),
        in_specs=[rows, rows] + wspecs + wspecs + [const(n, LANES)] * 3 + [const(n, LANES)],
        out_specs=[pl.BlockSpec((1, KV_HEADS, n, LANES), lambda b: (b, 0, 0, 0)),
                   pl.BlockSpec((1, n, 2 * LANES), lambda b: (b, 0, 0))],
        out_shape=[jax.ShapeDtypeStruct((B, KV_HEADS, n, LANES), BF16),
                   jax.ShapeDtypeStruct((B, n, 2 * LANES), BF16)],
        compiler_params=pltpu.CompilerParams(dimension_semantics=("arbitrary",), vmem_limit_bytes=VMEM_LIMIT),
        name="compress_prompt",
    )(kr, vr, *prm["cmp_k"], *prm["cmp_v"], *tables, ovl)


def _page_copies(pt_ref, pool_ref, buf, sem, b, n_pages, start):
    def body(p, carry):
        page = pt_ref[b, p] if start else 0
        cp = pltpu.make_async_copy(pool_ref.at[page], buf.at[pl.ds(p * PAGE_SIZE, PAGE_SIZE)], sem)
        if start:
            cp.start()
        else:
            cp.wait()
        return carry
    lax.fori_loop(0, n_pages, body, 0)


def _compress_sample_body(n, n_pages, pt_ref, pk_ref, pv_ref, w1k, b1k, w2k, b2k, w1v, b1v, w2v, b2v,
                          cos_ref, s1_ref, s2_ref, kc_ref, vc_ref, kb0, kb1, vb0, vb1, sem):
    s = pl.program_id(0)
    bufs = ((kb0, vb0, 0), (kb1, vb1, 1))

    def start(b, slot):
        kb, vb, si = bufs[slot]
        _page_copies(pt_ref, pk_ref, kb, sem.at[2 * si], b, n_pages, True)
        _page_copies(pt_ref, pv_ref, vb, sem.at[2 * si + 1], b, n_pages, True)

    def finish(slot):
        kb, vb, si = bufs[slot]
        _page_copies(pt_ref, pk_ref, kb, sem.at[2 * si], 0, n_pages, False)
        _page_copies(pt_ref, pv_ref, vb, sem.at[2 * si + 1], 0, n_pages, False)
        kc = _compress_mlp(lambda t: kb[pl.ds(t, n, stride=CMP_STRIDE), :], n, w1k, b1k, w2k, b2k)
        kc_ref[slot] = _rope(kc, cos_ref[...], s1_ref[...], s2_ref[...]).astype(BF16)
        vc = _compress_mlp(lambda t: vb[pl.ds(t, n, stride=CMP_STRIDE), :], n, w1v, b1v, w2v, b2v)
        vc_ref[slot] = vc.astype(BF16)

    @pl.when(s == 0)
    def _():
        start(0, 0)

    start(2 * s + 1, 1)
    finish(0)

    @pl.when(s + 1 < pl.num_programs(0))
    def _():
        start(2 * s + 2, 0)

    finish(1)


def _compress_sample(page_table, pool_k, pool_v, prm, tables):
    Bd, n_pages = page_table.shape
    past = n_pages * PAGE_SIZE
    n = past // CMP_STRIDE
    const = lambda *shape: pl.BlockSpec(shape, lambda s, pt: (0,) * len(shape))
    wspecs = [const(CMP_STRIDE * LANES, 4 * LANES), const(1, 2 * LANES), const(2 * LANES, LANES), const(1, LANES)]
    anyspec = pl.BlockSpec(memory_space=pl.ANY)
    out_spec = pl.BlockSpec((2, n, LANES), lambda s, pt: (s, 0, 0))
    grid_spec = pltpu.PrefetchScalarGridSpec(
        num_scalar_prefetch=1,
        grid=(Bd // 2,),
        in_specs=[anyspec, anyspec] + wspecs + wspecs + [const(n, LANES)] * 3,
        out_specs=[out_spec, out_spec],
        scratch_shapes=[pltpu.VMEM((past, LANES), F32)] * 4 + [pltpu.SemaphoreType.DMA((4,))],
    )
    return pl.pallas_call(
        functools.partial(_compress_sample_body, n, n_pages),
        grid_spec=grid_spec,
        out_shape=[jax.ShapeDtypeStruct((Bd, n, LANES), BF16)] * 2,
        compiler_params=pltpu.CompilerParams(dimension_semantics=("arbitrary",), vmem_limit_bytes=VMEM_LIMIT),
        name="compress_sample",
    )(page_table, pool_k, pool_v, *prm["cmp_k"], *prm["cmp_v"], *tables)


def _mixer_tail(o, sza, sra, gpb, x, wpa_ref, wo_ref, fg_ref):
    pa = _dot((o * sza).astype(BF16), wpa_ref[...])
    merged = sra * pa + gpb
    hn = x + _dot(merged.astype(BF16), wo_ref[...])
    r = lax.rsqrt(jnp.mean(hn * hn, axis=-1, keepdims=True) + EPS)
    return (hn * r) * fg_ref[...]


def _topk_mask(score, blk, n_blocks, axis):
    cnt = jnp.zeros(score.shape, jnp.int32)
    for sp in range(n_blocks):
        b = lax.slice_in_dim(score, sp, sp + 1, axis=axis)
        ge = jnp.where(b >= score, 1, 0)
        gt = jnp.where(b > score, 1, 0)
        cnt = cnt + jnp.where(blk > sp, ge, gt)
    return cnt < TOP_N


def _attn_prompt_body(n_sel, q_ref, ka_ref, vs_ref, kw_ref, vw_ref, kc_ref, vco_ref, gat_ref, sza_ref, sra_ref,
                      gpb_ref, x_ref, wpa_ref, wo_ref, fg_ref, y_ref):
    i = pl.program_id(1)
    M = HPG * TQ
    qpos = i * TQ + (_iota((M, 1), 0) & (TQ - 1))
    n_cmp_pad = kc_ref.shape[2]
    gt = gat_ref[...]
    o_groups = []
    for g in range(KV_HEADS):
        qs = q_ref[HPG * g:HPG * (g + 1)].reshape(M, LANES)

        s = _nt(qs, kc_ref[0, g])
        mk = (_iota((M, n_cmp_pad), 1) * CMP_STRIDE + (CMP_BLOCK - 1)) <= qpos
        s = jnp.where(mk, s, NEG)
        e = jnp.exp(s - jnp.max(s, axis=-1, keepdims=True))
        p = jnp.where(mk, e / jnp.sum(e, axis=-1, keepdims=True), 0.0)
        r = _dot(p.astype(BF16), vco_ref[0])
        o_cmp = r[:, 0:LANES]
        impc = r[:, LANES:2 * LANES]
        imp = impc[0:TQ] + impc[TQ:2 * TQ] + impc[2 * TQ:3 * TQ] + impc[3 * TQ:4 * TQ]
        sc = imp.T[HEAD_DIM:HEAD_DIM + n_sel]
        blk = _iota((n_sel, TQ), 0)
        cur = (i * TQ + _iota((n_sel, TQ), 1)) // SEL_BLOCK
        valid = blk <= cur
        forced = (valid & (blk > cur - N_LOCAL_BLOCKS)) | (blk < N_INIT_BLOCKS)
        score = jnp.where(forced, BIG, jnp.where(valid, sc, NEG))
        sel = _topk_mask(score, blk, n_sel, 0)
        selneg = jnp.where(sel, 0.0, NEG)
        seln_t = jnp.concatenate([jnp.zeros((LANES - n_sel, TQ), F32), selneg], axis=0).T.astype(BF16)
        qa = qs + jnp.concatenate([seln_t] * HPG, axis=0)

        def scores(kt):
            k = ka_ref[g, pl.ds(pl.multiple_of(kt * KT, KT), KT), :]
            return _nt(qa, k)

        def update(carry, s, kt):
            m, l, acc = carry
            mn = jnp.maximum(m, jnp.max(s, axis=-1, keepdims=True))
            a = jnp.exp(m - mn)
            pe = jnp.exp(s - mn)
            l = a * l + jnp.sum(pe, axis=-1, keepdims=True)
            v = vs_ref[pl.ds(pl.multiple_of(kt * KT, KT), KT), :]
            acc = a * acc + _dot(pe.astype(BF16), v)
            return mn, l, acc

        n_full = i // (KT // TQ)
        init = (jnp.full((M, 1), NEG, F32), jnp.zeros((M, 1), F32), jnp.zeros((M, LANES), F32))
        carry = lax.fori_loop(0, n_full, lambda kt, c: update(c, scores(kt), kt), init)
        s = scores(n_full)
        s = jnp.where(n_full * KT + _iota((M, KT), 1) <= qpos, s, NEG)
        _, l, acc = update(carry, s, n_full)
        o_sel = acc / l

        st = pl.multiple_of(jnp.maximum(i * TQ - WINDOW, 0), TQ)
        s = _nt(qs, kw_ref[g, pl.ds(st, WIN_KEYS), :])
        kpos = st + _iota((M, WIN_KEYS), 1)
        mk = (kpos <= qpos) & (kpos > qpos - WINDOW)
        s = jnp.where(mk, s, NEG)
        e = jnp.exp(s - jnp.max(s, axis=-1, keepdims=True))
        o_win = _dot(e.astype(BF16), vw_ref[pl.ds(st, WIN_KEYS), :]) / jnp.sum(e, axis=-1, keepdims=True)

        def gcol(c):
            return jnp.concatenate(
                [gt[:, 3 * (HPG * g + j) + c:3 * (HPG * g + j) + c + 1] for j in range(HPG)], axis=0)

        o_groups.append(gcol(0) * o_cmp + gcol(1) * o_sel + gcol(2) * o_win)

    low = _iota((TQ, LANES), 1) < HEAD_DIM
    slabs = []
    for pp in range(N_HEADS // 2):
        g, j0 = pp // 2, (2 * pp) % HPG
        a = o_groups[g][j0 * TQ:(j0 + 1) * TQ]
        b = o_groups[g][(j0 + 1) * TQ:(j0 + 2) * TQ]
        if g == 0:
            b = pltpu.roll(b, HEAD_DIM, 1)
        else:
            a = pltpu.roll(a, HEAD_DIM, 1)
        slabs.append(jnp.where(low, a, b))
    o = jnp.concatenate(slabs, axis=1)
    y_ref[...] = _mixer_tail(o, sza_ref[...], sra_ref[...], gpb_ref[...], x_ref[...], wpa_ref, wo_ref, fg_ref)


def _attn_prompt(B, S, q_hm, kaug, vsb, kwp, vwb, kcp, vco, gates, sza, sra, gpb, x2d, prm):
    nq = S // TQ
    n_sel = S // SEL_BLOCK
    n_cmp_pad = kcp.shape[2]
    row = lambda width: pl.BlockSpec((TQ, width), lambda b, i: (b * nq + i, 0))
    const = lambda *shape: pl.BlockSpec(shape, lambda b, i: (0,) * len(shape))
    in_specs = [
        pl.BlockSpec((N_HEADS, TQ, LANES), lambda b, i: (0, b * nq + i, 0)),
        pl.BlockSpec((KV_HEADS, S, LANES), lambda b, i: (0, b, 0)),
        pl.BlockSpec((S, LANES), lambda b, i: (b, 0)),
        pl.BlockSpec((KV_HEADS, S, LANES), lambda b, i: (0, b, 0)),
        pl.BlockSpec((S, LANES), lambda b, i: (b, 0)),
        pl.BlockSpec((1, KV_HEADS, n_cmp_pad, LANES), lambda b, i: (b, 0, 0, 0)),
        pl.BlockSpec((1, n_cmp_pad, 2 * LANES), lambda b, i: (b, 0, 0)),
        row(LANES), row(ATTN_WIDTH), row(D_MODEL), row(D_MODEL), row(D_MODEL),
        const(ATTN_WIDTH, D_MODEL), const(D_MODEL, D_MODEL), const(1, D_MODEL),
    ]
    return pl.pallas_call(
        functools.partial(_attn_prompt_body, n_sel),
        grid=(B, nq),
        in_specs=in_specs,
        out_specs=row(D_MODEL),
        out_shape=jax.ShapeDtypeStruct((B * S, D_MODEL), F32),
        compiler_params=pltpu.CompilerParams(dimension_semantics=("arbitrary", "arbitrary"),
                                             vmem_limit_bytes=VMEM_LIMIT),
        name="attn_prompt",
    )(q_hm, kaug, vsb, kwp, vwb, kcp, vco, gates, sza, sra, gpb, x2d, prm["w_pa"], prm["w_o"], prm["final_g"])


SEL_HALF = 64
TOK_PAD = 8


def _attn_sample_body(n_pages, tn, pt_ref, pk_ref, pv_ref, qa_ref, kc_ref, vc_ref, ovl_ref, kt_ref, vt_ref,
                      ckw_ref, cvw_ref, kwn_ref, vwn_ref, o_ref, kb0, kb1, vb0, vb1, sem):
    s_id = pl.program_id(0)
    past = n_pages * PAGE_SIZE
    n_past_blk = past // SEL_BLOCK
    n_blk_pad = ovl_ref.shape[1]
    R = N_HEADS * TOK_PAD
    GR = HPG * TOK_PAD
    CK = 512
    chunks_per_half = SEL_HALF * SEL_BLOCK // CK
    wc = ckw_ref.shape[1]
    bufs = ((kb0, vb0, 0), (kb1, vb1, 1))
    t_row = _iota((R, 1), 0) & (TOK_PAD - 1)

    def start(b, slot):
        kb, vb, si = bufs[slot]
        _page_copies(pt_ref, pk_ref, kb, sem.at[2 * si], b, n_pages, True)
        _page_copies(pt_ref, pv_ref, vb, sem.at[2 * si + 1], b, n_pages, True)

    def finish(slot):
        kb, vb, si = bufs[slot]
        qa = qa_ref[slot]

        n_cmp_pad = kc_ref.shape[1]
        s = _nt(qa, kc_ref[slot])
        mk = _iota((R, n_cmp_pad), 1) < n_cmp_pad - 1
        s = jnp.where(mk, s, NEG)
        e = jnp.exp(s - jnp.max(s, axis=-1, keepdims=True))
        p = jnp.where(mk, e / jnp.sum(e, axis=-1, keepdims=True), 0.0).astype(BF16)
        o_cmp = _dot(p, vc_ref[slot])
        impc = _dot(p, ovl_ref[...])
        imp = jnp.concatenate(
            [impc[g * GR:g * GR + TOK_PAD] + impc[g * GR + TOK_PAD:g * GR + 2 * TOK_PAD]
             + impc[g * GR + 2 * TOK_PAD:g * GR + 3 * TOK_PAD] + impc[g * GR + 3 * TOK_PAD:g * GR + 4 * TOK_PAD]
             for g in range(KV_HEADS)], axis=0)
        nr = KV_HEADS * TOK_PAD
        blk = _iota((nr, n_blk_pad), 1)
        cur = (past + jnp.minimum(_iota((nr, n_blk_pad), 0) & (TOK_PAD - 1), tn - 1)) // SEL_BLOCK
        valid = blk <= cur
        forced = (valid & (blk > cur - N_LOCAL_BLOCKS)) | (blk < N_INIT_BLOCKS)
        score = jnp.where(forced, BIG, jnp.where(valid, imp, NEG))
        sel = _topk_mask(score, blk, n_past_blk + 1, 1)
        selneg = jnp.where(sel, 0.0, NEG)
        selneg = jnp.concatenate(
            [selneg[g * TOK_PAD:(g + 1) * TOK_PAD] for g in range(KV_HEADS) for _ in range(HPG)], axis=0)
        low = _iota((R, LANES), 1) < SEL_HALF
        qaug = []
        for hf in range(n_past_blk // SEL_HALF):
            slab = selneg[:, (hf // 2) * LANES:(hf // 2 + 1) * LANES]
            if hf % 2:
                slab = pltpu.roll(slab, SEL_HALF, 1)
            qaug.append(jnp.concatenate([qa, jnp.where(low, slab, 0.0).astype(BF16)], axis=1))

        _page_copies(pt_ref, pk_ref, kb, sem.at[2 * si], 0, n_pages, False)
        _page_copies(pt_ref, pv_ref, vb, sem.at[2 * si + 1], 0, n_pages, False)

        def update(carry, s, v):
            m, l, acc = carry
            mn = jnp.maximum(m, jnp.max(s, axis=-1, keepdims=True))
            a = jnp.exp(m - mn)
            pe = jnp.exp(s - mn)
            l = a * l + jnp.sum(pe, axis=-1, keepdims=True)
            acc = a * acc + _dot(pe.astype(BF16), v)
            return mn, l, acc

        carry = (jnp.full((R, 1), NEG, F32), jnp.zeros((R, 1), F32), jnp.zeros((R, LANES), F32))
        oh_lane = _iota((CK, LANES), 1)
        oh_row = _iota((CK, LANES), 0) // SEL_BLOCK
        for hf in range(n_past_blk // SEL_HALF):
            for c in range(chunks_per_half):
                off = (hf * chunks_per_half + c) * CK
                k = kb[pl.ds(off, CK), :].astype(BF16)
                onehot = jnp.where(oh_lane == c * (CK // SEL_BLOCK) + oh_row, 1.0, 0.0).astype(BF16)
                s = _nt(qaug[hf], jnp.concatenate([k, onehot], axis=1))
                carry = update(carry, s, vb[pl.ds(off, CK), :].astype(BF16))
        col = _iota((R, LANES), 1)
        s = jnp.where(col <= t_row, _nt(qa, kt_ref[slot]), NEG)
        _, l, acc = update(carry, s, vt_ref[slot])
        o_sel = acc / l

        sa = jnp.where(_iota((R, wc), 1) > t_row, _nt(qa, ckw_ref[slot].astype(BF16)), NEG)
        sb = jnp.where(col <= t_row, _nt(qa, kwn_ref[slot]), NEG)
        m = jnp.maximum(jnp.max(sa, axis=-1, keepdims=True), jnp.max(sb, axis=-1, keepdims=True))
        ea, eb = jnp.exp(sa - m), jnp.exp(sb - m)
        den = jnp.sum(ea, axis=-1, keepdims=True) + jnp.sum(eb, axis=-1, keepdims=True)
        o_win = (_dot(ea.astype(BF16), cvw_ref[slot].astype(BF16)) + _dot(eb.astype(BF16), vwn_ref[slot])) / den

        o_ref[slot, 0] = o_cmp
        o_ref[slot, 1] = o_sel
        o_ref[slot, 2] = o_win

    @pl.when(s_id == 0)
    def _():
        start(0, 0)

    start(2 * s_id + 1, 1)
    finish(0)

    @pl.when(s_id + 1 < pl.num_programs(0))
    def _():
        start(2 * s_id + 2, 0)

    finish(1)


def _attn_sample(page_table, pool_k, pool_v, tn, qa, kc, vc, ovl, ktail, vtail, ckw, cvw, kwn, vwn):
    Bd, n_pages = page_table.shape
    past = n_pages * PAGE_SIZE
    R = N_HEADS * TOK_PAD
    wc = ckw.shape[1]
    const = lambda *shape: pl.BlockSpec(shape, lambda s, pt: (0,) * len(shape))
    pair = lambda *shape: pl.BlockSpec((2,) + shape, lambda s, pt: (s,) + (0,) * len(shape))
    anyspec = pl.BlockSpec(memory_space=pl.ANY)
    grid_spec = pltpu.PrefetchScalarGridSpec(
        num_scalar_prefetch=1,
        grid=(Bd // 2,),
        in_specs=[anyspec, anyspec, pair(R, LANES), pair(kc.shape[1], LANES), pair(vc.shape[1], LANES),
                  const(*ovl.shape), pair(LANES, LANES), pair(LANES, LANES),
                  pair(wc, LANES), pair(wc, LANES), pair(LANES, LANES), pair(LANES, LANES)],
        out_specs=pair(3, R, LANES),
        scratch_shapes=[pltpu.VMEM((past, LANES), F32)] * 4 + [pltpu.SemaphoreType.DMA((4,))],
    )
    return pl.pallas_call(
        functools.partial(_attn_sample_body, n_pages, tn),
        grid_spec=grid_spec,
        out_shape=jax.ShapeDtypeStruct((Bd, 3, R, LANES), F32),
        compiler_params=pltpu.CompilerParams(dimension_semantics=("arbitrary",), vmem_limit_bytes=VMEM_LIMIT),
        name="attn_sample",
    )(page_table, pool_k, pool_v, qa, kc, vc, ovl, ktail, vtail, ckw, cvw, kwn, vwn)


def _mixer_sample_body(o3_ref, g3_ref, sza_ref, sra_ref, gpb_ref, x_ref, wpa_ref, wo_ref, fg_ref, y_ref):
    o = g3_ref[0] * o3_ref[0] + g3_ref[1] * o3_ref[1] + g3_ref[2] * o3_ref[2]
    y_ref[...] = _mixer_tail(o, sza_ref[...], sra_ref[...], gpb_ref[...], x_ref[...], wpa_ref, wo_ref, fg_ref)


def _mixer_sample(o3, g3, sza, sra, gpb, x2d, prm):
    T = x2d.shape[0]
    full = lambda *shape: pl.BlockSpec(shape, lambda i: (0,) * len(shape))
    return pl.pallas_call(
        _mixer_sample_body,
        grid=(1,),
        in_specs=[full(3, T, ATTN_WIDTH), full(3, T, ATTN_WIDTH), full(T, ATTN_WIDTH), full(T, D_MODEL),
                  full(T, D_MODEL), full(T, D_MODEL), full(ATTN_WIDTH, D_MODEL), full(D_MODEL, D_MODEL),
                  full(1, D_MODEL)],
        out_specs=full(T, D_MODEL),
        out_shape=jax.ShapeDtypeStruct((T, D_MODEL), F32),
        compiler_params=pltpu.CompilerParams(dimension_semantics=("arbitrary",), vmem_limit_bytes=VMEM_LIMIT),
        name="mixer_sample",
    )(o3, g3, sza, sra, gpb, x2d, prm["w_pa"], prm["w_o"], prm["final_g"])


def _overlap(n_cmp, n_sel):
    cs = np.arange(n_cmp)[:, None] * CMP_STRIDE
    ss = np.arange(n_sel)[None, :] * SEL_BLOCK
    ov = np.minimum(cs + CMP_BLOCK, ss + SEL_BLOCK) - np.maximum(cs, ss)
    return np.clip(ov, 0, None).astype(np.float32) / CMP_BLOCK


def _cmp_weights(w1, b1, w2, b2):
    w1r = w1.reshape(2, CMP_STRIDE, HEAD_DIM, CMP_HIDDEN).transpose(1, 2, 0, 3)
    big = jnp.zeros((CMP_STRIDE, KV_HEADS, HEAD_DIM, KV_HEADS, 2, CMP_HIDDEN), F32)
    w2b = jnp.zeros((KV_HEADS, CMP_HIDDEN, KV_HEADS, HEAD_DIM), F32)
    for g in range(KV_HEADS):
        big = big.at[:, g, :, g].set(w1r)
        w2b = w2b.at[g, :, g].set(w2)
    return (big.reshape(CMP_STRIDE * LANES, 4 * LANES).astype(BF16), jnp.tile(b1, KV_HEADS)[None],
            w2b.reshape(KV_HEADS * CMP_HIDDEN, LANES).astype(BF16), jnp.tile(b2, KV_HEADS)[None])


def kernel(x_prompt, x_sample, cache_k_cmp, cache_v_cmp, cache_k_sel, cache_v_sel, cache_k_win, cache_v_win, page_table, norm_g, w_in, cmp_k_w1, cmp_k_b1, cmp_k_w2, cmp_k_b2, cmp_v_w1, cmp_v_b1, cmp_v_w2, cmp_v_b2, v_norm_g, v_norm_b, w_spatial, b_spatial, w_pa, w_pb, w_o, final_g):
    B, S, _ = x_prompt.shape
    Bd, tn, _ = x_sample.shape
    depth = w_in.shape[0]
    assert depth == 1, "single-layer step"
    n_pages = page_table.shape[1]
    past = n_pages * PAGE_SIZE
    n_phys = cache_k_cmp.shape[1]
    wc = cache_k_win.shape[2]

    split = OFF_GL + 3 * N_HEADS
    w = w_in[0]
    prm = {
        "norm_g": norm_g,
        "w_in": jnp.concatenate([w[:, :split], jnp.zeros((D_MODEL, GL_PAD), F32), w[:, split:]], axis=1).astype(BF16),
        "w_sp": w_spatial[0],
        "b_sp": jnp.repeat(b_spatial[0].T, LANES, axis=1),
        "w_sp_s": jnp.stack([jnp.kron(jnp.eye(CHUNK // tn, dtype=F32), w_spatial[0, g, :tn, :tn])
                             for g in range(GMLP_GROUPS)]),
        "b_sp_s": jnp.tile(jnp.repeat(b_spatial[0, :, :tn].T, LANES, axis=1), (CHUNK // tn, 1)),
        "v_norm_g": v_norm_g, "v_norm_b": v_norm_b,
        "w_pb": w_pb[0].astype(BF16), "w_pa": w_pa[0].astype(BF16), "w_o": w_o[0].astype(BF16),
        "final_g": final_g[None],
        "cmp_k": _cmp_weights(cmp_k_w1[0], cmp_k_b1[0], cmp_k_w2[0], cmp_k_b2[0]),
        "cmp_v": _cmp_weights(cmp_v_w1[0], cmp_v_b1[0], cmp_v_w2[0], cmp_v_b2[0]),
    }

    xp = x_prompt.reshape(B * S, D_MODEL)
    (q_hm, p_kcmp, p_vcmp, p_ksel, p_vsel, p_kwin, p_vwin, kaug, kwp, vsb, vwb, gates, sza, sra, gpb) = _in_project(
        xp, _rope_tables(jnp.arange(S)), prm, TM_PROMPT, False)
    n_half = S // CMP_STRIDE
    n_cmp = (S - CMP_BLOCK) // CMP_STRIDE + 1
    n_sel = S // SEL_BLOCK
    ovl = np.zeros((n_half, LANES), np.float32)
    ovl[:n_cmp, HEAD_DIM:HEAD_DIM + n_sel] = _overlap(n_cmp, n_sel)
    cend_tables = _rope_tables(jnp.arange(n_half) * CMP_STRIDE + CMP_BLOCK - 1)
    kcp, vco = _compress_prompt(p_kcmp.reshape(B, S, LANES), p_vcmp.reshape(B, S, LANES), prm, cend_tables,
                                jnp.asarray(ovl, BF16))
    y_prompt = _attn_prompt(B, S, q_hm, kaug, vsb, kwp, vwb, kcp, vco, gates, sza, sra, gpb, xp, prm)

    xs = x_sample.reshape(Bd * tn, D_MODEL)
    pos_s = jnp.tile(past + jnp.arange(tn), Bd)
    (q_s, s_kcmp, s_vcmp, s_ksel, s_vsel, s_kwin, s_vwin, gates_s, sza_s, sra_s, gpb_s, vn_s) = _in_project(
        xs, _rope_tables(pos_s), prm, Bd * tn, True)
    n_half_s = past // CMP_STRIDE
    pools = [c[0].reshape(n_phys, PAGE_SIZE, LANES) for c in (cache_k_cmp, cache_v_cmp, cache_k_sel, cache_v_sel)]
    cend_s = _rope_tables(jnp.arange(n_half_s) * CMP_STRIDE + CMP_BLOCK - 1)
    kc_s, vc_s = _compress_sample(page_table, pools[0], pools[1], prm, cend_s)

    n_cmp_s = (past + tn - CMP_BLOCK) // CMP_STRIDE + 1
    n_blk_s = past // SEL_BLOCK + -(-tn // SEL_BLOCK)
    assert n_cmp_s == n_half_s - 1 and tn <= TOK_PAD
    ovl_s = np.zeros((n_half_s, 2 * LANES), np.float32)
    ovl_s[:n_cmp_s, :n_blk_s] = _overlap(n_cmp_s, n_blk_s)
    q5 = q_s.reshape(Bd, tn, KV_HEADS, HPG, HEAD_DIM).transpose(0, 2, 3, 1, 4)
    q5 = jnp.pad(q5, ((0, 0), (0, 0), (0, 0), (0, TOK_PAD - tn), (0, 0))).reshape(Bd, KV_HEADS, HPG * TOK_PAD, HEAD_DIM)
    qa = jnp.zeros((Bd, KV_HEADS, HPG * TOK_PAD, KV_HEADS, HEAD_DIM), F32)
    for g in range(KV_HEADS):
        qa = qa.at[:, g, :, g].set(q5[:, g])
    qa = qa.reshape(Bd, N_HEADS * TOK_PAD, LANES).astype(BF16)
    pad_rows = lambda t: jnp.pad(t.reshape(Bd, tn, LANES), ((0, 0), (0, LANES - tn), (0, 0))).astype(BF16)
    ckw = cache_k_win[0].reshape(Bd, wc, LANES)
    cvw = cache_v_win[0].reshape(Bd, wc, LANES)
    o3 = _attn_sample(page_table, pools[2], pools[3], tn, qa, kc_s, vc_s, jnp.asarray(ovl_s, BF16),
                      pad_rows(s_ksel), pad_rows(s_vsel), ckw, cvw, pad_rows(s_kwin), pad_rows(s_vwin))
    s_k_win = jnp.concatenate([ckw[:, tn:], s_kwin.reshape(Bd, tn, LANES)], axis=1)
    s_v_win = jnp.concatenate([cvw[:, tn:], s_vwin.reshape(Bd, tn, LANES)], axis=1)
    first_group = jnp.arange(N_HEADS * TOK_PAD)[:, None] < HPG * TOK_PAD
    o3r = jnp.where(first_group, o3[..., :HEAD_DIM], o3[..., HEAD_DIM:])
    o3r = o3r.reshape(Bd, 3, N_HEADS, TOK_PAD, HEAD_DIM)[:, :, :, :tn]
    o3r = o3r.transpose(1, 0, 3, 2, 4).reshape(3, Bd * tn, ATTN_WIDTH)
    g3 = gates_s[:, :3 * N_HEADS].reshape(Bd * tn, N_HEADS, 3).transpose(2, 0, 1)
    g3 = jnp.repeat(g3, HEAD_DIM, axis=2)
    y_sample = _mixer_sample(o3r, g3, sza_s, sra_s, gpb_s, xs, prm)

    kv5 = lambda t, b, n: t.reshape(1, b, n, KV_HEADS, HEAD_DIM)
    pw = min(WINDOW, S)
    win = lambda t: kv5(t, B, S)[:, :, S - pw:]
    return (y_prompt.reshape(B, S, D_MODEL), y_sample.reshape(Bd, tn, D_MODEL),
            kv5(p_kcmp, B, S), kv5(p_vcmp, B, S), kv5(p_ksel, B, S), kv5(p_vsel, B, S), win(p_kwin), win(p_vwin),
            kv5(s_kcmp, Bd, tn), kv5(s_vcmp, Bd, tn), kv5(s_ksel, Bd, tn), kv5(s_vsel, Bd, tn),
            kv5(s_k_win, Bd, wc), kv5(s_v_win, Bd, wc), vn_s.reshape(1, Bd, tn, GMLP_WIDTH))
```

```python
import functools

import numpy as np
import jax
import jax.numpy as jnp
from jax import lax
from jax.experimental import pallas as pl
from jax.experimental.pallas import tpu as pltpu

F32 = jnp.float32
BF16 = jnp.bfloat16

D_MODEL = 1024
HEAD_DIM = 64
N_HEADS = 8
KV_HEADS = 2
HPG = N_HEADS // KV_HEADS
ATTN_WIDTH = N_HEADS * HEAD_DIM
KV_WIDTH = KV_HEADS * HEAD_DIM
ROT_DIM = HEAD_DIM // 4
ROT_HALF = ROT_DIM // 2
ROPE_THETA = 500000.0
CMP_BLOCK = 32
CMP_STRIDE = 16
CMP_HIDDEN = 128
SEL_BLOCK = 64
TOP_N = 16
N_INIT_BLOCKS = 1
N_LOCAL_BLOCKS = 2
WINDOW = 512
CHUNK = 128
GMLP_GROUPS = 4
GMLP_WIDTH = 512
PAGE_SIZE = 128
NEG = -1e30
BIG = 1e30
EPS = 1e-6

LANES = 128
GL_PAD = LANES - 3 * N_HEADS

OFF_Q = 0
OFF_KV = ATTN_WIDTH
OFF_GL = OFF_KV + 6 * KV_WIDTH
OFF_ZA = OFF_GL + LANES
OFF_U = OFF_ZA + ATTN_WIDTH
OFF_V = OFF_U + GMLP_WIDTH
OFF_ZB = OFF_V + GMLP_WIDTH
OFF_RA = OFF_ZB + GMLP_WIDTH
OFF_RB = OFF_RA + D_MODEL
W_TOT = OFF_RB + D_MODEL

VMEM_LIMIT = 56 * 1024 * 1024

TM_PROMPT = 512
TQ = 128
KT = 512
WIN_KEYS = WINDOW + TQ
SEL_HALF = 64
TOK_PAD = 8
CK_PAGES = 4


def _nt(a, b):
    return lax.dot_general(a, b, (((1,), (1,)), ((), ())), preferred_element_type=F32)


def _dot(a, b):
    return jnp.dot(a, b, preferred_element_type=F32)


def _iota(shape, dim):
    return lax.broadcasted_iota(jnp.int32, shape, dim)


def _rope(slab, cos, s1, s2):
    return slab * cos + pltpu.roll(slab, LANES - ROT_HALF, 1) * s1 + pltpu.roll(slab, ROT_HALF, 1) * s2


def _rope_tables(pos):
    n = pos.shape[0]
    inv = jnp.power(jnp.float32(ROPE_THETA), -jnp.arange(0, ROT_DIM, 2, dtype=F32) / ROT_DIM)
    ang = pos.astype(F32)[:, None] * inv[None, :]
    cos, sin = jnp.cos(ang), jnp.sin(ang)
    rest = HEAD_DIM - ROT_DIM
    c = jnp.concatenate([cos, cos, jnp.ones((n, rest), F32)], axis=1)
    s1 = jnp.concatenate([-sin, jnp.zeros((n, HEAD_DIM - ROT_HALF), F32)], axis=1)
    s2 = jnp.concatenate([jnp.zeros((n, ROT_HALF), F32), sin, jnp.zeros((n, rest), F32)], axis=1)
    return tuple(jnp.tile(t, (1, LANES // HEAD_DIM)) for t in (c, s1, s2))


def _sigmoid(x):
    return 1.0 / (1.0 + jnp.exp(-x))


def _gelu(x):
    return jax.nn.gelu(x, approximate=True)


def _inproj_body(tm, sample, pos_tiles, x_ref, ng_ref, w_ref, cos_ref, s1_ref, s2_ref, wsp_ref, bsp_ref,
                 vng_ref, vnb_ref, wpb_ref, *outs):
    if sample:
        (q_ref, kcmp_ref, vcmp_ref, ksel_ref, vsel_ref, kwin_ref, vwin_ref,
         gat_ref, sza_ref, sra_ref, gpb_ref, vn_ref) = outs
    else:
        (q_ref, kcmp_ref, vcmp_ref, ksel_ref, vsel_ref, kwin_ref, vwin_ref,
         kcr_ref, vcr_ref, kaug_ref, kwt_ref, vsb_ref, vwb_ref, gat_ref, sza_ref, sra_ref, gpb_ref) = outs

    x = x_ref[...]
    r = lax.rsqrt(jnp.mean(x * x, axis=-1, keepdims=True) + EPS)
    h = ((x * r) * ng_ref[...]).astype(BF16)

    def proj(lo, hi):
        return _dot(h, w_ref[:, lo:hi])

    cos, s1, s2 = cos_ref[...], s1_ref[...], s2_ref[...]
    low = _iota((tm, LANES), 1) < HEAD_DIM

    q = proj(OFF_Q, OFF_Q + ATTN_WIDTH)
    for pp in range(N_HEADS // 2):
        slab = _rope(q[:, pp * LANES:(pp + 1) * LANES], cos, s1, s2) * (HEAD_DIM ** -0.5)
        if sample:
            q_ref[:, pp * LANES:(pp + 1) * LANES] = slab
        elif (2 * pp) // HPG == 0:
            q_ref[2 * pp] = jnp.where(low, slab, 0.0).astype(BF16)
            q_ref[2 * pp + 1] = jnp.where(low, pltpu.roll(slab, HEAD_DIM, 1), 0.0).astype(BF16)
        else:
            q_ref[2 * pp] = jnp.where(low, 0.0, pltpu.roll(slab, HEAD_DIM, 1)).astype(BF16)
            q_ref[2 * pp + 1] = jnp.where(low, 0.0, slab).astype(BF16)

    kv = proj(OFF_KV, OFF_KV + 6 * KV_WIDTH)
    kcmp = kv[:, 0:LANES]
    vcmp = kv[:, LANES:2 * LANES]
    ksel = _rope(kv[:, 2 * LANES:3 * LANES], cos, s1, s2)
    vsel = kv[:, 3 * LANES:4 * LANES]
    kwin = _rope(kv[:, 4 * LANES:5 * LANES], cos, s1, s2)
    vwin = kv[:, 5 * LANES:6 * LANES]
    if sample:
        for ref, val in ((kcmp_ref, kcmp), (vcmp_ref, vcmp), (ksel_ref, ksel), (vsel_ref, vsel),
                         (kwin_ref, kwin), (vwin_ref, vwin)):
            ref[...] = val
    else:
        ksel_t = ksel.T
        kwin_t = kwin.T
        for ref, val in ((kcmp_ref, kcmp.T), (vcmp_ref, vcmp.T), (ksel_ref, ksel_t), (vsel_ref, vsel.T),
                         (kwin_ref, kwin_t), (vwin_ref, vwin.T)):
            ref[0] = val
        kcr_ref[...] = kcmp
        vcr_ref[...] = vcmp
        base = (pl.program_id(0) % pos_tiles) * tm
        blk = (base + _iota((LANES, tm), 1)) // SEL_BLOCK
        onehot = jnp.where(_iota((LANES, tm), 0) == blk, 1.0, 0.0)
        kaug_ref[0, 0:LANES] = ksel_t.astype(BF16)
        kaug_ref[0, LANES:2 * LANES] = onehot.astype(BF16)
        kwt_ref[0] = kwin_t.astype(BF16)
        vsb_ref[...] = vsel.astype(BF16)
        vwb_ref[...] = vwin.astype(BF16)

    gat_ref[...] = _sigmoid(proj(OFF_GL, OFF_GL + LANES))
    za = proj(OFF_ZA, OFF_ZA + ATTN_WIDTH)
    sza_ref[...] = za * _sigmoid(za)
    sra_ref[...] = _sigmoid(proj(OFF_RA, OFF_RA + D_MODEL))

    v = proj(OFF_V, OFF_V + GMLP_WIDTH)
    gv = _gelu(v)
    mu = jnp.mean(gv, axis=-1, keepdims=True)
    var = jnp.mean(jnp.square(gv - mu), axis=-1, keepdims=True)
    vn = ((gv - mu) * lax.rsqrt(var + EPS)) * vng_ref[...] + vnb_ref[...]
    if sample:
        vn_ref[...] = vn
    vnb16 = vn.astype(BF16)
    n_chunk = tm // CHUNK
    tri = _iota((CHUNK, CHUNK), 0) >= _iota((CHUNK, CHUNK), 1)
    mixed = []
    for g in range(GMLP_GROUPS):
        wm = jnp.where(tri, wsp_ref[g], 0.0).astype(BF16)
        cat = jnp.concatenate(
            [vnb16[c * CHUNK:(c + 1) * CHUNK, g * LANES:(g + 1) * LANES] for c in range(n_chunk)], axis=1)
        mixed.append(_dot(wm, cat))
    bsp = bsp_ref[...]
    sg = jnp.concatenate(
        [jnp.concatenate([mixed[g][:, c * LANES:(c + 1) * LANES] for g in range(GMLP_GROUPS)], axis=1) + bsp
         for c in range(n_chunk)], axis=0)
    u = proj(OFF_U, OFF_U + GMLP_WIDTH)
    zb = proj(OFF_ZB, OFF_ZB + GMLP_WIDTH)
    t = (_gelu(u) * sg) * (zb * _sigmoid(zb))
    pb = _dot(t.astype(BF16), wpb_ref[...])
    gpb_ref[...] = _sigmoid(proj(OFF_RB, OFF_RB + D_MODEL)) * pb


def _in_project(x2d, tables, prm, tm, sample, seq):
    T = x2d.shape[0]
    nt = T // tm
    pos_tiles = tables[0].shape[0] // tm
    const = lambda *shape: pl.BlockSpec(shape, lambda i: (0,) * len(shape))
    row = lambda width: pl.BlockSpec((tm, width), lambda i: (i, 0))
    tab = pl.BlockSpec((tm, LANES), lambda i: (i % pos_tiles, 0))
    in_specs = [
        row(D_MODEL), const(1, D_MODEL),
        pl.BlockSpec((D_MODEL, W_TOT), lambda i: (0, 0), pipeline_mode=pl.Buffered(1)),
        tab, tab, tab,
        const(GMLP_GROUPS, CHUNK, CHUNK), const(CHUNK, GMLP_WIDTH), const(1, GMLP_WIDTH), const(1, GMLP_WIDTH),
        const(GMLP_WIDTH, D_MODEL),
    ]
    f32rows = lambda width: jax.ShapeDtypeStruct((T, width), F32)
    tail_shapes = [f32rows(LANES), f32rows(ATTN_WIDTH), f32rows(D_MODEL), f32rows(D_MODEL)]
    tail_specs = [row(LANES), row(ATTN_WIDTH), row(D_MODEL), row(D_MODEL)]
    if sample:
        out_shape = [f32rows(ATTN_WIDTH)] + [f32rows(KV_WIDTH)] * 6 + tail_shapes + [f32rows(GMLP_WIDTH)]
        out_specs = [row(ATTN_WIDTH)] + [row(KV_WIDTH)] * 6 + tail_specs + [row(GMLP_WIDTH)]
    else:
        nb = T // seq
        bf = lambda *shape: jax.ShapeDtypeStruct(shape, BF16)
        tposed = lambda rows: pl.BlockSpec((1, rows, tm), lambda i: (i // pos_tiles, 0, i % pos_tiles))
        out_shape = ([bf(N_HEADS, T, LANES)] + [jax.ShapeDtypeStruct((nb, KV_WIDTH, seq), F32)] * 6
                     + [f32rows(KV_WIDTH)] * 2
                     + [bf(nb, 2 * LANES, seq), bf(nb, LANES, seq), bf(T, LANES), bf(T, LANES)] + tail_shapes)
        out_specs = ([pl.BlockSpec((N_HEADS, tm, LANES), lambda i: (0, i, 0))] + [tposed(KV_WIDTH)] * 6
                     + [row(KV_WIDTH)] * 2
                     + [tposed(2 * LANES), tposed(LANES), row(LANES), row(LANES)] + tail_specs)
    return pl.pallas_call(
        functools.partial(_inproj_body, tm, sample, pos_tiles),
        grid=(nt,),
        in_specs=in_specs,
        out_specs=out_specs,
        out_shape=out_shape,
        compiler_params=pltpu.CompilerParams(dimension_semantics=("arbitrary",), vmem_limit_bytes=VMEM_LIMIT),
        name="in_project_sample" if sample else "in_project_prompt",
    )(x2d, prm["norm_g"], prm["w_in"], *tables, prm["w_sp_s" if sample else "w_sp"],
      prm["b_sp_s" if sample else "b_sp"], prm["v_norm_g"], prm["v_norm_b"], prm["w_pb"])


def _compress_mlp(load_rows, n, w1_ref, b1_ref, w2_ref, b2_ref):
    xcat = jnp.concatenate([load_rows(t) for t in range(CMP_STRIDE)], axis=1).astype(BF16)
    hh = _dot(xcat, w1_ref[...])
    hid = jnp.concatenate(
        [hh[:, 0:LANES] + pltpu.roll(hh[:, LANES:2 * LANES], n - 1, 0),
         hh[:, 2 * LANES:3 * LANES] + pltpu.roll(hh[:, 3 * LANES:4 * LANES], n - 1, 0)], axis=1) + b1_ref[...]
    return _dot(_gelu(hid).astype(BF16), w2_ref[...]) + b2_ref[...]


def _compress_prompt_body(n, kr_ref, vr_ref, w1k, b1k, w2k, b2k, w1v, b1v, w2v, b2v, cos_ref, s1_ref, s2_ref,
                          ovl_ref, kct_ref, vco_ref):
    kc = _compress_mlp(lambda t: kr_ref[0, pl.ds(t, n, stride=CMP_STRIDE), :], n, w1k, b1k, w2k, b2k)
    kct_ref[0] = _rope(kc, cos_ref[...], s1_ref[...], s2_ref[...]).T.astype(BF16)
    vc = _compress_mlp(lambda t: vr_ref[0, pl.ds(t, n, stride=CMP_STRIDE), :], n, w1v, b1v, w2v, b2v)
    vco_ref[0, :, 0:LANES] = vc.astype(BF16)
    vco_ref[0, :, LANES:2 * LANES] = ovl_ref[...]


def _compress_prompt(kr, vr, prm, tables, ovl):
    B, S, _ = kr.shape
    n = S // CMP_STRIDE
    const = lambda *shape: pl.BlockSpec(shape, lambda b: (0,) * len(shape))
    rows = pl.BlockSpec((1, S, LANES), lambda b: (b, 0, 0))
    wspecs = [const(CMP_STRIDE * LANES, 4 * LANES), const(1, 2 * LANES), const(2 * LANES, LANES), const(1, LANES)]
    return pl.pallas_call(
        functools.partial(_compress_prompt_body, n),
        grid=(B,),
        in_specs=[rows, rows] + wspecs + wspecs + [const(n, LANES)] * 3 + [const(n, LANES)],
        out_specs=[pl.BlockSpec((1, LANES, n), lambda b: (b, 0, 0)),
                   pl.BlockSpec((1, n, 2 * LANES), lambda b: (b, 0, 0))],
        out_shape=[jax.ShapeDtypeStruct((B, LANES, n), BF16),
                   jax.ShapeDtypeStruct((B, n, 2 * LANES), BF16)],
        compiler_params=pltpu.CompilerParams(dimension_semantics=("arbitrary",), vmem_limit_bytes=VMEM_LIMIT),
        name="compress_prompt",
    )(kr, vr, *prm["cmp_k"], *prm["cmp_v"], *tables, ovl)


def _page_copies(pt_ref, pool_ref, buf, sem, b, n_pages, start):
    def body(p, carry):
        page = pt_ref[b, p] if start else 0
        cp = pltpu.make_async_copy(pool_ref.at[page], buf.at[p], sem)
        if start:
            cp.start()
        else:
            cp.wait()
        return carry
    lax.fori_loop(0, n_pages, body, 0)


def _compress_sample_body(n, n_pages, pt_ref, pk_ref, pv_ref, w1k, b1k, w2k, b2k, w1v, b1v, w2v, b2v,
                          cos_ref, s1_ref, s2_ref, kct_ref, vc_ref, kb0, kb1, vb0, vb1, rows, sem):
    s = pl.program_id(0)
    bufs = ((kb0, vb0, 0), (kb1, vb1, 1))

    def start(b, slot):
        kb, vb, si = bufs[slot]
        _page_copies(pt_ref, pk_ref, kb, sem.at[2 * si], b, n_pages, True)
        _page_copies(pt_ref, pv_ref, vb, sem.at[2 * si + 1], b, n_pages, True)

    def to_rows(buf):
        for p in range(n_pages):
            rows[p * PAGE_SIZE:(p + 1) * PAGE_SIZE, :] = buf[p].T

    def finish(slot):
        kb, vb, si = bufs[slot]
        _page_copies(pt_ref, pk_ref, kb, sem.at[2 * si], 0, n_pages, False)
        _page_copies(pt_ref, pv_ref, vb, sem.at[2 * si + 1], 0, n_pages, False)
        to_rows(kb)
        kc = _compress_mlp(lambda t: rows[pl.ds(t, n, stride=CMP_STRIDE), :], n, w1k, b1k, w2k, b2k)
        kct_ref[slot] = _rope(kc, cos_ref[...], s1_ref[...], s2_ref[...]).T.astype(BF16)
        to_rows(vb)
        vc = _compress_mlp(lambda t: rows[pl.ds(t, n, stride=CMP_STRIDE), :], n, w1v, b1v, w2v, b2v)
        vc_ref[slot] = vc.astype(BF16)

    @pl.when(s == 0)
    def _():
        start(0, 0)

    start(2 * s + 1, 1)
    finish(0)

    @pl.when(s + 1 < pl.num_programs(0))
    def _():
        start(2 * s + 2, 0)

    finish(1)


def _compress_sample(page_table, pool_k, pool_v, prm, tables):
    Bd, n_pages = page_table.shape
    past = n_pages * PAGE_SIZE
    n = past // CMP_STRIDE
    const = lambda *shape: pl.BlockSpec(shape, lambda s, pt: (0,) * len(shape))
    wspecs = [const(CMP_STRIDE * LANES, 4 * LANES), const(1, 2 * LANES), const(2 * LANES, LANES), const(1, LANES)]
    anyspec = pl.BlockSpec(memory_space=pl.ANY)
    grid_spec = pltpu.PrefetchScalarGridSpec(
        num_scalar_prefetch=1,
        grid=(Bd // 2,),
        in_specs=[anyspec, anyspec] + wspecs + wspecs + [const(n, LANES)] * 3,
        out_specs=[pl.BlockSpec((2, LANES, n), lambda s, pt: (s, 0, 0)),
                   pl.BlockSpec((2, n, LANES), lambda s, pt: (s, 0, 0))],
        scratch_shapes=[pltpu.VMEM((n_pages, LANES, PAGE_SIZE), F32)] * 4 + [pltpu.VMEM((past, LANES), F32)]
        + [pltpu.SemaphoreType.DMA((4,))],
    )
    return pl.pallas_call(
        functools.partial(_compress_sample_body, n, n_pages),
        grid_spec=grid_spec,
        out_shape=[jax.ShapeDtypeStruct((Bd, LANES, n), BF16), jax.ShapeDtypeStruct((Bd, n, LANES), BF16)],
        compiler_params=pltpu.CompilerParams(dimension_semantics=("arbitrary",), vmem_limit_bytes=VMEM_LIMIT),
        name="compress_sample",
    )(page_table, pool_k, pool_v, *prm["cmp_k"], *prm["cmp_v"], *tables)


def _mixer_tail(o, sza, sra, gpb, x, wpa_ref, wo_ref, fg_ref):
    pa = _dot((o * sza).astype(BF16), wpa_ref[...])
    merged = sra * pa + gpb
    hn = x + _dot(merged.astype(BF16), wo_ref[...])
    r = lax.rsqrt(jnp.mean(hn * hn, axis=-1, keepdims=True) + EPS)
    return (hn * r) * fg_ref[...]


def _topk_mask(score, blk, n_blocks, axis):
    cnt = jnp.zeros(score.shape, jnp.int32)
    for sp in range(n_blocks):
        b = lax.slice_in_dim(score, sp, sp + 1, axis=axis)
        ge = jnp.where(b >= score, 1, 0)
        gt = jnp.where(b > score, 1, 0)
        cnt = cnt + jnp.where(blk > sp, ge, gt)
    return cnt < TOP_N


def _attn_prompt_body(n_sel, q_ref, ka_ref, vs_ref, kw_ref, vw_ref, kc_ref, vco_ref, gat_ref, sza_ref, sra_ref,
                      gpb_ref, x_ref, wpa_ref, wo_ref, fg_ref, y_ref):
    i = pl.program_id(1)
    M = HPG * TQ
    qpos = i * TQ + (_iota((M, 1), 0) & (TQ - 1))
    n_cmp_pad = kc_ref.shape[2]
    gt = gat_ref[...]
    o_groups = []
    for g in range(KV_HEADS):
        qs = q_ref[HPG * g:HPG * (g + 1)].reshape(M, LANES)

        s = _dot(qs, kc_ref[0])
        mk = (_iota((M, n_cmp_pad), 1) * CMP_STRIDE + (CMP_BLOCK - 1)) <= qpos
        s = jnp.where(mk, s, NEG)
        e = jnp.exp(s - jnp.max(s, axis=-1, keepdims=True))
        p = jnp.where(mk, e / jnp.sum(e, axis=-1, keepdims=True), 0.0)
        r = _dot(p.astype(BF16), vco_ref[0])
        o_cmp = r[:, 0:LANES]
        impc = r[:, LANES:2 * LANES]
        imp = impc[0:TQ] + impc[TQ:2 * TQ] + impc[2 * TQ:3 * TQ] + impc[3 * TQ:4 * TQ]
        sc = imp.T[HEAD_DIM:HEAD_DIM + n_sel]
        blk = _iota((n_sel, TQ), 0)
        cur = (i * TQ + _iota((n_sel, TQ), 1)) // SEL_BLOCK
        valid = blk <= cur
        forced = (valid & (blk > cur - N_LOCAL_BLOCKS)) | (blk < N_INIT_BLOCKS)
        score = jnp.where(forced, BIG, jnp.where(valid, sc, NEG))
        sel = _topk_mask(score, blk, n_sel, 0)
        selneg = jnp.where(sel, 0.0, NEG)
        seln_t = jnp.concatenate([selneg, jnp.zeros((LANES - n_sel, TQ), F32)], axis=0).T.astype(BF16)
        qa = jnp.concatenate([qs, jnp.concatenate([seln_t] * HPG, axis=0)], axis=1)

        def scores(kt):
            return _dot(qa, ka_ref[0, :, pl.ds(pl.multiple_of(kt * KT, KT), KT)])

        def update(carry, s, kt):
            m, l, acc = carry
            mn = jnp.maximum(m, jnp.max(s, axis=-1, keepdims=True))
            a = jnp.exp(m - mn)
            pe = jnp.exp(s - mn)
            l = a * l + jnp.sum(pe, axis=-1, keepdims=True)
            v = vs_ref[pl.ds(pl.multiple_of(kt * KT, KT), KT), :]
            acc = a * acc + _dot(pe.astype(BF16), v)
            return mn, l, acc

        n_full = i // (KT // TQ)
        init = (jnp.full((M, 1), NEG, F32), jnp.zeros((M, 1), F32), jnp.zeros((M, LANES), F32))
        carry = lax.fori_loop(0, n_full, lambda kt, c: update(c, scores(kt), kt), init)
        s = scores(n_full)
        s = jnp.where(n_full * KT + _iota((M, KT), 1) <= qpos, s, NEG)
        _, l, acc = update(carry, s, n_full)
        o_sel = acc / l

        st = pl.multiple_of(jnp.maximum(i * TQ - WINDOW, 0), TQ)
        s = _dot(qs, kw_ref[0, :, pl.ds(st, WIN_KEYS)])
        kpos = st + _iota((M, WIN_KEYS), 1)
        mk = (kpos <= qpos) & (kpos > qpos - WINDOW)
        s = jnp.where(mk, s, NEG)
        e = jnp.exp(s - jnp.max(s, axis=-1, keepdims=True))
        o_win = _dot(e.astype(BF16), vw_ref[pl.ds(st, WIN_KEYS), :]) / jnp.sum(e, axis=-1, keepdims=True)

        def gcol(c):
            return jnp.concatenate(
                [gt[:, 3 * (HPG * g + j) + c:3 * (HPG * g + j) + c + 1] for j in range(HPG)], axis=0)

        o_groups.append(gcol(0) * o_cmp + gcol(1) * o_sel + gcol(2) * o_win)

    low = _iota((TQ, LANES), 1) < HEAD_DIM
    slabs = []
    for pp in range(N_HEADS // 2):
        g, j0 = pp // 2, (2 * pp) % HPG
        a = o_groups[g][j0 * TQ:(j0 + 1) * TQ]
        b = o_groups[g][(j0 + 1) * TQ:(j0 + 2) * TQ]
        if g == 0:
            b = pltpu.roll(b, HEAD_DIM, 1)
        else:
            a = pltpu.roll(a, HEAD_DIM, 1)
        slabs.append(jnp.where(low, a, b))
    o = jnp.concatenate(slabs, axis=1)
    y_ref[...] = _mixer_tail(o, sza_ref[...], sra_ref[...], gpb_ref[...], x_ref[...], wpa_ref, wo_ref, fg_ref)


def _attn_prompt(B, S, q_hm, kaug, vsb, kwt, vwb, kct, vco, gates, sza, sra, gpb, x2d, prm):
    nq = S // TQ
    n_sel = S // SEL_BLOCK
    n_cmp_pad = kct.shape[2]
    row = lambda width: pl.BlockSpec((TQ, width), lambda b, i: (b * nq + i, 0))
    const = lambda *shape: pl.BlockSpec(shape, lambda b, i: (0,) * len(shape))
    batch = lambda *shape: pl.BlockSpec((1,) + shape, lambda b, i: (b,) + (0,) * len(shape))
    in_specs = [
        pl.BlockSpec((N_HEADS, TQ, LANES), lambda b, i: (0, b * nq + i, 0)),
        batch(2 * LANES, S),
        pl.BlockSpec((S, LANES), lambda b, i: (b, 0)),
        batch(LANES, S),
        pl.BlockSpec((S, LANES), lambda b, i: (b, 0)),
        batch(LANES, n_cmp_pad),
        batch(n_cmp_pad, 2 * LANES),
        row(LANES), row(ATTN_WIDTH), row(D_MODEL), row(D_MODEL), row(D_MODEL),
        const(ATTN_WIDTH, D_MODEL), const(D_MODEL, D_MODEL), const(1, D_MODEL),
    ]
    return pl.pallas_call(
        functools.partial(_attn_prompt_body, n_sel),
        grid=(B, nq),
        in_specs=in_specs,
        out_specs=row(D_MODEL),
        out_shape=jax.ShapeDtypeStruct((B * S, D_MODEL), F32),
        compiler_params=pltpu.CompilerParams(dimension_semantics=("arbitrary", "arbitrary"),
                                             vmem_limit_bytes=VMEM_LIMIT),
        name="attn_prompt",
    )(q_hm, kaug, vsb, kwt, vwb, kct, vco, gates, sza, sra, gpb, x2d, prm["w_pa"], prm["w_o"], prm["final_g"])


def _attn_sample_body(n_pages, tn, pt_ref, pk_ref, pv_ref, qa_ref, kc_ref, vc_ref, ovl_ref, kt_ref, vt_ref,
                      ckw_ref, cvw_ref, kwn_ref, vwn_ref, o_ref, kb0, kb1, vb0, vb1, sem):
    s_id = pl.program_id(0)
    past = n_pages * PAGE_SIZE
    n_past_blk = past // SEL_BLOCK
    n_blk_pad = ovl_ref.shape[1]
    R = N_HEADS * TOK_PAD
    GR = HPG * TOK_PAD
    CK = CK_PAGES * PAGE_SIZE
    blk_per_chunk = CK // SEL_BLOCK
    chunks_per_half = SEL_HALF // blk_per_chunk
    wc = ckw_ref.shape[2]
    bufs = ((kb0, vb0, 0), (kb1, vb1, 1))
    t_row = _iota((R, 1), 0) & (TOK_PAD - 1)

    def start(b, slot):
        kb, vb, si = bufs[slot]
        _page_copies(pt_ref, pk_ref, kb, sem.at[2 * si], b, n_pages, True)
        _page_copies(pt_ref, pv_ref, vb, sem.at[2 * si + 1], b, n_pages, True)

    def finish(slot):
        kb, vb, si = bufs[slot]
        qa = qa_ref[slot]

        n_cmp_pad = kc_ref.shape[2]
        s = _dot(qa, kc_ref[slot])
        mk = _iota((R, n_cmp_pad), 1) < n_cmp_pad - 1
        s = jnp.where(mk, s, NEG)
        e = jnp.exp(s - jnp.max(s, axis=-1, keepdims=True))
        p = jnp.where(mk, e / jnp.sum(e, axis=-1, keepdims=True), 0.0).astype(BF16)
        o_cmp = _dot(p, vc_ref[slot])
        impc = _dot(p, ovl_ref[...])
        imp = jnp.concatenate(
            [impc[g * GR:g * GR + TOK_PAD] + impc[g * GR + TOK_PAD:g * GR + 2 * TOK_PAD]
             + impc[g * GR + 2 * TOK_PAD:g * GR + 3 * TOK_PAD] + impc[g * GR + 3 * TOK_PAD:g * GR + 4 * TOK_PAD]
             for g in range(KV_HEADS)], axis=0)
        nr = KV_HEADS * TOK_PAD
        blk = _iota((nr, n_blk_pad), 1)
        cur = (past + jnp.minimum(_iota((nr, n_blk_pad), 0) & (TOK_PAD - 1), tn - 1)) // SEL_BLOCK
        valid = blk <= cur
        forced = (valid & (blk > cur - N_LOCAL_BLOCKS)) | (blk < N_INIT_BLOCKS)
        score = jnp.where(forced, BIG, jnp.where(valid, imp, NEG))
        sel = _topk_mask(score, blk, n_past_blk + 1, 1)
        selneg = jnp.where(sel, 0.0, NEG)
        selneg = jnp.concatenate(
            [selneg[g * TOK_PAD:(g + 1) * TOK_PAD] for g in range(KV_HEADS) for _ in range(HPG)], axis=0)
        low = _iota((R, LANES), 1) < SEL_HALF
        qaug = []
        for hf in range(n_past_blk // SEL_HALF):
            slab = selneg[:, (hf // 2) * LANES:(hf // 2 + 1) * LANES]
            if hf % 2:
                slab = pltpu.roll(slab, SEL_HALF, 1)
            qaug.append(jnp.concatenate([qa, jnp.where(low, slab, 0.0).astype(BF16)], axis=1))

        _page_copies(pt_ref, pk_ref, kb, sem.at[2 * si], 0, n_pages, False)
        _page_copies(pt_ref, pv_ref, vb, sem.at[2 * si + 1], 0, n_pages, False)

        def update(carry, s, pv_dot):
            m, l, acc = carry
            mn = jnp.maximum(m, jnp.max(s, axis=-1, keepdims=True))
            a = jnp.exp(m - mn)
            pe = jnp.exp(s - mn)
            l = a * l + jnp.sum(pe, axis=-1, keepdims=True)
            acc = a * acc + pv_dot(pe.astype(BF16))
            return mn, l, acc

        carry = (jnp.full((R, 1), NEG, F32), jnp.zeros((R, 1), F32), jnp.zeros((R, LANES), F32))
        oh_row = _iota((LANES, CK), 0)
        oh_blk = _iota((LANES, CK), 1) // SEL_BLOCK
        for hf in range(n_past_blk // SEL_HALF):
            for c in range(chunks_per_half):
                p0 = (hf * chunks_per_half + c) * CK_PAGES
                kt = jnp.concatenate([kb[p0 + j] for j in range(CK_PAGES)], axis=1).astype(BF16)
                onehot = jnp.where(oh_row == c * blk_per_chunk + oh_blk, 1.0, 0.0).astype(BF16)
                s = _dot(qaug[hf], jnp.concatenate([kt, onehot], axis=0))
                vt = jnp.concatenate([vb[p0 + j] for j in range(CK_PAGES)], axis=1).astype(BF16)
                carry = update(carry, s, lambda pe, vt=vt: _nt(pe, vt))
        col = _iota((R, LANES), 1)
        s = jnp.where(col <= t_row, _nt(qa, kt_ref[slot]), NEG)
        _, l, acc = update(carry, s, lambda pe: _dot(pe, vt_ref[slot]))
        o_sel = acc / l

        sa = jnp.where(_iota((R, wc), 1) > t_row, _dot(qa, ckw_ref[slot].astype(BF16)), NEG)
        sb = jnp.where(col <= t_row, _nt(qa, kwn_ref[slot]), NEG)
        m = jnp.maximum(jnp.max(sa, axis=-1, keepdims=True), jnp.max(sb, axis=-1, keepdims=True))
        ea, eb = jnp.exp(sa - m), jnp.exp(sb - m)
        den = jnp.sum(ea, axis=-1, keepdims=True) + jnp.sum(eb, axis=-1, keepdims=True)
        o_win = (_nt(ea.astype(BF16), cvw_ref[slot].astype(BF16)) + _dot(eb.astype(BF16), vwn_ref[slot])) / den

        o_ref[slot, 0] = o_cmp
        o_ref[slot, 1] = o_sel
        o_ref[slot, 2] = o_win

    @pl.when(s_id == 0)
    def _():
        start(0, 0)

    start(2 * s_id + 1, 1)
    finish(0)

    @pl.when(s_id + 1 < pl.num_programs(0))
    def _():
        start(2 * s_id + 2, 0)

    finish(1)


def _attn_sample(page_table, pool_k, pool_v, tn, qa, kct, vc, ovl, ktail, vtail, ckw, cvw, kwn, vwn):
    Bd, n_pages = page_table.shape
    R = N_HEADS * TOK_PAD
    wc = ckw.shape[2]
    const = lambda *shape: pl.BlockSpec(shape, lambda s, pt: (0,) * len(shape))
    pair = lambda *shape: pl.BlockSpec((2,) + shape, lambda s, pt: (s,) + (0,) * len(shape))
    anyspec = pl.BlockSpec(memory_space=pl.ANY)
    grid_spec = pltpu.PrefetchScalarGridSpec(
        num_scalar_prefetch=1,
        grid=(Bd // 2,),
        in_specs=[anyspec, anyspec, pair(R, LANES), pair(LANES, kct.shape[2]), pair(vc.shape[1], LANES),
                  const(*ovl.shape), pair(LANES, LANES), pair(LANES, LANES),
                  pair(LANES, wc), pair(LANES, wc), pair(LANES, LANES), pair(LANES, LANES)],
        out_specs=pair(3, R, LANES),
        scratch_shapes=[pltpu.VMEM((n_pages, LANES, PAGE_SIZE), F32)] * 4 + [pltpu.SemaphoreType.DMA((4,))],
    )
    return pl.pallas_call(
        functools.partial(_attn_sample_body, n_pages, tn),
        grid_spec=grid_spec,
        out_shape=jax.ShapeDtypeStruct((Bd, 3, R, LANES), F32),
        compiler_params=pltpu.CompilerParams(dimension_semantics=("arbitrary",), vmem_limit_bytes=VMEM_LIMIT),
        name="attn_sample",
    )(page_table, pool_k, pool_v, qa, kct, vc, ovl, ktail, vtail, ckw, cvw, kwn, vwn)


def _mixer_sample_body(o3_ref, g3_ref, sza_ref, sra_ref, gpb_ref, x_ref, wpa_ref, wo_ref, fg_ref, y_ref):
    o = g3_ref[0] * o3_ref[0] + g3_ref[1] * o3_ref[1] + g3_ref[2] * o3_ref[2]
    y_ref[...] = _mixer_tail(o, sza_ref[...], sra_ref[...], gpb_ref[...], x_ref[...], wpa_ref, wo_ref, fg_ref)


def _mixer_sample(o3, g3, sza, sra, gpb, x2d, prm):
    T = x2d.shape[0]
    full = lambda *shape: pl.BlockSpec(shape, lambda i: (0,) * len(shape))
    return pl.pallas_call(
        _mixer_sample_body,
        grid=(1,),
        in_specs=[full(3, T, ATTN_WIDTH), full(3, T, ATTN_WIDTH), full(T, ATTN_WIDTH), full(T, D_MODEL),
                  full(T, D_MODEL), full(T, D_MODEL), full(ATTN_WIDTH, D_MODEL), full(D_MODEL, D_MODEL),
                  full(1, D_MODEL)],
        out_specs=full(T, D_MODEL),
        out_shape=jax.ShapeDtypeStruct((T, D_MODEL), F32),
        compiler_params=pltpu.CompilerParams(dimension_semantics=("arbitrary",), vmem_limit_bytes=VMEM_LIMIT),
        name="mixer_sample",
    )(o3, g3, sza, sra, gpb, x2d, prm["w_pa"], prm["w_o"], prm["final_g"])


def _overlap(n_cmp, n_sel):
    cs = np.arange(n_cmp)[:, None] * CMP_STRIDE
    ss = np.arange(n_sel)[None, :] * SEL_BLOCK
    ov = np.minimum(cs + CMP_BLOCK, ss + SEL_BLOCK) - np.maximum(cs, ss)
    return np.clip(ov, 0, None).astype(np.float32) / CMP_BLOCK


def _cmp_weights(w1, b1, w2, b2):
    w1r = w1.reshape(2, CMP_STRIDE, HEAD_DIM, CMP_HIDDEN).transpose(1, 2, 0, 3)
    big = jnp.zeros((CMP_STRIDE, KV_HEADS, HEAD_DIM, KV_HEADS, 2, CMP_HIDDEN), F32)
    w2b = jnp.zeros((KV_HEADS, CMP_HIDDEN, KV_HEADS, HEAD_DIM), F32)
    for g in range(KV_HEADS):
        big = big.at[:, g, :, g].set(w1r)
        w2b = w2b.at[g, :, g].set(w2)
    return (big.reshape(CMP_STRIDE * LANES, 4 * LANES).astype(BF16), jnp.tile(b1, KV_HEADS)[None],
            w2b.reshape(KV_HEADS * CMP_HIDDEN, LANES).astype(BF16), jnp.tile(b2, KV_HEADS)[None])


def _seq_minor(t):
    lead = t.shape[:-3]
    n = len(lead)
    return t.transpose(*range(n), n + 1, n + 2, n).reshape(*lead, KV_WIDTH, t.shape[-3])


def _seq_major(t):
    lead = t.shape[:-2]
    n = len(lead)
    return t.reshape(*lead, KV_HEADS, HEAD_DIM, t.shape[-1]).transpose(*range(n), n + 2, n, n + 1)


def kernel(x_prompt, x_sample, cache_k_cmp, cache_v_cmp, cache_k_sel, cache_v_sel, cache_k_win, cache_v_win, page_table, norm_g, w_in, cmp_k_w1, cmp_k_b1, cmp_k_w2, cmp_k_b2, cmp_v_w1, cmp_v_b1, cmp_v_w2, cmp_v_b2, v_norm_g, v_norm_b, w_spatial, b_spatial, w_pa, w_pb, w_o, final_g):
    B, S, _ = x_prompt.shape
    Bd, tn, _ = x_sample.shape
    depth = w_in.shape[0]
    assert depth == 1, "single-layer step"
    assert Bd * tn == CHUNK, "the sample tokens form one 128-row tile"
    n_pages = page_table.shape[1]
    past = n_pages * PAGE_SIZE

    split = OFF_GL + 3 * N_HEADS
    w = w_in[0]
    prm = {
        "norm_g": norm_g,
        "w_in": jnp.concatenate([w[:, :split], jnp.zeros((D_MODEL, GL_PAD), F32), w[:, split:]], axis=1).astype(BF16),
        "w_sp": w_spatial[0],
        "b_sp": jnp.repeat(b_spatial[0].T, LANES, axis=1),
        "w_sp_s": jnp.stack([jnp.kron(jnp.eye(CHUNK // tn, dtype=F32), w_spatial[0, g, :tn, :tn])
                             for g in range(GMLP_GROUPS)]),
        "b_sp_s": jnp.tile(jnp.repeat(b_spatial[0, :, :tn].T, LANES, axis=1), (CHUNK // tn, 1)),
        "v_norm_g": v_norm_g, "v_norm_b": v_norm_b,
        "w_pb": w_pb[0].astype(BF16), "w_pa": w_pa[0].astype(BF16), "w_o": w_o[0].astype(BF16),
        "final_g": final_g[None],
        "cmp_k": _cmp_weights(cmp_k_w1[0], cmp_k_b1[0], cmp_k_w2[0], cmp_k_b2[0]),
        "cmp_v": _cmp_weights(cmp_v_w1[0], cmp_v_b1[0], cmp_v_w2[0], cmp_v_b2[0]),
    }

    xp = x_prompt.reshape(B * S, D_MODEL)
    (q_hm, p_kcmp, p_vcmp, p_ksel, p_vsel, p_kwin, p_vwin, kcr, vcr, kaug, kwt, vsb, vwb,
     gates, sza, sra, gpb) = _in_project(xp, _rope_tables(jnp.arange(S)), prm, TM_PROMPT, False, S)
    n_half = S // CMP_STRIDE
    n_cmp = (S - CMP_BLOCK) // CMP_STRIDE + 1
    n_sel = S // SEL_BLOCK
    assert n_sel <= SEL_HALF
    ovl = np.zeros((n_half, LANES), np.float32)
    ovl[:n_cmp, HEAD_DIM:HEAD_DIM + n_sel] = _overlap(n_cmp, n_sel)
    cend_tables = _rope_tables(jnp.arange(n_half) * CMP_STRIDE + CMP_BLOCK - 1)
    kct, vco = _compress_prompt(kcr.reshape(B, S, LANES), vcr.reshape(B, S, LANES), prm, cend_tables,
                                jnp.asarray(ovl, BF16))
    y_prompt = _attn_prompt(B, S, q_hm, kaug, vsb, kwt, vwb, kct, vco, gates, sza, sra, gpb, xp, prm)

    xs = x_sample.reshape(Bd * tn, D_MODEL)
    pos_s = jnp.tile(past + jnp.arange(tn), Bd)
    (q_s, s_kcmp, s_vcmp, s_ksel, s_vsel, s_kwin, s_vwin, gates_s, sza_s, sra_s, gpb_s, vn_s) = _in_project(
        xs, _rope_tables(pos_s), prm, Bd * tn, True, tn)
    n_half_s = past // CMP_STRIDE
    pools = [_seq_minor(c[0]) for c in (cache_k_cmp, cache_v_cmp, cache_k_sel, cache_v_sel)]
    cend_s = _rope_tables(jnp.arange(n_half_s) * CMP_STRIDE + CMP_BLOCK - 1)
    kct_s, vc_s = _compress_sample(page_table, pools[0], pools[1], prm, cend_s)

    n_cmp_s = (past + tn - CMP_BLOCK) // CMP_STRIDE + 1
    n_blk_s = past // SEL_BLOCK + -(-tn // SEL_BLOCK)
    assert n_cmp_s == n_half_s - 1 and tn <= TOK_PAD and (past // SEL_BLOCK) % SEL_HALF == 0
    ovl_s = np.zeros((n_half_s, 2 * LANES), np.float32)
    ovl_s[:n_cmp_s, :n_blk_s] = _overlap(n_cmp_s, n_blk_s)
    q5 = q_s.reshape(Bd, tn, KV_HEADS, HPG, HEAD_DIM).transpose(0, 2, 3, 1, 4)
    q5 = jnp.pad(q5, ((0, 0), (0, 0), (0, 0), (0, TOK_PAD - tn), (0, 0))).reshape(Bd, KV_HEADS, HPG * TOK_PAD, HEAD_DIM)
    qa = jnp.zeros((Bd, KV_HEADS, HPG * TOK_PAD, KV_HEADS, HEAD_DIM), F32)
    for g in range(KV_HEADS):
        qa = qa.at[:, g, :, g].set(q5[:, g])
    qa = qa.reshape(Bd, N_HEADS * TOK_PAD, LANES).astype(BF16)
    pad_rows = lambda t: jnp.pad(t.reshape(Bd, tn, LANES), ((0, 0), (0, LANES - tn), (0, 0))).astype(BF16)
    ckw = _seq_minor(cache_k_win[0])
    cvw = _seq_minor(cache_v_win[0])
    o3 = _attn_sample(page_table, pools[2], pools[3], tn, qa, kct_s, vc_s, jnp.asarray(ovl_s, BF16),
                      pad_rows(s_ksel), pad_rows(s_vsel), ckw, cvw, pad_rows(s_kwin), pad_rows(s_vwin))
    new_t = lambda t: t.reshape(Bd, tn, LANES).transpose(0, 2, 1)
    s_k_win = _seq_major(jnp.concatenate([ckw[:, :, tn:], new_t(s_kwin)], axis=2))[None]
    s_v_win = _seq_major(jnp.concatenate([cvw[:, :, tn:], new_t(s_vwin)], axis=2))[None]
    first_group = jnp.arange(N_HEADS * TOK_PAD)[:, None] < HPG * TOK_PAD
    o3r = jnp.where(first_group, o3[..., :HEAD_DIM], o3[..., HEAD_DIM:])
    o3r = o3r.reshape(Bd, 3, N_HEADS, TOK_PAD, HEAD_DIM)[:, :, :, :tn]
    o3r = o3r.transpose(1, 0, 3, 2, 4).reshape(3, Bd * tn, ATTN_WIDTH)
    g3 = gates_s[:, :3 * N_HEADS].reshape(Bd * tn, N_HEADS, 3).transpose(2, 0, 1)
    g3 = jnp.repeat(g3, HEAD_DIM, axis=2)
    y_sample = _mixer_sample(o3r, g3, sza_s, sra_s, gpb_s, xs, prm)

    kv5 = lambda t, b, n: t.reshape(1, b, n, KV_HEADS, HEAD_DIM)
    pw = min(WINDOW, S)
    return (y_prompt.reshape(B, S, D_MODEL), y_sample.reshape(Bd, tn, D_MODEL),
            _seq_major(p_kcmp)[None], _seq_major(p_vcmp)[None], _seq_major(p_ksel)[None], _seq_major(p_vsel)[None],
            _seq_major(p_kwin[:, :, S - pw:])[None], _seq_major(p_vwin[:, :, S - pw:])[None],
            kv5(s_kcmp, Bd, tn), kv5(s_vcmp, Bd, tn), kv5(s_ksel, Bd, tn), kv5(s_vsel, Bd, tn),
            s_k_win, s_v_win, vn_s.reshape(1, Bd, tn, GMLP_WIDTH))
```

```python
import functools

import numpy as np
import jax
import jax.numpy as jnp
from jax import lax
from jax.experimental import pallas as pl
from jax.experimental.pallas import tpu as pltpu

F32 = jnp.float32
BF16 = jnp.bfloat16

D_MODEL = 1024
HEAD_DIM = 64
N_HEADS = 8
KV_HEADS = 2
HPG = N_HEADS // KV_HEADS
ATTN_WIDTH = N_HEADS * HEAD_DIM
KV_WIDTH = KV_HEADS * HEAD_DIM
ROT_DIM = HEAD_DIM // 4
ROT_HALF = ROT_DIM // 2
ROPE_THETA = 500000.0
CMP_BLOCK = 32
CMP_STRIDE = 16
CMP_HIDDEN = 128
SEL_BLOCK = 64
TOP_N = 16
N_INIT_BLOCKS = 1
N_LOCAL_BLOCKS = 2
WINDOW = 512
CHUNK = 128
GMLP_GROUPS = 4
GMLP_WIDTH = 512
PAGE_SIZE = 128
NEG = -1e30
BIG = 1e30
EPS = 1e-6
LOG2E = 1.4426950408889634

LANES = 128
GL_PAD = LANES - 3 * N_HEADS

OFF_Q = 0
OFF_KV = ATTN_WIDTH
OFF_GL = OFF_KV + 6 * KV_WIDTH
OFF_ZA = OFF_GL + LANES
OFF_U = OFF_ZA + ATTN_WIDTH
OFF_V = OFF_U + GMLP_WIDTH
OFF_ZB = OFF_V + GMLP_WIDTH
OFF_RA = OFF_ZB + GMLP_WIDTH
OFF_RB = OFF_RA + D_MODEL
W_TOT = OFF_RB + D_MODEL

VMEM_LIMIT = 56 * 1024 * 1024

TM_PROMPT = 512
TQ = 128
KT = 512
WIN_KEYS = WINDOW + TQ
SEL_HALF = 64
TOK_PAD = 8
CK_PAGES = 4
RANK_GROUP = 8


def _nt(a, b):
    return lax.dot_general(a, b, (((1,), (1,)), ((), ())), preferred_element_type=F32)


def _dot(a, b):
    return jnp.dot(a, b, preferred_element_type=F32)


def _iota(shape, dim):
    return lax.broadcasted_iota(jnp.int32, shape, dim)


def _rope(slab, cos, s1, s2):
    return slab * cos + pltpu.roll(slab, LANES - ROT_HALF, 1) * s1 + pltpu.roll(slab, ROT_HALF, 1) * s2


def _rope_tables(pos):
    n = pos.shape[0]
    inv = jnp.power(jnp.float32(ROPE_THETA), -jnp.arange(0, ROT_DIM, 2, dtype=F32) / ROT_DIM)
    ang = pos.astype(F32)[:, None] * inv[None, :]
    cos, sin = jnp.cos(ang), jnp.sin(ang)
    rest = HEAD_DIM - ROT_DIM
    c = jnp.concatenate([cos, cos, jnp.ones((n, rest), F32)], axis=1)
    s1 = jnp.concatenate([-sin, jnp.zeros((n, HEAD_DIM - ROT_HALF), F32)], axis=1)
    s2 = jnp.concatenate([jnp.zeros((n, ROT_HALF), F32), sin, jnp.zeros((n, rest), F32)], axis=1)
    return tuple(jnp.tile(t, (1, LANES // HEAD_DIM)) for t in (c, s1, s2))


def _sigmoid(x):
    return 1.0 / (1.0 + jnp.exp(-x))


def _gelu(x):
    return jax.nn.gelu(x, approximate=True)


def _inproj_body(tm, sample, pos_tiles, x_ref, ng_ref, w_ref, cos_ref, s1_ref, s2_ref, wsp_ref, bsp_ref,
                 vng_ref, vnb_ref, wpb_ref, *outs):
    if sample:
        (q_ref, kcmp_ref, vcmp_ref, ksel_ref, vsel_ref, kwin_ref, vwin_ref,
         gat_ref, sza_ref, sra_ref, gpb_ref, vn_ref) = outs
    else:
        (q_ref, kcmp_ref, vcmp_ref, ksel_ref, vsel_ref, kwin_ref, vwin_ref,
         kcr_ref, vcr_ref, kaug_ref, kwt_ref, vsb_ref, vwb_ref, gat_ref, sza_ref, sra_ref, gpb_ref) = outs

    x = x_ref[...]
    r = lax.rsqrt(jnp.mean(x * x, axis=-1, keepdims=True) + EPS)
    h = ((x * r) * ng_ref[...]).astype(BF16)

    def proj(lo, hi):
        return _dot(h, w_ref[:, lo:hi])

    cos, s1, s2 = cos_ref[...], s1_ref[...], s2_ref[...]
    low = _iota((tm, LANES), 1) < HEAD_DIM

    q = proj(OFF_Q, OFF_Q + ATTN_WIDTH)
    q_scale = HEAD_DIM ** -0.5 if sample else HEAD_DIM ** -0.5 * LOG2E
    for pp in range(N_HEADS // 2):
        slab = _rope(q[:, pp * LANES:(pp + 1) * LANES], cos, s1, s2) * q_scale
        if sample:
            q_ref[:, pp * LANES:(pp + 1) * LANES] = slab
        elif (2 * pp) // HPG == 0:
            q_ref[2 * pp] = jnp.where(low, slab, 0.0).astype(BF16)
            q_ref[2 * pp + 1] = jnp.where(low, pltpu.roll(slab, HEAD_DIM, 1), 0.0).astype(BF16)
        else:
            q_ref[2 * pp] = jnp.where(low, 0.0, pltpu.roll(slab, HEAD_DIM, 1)).astype(BF16)
            q_ref[2 * pp + 1] = jnp.where(low, 0.0, slab).astype(BF16)

    kv = proj(OFF_KV, OFF_KV + 6 * KV_WIDTH)
    kcmp = kv[:, 0:LANES]
    vcmp = kv[:, LANES:2 * LANES]
    ksel = _rope(kv[:, 2 * LANES:3 * LANES], cos, s1, s2)
    vsel = kv[:, 3 * LANES:4 * LANES]
    kwin = _rope(kv[:, 4 * LANES:5 * LANES], cos, s1, s2)
    vwin = kv[:, 5 * LANES:6 * LANES]
    if sample:
        for ref, val in ((kcmp_ref, kcmp), (vcmp_ref, vcmp), (ksel_ref, ksel), (vsel_ref, vsel),
                         (kwin_ref, kwin), (vwin_ref, vwin)):
            ref[...] = val
    else:
        ksel_t = ksel.T
        kwin_t = kwin.T
        for ref, val in ((kcmp_ref, kcmp.T), (vcmp_ref, vcmp.T), (ksel_ref, ksel_t), (vsel_ref, vsel.T),
                         (kwin_ref, kwin_t), (vwin_ref, vwin.T)):
            ref[0] = val
        kcr_ref[...] = kcmp
        vcr_ref[...] = vcmp
        base = (pl.program_id(0) % pos_tiles) * tm
        blk = (base + _iota((LANES, tm), 1)) // SEL_BLOCK
        onehot = jnp.where(_iota((LANES, tm), 0) == blk, 1.0, 0.0)
        kaug_ref[0, 0:LANES] = ksel_t.astype(BF16)
        kaug_ref[0, LANES:2 * LANES] = onehot.astype(BF16)
        kwt_ref[0] = kwin_t.astype(BF16)
        for ref, val in ((vsb_ref, vsel), (vwb_ref, vwin)):
            ref[0] = jnp.where(low, val, 1.0).astype(BF16)
            ref[1] = jnp.where(low, 1.0, val).astype(BF16)

    gat_ref[...] = _sigmoid(proj(OFF_GL, OFF_GL + LANES))
    za = proj(OFF_ZA, OFF_ZA + ATTN_WIDTH)
    sza_ref[...] = za * _sigmoid(za)
    sra_ref[...] = _sigmoid(proj(OFF_RA, OFF_RA + D_MODEL))

    v = proj(OFF_V, OFF_V + GMLP_WIDTH)
    gv = _gelu(v)
    mu = jnp.mean(gv, axis=-1, keepdims=True)
    var = jnp.mean(jnp.square(gv - mu), axis=-1, keepdims=True)
    vn = ((gv - mu) * lax.rsqrt(var + EPS)) * vng_ref[...] + vnb_ref[...]
    if sample:
        vn_ref[...] = vn
    vnb16 = vn.astype(BF16)
    n_chunk = tm // CHUNK
    tri = _iota((CHUNK, CHUNK), 0) >= _iota((CHUNK, CHUNK), 1)
    mixed = []
    for g in range(GMLP_GROUPS):
        wm = jnp.where(tri, wsp_ref[g], 0.0).astype(BF16)
        cat = jnp.concatenate(
            [vnb16[c * CHUNK:(c + 1) * CHUNK, g * LANES:(g + 1) * LANES] for c in range(n_chunk)], axis=1)
        mixed.append(_dot(wm, cat))
    bsp = bsp_ref[...]
    sg = jnp.concatenate(
        [jnp.concatenate([mixed[g][:, c * LANES:(c + 1) * LANES] for g in range(GMLP_GROUPS)], axis=1) + bsp
         for c in range(n_chunk)], axis=0)
    u = proj(OFF_U, OFF_U + GMLP_WIDTH)
    zb = proj(OFF_ZB, OFF_ZB + GMLP_WIDTH)
    t = (_gelu(u) * sg) * (zb * _sigmoid(zb))
    pb = _dot(t.astype(BF16), wpb_ref[...])
    gpb_ref[...] = _sigmoid(proj(OFF_RB, OFF_RB + D_MODEL)) * pb


def _in_project(x2d, tables, prm, tm, sample, seq):
    T = x2d.shape[0]
    nt = T // tm
    pos_tiles = tables[0].shape[0] // tm
    const = lambda *shape: pl.BlockSpec(shape, lambda i: (0,) * len(shape))
    row = lambda width: pl.BlockSpec((tm, width), lambda i: (i, 0))
    tab = pl.BlockSpec((tm, LANES), lambda i: (i % pos_tiles, 0))
    in_specs = [
        row(D_MODEL), const(1, D_MODEL),
        pl.BlockSpec((D_MODEL, W_TOT), lambda i: (0, 0), pipeline_mode=pl.Buffered(1)),
        tab, tab, tab,
        const(GMLP_GROUPS, CHUNK, CHUNK), const(CHUNK, GMLP_WIDTH), const(1, GMLP_WIDTH), const(1, GMLP_WIDTH),
        const(GMLP_WIDTH, D_MODEL),
    ]
    f32rows = lambda width: jax.ShapeDtypeStruct((T, width), F32)
    tail_shapes = [f32rows(LANES), f32rows(ATTN_WIDTH), f32rows(D_MODEL), f32rows(D_MODEL)]
    tail_specs = [row(LANES), row(ATTN_WIDTH), row(D_MODEL), row(D_MODEL)]
    if sample:
        out_shape = [f32rows(ATTN_WIDTH)] + [f32rows(KV_WIDTH)] * 6 + tail_shapes + [f32rows(GMLP_WIDTH)]
        out_specs = [row(ATTN_WIDTH)] + [row(KV_WIDTH)] * 6 + tail_specs + [row(GMLP_WIDTH)]
    else:
        nb = T // seq
        bf = lambda *shape: jax.ShapeDtypeStruct(shape, BF16)
        tposed = lambda rows: pl.BlockSpec((1, rows, tm), lambda i: (i // pos_tiles, 0, i % pos_tiles))
        out_shape = ([bf(N_HEADS, T, LANES)] + [jax.ShapeDtypeStruct((nb, KV_WIDTH, seq), F32)] * 6
                     + [f32rows(KV_WIDTH)] * 2
                     + [bf(nb, 2 * LANES, seq), bf(nb, LANES, seq), bf(KV_HEADS, T, LANES), bf(KV_HEADS, T, LANES)]
                     + tail_shapes)
        heads = lambda n: pl.BlockSpec((n, tm, LANES), lambda i: (0, i, 0))
        out_specs = ([heads(N_HEADS)] + [tposed(KV_WIDTH)] * 6 + [row(KV_WIDTH)] * 2
                     + [tposed(2 * LANES), tposed(LANES), heads(KV_HEADS), heads(KV_HEADS)] + tail_specs)
    return pl.pallas_call(
        functools.partial(_inproj_body, tm, sample, pos_tiles),
        grid=(nt,),
        in_specs=in_specs,
        out_specs=out_specs,
        out_shape=out_shape,
        compiler_params=pltpu.CompilerParams(dimension_semantics=("arbitrary",), vmem_limit_bytes=VMEM_LIMIT),
        name="in_project_sample" if sample else "in_project_prompt",
    )(x2d, prm["norm_g"], prm["w_in"], *tables, prm["w_sp_s" if sample else "w_sp"],
      prm["b_sp_s" if sample else "b_sp"], prm["v_norm_g"], prm["v_norm_b"], prm["w_pb"])


def _compress_mlp(load_rows, n, w1_ref, b1_ref, w2_ref, b2_ref):
    xcat = jnp.concatenate([load_rows(t) for t in range(CMP_STRIDE)], axis=1).astype(BF16)
    hh = _dot(xcat, w1_ref[...])
    hid = jnp.concatenate(
        [hh[:, 0:LANES] + pltpu.roll(hh[:, LANES:2 * LANES], n - 1, 0),
         hh[:, 2 * LANES:3 * LANES] + pltpu.roll(hh[:, 3 * LANES:4 * LANES], n - 1, 0)], axis=1) + b1_ref[...]
    return _dot(_gelu(hid).astype(BF16), w2_ref[...]) + b2_ref[...]


def _compress_prompt_body(n, kr_ref, vr_ref, w1k, b1k, w2k, b2k, w1v, b1v, w2v, b2v, cos_ref, s1_ref, s2_ref,
                          ovl_ref, kct_ref, vco_ref):
    kc = _compress_mlp(lambda t: kr_ref[0, pl.ds(t, n, stride=CMP_STRIDE), :], n, w1k, b1k, w2k, b2k)
    kct_ref[0] = _rope(kc, cos_ref[...], s1_ref[...], s2_ref[...]).T.astype(BF16)
    vc = _compress_mlp(lambda t: vr_ref[0, pl.ds(t, n, stride=CMP_STRIDE), :], n, w1v, b1v, w2v, b2v)
    vco_ref[0, :, 0:LANES] = vc.astype(BF16)
    vco_ref[0, :, LANES:2 * LANES] = ovl_ref[...]


def _compress_prompt(kr, vr, prm, tables, ovl):
    B, S, _ = kr.shape
    n = S // CMP_STRIDE
    const = lambda *shape: pl.BlockSpec(shape, lambda b: (0,) * len(shape))
    rows = pl.BlockSpec((1, S, LANES), lambda b: (b, 0, 0))
    wspecs = [const(CMP_STRIDE * LANES, 4 * LANES), const(1, 2 * LANES), const(2 * LANES, LANES), const(1, LANES)]
    return pl.pallas_call(
        functools.partial(_compress_prompt_body, n),
        grid=(B,),
        in_specs=[rows, rows] + wspecs + wspecs + [const(n, LANES)] * 3 + [const(n, LANES)],
        out_specs=[pl.BlockSpec((1, LANES, n), lambda b: (b, 0, 0)),
                   pl.BlockSpec((1, n, 2 * LANES), lambda b: (b, 0, 0))],
        out_shape=[jax.ShapeDtypeStruct((B, LANES, n), BF16),
                   jax.ShapeDtypeStruct((B, n, 2 * LANES), BF16)],
        compiler_params=pltpu.CompilerParams(dimension_semantics=("arbitrary",), vmem_limit_bytes=VMEM_LIMIT),
        name="compress_prompt",
    )(kr, vr, *prm["cmp_k"], *prm["cmp_v"], *tables, ovl)


def _page_copies(pt_ref, pool_ref, buf, sem, b, n_pages, start):
    def body(p, carry):
        page = pt_ref[b, p] if start else 0
        cp = pltpu.make_async_copy(pool_ref.at[page], buf.at[p], sem)
        if start:
            cp.start()
        else:
            cp.wait()
        return carry
    lax.fori_loop(0, n_pages, body, 0)


def _compress_sample_body(n, n_pages, pt_ref, pk_ref, pv_ref, w1k, b1k, w2k, b2k, w1v, b1v, w2v, b2v,
                          cos_ref, s1_ref, s2_ref, kct_ref, vc_ref, kb0, kb1, vb0, vb1, rows, sem):
    s = pl.program_id(0)
    bufs = ((kb0, vb0, 0), (kb1, vb1, 1))

    def start(b, slot):
        kb, vb, si = bufs[slot]
        _page_copies(pt_ref, pk_ref, kb, sem.at[2 * si], b, n_pages, True)
        _page_copies(pt_ref, pv_ref, vb, sem.at[2 * si + 1], b, n_pages, True)

    def to_rows(buf):
        for p in range(n_pages):
            rows[p * PAGE_SIZE:(p + 1) * PAGE_SIZE, :] = buf[p].T

    def finish(slot):
        kb, vb, si = bufs[slot]
        _page_copies(pt_ref, pk_ref, kb, sem.at[2 * si], 0, n_pages, False)
        _page_copies(pt_ref, pv_ref, vb, sem.at[2 * si + 1], 0, n_pages, False)
        to_rows(kb)
        kc = _compress_mlp(lambda t: rows[pl.ds(t, n, stride=CMP_STRIDE), :], n, w1k, b1k, w2k, b2k)
        kct_ref[slot] = _rope(kc, cos_ref[...], s1_ref[...], s2_ref[...]).T.astype(BF16)
        to_rows(vb)
        vc = _compress_mlp(lambda t: rows[pl.ds(t, n, stride=CMP_STRIDE), :], n, w1v, b1v, w2v, b2v)
        vc_ref[slot] = vc.astype(BF16)

    @pl.when(s == 0)
    def _():
        start(0, 0)

    start(2 * s + 1, 1)
    finish(0)

    @pl.when(s + 1 < pl.num_programs(0))
    def _():
        start(2 * s + 2, 0)

    finish(1)


def _compress_sample(page_table, pool_k, pool_v, prm, tables):
    Bd, n_pages = page_table.shape
    past = n_pages * PAGE_SIZE
    n = past // CMP_STRIDE
    const = lambda *shape: pl.BlockSpec(shape, lambda s, pt: (0,) * len(shape))
    wspecs = [const(CMP_STRIDE * LANES, 4 * LANES), const(1, 2 * LANES), const(2 * LANES, LANES), const(1, LANES)]
    anyspec = pl.BlockSpec(memory_space=pl.ANY)
    grid_spec = pltpu.PrefetchScalarGridSpec(
        num_scalar_prefetch=1,
        grid=(Bd // 2,),
        in_specs=[anyspec, anyspec] + wspecs + wspecs + [const(n, LANES)] * 3,
        out_specs=[pl.BlockSpec((2, LANES, n), lambda s, pt: (s, 0, 0)),
                   pl.BlockSpec((2, n, LANES), lambda s, pt: (s, 0, 0))],
        scratch_shapes=[pltpu.VMEM((n_pages, LANES, PAGE_SIZE), F32)] * 4 + [pltpu.VMEM((past, LANES), F32)]
        + [pltpu.SemaphoreType.DMA((4,))],
    )
    return pl.pallas_call(
        functools.partial(_compress_sample_body, n, n_pages),
        grid_spec=grid_spec,
        out_shape=[jax.ShapeDtypeStruct((Bd, LANES, n), BF16), jax.ShapeDtypeStruct((Bd, n, LANES), BF16)],
        compiler_params=pltpu.CompilerParams(dimension_semantics=("arbitrary",), vmem_limit_bytes=VMEM_LIMIT),
        name="compress_sample",
    )(page_table, pool_k, pool_v, *prm["cmp_k"], *prm["cmp_v"], *tables)


def _mixer_tail(o, sza, sra, gpb, x, wpa_ref, wo_ref, fg_ref):
    pa = _dot((o * sza).astype(BF16), wpa_ref[...])
    merged = sra * pa + gpb
    hn = x + _dot(merged.astype(BF16), wo_ref[...])
    r = lax.rsqrt(jnp.mean(hn * hn, axis=-1, keepdims=True) + EPS)
    return (hn * r) * fg_ref[...]


def _topk_mask(score, blk, n_blocks, axis):
    cnt = jnp.zeros(score.shape, jnp.int32)
    for sp in range(n_blocks):
        b = lax.slice_in_dim(score, sp, sp + 1, axis=axis)
        ge = jnp.where(b >= score, 1, 0)
        gt = jnp.where(b > score, 1, 0)
        cnt = cnt + jnp.where(blk > sp, ge, gt)
    return cnt < TOP_N


def _attn_prompt_body(n_sel, q_ref, ka_ref, vs_ref, kw_ref, vw_ref, kc_ref, vco_ref, gat_ref, sza_ref, sra_ref,
                      gpb_ref, x_ref, wpa_ref, wo_ref, fg_ref, y_ref):
    i = pl.program_id(1)
    M = HPG * TQ
    G = range(KV_HEADS)
    qpos = i * TQ + (_iota((M, 1), 0) & (TQ - 1))
    n_cmp_pad = kc_ref.shape[2]
    gt = gat_ref[...]
    qs = [q_ref[HPG * g:HPG * (g + 1)].reshape(M, LANES) for g in G]

    mk = (_iota((M, n_cmp_pad), 1) * CMP_STRIDE + (CMP_BLOCK - 1)) <= qpos
    o_cmp, qa = [], []
    blk = _iota((n_sel, TQ), 0)
    cur = (i * TQ + _iota((n_sel, TQ), 1)) // SEL_BLOCK
    valid = blk <= cur
    forced = (valid & (blk > cur - N_LOCAL_BLOCKS)) | (blk < N_INIT_BLOCKS)
    scores = []
    for g in G:
        s = jnp.where(mk, _dot(qs[g], kc_ref[0]), NEG)
        e = jnp.exp2(s - jnp.max(s, axis=-1, keepdims=True))
        p = jnp.where(mk, e / jnp.sum(e, axis=-1, keepdims=True), 0.0)
        r = _dot(p.astype(BF16), vco_ref[0])
        o_cmp.append(r[:, 0:LANES])
        impc = r[:, LANES:2 * LANES]
        imp = impc[0:TQ] + impc[TQ:2 * TQ] + impc[2 * TQ:3 * TQ] + impc[3 * TQ:4 * TQ]
        sc = imp.T[HEAD_DIM:HEAD_DIM + n_sel]
        scores.append(jnp.where(forced, BIG, jnp.where(valid, sc, NEG)))

    last_blk = ((i + 1) * TQ - 1) // SEL_BLOCK

    def count_group(k, cnts):
        out = []
        for g in G:
            c = cnts[g]
            for sp in range(k * RANK_GROUP, (k + 1) * RANK_GROUP):
                b = scores[g][sp:sp + 1]
                c = c + jnp.where(blk > sp, jnp.where(b >= scores[g], 1, 0), jnp.where(b > scores[g], 1, 0))
            out.append(c)
        return tuple(out)

    cnts = (jnp.zeros((n_sel, TQ), jnp.int32),) * KV_HEADS
    for k in range(n_sel // RANK_GROUP):
        cnts = lax.cond(k * RANK_GROUP <= last_blk, functools.partial(count_group, k), lambda c: c, cnts)
    for g in G:
        selneg = jnp.where(cnts[g] < TOP_N, 0.0, NEG)
        seln_t = jnp.concatenate([selneg, jnp.zeros((LANES - n_sel, TQ), F32)], axis=0).T.astype(BF16)
        qa.append(jnp.concatenate([qs[g], jnp.concatenate([seln_t] * HPG, axis=0)], axis=1))

    def update(carry, s, v):
        m, acc = carry
        mn = jnp.maximum(m, jnp.max(s, axis=-1, keepdims=True))
        acc = jnp.exp2(m - mn) * acc + _dot(jnp.exp2(s - mn).astype(BF16), v)
        return mn, acc

    def step(kt, carries, causal):
        off = pl.multiple_of(kt * KT, KT)
        k = ka_ref[0, :, pl.ds(off, KT)]
        ss = [_dot(qa[g], k) for g in G]
        if causal:
            keep = kt * KT + _iota((M, KT), 1) <= qpos
            ss = [jnp.where(keep, s, NEG) for s in ss]
        return tuple(update(carries[g], ss[g], vs_ref[g, pl.ds(off, KT), :]) for g in G)

    n_full = i // (KT // TQ)
    init = (jnp.full((M, 1), NEG, F32), jnp.zeros((M, LANES), F32))
    carries = lax.fori_loop(0, n_full, lambda kt, c: step(kt, c, False), (init,) * KV_HEADS)
    carries = step(n_full, carries, True)
    sum_lane = [(KV_HEADS - 1 - g) * HEAD_DIM for g in G]
    o_sel = [carries[g][1] / carries[g][1][:, sum_lane[g]:sum_lane[g] + 1] for g in G]

    st = pl.multiple_of(jnp.maximum(i * TQ - WINDOW, 0), TQ)
    kpos = st + _iota((M, WIN_KEYS), 1)
    mkw = (kpos <= qpos) & (kpos > qpos - WINDOW)
    kw = kw_ref[0, :, pl.ds(st, WIN_KEYS)]
    o_groups = []
    for g in G:
        s = jnp.where(mkw, _dot(qs[g], kw), NEG)
        e = jnp.exp2(s - jnp.max(s, axis=-1, keepdims=True))
        r = _dot(e.astype(BF16), vw_ref[g, pl.ds(st, WIN_KEYS), :])
        o_win = r / r[:, sum_lane[g]:sum_lane[g] + 1]

        def gcol(c, g=g):
            return jnp.concatenate(
                [gt[:, 3 * (HPG * g + j) + c:3 * (HPG * g + j) + c + 1] for j in range(HPG)], axis=0)

        o_groups.append(gcol(0) * o_cmp[g] + gcol(1) * o_sel[g] + gcol(2) * o_win)

    low = _iota((TQ, LANES), 1) < HEAD_DIM
    slabs = []
    for pp in range(N_HEADS // 2):
        g, j0 = pp // 2, (2 * pp) % HPG
        a = o_groups[g][j0 * TQ:(j0 + 1) * TQ]
        b = o_groups[g][(j0 + 1) * TQ:(j0 + 2) * TQ]
        if g == 0:
            b = pltpu.roll(b, HEAD_DIM, 1)
        else:
            a = pltpu.roll(a, HEAD_DIM, 1)
        slabs.append(jnp.where(low, a, b))
    o = jnp.concatenate(slabs, axis=1)
    y_ref[...] = _mixer_tail(o, sza_ref[...], sra_ref[...], gpb_ref[...], x_ref[...], wpa_ref, wo_ref, fg_ref)


def _attn_prompt(B, S, q_hm, kaug, vsb, kwt, vwb, kct, vco, gates, sza, sra, gpb, x2d, prm):
    nq = S // TQ
    n_sel = S // SEL_BLOCK
    n_cmp_pad = kct.shape[2]
    row = lambda width: pl.BlockSpec((TQ, width), lambda b, i: (b * nq + i, 0))
    const = lambda *shape: pl.BlockSpec(shape, lambda b, i: (0,) * len(shape))
    batch = lambda *shape: pl.BlockSpec((1,) + shape, lambda b, i: (b,) + (0,) * len(shape))
    in_specs = [
        pl.BlockSpec((N_HEADS, TQ, LANES), lambda b, i: (0, b * nq + i, 0)),
        batch(2 * LANES, S),
        pl.BlockSpec((KV_HEADS, S, LANES), lambda b, i: (0, b, 0)),
        batch(LANES, S),
        pl.BlockSpec((KV_HEADS, S, LANES), lambda b, i: (0, b, 0)),
        batch(LANES, n_cmp_pad),
        batch(n_cmp_pad, 2 * LANES),
        row(LANES), row(ATTN_WIDTH), row(D_MODEL), row(D_MODEL), row(D_MODEL),
        const(ATTN_WIDTH, D_MODEL), const(D_MODEL, D_MODEL), const(1, D_MODEL),
    ]
    return pl.pallas_call(
        functools.partial(_attn_prompt_body, n_sel),
        grid=(B, nq),
        in_specs=in_specs,
        out_specs=row(D_MODEL),
        out_shape=jax.ShapeDtypeStruct((B * S, D_MODEL), F32),
        compiler_params=pltpu.CompilerParams(dimension_semantics=("arbitrary", "arbitrary"),
                                             vmem_limit_bytes=VMEM_LIMIT),
        name="attn_prompt",
    )(q_hm, kaug, vsb, kwt, vwb, kct, vco, gates, sza, sra, gpb, x2d, prm["w_pa"], prm["w_o"], prm["final_g"])


def _attn_sample_body(n_pages, tn, pt_ref, pk_ref, pv_ref, qa_ref, kc_ref, vc_ref, ovl_ref, kt_ref, vt_ref,
                      ckw_ref, cvw_ref, kwn_ref, vwn_ref, o_ref, kb0, kb1, vb0, vb1, sem):
    s_id = pl.program_id(0)
    past = n_pages * PAGE_SIZE
    n_past_blk = past // SEL_BLOCK
    n_blk_pad = ovl_ref.shape[1]
    R = N_HEADS * TOK_PAD
    GR = HPG * TOK_PAD
    CK = CK_PAGES * PAGE_SIZE
    blk_per_chunk = CK // SEL_BLOCK
    chunks_per_half = SEL_HALF // blk_per_chunk
    wc = ckw_ref.shape[2]
    bufs = ((kb0, vb0, 0), (kb1, vb1, 1))
    t_row = _iota((R, 1), 0) & (TOK_PAD - 1)

    def start(b, slot):
        kb, vb, si = bufs[slot]
        _page_copies(pt_ref, pk_ref, kb, sem.at[2 * si], b, n_pages, True)
        _page_copies(pt_ref, pv_ref, vb, sem.at[2 * si + 1], b, n_pages, True)

    def finish(slot):
        kb, vb, si = bufs[slot]
        qa = qa_ref[slot]

        n_cmp_pad = kc_ref.shape[2]
        s = _dot(qa, kc_ref[slot])
        mk = _iota((R, n_cmp_pad), 1) < n_cmp_pad - 1
        s = jnp.where(mk, s, NEG)
        e = jnp.exp(s - jnp.max(s, axis=-1, keepdims=True))
        p = jnp.where(mk, e / jnp.sum(e, axis=-1, keepdims=True), 0.0).astype(BF16)
        o_cmp = _dot(p, vc_ref[slot])
        impc = _dot(p, ovl_ref[...])
        imp = jnp.concatenate(
            [impc[g * GR:g * GR + TOK_PAD] + impc[g * GR + TOK_PAD:g * GR + 2 * TOK_PAD]
             + impc[g * GR + 2 * TOK_PAD:g * GR + 3 * TOK_PAD] + impc[g * GR + 3 * TOK_PAD:g * GR + 4 * TOK_PAD]
             for g in range(KV_HEADS)], axis=0)
        nr = KV_HEADS * TOK_PAD
        blk = _iota((nr, n_blk_pad), 1)
        cur = (past + jnp.minimum(_iota((nr, n_blk_pad), 0) & (TOK_PAD - 1), tn - 1)) // SEL_BLOCK
        valid = blk <= cur
        forced = (valid & (blk > cur - N_LOCAL_BLOCKS)) | (blk < N_INIT_BLOCKS)
        score = jnp.where(forced, BIG, jnp.where(valid, imp, NEG))
        sel = _topk_mask(score, blk, n_past_blk + 1, 1)
        selneg = jnp.where(sel, 0.0, NEG)
        selneg = jnp.concatenate(
            [selneg[g * TOK_PAD:(g + 1) * TOK_PAD] for g in range(KV_HEADS) for _ in range(HPG)], axis=0)
        low = _iota((R, LANES), 1) < SEL_HALF
        qaug = []
        for hf in range(n_past_blk // SEL_HALF):
            slab = selneg[:, (hf // 2) * LANES:(hf // 2 + 1) * LANES]
            if hf % 2:
                slab = pltpu.roll(slab, SEL_HALF, 1)
            qaug.append(jnp.concatenate([qa, jnp.where(low, slab, 0.0).astype(BF16)], axis=1))

        _page_copies(pt_ref, pk_ref, kb, sem.at[2 * si], 0, n_pages, False)
        _page_copies(pt_ref, pv_ref, vb, sem.at[2 * si + 1], 0, n_pages, False)

        def update(carry, s, pv_dot):
            m, l, acc = carry
            mn = jnp.maximum(m, jnp.max(s, axis=-1, keepdims=True))
            a = jnp.exp(m - mn)
            pe = jnp.exp(s - mn)
            l = a * l + jnp.sum(pe, axis=-1, keepdims=True)
            acc = a * acc + pv_dot(pe.astype(BF16))
            return mn, l, acc

        carry = (jnp.full((R, 1), NEG, F32), jnp.zeros((R, 1), F32), jnp.zeros((R, LANES), F32))
        oh_row = _iota((LANES, CK), 0)
        oh_blk = _iota((LANES, CK), 1) // SEL_BLOCK
        for hf in range(n_past_blk // SEL_HALF):
            for c in range(chunks_per_half):
                p0 = (hf * chunks_per_half + c) * CK_PAGES
                kt = jnp.concatenate([kb[p0 + j] for j in range(CK_PAGES)], axis=1).astype(BF16)
                onehot = jnp.where(oh_row == c * blk_per_chunk + oh_blk, 1.0, 0.0).astype(BF16)
                s = _dot(qaug[hf], jnp.concatenate([kt, onehot], axis=0))
                vt = jnp.concatenate([vb[p0 + j] for j in range(CK_PAGES)], axis=1).astype(BF16)
                carry = update(carry, s, lambda pe, vt=vt: _nt(pe, vt))
        col = _iota((R, LANES), 1)
        s = jnp.where(col <= t_row, _nt(qa, kt_ref[slot]), NEG)
        _, l, acc = update(carry, s, lambda pe: _dot(pe, vt_ref[slot]))
        o_sel = acc / l

        sa = jnp.where(_iota((R, wc), 1) > t_row, _dot(qa, ckw_ref[slot].astype(BF16)), NEG)
        sb = jnp.where(col <= t_row, _nt(qa, kwn_ref[slot]), NEG)
        m = jnp.maximum(jnp.max(sa, axis=-1, keepdims=True), jnp.max(sb, axis=-1, keepdims=True))
        ea, eb = jnp.exp(sa - m), jnp.exp(sb - m)
        den = jnp.sum(ea, axis=-1, keepdims=True) + jnp.sum(eb, axis=-1, keepdims=True)
        o_win = (_nt(ea.astype(BF16), cvw_ref[slot].astype(BF16)) + _dot(eb.astype(BF16), vwn_ref[slot])) / den

        o_ref[slot, 0] = o_cmp
        o_ref[slot, 1] = o_sel
        o_ref[slot, 2] = o_win

    @pl.when(s_id == 0)
    def _():
        start(0, 0)

    start(2 * s_id + 1, 1)
    finish(0)

    @pl.when(s_id + 1 < pl.num_programs(0))
    def _():
        start(2 * s_id + 2, 0)

    finish(1)


def _attn_sample(page_table, pool_k, pool_v, tn, qa, kct, vc, ovl, ktail, vtail, ckw, cvw, kwn, vwn):
    Bd, n_pages = page_table.shape
    R = N_HEADS * TOK_PAD
    wc = ckw.shape[2]
    const = lambda *shape: pl.BlockSpec(shape, lambda s, pt: (0,) * len(shape))
    pair = lambda *shape: pl.BlockSpec((2,) + shape, lambda s, pt: (s,) + (0,) * len(shape))
    anyspec = pl.BlockSpec(memory_space=pl.ANY)
    grid_spec = pltpu.PrefetchScalarGridSpec(
        num_scalar_prefetch=1,
        grid=(Bd // 2,),
        in_specs=[anyspec, anyspec, pair(R, LANES), pair(LANES, kct.shape[2]), pair(vc.shape[1], LANES),
                  const(*ovl.shape), pair(LANES, LANES), pair(LANES, LANES),
                  pair(LANES, wc), pair(LANES, wc), pair(LANES, LANES), pair(LANES, LANES)],
        out_specs=pair(3, R, LANES),
        scratch_shapes=[pltpu.VMEM((n_pages, LANES, PAGE_SIZE), F32)] * 4 + [pltpu.SemaphoreType.DMA((4,))],
    )
    return pl.pallas_call(
        functools.partial(_attn_sample_body, n_pages, tn),
        grid_spec=grid_spec,
        out_shape=jax.ShapeDtypeStruct((Bd, 3, R, LANES), F32),
        compiler_params=pltpu.CompilerParams(dimension_semantics=("arbitrary",), vmem_limit_bytes=VMEM_LIMIT),
        name="attn_sample",
    )(page_table, pool_k, pool_v, qa, kct, vc, ovl, ktail, vtail, ckw, cvw, kwn, vwn)


def _mixer_sample_body(o3_ref, g3_ref, sza_ref, sra_ref, gpb_ref, x_ref, wpa_ref, wo_ref, fg_ref, y_ref):
    o = g3_ref[0] * o3_ref[0] + g3_ref[1] * o3_ref[1] + g3_ref[2] * o3_ref[2]
    y_ref[...] = _mixer_tail(o, sza_ref[...], sra_ref[...], gpb_ref[...], x_ref[...], wpa_ref, wo_ref, fg_ref)


def _mixer_sample(o3, g3, sza, sra, gpb, x2d, prm):
    T = x2d.shape[0]
    full = lambda *shape: pl.BlockSpec(shape, lambda i: (0,) * len(shape))
    return pl.pallas_call(
        _mixer_sample_body,
        grid=(1,),
        in_specs=[full(3, T, ATTN_WIDTH), full(3, T, ATTN_WIDTH), full(T, ATTN_WIDTH), full(T, D_MODEL),
                  full(T, D_MODEL), full(T, D_MODEL), full(ATTN_WIDTH, D_MODEL), full(D_MODEL, D_MODEL),
                  full(1, D_MODEL)],
        out_specs=full(T, D_MODEL),
        out_shape=jax.ShapeDtypeStruct((T, D_MODEL), F32),
        compiler_params=pltpu.CompilerParams(dimension_semantics=("arbitrary",), vmem_limit_bytes=VMEM_LIMIT),
        name="mixer_sample",
    )(o3, g3, sza, sra, gpb, x2d, prm["w_pa"], prm["w_o"], prm["final_g"])


def _overlap(n_cmp, n_sel):
    cs = np.arange(n_cmp)[:, None] * CMP_STRIDE
    ss = np.arange(n_sel)[None, :] * SEL_BLOCK
    ov = np.minimum(cs + CMP_BLOCK, ss + SEL_BLOCK) - np.maximum(cs, ss)
    return np.clip(ov, 0, None).astype(np.float32) / CMP_BLOCK


def _cmp_weights(w1, b1, w2, b2):
    w1r = w1.reshape(2, CMP_STRIDE, HEAD_DIM, CMP_HIDDEN).transpose(1, 2, 0, 3)
    big = jnp.zeros((CMP_STRIDE, KV_HEADS, HEAD_DIM, KV_HEADS, 2, CMP_HIDDEN), F32)
    w2b = jnp.zeros((KV_HEADS, CMP_HIDDEN, KV_HEADS, HEAD_DIM), F32)
    for g in range(KV_HEADS):
        big = big.at[:, g, :, g].set(w1r)
        w2b = w2b.at[g, :, g].set(w2)
    return (big.reshape(CMP_STRIDE * LANES, 4 * LANES).astype(BF16), jnp.tile(b1, KV_HEADS)[None],
            w2b.reshape(KV_HEADS * CMP_HIDDEN, LANES).astype(BF16), jnp.tile(b2, KV_HEADS)[None])


def _seq_minor(t):
    lead = t.shape[:-3]
    n = len(lead)
    return t.transpose(*range(n), n + 1, n + 2, n).reshape(*lead, KV_WIDTH, t.shape[-3])


def _seq_major(t):
    lead = t.shape[:-2]
    n = len(lead)
    return t.reshape(*lead, KV_HEADS, HEAD_DIM, t.shape[-1]).transpose(*range(n), n + 2, n, n + 1)


def kernel(x_prompt, x_sample, cache_k_cmp, cache_v_cmp, cache_k_sel, cache_v_sel, cache_k_win, cache_v_win, page_table, norm_g, w_in, cmp_k_w1, cmp_k_b1, cmp_k_w2, cmp_k_b2, cmp_v_w1, cmp_v_b1, cmp_v_w2, cmp_v_b2, v_norm_g, v_norm_b, w_spatial, b_spatial, w_pa, w_pb, w_o, final_g):
    B, S, _ = x_prompt.shape
    Bd, tn, _ = x_sample.shape
    depth = w_in.shape[0]
    assert depth == 1, "single-layer step"
    assert Bd * tn == CHUNK, "the sample tokens form one 128-row tile"
    n_pages = page_table.shape[1]
    past = n_pages * PAGE_SIZE

    split = OFF_GL + 3 * N_HEADS
    w = w_in[0]
    prm = {
        "norm_g": norm_g,
        "w_in": jnp.concatenate([w[:, :split], jnp.zeros((D_MODEL, GL_PAD), F32), w[:, split:]], axis=1).astype(BF16),
        "w_sp": w_spatial[0],
        "b_sp": jnp.repeat(b_spatial[0].T, LANES, axis=1),
        "w_sp_s": jnp.stack([jnp.kron(jnp.eye(CHUNK // tn, dtype=F32), w_spatial[0, g, :tn, :tn])
                             for g in range(GMLP_GROUPS)]),
        "b_sp_s": jnp.tile(jnp.repeat(b_spatial[0, :, :tn].T, LANES, axis=1), (CHUNK // tn, 1)),
        "v_norm_g": v_norm_g, "v_norm_b": v_norm_b,
        "w_pb": w_pb[0].astype(BF16), "w_pa": w_pa[0].astype(BF16), "w_o": w_o[0].astype(BF16),
        "final_g": final_g[None],
        "cmp_k": _cmp_weights(cmp_k_w1[0], cmp_k_b1[0], cmp_k_w2[0], cmp_k_b2[0]),
        "cmp_v": _cmp_weights(cmp_v_w1[0], cmp_v_b1[0], cmp_v_w2[0], cmp_v_b2[0]),
    }

    xp = x_prompt.reshape(B * S, D_MODEL)
    (q_hm, p_kcmp, p_vcmp, p_ksel, p_vsel, p_kwin, p_vwin, kcr, vcr, kaug, kwt, vsb, vwb,
     gates, sza, sra, gpb) = _in_project(xp, _rope_tables(jnp.arange(S)), prm, TM_PROMPT, False, S)
    n_half = S // CMP_STRIDE
    n_cmp = (S - CMP_BLOCK) // CMP_STRIDE + 1
    n_sel = S // SEL_BLOCK
    assert n_sel <= SEL_HALF
    ovl = np.zeros((n_half, LANES), np.float32)
    ovl[:n_cmp, HEAD_DIM:HEAD_DIM + n_sel] = _overlap(n_cmp, n_sel)
    cend_tables = _rope_tables(jnp.arange(n_half) * CMP_STRIDE + CMP_BLOCK - 1)
    kct, vco = _compress_prompt(kcr.reshape(B, S, LANES), vcr.reshape(B, S, LANES), prm, cend_tables,
                                jnp.asarray(ovl, BF16))
    y_prompt = _attn_prompt(B, S, q_hm, kaug, vsb, kwt, vwb, kct, vco, gates, sza, sra, gpb, xp, prm)

    xs = x_sample.reshape(Bd * tn, D_MODEL)
    pos_s = jnp.tile(past + jnp.arange(tn), Bd)
    (q_s, s_kcmp, s_vcmp, s_ksel, s_vsel, s_kwin, s_vwin, gates_s, sza_s, sra_s, gpb_s, vn_s) = _in_project(
        xs, _rope_tables(pos_s), prm, Bd * tn, True, tn)
    n_half_s = past // CMP_STRIDE
    pools = [_seq_minor(c[0]) for c in (cache_k_cmp, cache_v_cmp, cache_k_sel, cache_v_sel)]
    cend_s = _rope_tables(jnp.arange(n_half_s) * CMP_STRIDE + CMP_BLOCK - 1)
    kct_s, vc_s = _compress_sample(page_table, pools[0], pools[1], prm, cend_s)

    n_cmp_s = (past + tn - CMP_BLOCK) // CMP_STRIDE + 1
    n_blk_s = past // SEL_BLOCK + -(-tn // SEL_BLOCK)
    assert n_cmp_s == n_half_s - 1 and tn <= TOK_PAD and (past // SEL_BLOCK) % SEL_HALF == 0
    ovl_s = np.zeros((n_half_s, 2 * LANES), np.float32)
    ovl_s[:n_cmp_s, :n_blk_s] = _overlap(n_cmp_s, n_blk_s)
    q5 = q_s.reshape(Bd, tn, KV_HEADS, HPG, HEAD_DIM).transpose(0, 2, 3, 1, 4)
    q5 = jnp.pad(q5, ((0, 0), (0, 0), (0, 0), (0, TOK_PAD - tn), (0, 0))).reshape(Bd, KV_HEADS, HPG * TOK_PAD, HEAD_DIM)
    qa = jnp.zeros((Bd, KV_HEADS, HPG * TOK_PAD, KV_HEADS, HEAD_DIM), F32)
    for g in range(KV_HEADS):
        qa = qa.at[:, g, :, g].set(q5[:, g])
    qa = qa.reshape(Bd, N_HEADS * TOK_PAD, LANES).astype(BF16)
    pad_rows = lambda t: jnp.pad(t.reshape(Bd, tn, LANES), ((0, 0), (0, LANES - tn), (0, 0))).astype(BF16)
    ckw = _seq_minor(cache_k_win[0])
    cvw = _seq_minor(cache_v_win[0])
    o3 = _attn_sample(page_table, pools[2], pools[3], tn, qa, kct_s, vc_s, jnp.asarray(ovl_s, BF16),
                      pad_rows(s_ksel), pad_rows(s_vsel), ckw, cvw, pad_rows(s_kwin), pad_rows(s_vwin))
    new_t = lambda t: t.reshape(Bd, tn, LANES).transpose(0, 2, 1)
    s_k_win = _seq_major(jnp.concatenate([ckw[:, :, tn:], new_t(s_kwin)], axis=2))[None]
    s_v_win = _seq_major(jnp.concatenate([cvw[:, :, tn:], new_t(s_vwin)], axis=2))[None]
    first_group = jnp.arange(N_HEADS * TOK_PAD)[:, None] < HPG * TOK_PAD
    o3r = jnp.where(first_group, o3[..., :HEAD_DIM], o3[..., HEAD_DIM:])
    o3r = o3r.reshape(Bd, 3, N_HEADS, TOK_PAD, HEAD_DIM)[:, :, :, :tn]
    o3r = o3r.transpose(1, 0, 3, 2, 4).reshape(3, Bd * tn, ATTN_WIDTH)
    g3 = gates_s[:, :3 * N_HEADS].reshape(Bd * tn, N_HEADS, 3).transpose(2, 0, 1)
    g3 = jnp.repeat(g3, HEAD_DIM, axis=2)
    y_sample = _mixer_sample(o3r, g3, sza_s, sra_s, gpb_s, xs, prm)

    kv5 = lambda t, b, n: t.reshape(1, b, n, KV_HEADS, HEAD_DIM)
    pw = min(WINDOW, S)
    return (y_prompt.reshape(B, S, D_MODEL), y_sample.reshape(Bd, tn, D_MODEL),
            _seq_major(p_kcmp)[None], _seq_major(p_vcmp)[None], _seq_major(p_ksel)[None], _seq_major(p_vsel)[None],
            _seq_major(p_kwin[:, :, S - pw:])[None], _seq_major(p_vwin[:, :, S - pw:])[None],
            kv5(s_kcmp, Bd, tn), kv5(s_vcmp, Bd, tn), kv5(s_ksel, Bd, tn), kv5(s_vsel, Bd, tn),
            s_k_win, s_v_win, vn_s.reshape(1, Bd, tn, GMLP_WIDTH))
```

```python
import functools

import numpy as np
import jax
import jax.numpy as jnp
from jax import lax
from jax.experimental import pallas as pl
from jax.experimental.pallas import tpu as pltpu

F32 = jnp.float32
BF16 = jnp.bfloat16

D_MODEL = 1024
HEAD_DIM = 64
N_HEADS = 8
KV_HEADS = 2
HPG = N_HEADS // KV_HEADS
ATTN_WIDTH = N_HEADS * HEAD_DIM
KV_WIDTH = KV_HEADS * HEAD_DIM
ROT_DIM = HEAD_DIM // 4
ROT_HALF = ROT_DIM // 2
ROPE_THETA = 500000.0
CMP_BLOCK = 32
CMP_STRIDE = 16
CMP_HIDDEN = 128
SEL_BLOCK = 64
TOP_N = 16
N_INIT_BLOCKS = 1
N_LOCAL_BLOCKS = 2
WINDOW = 512
CHUNK = 128
GMLP_GROUPS = 4
GMLP_WIDTH = 512
PAGE_SIZE = 128
NEG = -1e30
BIG = 1e30
EPS = 1e-6
LOG2E = 1.4426950408889634

LANES = 128
GL_PAD = LANES - 3 * N_HEADS

OFF_Q = 0
OFF_KV = ATTN_WIDTH
OFF_GL = OFF_KV + 6 * KV_WIDTH
OFF_ZA = OFF_GL + LANES
OFF_U = OFF_ZA + ATTN_WIDTH
OFF_V = OFF_U + GMLP_WIDTH
OFF_ZB = OFF_V + GMLP_WIDTH
OFF_RA = OFF_ZB + GMLP_WIDTH
OFF_RB = OFF_RA + D_MODEL
W_TOT = OFF_RB + D_MODEL

VMEM_LIMIT = 56 * 1024 * 1024

TM_PROMPT = 512
TQ = 128
KT = 512
WIN_KEYS = WINDOW + TQ
SEL_HALF = 64
TOK_PAD = 8
RANK_GROUP = 8


def _nt(a, b):
    return lax.dot_general(a, b, (((1,), (1,)), ((), ())), preferred_element_type=F32)


def _dot(a, b):
    return jnp.dot(a, b, preferred_element_type=F32)


def _iota(shape, dim):
    return lax.broadcasted_iota(jnp.int32, shape, dim)


def _rope(slab, cos, s1, s2):
    return slab * cos + pltpu.roll(slab, LANES - ROT_HALF, 1) * s1 + pltpu.roll(slab, ROT_HALF, 1) * s2


def _rope_tables(pos):
    n = pos.shape[0]
    inv = jnp.power(jnp.float32(ROPE_THETA), -jnp.arange(0, ROT_DIM, 2, dtype=F32) / ROT_DIM)
    ang = pos.astype(F32)[:, None] * inv[None, :]
    cos, sin = jnp.cos(ang), jnp.sin(ang)
    rest = HEAD_DIM - ROT_DIM
    c = jnp.concatenate([cos, cos, jnp.ones((n, rest), F32)], axis=1)
    s1 = jnp.concatenate([-sin, jnp.zeros((n, HEAD_DIM - ROT_HALF), F32)], axis=1)
    s2 = jnp.concatenate([jnp.zeros((n, ROT_HALF), F32), sin, jnp.zeros((n, rest), F32)], axis=1)
    return tuple(jnp.tile(t, (1, LANES // HEAD_DIM)) for t in (c, s1, s2))


def _sigmoid(x):
    return 1.0 / (1.0 + jnp.exp(-x))


def _gelu(x):
    return jax.nn.gelu(x, approximate=True)


def _inproj_body(tm, sample, pos_tiles, x_ref, ng_ref, w_ref, cos_ref, s1_ref, s2_ref, wsp_ref, bsp_ref,
                 vng_ref, vnb_ref, wpb_ref, *outs):
    if sample:
        (q_ref, kcmp_ref, vcmp_ref, ksel_ref, vsel_ref, kwin_ref, vwin_ref,
         gat_ref, sza_ref, sra_ref, gpb_ref, vn_ref) = outs
    else:
        (q_ref, kcmp_ref, vcmp_ref, ksel_ref, vsel_ref, kwin_ref, vwin_ref,
         kcr_ref, vcr_ref, kaug_ref, kwt_ref, vsb_ref, vwb_ref, gat_ref, sza_ref, sra_ref, gpb_ref) = outs

    x = x_ref[...]
    r = lax.rsqrt(jnp.mean(x * x, axis=-1, keepdims=True) + EPS)
    h = ((x * r) * ng_ref[...]).astype(BF16)

    def proj(lo, hi):
        return _dot(h, w_ref[:, lo:hi])

    cos, s1, s2 = cos_ref[...], s1_ref[...], s2_ref[...]
    low = _iota((tm, LANES), 1) < HEAD_DIM

    q = proj(OFF_Q, OFF_Q + ATTN_WIDTH)
    q_scale = HEAD_DIM ** -0.5 if sample else HEAD_DIM ** -0.5 * LOG2E
    for pp in range(N_HEADS // 2):
        slab = _rope(q[:, pp * LANES:(pp + 1) * LANES], cos, s1, s2) * q_scale
        if sample:
            q_ref[:, pp * LANES:(pp + 1) * LANES] = slab
        elif (2 * pp) // HPG == 0:
            q_ref[2 * pp] = jnp.where(low, slab, 0.0).astype(BF16)
            q_ref[2 * pp + 1] = jnp.where(low, pltpu.roll(slab, HEAD_DIM, 1), 0.0).astype(BF16)
        else:
            q_ref[2 * pp] = jnp.where(low, 0.0, pltpu.roll(slab, HEAD_DIM, 1)).astype(BF16)
            q_ref[2 * pp + 1] = jnp.where(low, 0.0, slab).astype(BF16)

    kv = proj(OFF_KV, OFF_KV + 6 * KV_WIDTH)
    kcmp = kv[:, 0:LANES]
    vcmp = kv[:, LANES:2 * LANES]
    ksel = _rope(kv[:, 2 * LANES:3 * LANES], cos, s1, s2)
    vsel = kv[:, 3 * LANES:4 * LANES]
    kwin = _rope(kv[:, 4 * LANES:5 * LANES], cos, s1, s2)
    vwin = kv[:, 5 * LANES:6 * LANES]
    if sample:
        for ref, val in ((kcmp_ref, kcmp), (vcmp_ref, vcmp), (ksel_ref, ksel), (vsel_ref, vsel),
                         (kwin_ref, kwin), (vwin_ref, vwin)):
            ref[...] = val
    else:
        ksel_t = ksel.T
        kwin_t = kwin.T
        for ref, val in ((kcmp_ref, kcmp.T), (vcmp_ref, vcmp.T), (ksel_ref, ksel_t), (vsel_ref, vsel.T),
                         (kwin_ref, kwin_t), (vwin_ref, vwin.T)):
            ref[0] = val
        kcr_ref[...] = kcmp
        vcr_ref[...] = vcmp
        base = (pl.program_id(0) % pos_tiles) * tm
        blk = (base + _iota((LANES, tm), 1)) // SEL_BLOCK
        onehot = jnp.where(_iota((LANES, tm), 0) == blk, 1.0, 0.0)
        kaug_ref[0, 0:LANES] = ksel_t.astype(BF16)
        kaug_ref[0, LANES:2 * LANES] = onehot.astype(BF16)
        kwt_ref[0] = kwin_t.astype(BF16)
        for ref, val in ((vsb_ref, vsel), (vwb_ref, vwin)):
            ref[0] = jnp.where(low, val, 1.0).astype(BF16)
            ref[1] = jnp.where(low, 1.0, val).astype(BF16)

    gat_ref[...] = _sigmoid(proj(OFF_GL, OFF_GL + LANES))
    za = proj(OFF_ZA, OFF_ZA + ATTN_WIDTH)
    sza_ref[...] = za * _sigmoid(za)
    sra_ref[...] = _sigmoid(proj(OFF_RA, OFF_RA + D_MODEL))

    v = proj(OFF_V, OFF_V + GMLP_WIDTH)
    gv = _gelu(v)
    mu = jnp.mean(gv, axis=-1, keepdims=True)
    var = jnp.mean(jnp.square(gv - mu), axis=-1, keepdims=True)
    vn = ((gv - mu) * lax.rsqrt(var + EPS)) * vng_ref[...] + vnb_ref[...]
    if sample:
        vn_ref[...] = vn
    vnb16 = vn.astype(BF16)
    n_chunk = tm // CHUNK
    tri = _iota((CHUNK, CHUNK), 0) >= _iota((CHUNK, CHUNK), 1)
    mixed = []
    for g in range(GMLP_GROUPS):
        wm = jnp.where(tri, wsp_ref[g], 0.0).astype(BF16)
        cat = jnp.concatenate(
            [vnb16[c * CHUNK:(c + 1) * CHUNK, g * LANES:(g + 1) * LANES] for c in range(n_chunk)], axis=1)
        mixed.append(_dot(wm, cat))
    bsp = bsp_ref[...]
    sg = jnp.concatenate(
        [jnp.concatenate([mixed[g][:, c * LANES:(c + 1) * LANES] for g in range(GMLP_GROUPS)], axis=1) + bsp
         for c in range(n_chunk)], axis=0)
    u = proj(OFF_U, OFF_U + GMLP_WIDTH)
    zb = proj(OFF_ZB, OFF_ZB + GMLP_WIDTH)
    t = (_gelu(u) * sg) * (zb * _sigmoid(zb))
    pb = _dot(t.astype(BF16), wpb_ref[...])
    gpb_ref[...] = _sigmoid(proj(OFF_RB, OFF_RB + D_MODEL)) * pb


def _in_project(x2d, tables, prm, tm, sample, seq):
    T = x2d.shape[0]
    nt = T // tm
    pos_tiles = tables[0].shape[0] // tm
    const = lambda *shape: pl.BlockSpec(shape, lambda i: (0,) * len(shape))
    row = lambda width: pl.BlockSpec((tm, width), lambda i: (i, 0))
    tab = pl.BlockSpec((tm, LANES), lambda i: (i % pos_tiles, 0))
    in_specs = [
        row(D_MODEL), const(1, D_MODEL),
        pl.BlockSpec((D_MODEL, W_TOT), lambda i: (0, 0), pipeline_mode=pl.Buffered(1)),
        tab, tab, tab,
        const(GMLP_GROUPS, CHUNK, CHUNK), const(CHUNK, GMLP_WIDTH), const(1, GMLP_WIDTH), const(1, GMLP_WIDTH),
        const(GMLP_WIDTH, D_MODEL),
    ]
    f32rows = lambda width: jax.ShapeDtypeStruct((T, width), F32)
    tail_shapes = [f32rows(LANES), f32rows(ATTN_WIDTH), f32rows(D_MODEL), f32rows(D_MODEL)]
    tail_specs = [row(LANES), row(ATTN_WIDTH), row(D_MODEL), row(D_MODEL)]
    if sample:
        out_shape = [f32rows(ATTN_WIDTH)] + [f32rows(KV_WIDTH)] * 6 + tail_shapes + [f32rows(GMLP_WIDTH)]
        out_specs = [row(ATTN_WIDTH)] + [row(KV_WIDTH)] * 6 + tail_specs + [row(GMLP_WIDTH)]
    else:
        nb = T // seq
        bf = lambda *shape: jax.ShapeDtypeStruct(shape, BF16)
        tposed = lambda rows: pl.BlockSpec((1, rows, tm), lambda i: (i // pos_tiles, 0, i % pos_tiles))
        out_shape = ([bf(N_HEADS, T, LANES)] + [jax.ShapeDtypeStruct((nb, KV_WIDTH, seq), F32)] * 6
                     + [f32rows(KV_WIDTH)] * 2
                     + [bf(nb, 2 * LANES, seq), bf(nb, LANES, seq), bf(KV_HEADS, T, LANES), bf(KV_HEADS, T, LANES)]
                     + tail_shapes)
        heads = lambda n: pl.BlockSpec((n, tm, LANES), lambda i: (0, i, 0))
        out_specs = ([heads(N_HEADS)] + [tposed(KV_WIDTH)] * 6 + [row(KV_WIDTH)] * 2
                     + [tposed(2 * LANES), tposed(LANES), heads(KV_HEADS), heads(KV_HEADS)] + tail_specs)
    return pl.pallas_call(
        functools.partial(_inproj_body, tm, sample, pos_tiles),
        grid=(nt,),
        in_specs=in_specs,
        out_specs=out_specs,
        out_shape=out_shape,
        compiler_params=pltpu.CompilerParams(dimension_semantics=("arbitrary",), vmem_limit_bytes=VMEM_LIMIT),
        name="in_project_sample" if sample else "in_project_prompt",
    )(x2d, prm["norm_g"], prm["w_in"], *tables, prm["w_sp_s" if sample else "w_sp"],
      prm["b_sp_s" if sample else "b_sp"], prm["v_norm_g"], prm["v_norm_b"], prm["w_pb"])


def _strided_halfblocks(load_rows):
    return jnp.concatenate([load_rows(t) for t in range(CMP_STRIDE)], axis=1).astype(BF16)


def _compress_mlp(xcat, n, w1_ref, b1_ref, w2_ref, b2_ref):
    hh = _dot(xcat, w1_ref[...])
    hid = jnp.concatenate(
        [hh[:, 0:LANES] + pltpu.roll(hh[:, LANES:2 * LANES], n - 1, 0),
         hh[:, 2 * LANES:3 * LANES] + pltpu.roll(hh[:, 3 * LANES:4 * LANES], n - 1, 0)], axis=1) + b1_ref[...]
    return _dot(_gelu(hid).astype(BF16), w2_ref[...]) + b2_ref[...]


def _compress_prompt_body(n, kr_ref, vr_ref, w1k, b1k, w2k, b2k, w1v, b1v, w2v, b2v, cos_ref, s1_ref, s2_ref,
                          ovl_ref, kct_ref, vco_ref):
    kc = _compress_mlp(_strided_halfblocks(lambda t: kr_ref[0, pl.ds(t, n, stride=CMP_STRIDE), :]),
                       n, w1k, b1k, w2k, b2k)
    kct_ref[0] = _rope(kc, cos_ref[...], s1_ref[...], s2_ref[...]).T.astype(BF16)
    vc = _compress_mlp(_strided_halfblocks(lambda t: vr_ref[0, pl.ds(t, n, stride=CMP_STRIDE), :]),
                       n, w1v, b1v, w2v, b2v)
    vco_ref[0, :, 0:LANES] = vc.astype(BF16)
    vco_ref[0, :, LANES:2 * LANES] = ovl_ref[...]


def _compress_prompt(kr, vr, prm, tables, ovl):
    B, S, _ = kr.shape
    n = S // CMP_STRIDE
    const = lambda *shape: pl.BlockSpec(shape, lambda b: (0,) * len(shape))
    rows = pl.BlockSpec((1, S, LANES), lambda b: (b, 0, 0))
    wspecs = [const(CMP_STRIDE * LANES, 4 * LANES), const(1, 2 * LANES), const(2 * LANES, LANES), const(1, LANES)]
    return pl.pallas_call(
        functools.partial(_compress_prompt_body, n),
        grid=(B,),
        in_specs=[rows, rows] + wspecs + wspecs + [const(n, LANES)] * 3 + [const(n, LANES)],
        out_specs=[pl.BlockSpec((1, LANES, n), lambda b: (b, 0, 0)),
                   pl.BlockSpec((1, n, 2 * LANES), lambda b: (b, 0, 0))],
        out_shape=[jax.ShapeDtypeStruct((B, LANES, n), BF16),
                   jax.ShapeDtypeStruct((B, n, 2 * LANES), BF16)],
        compiler_params=pltpu.CompilerParams(dimension_semantics=("arbitrary",), vmem_limit_bytes=VMEM_LIMIT),
        name="compress_prompt",
    )(kr, vr, *prm["cmp_k"], *prm["cmp_v"], *tables, ovl)


def _page_copies(pt_ref, pool_ref, buf, sem, b, n_pages, start):
    def body(p, carry):
        page = pt_ref[b, p] if start else 0
        cp = pltpu.make_async_copy(pool_ref.at[page], buf.at[p], sem)
        if start:
            cp.start()
        else:
            cp.wait()
        return carry
    lax.fori_loop(0, n_pages, body, 0)


def _compress_sample_body(n, n_pages, pt_ref, pk_ref, pv_ref, w1k, b1k, w2k, b2k, w1v, b1v, w2v, b2v,
                          cos_ref, s1_ref, s2_ref, perm_ref, kct_ref, vc_ref, kb0, kb1, vb0, vb1, xk, xv, sem):
    s = pl.program_id(0)
    bufs = ((kb0, vb0, 0), (kb1, vb1, 1))

    def start(b, slot):
        kb, vb, si = bufs[slot]
        _page_copies(pt_ref, pk_ref, kb, sem.at[2 * si], b, n_pages, True)
        _page_copies(pt_ref, pv_ref, vb, sem.at[2 * si + 1], b, n_pages, True)

    hb_per_page = PAGE_SIZE // CMP_STRIDE

    def halfblocks(buf, xcat):
        pairs = jnp.concatenate([buf[pl.ds(0, n_pages // 2, stride=2)], buf[pl.ds(1, n_pages // 2, stride=2)]], axis=2)
        y = _dot(pairs.reshape(n_pages // 2 * LANES, 2 * PAGE_SIZE).astype(BF16), perm_ref[...])
        for p in range(n_pages):
            z = y[p // 2 * LANES:(p // 2 + 1) * LANES, p % 2 * PAGE_SIZE:(p % 2 + 1) * PAGE_SIZE].T
            for t in range(CMP_STRIDE):
                xcat[p * hb_per_page:(p + 1) * hb_per_page, t * LANES:(t + 1) * LANES] = (
                    z[t * hb_per_page:(t + 1) * hb_per_page])

    def finish(slot):
        kb, vb, si = bufs[slot]
        _page_copies(pt_ref, pk_ref, kb, sem.at[2 * si], 0, n_pages, False)
        _page_copies(pt_ref, pv_ref, vb, sem.at[2 * si + 1], 0, n_pages, False)
        halfblocks(kb, xk)
        halfblocks(vb, xv)
        kc = _compress_mlp(xk[...].astype(BF16), n, w1k, b1k, w2k, b2k)
        kct_ref[slot] = _rope(kc, cos_ref[...], s1_ref[...], s2_ref[...]).T.astype(BF16)
        vc_ref[slot] = _compress_mlp(xv[...].astype(BF16), n, w1v, b1v, w2v, b2v).astype(BF16)

    @pl.when(s == 0)
    def _():
        start(0, 0)

    start(2 * s + 1, 1)
    finish(0)

    @pl.when(s + 1 < pl.num_programs(0))
    def _():
        start(2 * s + 2, 0)

    finish(1)


def _compress_sample(page_table, pool_k, pool_v, prm, tables):
    Bd, n_pages = page_table.shape
    past = n_pages * PAGE_SIZE
    n = past // CMP_STRIDE
    const = lambda *shape: pl.BlockSpec(shape, lambda s, pt: (0,) * len(shape))
    wspecs = [const(CMP_STRIDE * LANES, 4 * LANES), const(1, 2 * LANES), const(2 * LANES, LANES), const(1, LANES)]
    anyspec = pl.BlockSpec(memory_space=pl.ANY)
    grid_spec = pltpu.PrefetchScalarGridSpec(
        num_scalar_prefetch=1,
        grid=(Bd // 2,),
        in_specs=[anyspec, anyspec] + wspecs + wspecs + [const(n, LANES)] * 3 + [const(2 * PAGE_SIZE, 2 * PAGE_SIZE)],
        out_specs=[pl.BlockSpec((2, LANES, n), lambda s, pt: (s, 0, 0)),
                   pl.BlockSpec((2, n, LANES), lambda s, pt: (s, 0, 0))],
        scratch_shapes=[pltpu.VMEM((n_pages, LANES, PAGE_SIZE), F32)] * 4
        + [pltpu.VMEM((n, CMP_STRIDE * LANES), F32)] * 2 + [pltpu.SemaphoreType.DMA((4,))],
    )
    hb = PAGE_SIZE // CMP_STRIDE
    perm = np.zeros((2 * PAGE_SIZE, 2 * PAGE_SIZE), np.float32)
    for side in range(2):
        for jj in range(hb):
            for t in range(CMP_STRIDE):
                perm[side * PAGE_SIZE + CMP_STRIDE * jj + t, side * PAGE_SIZE + t * hb + jj] = 1.0
    return pl.pallas_call(
        functools.partial(_compress_sample_body, n, n_pages),
        grid_spec=grid_spec,
        out_shape=[jax.ShapeDtypeStruct((Bd, LANES, n), BF16), jax.ShapeDtypeStruct((Bd, n, LANES), BF16)],
        compiler_params=pltpu.CompilerParams(dimension_semantics=("arbitrary",), vmem_limit_bytes=VMEM_LIMIT),
        name="compress_sample",
    )(page_table, pool_k, pool_v, *prm["cmp_k"], *prm["cmp_v"], *tables, jnp.asarray(perm, BF16))


def _mixer_tail(o, sza, sra, gpb, x, wpa_ref, wo_ref, fg_ref):
    pa = _dot((o * sza).astype(BF16), wpa_ref[...])
    merged = sra * pa + gpb
    hn = x + _dot(merged.astype(BF16), wo_ref[...])
    r = lax.rsqrt(jnp.mean(hn * hn, axis=-1, keepdims=True) + EPS)
    return (hn * r) * fg_ref[...]


def _topk_mask(score, blk, n_blocks, axis):
    cnt = jnp.zeros(score.shape, jnp.int32)
    for sp in range(n_blocks):
        b = lax.slice_in_dim(score, sp, sp + 1, axis=axis)
        ge = jnp.where(b >= score, 1, 0)
        gt = jnp.where(b > score, 1, 0)
        cnt = cnt + jnp.where(blk > sp, ge, gt)
    return cnt < TOP_N


def _attn_prompt_body(n_sel, q_ref, ka_ref, vs_ref, kw_ref, vw_ref, kc_ref, vco_ref, gat_ref, sza_ref, sra_ref,
                      gpb_ref, x_ref, wpa_ref, wo_ref, fg_ref, y_ref):
    i = pl.program_id(1)
    M = HPG * TQ
    G = range(KV_HEADS)
    qpos = i * TQ + (_iota((M, 1), 0) & (TQ - 1))
    n_cmp_pad = kc_ref.shape[2]
    gt = gat_ref[...]
    qs = [q_ref[HPG * g:HPG * (g + 1)].reshape(M, LANES) for g in G]

    mk = (_iota((M, n_cmp_pad), 1) * CMP_STRIDE + (CMP_BLOCK - 1)) <= qpos
    o_cmp, qa = [], []
    blk = _iota((n_sel, TQ), 0)
    cur = (i * TQ + _iota((n_sel, TQ), 1)) // SEL_BLOCK
    valid = blk <= cur
    forced = (valid & (blk > cur - N_LOCAL_BLOCKS)) | (blk < N_INIT_BLOCKS)
    scores = []
    for g in G:
        s = jnp.where(mk, _dot(qs[g], kc_ref[0]), NEG)
        e = jnp.exp2(s - jnp.max(s, axis=-1, keepdims=True))
        p = jnp.where(mk, e / jnp.sum(e, axis=-1, keepdims=True), 0.0)
        r = _dot(p.astype(BF16), vco_ref[0])
        o_cmp.append(r[:, 0:LANES])
        impc = r[:, LANES:2 * LANES]
        imp = impc[0:TQ] + impc[TQ:2 * TQ] + impc[2 * TQ:3 * TQ] + impc[3 * TQ:4 * TQ]
        sc = imp.T[HEAD_DIM:HEAD_DIM + n_sel]
        scores.append(jnp.where(forced, BIG, jnp.where(valid, sc, NEG)))

    last_blk = ((i + 1) * TQ - 1) // SEL_BLOCK

    def count_group(k, cnts):
        out = []
        for g in G:
            c = cnts[g]
            for sp in range(k * RANK_GROUP, (k + 1) * RANK_GROUP):
                b = scores[g][sp:sp + 1]
                c = c + jnp.where(blk > sp, jnp.where(b >= scores[g], 1, 0), jnp.where(b > scores[g], 1, 0))
            out.append(c)
        return tuple(out)

    cnts = (jnp.zeros((n_sel, TQ), jnp.int32),) * KV_HEADS
    for k in range(n_sel // RANK_GROUP):
        cnts = lax.cond(k * RANK_GROUP <= last_blk, functools.partial(count_group, k), lambda c: c, cnts)
    for g in G:
        selneg = jnp.where(cnts[g] < TOP_N, 0.0, NEG)
        seln_t = jnp.concatenate([selneg, jnp.zeros((LANES - n_sel, TQ), F32)], axis=0).T.astype(BF16)
        qa.append(jnp.concatenate([qs[g], jnp.concatenate([seln_t] * HPG, axis=0)], axis=1))

    def update(carry, s, v):
        m, acc = carry
        mn = jnp.maximum(m, jnp.max(s, axis=-1, keepdims=True))
        acc = jnp.exp2(m - mn) * acc + _dot(jnp.exp2(s - mn).astype(BF16), v)
        return mn, acc

    def step(kt, carries, causal):
        off = pl.multiple_of(kt * KT, KT)
        k = ka_ref[0, :, pl.ds(off, KT)]
        ss = [_dot(qa[g], k) for g in G]
        if causal:
            keep = kt * KT + _iota((M, KT), 1) <= qpos
            ss = [jnp.where(keep, s, NEG) for s in ss]
        return tuple(update(carries[g], ss[g], vs_ref[g, pl.ds(off, KT), :]) for g in G)

    n_full = i // (KT // TQ)
    init = (jnp.full((M, 1), NEG, F32), jnp.zeros((M, LANES), F32))
    carries = lax.fori_loop(0, n_full, lambda kt, c: step(kt, c, False), (init,) * KV_HEADS)
    carries = step(n_full, carries, True)
    sum_lane = [(KV_HEADS - 1 - g) * HEAD_DIM for g in G]
    o_sel = [carries[g][1] / carries[g][1][:, sum_lane[g]:sum_lane[g] + 1] for g in G]

    st = pl.multiple_of(jnp.maximum(i * TQ - WINDOW, 0), TQ)
    kpos = st + _iota((M, WIN_KEYS), 1)
    mkw = (kpos <= qpos) & (kpos > qpos - WINDOW)
    kw = kw_ref[0, :, pl.ds(st, WIN_KEYS)]
    o_groups = []
    for g in G:
        s = jnp.where(mkw, _dot(qs[g], kw), NEG)
        e = jnp.exp2(s - jnp.max(s, axis=-1, keepdims=True))
        r = _dot(e.astype(BF16), vw_ref[g, pl.ds(st, WIN_KEYS), :])
        o_win = r / r[:, sum_lane[g]:sum_lane[g] + 1]

        def gcol(c, g=g):
            return jnp.concatenate(
                [gt[:, 3 * (HPG * g + j) + c:3 * (HPG * g + j) + c + 1] for j in range(HPG)], axis=0)

        o_groups.append(gcol(0) * o_cmp[g] + gcol(1) * o_sel[g] + gcol(2) * o_win)

    low = _iota((TQ, LANES), 1) < HEAD_DIM
    slabs = []
    for pp in range(N_HEADS // 2):
        g, j0 = pp // 2, (2 * pp) % HPG
        a = o_groups[g][j0 * TQ:(j0 + 1) * TQ]
        b = o_groups[g][(j0 + 1) * TQ:(j0 + 2) * TQ]
        if g == 0:
            b = pltpu.roll(b, HEAD_DIM, 1)
        else:
            a = pltpu.roll(a, HEAD_DIM, 1)
        slabs.append(jnp.where(low, a, b))
    o = jnp.concatenate(slabs, axis=1)
    y_ref[...] = _mixer_tail(o, sza_ref[...], sra_ref[...], gpb_ref[...], x_ref[...], wpa_ref, wo_ref, fg_ref)


def _attn_prompt(B, S, q_hm, kaug, vsb, kwt, vwb, kct, vco, gates, sza, sra, gpb, x2d, prm):
    nq = S // TQ
    n_sel = S // SEL_BLOCK
    n_cmp_pad = kct.shape[2]
    row = lambda width: pl.BlockSpec((TQ, width), lambda b, i: (b * nq + i, 0))
    const = lambda *shape: pl.BlockSpec(shape, lambda b, i: (0,) * len(shape))
    batch = lambda *shape: pl.BlockSpec((1,) + shape, lambda b, i: (b,) + (0,) * len(shape))
    in_specs = [
        pl.BlockSpec((N_HEADS, TQ, LANES), lambda b, i: (0, b * nq + i, 0)),
        batch(2 * LANES, S),
        pl.BlockSpec((KV_HEADS, S, LANES), lambda b, i: (0, b, 0)),
        batch(LANES, S),
        pl.BlockSpec((KV_HEADS, S, LANES), lambda b, i: (0, b, 0)),
        batch(LANES, n_cmp_pad),
        batch(n_cmp_pad, 2 * LANES),
        row(LANES), row(ATTN_WIDTH), row(D_MODEL), row(D_MODEL), row(D_MODEL),
        const(ATTN_WIDTH, D_MODEL), const(D_MODEL, D_MODEL), const(1, D_MODEL),
    ]
    return pl.pallas_call(
        functools.partial(_attn_prompt_body, n_sel),
        grid=(B, nq),
        in_specs=in_specs,
        out_specs=row(D_MODEL),
        out_shape=jax.ShapeDtypeStruct((B * S, D_MODEL), F32),
        compiler_params=pltpu.CompilerParams(dimension_semantics=("arbitrary", "arbitrary"),
                                             vmem_limit_bytes=VMEM_LIMIT),
        name="attn_prompt",
    )(q_hm, kaug, vsb, kwt, vwb, kct, vco, gates, sza, sra, gpb, x2d, prm["w_pa"], prm["w_o"], prm["final_g"])


def _attn_sample_body(n_pages, tn, pt_ref, pk_ref, pv_ref, qa_ref, kc_ref, vc_ref, ovl_ref, oh_ref, kt_ref, vt_ref,
                      ckw_ref, cvw_ref, kwn_ref, vwn_ref, o_ref, kb0, kb1, vb0, vb1, sem):
    s_id = pl.program_id(0)
    past = n_pages * PAGE_SIZE
    n_past_blk = past // SEL_BLOCK
    n_blk_pad = ovl_ref.shape[1]
    R = N_HEADS * TOK_PAD
    GR = HPG * TOK_PAD
    wc = ckw_ref.shape[2]
    bufs = ((kb0, vb0, 0), (kb1, vb1, 1))
    t_row = _iota((R, 1), 0) & (TOK_PAD - 1)

    def start(b, slot):
        kb, vb, si = bufs[slot]
        _page_copies(pt_ref, pk_ref, kb, sem.at[2 * si], b, n_pages, True)
        _page_copies(pt_ref, pv_ref, vb, sem.at[2 * si + 1], b, n_pages, True)

    def finish(slot):
        kb, vb, si = bufs[slot]
        qa = qa_ref[slot]

        n_cmp_pad = kc_ref.shape[2]
        s = _dot(qa, kc_ref[slot])
        mk = _iota((R, n_cmp_pad), 1) < n_cmp_pad - 1
        s = jnp.where(mk, s, NEG)
        e = jnp.exp(s - jnp.max(s, axis=-1, keepdims=True))
        p = jnp.where(mk, e / jnp.sum(e, axis=-1, keepdims=True), 0.0).astype(BF16)
        o_cmp = _dot(p, vc_ref[slot])
        impc = _dot(p, ovl_ref[...])
        imp = jnp.concatenate(
            [impc[g * GR:g * GR + TOK_PAD] + impc[g * GR + TOK_PAD:g * GR + 2 * TOK_PAD]
             + impc[g * GR + 2 * TOK_PAD:g * GR + 3 * TOK_PAD] + impc[g * GR + 3 * TOK_PAD:g * GR + 4 * TOK_PAD]
             for g in range(KV_HEADS)], axis=0)
        nr = KV_HEADS * TOK_PAD
        blk = _iota((nr, n_blk_pad), 1)
        cur = (past + jnp.minimum(_iota((nr, n_blk_pad), 0) & (TOK_PAD - 1), tn - 1)) // SEL_BLOCK
        valid = blk <= cur
        forced = (valid & (blk > cur - N_LOCAL_BLOCKS)) | (blk < N_INIT_BLOCKS)
        score = jnp.where(forced, BIG, jnp.where(valid, imp, NEG))
        sel = _topk_mask(score, blk, n_past_blk + 1, 1)
        selneg = jnp.where(sel, 0.0, NEG)
        selneg = jnp.concatenate(
            [selneg[g * TOK_PAD:(g + 1) * TOK_PAD] for g in range(KV_HEADS) for _ in range(HPG)], axis=0)
        low = _iota((R, LANES), 1) < SEL_HALF
        qaug = []
        for hf in range(n_past_blk // SEL_HALF):
            slab = selneg[:, (hf // 2) * LANES:(hf // 2 + 1) * LANES]
            if hf % 2:
                slab = pltpu.roll(slab, SEL_HALF, 1)
            qaug.append(jnp.concatenate([qa, jnp.where(low, slab, 0.0).astype(BF16)], axis=1))

        _page_copies(pt_ref, pk_ref, kb, sem.at[2 * si], 0, n_pages, False)
        _page_copies(pt_ref, pv_ref, vb, sem.at[2 * si + 1], 0, n_pages, False)

        pages_per_half = SEL_HALF * SEL_BLOCK // PAGE_SIZE

        def half(buf, hf):
            return jnp.concatenate(
                [buf[hf * pages_per_half + j] for j in range(pages_per_half)], axis=1).astype(BF16)

        col = _iota((R, LANES), 1)
        ss = [_dot(qaug[hf], jnp.concatenate([half(kb, hf), oh_ref[...]], axis=0)) for hf in range(len(qaug))]
        ss.append(jnp.where(col <= t_row, _nt(qa, kt_ref[slot]), NEG))
        m = functools.reduce(jnp.maximum, [jnp.max(s, axis=-1, keepdims=True) for s in ss])
        es = [jnp.exp(s - m) for s in ss]
        l = functools.reduce(jnp.add, [jnp.sum(e, axis=-1, keepdims=True) for e in es])
        acc = _dot(es[-1].astype(BF16), vt_ref[slot])
        for hf in range(len(qaug)):
            acc = acc + _nt(es[hf].astype(BF16), half(vb, hf))
        o_sel = acc / l

        sa = jnp.where(_iota((R, wc), 1) > t_row, _dot(qa, ckw_ref[slot].astype(BF16)), NEG)
        sb = jnp.where(col <= t_row, _nt(qa, kwn_ref[slot]), NEG)
        m = jnp.maximum(jnp.max(sa, axis=-1, keepdims=True), jnp.max(sb, axis=-1, keepdims=True))
        ea, eb = jnp.exp(sa - m), jnp.exp(sb - m)
        den = jnp.sum(ea, axis=-1, keepdims=True) + jnp.sum(eb, axis=-1, keepdims=True)
        o_win = (_nt(ea.astype(BF16), cvw_ref[slot].astype(BF16)) + _dot(eb.astype(BF16), vwn_ref[slot])) / den

        o_ref[slot, 0] = o_cmp
        o_ref[slot, 1] = o_sel
        o_ref[slot, 2] = o_win

    @pl.when(s_id == 0)
    def _():
        start(0, 0)

    start(2 * s_id + 1, 1)
    finish(0)

    @pl.when(s_id + 1 < pl.num_programs(0))
    def _():
        start(2 * s_id + 2, 0)

    finish(1)


def _attn_sample(page_table, pool_k, pool_v, tn, qa, kct, vc, ovl, ktail, vtail, ckw, cvw, kwn, vwn):
    Bd, n_pages = page_table.shape
    R = N_HEADS * TOK_PAD
    wc = ckw.shape[2]
    const = lambda *shape: pl.BlockSpec(shape, lambda s, pt: (0,) * len(shape))
    pair = lambda *shape: pl.BlockSpec((2,) + shape, lambda s, pt: (s,) + (0,) * len(shape))
    anyspec = pl.BlockSpec(memory_space=pl.ANY)
    half_keys = SEL_HALF * SEL_BLOCK
    onehot = np.zeros((LANES, half_keys), np.float32)
    onehot[np.arange(half_keys) // SEL_BLOCK, np.arange(half_keys)] = 1.0
    grid_spec = pltpu.PrefetchScalarGridSpec(
        num_scalar_prefetch=1,
        grid=(Bd // 2,),
        in_specs=[anyspec, anyspec, pair(R, LANES), pair(LANES, kct.shape[2]), pair(vc.shape[1], LANES),
                  const(*ovl.shape), const(LANES, half_keys), pair(LANES, LANES), pair(LANES, LANES),
                  pair(LANES, wc), pair(LANES, wc), pair(LANES, LANES), pair(LANES, LANES)],
        out_specs=pair(3, R, LANES),
        scratch_shapes=[pltpu.VMEM((n_pages, LANES, PAGE_SIZE), F32)] * 4 + [pltpu.SemaphoreType.DMA((4,))],
    )
    return pl.pallas_call(
        functools.partial(_attn_sample_body, n_pages, tn),
        grid_spec=grid_spec,
        out_shape=jax.ShapeDtypeStruct((Bd, 3, R, LANES), F32),
        compiler_params=pltpu.CompilerParams(dimension_semantics=("arbitrary",), vmem_limit_bytes=VMEM_LIMIT),
        name="attn_sample",
    )(page_table, pool_k, pool_v, qa, kct, vc, ovl, jnp.asarray(onehot, BF16), ktail, vtail, ckw, cvw, kwn, vwn)


def _mixer_sample_body(o3_ref, g3_ref, sza_ref, sra_ref, gpb_ref, x_ref, wpa_ref, wo_ref, fg_ref, y_ref):
    o = g3_ref[0] * o3_ref[0] + g3_ref[1] * o3_ref[1] + g3_ref[2] * o3_ref[2]
    y_ref[...] = _mixer_tail(o, sza_ref[...], sra_ref[...], gpb_ref[...], x_ref[...], wpa_ref, wo_ref, fg_ref)


def _mixer_sample(o3, g3, sza, sra, gpb, x2d, prm):
    T = x2d.shape[0]
    full = lambda *shape: pl.BlockSpec(shape, lambda i: (0,) * len(shape))
    return pl.pallas_call(
        _mixer_sample_body,
        grid=(1,),
        in_specs=[full(3, T, ATTN_WIDTH), full(3, T, ATTN_WIDTH), full(T, ATTN_WIDTH), full(T, D_MODEL),
                  full(T, D_MODEL), full(T, D_MODEL), full(ATTN_WIDTH, D_MODEL), full(D_MODEL, D_MODEL),
                  full(1, D_MODEL)],
        out_specs=full(T, D_MODEL),
        out_shape=jax.ShapeDtypeStruct((T, D_MODEL), F32),
        compiler_params=pltpu.CompilerParams(dimension_semantics=("arbitrary",), vmem_limit_bytes=VMEM_LIMIT),
        name="mixer_sample",
    )(o3, g3, sza, sra, gpb, x2d, prm["w_pa"], prm["w_o"], prm["final_g"])


def _overlap(n_cmp, n_sel):
    cs = np.arange(n_cmp)[:, None] * CMP_STRIDE
    ss = np.arange(n_sel)[None, :] * SEL_BLOCK
    ov = np.minimum(cs + CMP_BLOCK, ss + SEL_BLOCK) - np.maximum(cs, ss)
    return np.clip(ov, 0, None).astype(np.float32) / CMP_BLOCK


def _cmp_weights(w1, b1, w2, b2):
    w1r = w1.reshape(2, CMP_STRIDE, HEAD_DIM, CMP_HIDDEN).transpose(1, 2, 0, 3)
    big = jnp.zeros((CMP_STRIDE, KV_HEADS, HEAD_DIM, KV_HEADS, 2, CMP_HIDDEN), F32)
    w2b = jnp.zeros((KV_HEADS, CMP_HIDDEN, KV_HEADS, HEAD_DIM), F32)
    for g in range(KV_HEADS):
        big = big.at[:, g, :, g].set(w1r)
        w2b = w2b.at[g, :, g].set(w2)
    return (big.reshape(CMP_STRIDE * LANES, 4 * LANES).astype(BF16), jnp.tile(b1, KV_HEADS)[None],
            w2b.reshape(KV_HEADS * CMP_HIDDEN, LANES).astype(BF16), jnp.tile(b2, KV_HEADS)[None])


def _seq_minor(t):
    lead = t.shape[:-3]
    n = len(lead)
    return t.transpose(*range(n), n + 1, n + 2, n).reshape(*lead, KV_WIDTH, t.shape[-3])


def _seq_major(t):
    lead = t.shape[:-2]
    n = len(lead)
    return t.reshape(*lead, KV_HEADS, HEAD_DIM, t.shape[-1]).transpose(*range(n), n + 2, n, n + 1)


def kernel(x_prompt, x_sample, cache_k_cmp, cache_v_cmp, cache_k_sel, cache_v_sel, cache_k_win, cache_v_win, page_table, norm_g, w_in, cmp_k_w1, cmp_k_b1, cmp_k_w2, cmp_k_b2, cmp_v_w1, cmp_v_b1, cmp_v_w2, cmp_v_b2, v_norm_g, v_norm_b, w_spatial, b_spatial, w_pa, w_pb, w_o, final_g):
    B, S, _ = x_prompt.shape
    Bd, tn, _ = x_sample.shape
    depth = w_in.shape[0]
    assert depth == 1, "single-layer step"
    assert Bd * tn == CHUNK, "the sample tokens form one 128-row tile"
    n_pages = page_table.shape[1]
    past = n_pages * PAGE_SIZE

    split = OFF_GL + 3 * N_HEADS
    w = w_in[0]
    prm = {
        "norm_g": norm_g,
        "w_in": jnp.concatenate([w[:, :split], jnp.zeros((D_MODEL, GL_PAD), F32), w[:, split:]], axis=1).astype(BF16),
        "w_sp": w_spatial[0],
        "b_sp": jnp.repeat(b_spatial[0].T, LANES, axis=1),
        "w_sp_s": jnp.stack([jnp.kron(jnp.eye(CHUNK // tn, dtype=F32), w_spatial[0, g, :tn, :tn])
                             for g in range(GMLP_GROUPS)]),
        "b_sp_s": jnp.tile(jnp.repeat(b_spatial[0, :, :tn].T, LANES, axis=1), (CHUNK // tn, 1)),
        "v_norm_g": v_norm_g, "v_norm_b": v_norm_b,
        "w_pb": w_pb[0].astype(BF16), "w_pa": w_pa[0].astype(BF16), "w_o": w_o[0].astype(BF16),
        "final_g": final_g[None],
        "cmp_k": _cmp_weights(cmp_k_w1[0], cmp_k_b1[0], cmp_k_w2[0], cmp_k_b2[0]),
        "cmp_v": _cmp_weights(cmp_v_w1[0], cmp_v_b1[0], cmp_v_w2[0], cmp_v_b2[0]),
    }

    xp = x_prompt.reshape(B * S, D_MODEL)
    (q_hm, p_kcmp, p_vcmp, p_ksel, p_vsel, p_kwin, p_vwin, kcr, vcr, kaug, kwt, vsb, vwb,
     gates, sza, sra, gpb) = _in_project(xp, _rope_tables(jnp.arange(S)), prm, TM_PROMPT, False, S)
    n_half = S // CMP_STRIDE
    n_cmp = (S - CMP_BLOCK) // CMP_STRIDE + 1
    n_sel = S // SEL_BLOCK
    assert n_sel <= SEL_HALF
    ovl = np.zeros((n_half, LANES), np.float32)
    ovl[:n_cmp, HEAD_DIM:HEAD_DIM + n_sel] = _overlap(n_cmp, n_sel)
    cend_tables = _rope_tables(jnp.arange(n_half) * CMP_STRIDE + CMP_BLOCK - 1)
    kct, vco = _compress_prompt(kcr.reshape(B, S, LANES), vcr.reshape(B, S, LANES), prm, cend_tables,
                                jnp.asarray(ovl, BF16))
    y_prompt = _attn_prompt(B, S, q_hm, kaug, vsb, kwt, vwb, kct, vco, gates, sza, sra, gpb, xp, prm)

    xs = x_sample.reshape(Bd * tn, D_MODEL)
    pos_s = jnp.tile(past + jnp.arange(tn), Bd)
    (q_s, s_kcmp, s_vcmp, s_ksel, s_vsel, s_kwin, s_vwin, gates_s, sza_s, sra_s, gpb_s, vn_s) = _in_project(
        xs, _rope_tables(pos_s), prm, Bd * tn, True, tn)
    n_half_s = past // CMP_STRIDE
    pools = [_seq_minor(c[0]) for c in (cache_k_cmp, cache_v_cmp, cache_k_sel, cache_v_sel)]
    cend_s = _rope_tables(jnp.arange(n_half_s) * CMP_STRIDE + CMP_BLOCK - 1)
    kct_s, vc_s = _compress_sample(page_table, pools[0], pools[1], prm, cend_s)

    n_cmp_s = (past + tn - CMP_BLOCK) // CMP_STRIDE + 1
    n_blk_s = past // SEL_BLOCK + -(-tn // SEL_BLOCK)
    assert n_cmp_s == n_half_s - 1 and tn <= TOK_PAD and (past // SEL_BLOCK) % SEL_HALF == 0
    ovl_s = np.zeros((n_half_s, 2 * LANES), np.float32)
    ovl_s[:n_cmp_s, :n_blk_s] = _overlap(n_cmp_s, n_blk_s)
    q5 = q_s.reshape(Bd, tn, KV_HEADS, HPG, HEAD_DIM).transpose(0, 2, 3, 1, 4)
    q5 = jnp.pad(q5, ((0, 0), (0, 0), (0, 0), (0, TOK_PAD - tn), (0, 0))).reshape(Bd, KV_HEADS, HPG * TOK_PAD, HEAD_DIM)
    qa = jnp.zeros((Bd, KV_HEADS, HPG * TOK_PAD, KV_HEADS, HEAD_DIM), F32)
    for g in range(KV_HEADS):
        qa = qa.at[:, g, :, g].set(q5[:, g])
    qa = qa.reshape(Bd, N_HEADS * TOK_PAD, LANES).astype(BF16)
    pad_rows = lambda t: jnp.pad(t.reshape(Bd, tn, LANES), ((0, 0), (0, LANES - tn), (0, 0))).astype(BF16)
    ckw = _seq_minor(cache_k_win[0])
    cvw = _seq_minor(cache_v_win[0])
    o3 = _attn_sample(page_table, pools[2], pools[3], tn, qa, kct_s, vc_s, jnp.asarray(ovl_s, BF16),
                      pad_rows(s_ksel), pad_rows(s_vsel), ckw, cvw, pad_rows(s_kwin), pad_rows(s_vwin))
    new_t = lambda t: t.reshape(Bd, tn, LANES).transpose(0, 2, 1)
    s_k_win = _seq_major(jnp.concatenate([ckw[:, :, tn:], new_t(s_kwin)], axis=2))[None]
    s_v_win = _seq_major(jnp.concatenate([cvw[:, :, tn:], new_t(s_vwin)], axis=2))[None]
    first_group = jnp.arange(N_HEADS * TOK_PAD)[:, None] < HPG * TOK_PAD
    o3r = jnp.where(first_group, o3[..., :HEAD_DIM], o3[..., HEAD_DIM:])
    o3r = o3r.reshape(Bd, 3, N_HEADS, TOK_PAD, HEAD_DIM)[:, :, :, :tn]
    o3r = o3r.transpose(1, 0, 3, 2, 4).reshape(3, Bd * tn, ATTN_WIDTH)
    g3 = gates_s[:, :3 * N_HEADS].reshape(Bd * tn, N_HEADS, 3).transpose(2, 0, 1)
    g3 = jnp.repeat(g3, HEAD_DIM, axis=2)
    y_sample = _mixer_sample(o3r, g3, sza_s, sra_s, gpb_s, xs, prm)

    kv5 = lambda t, b, n: t.reshape(1, b, n, KV_HEADS, HEAD_DIM)
    pw = min(WINDOW, S)
    return (y_prompt.reshape(B, S, D_MODEL), y_sample.reshape(Bd, tn, D_MODEL),
            _seq_major(p_kcmp)[None], _seq_major(p_vcmp)[None], _seq_major(p_ksel)[None], _seq_major(p_vsel)[None],
            _seq_major(p_kwin[:, :, S - pw:])[None], _seq_major(p_vwin[:, :, S - pw:])[None],
            kv5(s_kcmp, Bd, tn), kv5(s_vcmp, Bd, tn), kv5(s_ksel, Bd, tn), kv5(s_vsel, Bd, tn),
            s_k_win, s_v_win, vn_s.reshape(1, Bd, tn, GMLP_WIDTH))
```

```python
import functools

import numpy as np
import jax
import jax.numpy as jnp
from jax import lax
from jax.experimental import pallas as pl
from jax.experimental.pallas import tpu as pltpu

F32 = jnp.float32
BF16 = jnp.bfloat16

D_MODEL = 1024
HEAD_DIM = 64
N_HEADS = 8
KV_HEADS = 2
HPG = N_HEADS // KV_HEADS
ATTN_WIDTH = N_HEADS * HEAD_DIM
KV_WIDTH = KV_HEADS * HEAD_DIM
ROT_DIM = HEAD_DIM // 4
ROT_HALF = ROT_DIM // 2
ROPE_THETA = 500000.0
CMP_BLOCK = 32
CMP_STRIDE = 16
CMP_HIDDEN = 128
SEL_BLOCK = 64
TOP_N = 16
N_INIT_BLOCKS = 1
N_LOCAL_BLOCKS = 2
WINDOW = 512
CHUNK = 128
GMLP_GROUPS = 4
GMLP_WIDTH = 512
PAGE_SIZE = 128
NEG = -1e30
BIG = 1e30
EPS = 1e-6
LOG2E = 1.4426950408889634

LANES = 128
GL_PAD = LANES - 3 * N_HEADS

OFF_Q = 0
OFF_KV = ATTN_WIDTH
OFF_GL = OFF_KV + 6 * KV_WIDTH
OFF_ZA = OFF_GL + LANES
OFF_U = OFF_ZA + ATTN_WIDTH
OFF_V = OFF_U + GMLP_WIDTH
OFF_ZB = OFF_V + GMLP_WIDTH
OFF_RA = OFF_ZB + GMLP_WIDTH
OFF_RB = OFF_RA + D_MODEL
W_TOT = OFF_RB + D_MODEL

VMEM_LIMIT = 56 * 1024 * 1024

TM_PROMPT = 512
TQ = 128
KT = 512
WIN_KEYS = WINDOW + TQ
SEL_HALF = 64
TOK_PAD = 8
RANK_GROUP = 8


def _nt(a, b):
    return lax.dot_general(a, b, (((1,), (1,)), ((), ())), preferred_element_type=F32)


def _dot(a, b):
    return jnp.dot(a, b, preferred_element_type=F32)


def _iota(shape, dim):
    return lax.broadcasted_iota(jnp.int32, shape, dim)


def _rope(slab, cos, s1, s2):
    return slab * cos + pltpu.roll(slab, LANES - ROT_HALF, 1) * s1 + pltpu.roll(slab, ROT_HALF, 1) * s2


def _rope_tables(pos):
    n = pos.shape[0]
    inv = jnp.power(jnp.float32(ROPE_THETA), -jnp.arange(0, ROT_DIM, 2, dtype=F32) / ROT_DIM)
    ang = pos.astype(F32)[:, None] * inv[None, :]
    cos, sin = jnp.cos(ang), jnp.sin(ang)
    rest = HEAD_DIM - ROT_DIM
    c = jnp.concatenate([cos, cos, jnp.ones((n, rest), F32)], axis=1)
    s1 = jnp.concatenate([-sin, jnp.zeros((n, HEAD_DIM - ROT_HALF), F32)], axis=1)
    s2 = jnp.concatenate([jnp.zeros((n, ROT_HALF), F32), sin, jnp.zeros((n, rest), F32)], axis=1)
    return tuple(jnp.tile(t, (1, LANES // HEAD_DIM)) for t in (c, s1, s2))


def _sigmoid(x):
    return 1.0 / (1.0 + jnp.exp(-x))


def _gelu(x):
    return jax.nn.gelu(x, approximate=True)


def _inproj_body(tm, sample, pos_tiles, x_ref, ng_ref, w_ref, cos_ref, s1_ref, s2_ref, wsp_ref, bsp_ref,
                 vng_ref, vnb_ref, wpb_ref, *outs):
    if sample:
        (q_ref, kcmp_ref, vcmp_ref, ksel_ref, vsel_ref, kwin_ref, vwin_ref,
         gat_ref, sza_ref, sra_ref, gpb_ref, vn_ref) = outs
    else:
        (q_ref, kcmp_ref, vcmp_ref, ksel_ref, vsel_ref, kwin_ref, vwin_ref,
         kcr_ref, vcr_ref, kaug_ref, kwt_ref, vsb_ref, vwb_ref, gat_ref, sza_ref, sra_ref, gpb_ref) = outs

    x = x_ref[...]
    r = lax.rsqrt(jnp.mean(x * x, axis=-1, keepdims=True) + EPS)
    h = ((x * r) * ng_ref[...]).astype(BF16)

    def proj(lo, hi):
        return _dot(h, w_ref[:, lo:hi])

    cos, s1, s2 = cos_ref[...], s1_ref[...], s2_ref[...]
    low = _iota((tm, LANES), 1) < HEAD_DIM

    q = proj(OFF_Q, OFF_Q + ATTN_WIDTH)
    q_scale = HEAD_DIM ** -0.5 if sample else HEAD_DIM ** -0.5 * LOG2E
    for pp in range(N_HEADS // 2):
        slab = _rope(q[:, pp * LANES:(pp + 1) * LANES], cos, s1, s2) * q_scale
        if sample:
            q_ref[:, pp * LANES:(pp + 1) * LANES] = slab
        elif (2 * pp) // HPG == 0:
            q_ref[2 * pp] = jnp.where(low, slab, 0.0).astype(BF16)
            q_ref[2 * pp + 1] = jnp.where(low, pltpu.roll(slab, HEAD_DIM, 1), 0.0).astype(BF16)
        else:
            q_ref[2 * pp] = jnp.where(low, 0.0, pltpu.roll(slab, HEAD_DIM, 1)).astype(BF16)
            q_ref[2 * pp + 1] = jnp.where(low, 0.0, slab).astype(BF16)

    kv = proj(OFF_KV, OFF_KV + 6 * KV_WIDTH)
    kcmp = kv[:, 0:LANES]
    vcmp = kv[:, LANES:2 * LANES]
    ksel = _rope(kv[:, 2 * LANES:3 * LANES], cos, s1, s2)
    vsel = kv[:, 3 * LANES:4 * LANES]
    kwin = _rope(kv[:, 4 * LANES:5 * LANES], cos, s1, s2)
    vwin = kv[:, 5 * LANES:6 * LANES]
    if sample:
        for ref, val in ((kcmp_ref, kcmp), (vcmp_ref, vcmp), (ksel_ref, ksel), (vsel_ref, vsel),
                         (kwin_ref, kwin), (vwin_ref, vwin)):
            ref[...] = val
    else:
        ksel_t = ksel.T
        kwin_t = kwin.T
        for ref, val in ((kcmp_ref, kcmp.T), (vcmp_ref, vcmp.T), (ksel_ref, ksel_t), (vsel_ref, vsel.T),
                         (kwin_ref, kwin_t), (vwin_ref, vwin.T)):
            ref[0] = val
        kcr_ref[...] = kcmp
        vcr_ref[...] = vcmp
        base = (pl.program_id(0) % pos_tiles) * tm
        blk = (base + _iota((LANES, tm), 1)) // SEL_BLOCK
        onehot = jnp.where(_iota((LANES, tm), 0) == blk, 1.0, 0.0)
        kaug_ref[0, 0:LANES] = ksel_t.astype(BF16)
        kaug_ref[0, LANES:2 * LANES] = onehot.astype(BF16)
        kwt_ref[0] = kwin_t.astype(BF16)
        for ref, val in ((vsb_ref, vsel), (vwb_ref, vwin)):
            ref[0] = jnp.where(low, val, 1.0).astype(BF16)
            ref[1] = jnp.where(low, 1.0, val).astype(BF16)

    gat_ref[...] = _sigmoid(proj(OFF_GL, OFF_GL + LANES))
    za = proj(OFF_ZA, OFF_ZA + ATTN_WIDTH)
    sza_ref[...] = za * _sigmoid(za)
    sra_ref[...] = _sigmoid(proj(OFF_RA, OFF_RA + D_MODEL))

    v = proj(OFF_V, OFF_V + GMLP_WIDTH)
    gv = _gelu(v)
    mu = jnp.mean(gv, axis=-1, keepdims=True)
    var = jnp.mean(jnp.square(gv - mu), axis=-1, keepdims=True)
    vn = ((gv - mu) * lax.rsqrt(var + EPS)) * vng_ref[...] + vnb_ref[...]
    if sample:
        vn_ref[...] = vn
    vnb16 = vn.astype(BF16)
    n_chunk = tm // CHUNK
    tri = _iota((CHUNK, CHUNK), 0) >= _iota((CHUNK, CHUNK), 1)
    mixed = []
    for g in range(GMLP_GROUPS):
        wm = jnp.where(tri, wsp_ref[g], 0.0).astype(BF16)
        cat = jnp.concatenate(
            [vnb16[c * CHUNK:(c + 1) * CHUNK, g * LANES:(g + 1) * LANES] for c in range(n_chunk)], axis=1)
        mixed.append(_dot(wm, cat))
    bsp = bsp_ref[...]
    sg = jnp.concatenate(
        [jnp.concatenate([mixed[g][:, c * LANES:(c + 1) * LANES] for g in range(GMLP_GROUPS)], axis=1) + bsp
         for c in range(n_chunk)], axis=0)
    u = proj(OFF_U, OFF_U + GMLP_WIDTH)
    zb = proj(OFF_ZB, OFF_ZB + GMLP_WIDTH)
    t = (_gelu(u) * sg) * (zb * _sigmoid(zb))
    pb = _dot(t.astype(BF16), wpb_ref[...])
    gpb_ref[...] = _sigmoid(proj(OFF_RB, OFF_RB + D_MODEL)) * pb


def _in_project(x2d, tables, prm, tm, sample, seq):
    T = x2d.shape[0]
    nt = T // tm
    pos_tiles = tables[0].shape[0] // tm
    const = lambda *shape: pl.BlockSpec(shape, lambda i: (0,) * len(shape))
    row = lambda width: pl.BlockSpec((tm, width), lambda i: (i, 0))
    tab = pl.BlockSpec((tm, LANES), lambda i: (i % pos_tiles, 0))
    in_specs = [
        row(D_MODEL), const(1, D_MODEL),
        pl.BlockSpec((D_MODEL, W_TOT), lambda i: (0, 0), pipeline_mode=pl.Buffered(1)),
        tab, tab, tab,
        const(GMLP_GROUPS, CHUNK, CHUNK), const(CHUNK, GMLP_WIDTH), const(1, GMLP_WIDTH), const(1, GMLP_WIDTH),
        const(GMLP_WIDTH, D_MODEL),
    ]
    f32rows = lambda width: jax.ShapeDtypeStruct((T, width), F32)
    tail_shapes = [f32rows(LANES), f32rows(ATTN_WIDTH), f32rows(D_MODEL), f32rows(D_MODEL)]
    tail_specs = [row(LANES), row(ATTN_WIDTH), row(D_MODEL), row(D_MODEL)]
    if sample:
        out_shape = [f32rows(ATTN_WIDTH)] + [f32rows(KV_WIDTH)] * 6 + tail_shapes + [f32rows(GMLP_WIDTH)]
        out_specs = [row(ATTN_WIDTH)] + [row(KV_WIDTH)] * 6 + tail_specs + [row(GMLP_WIDTH)]
    else:
        nb = T // seq
        bf = lambda *shape: jax.ShapeDtypeStruct(shape, BF16)
        tposed = lambda rows: pl.BlockSpec((1, rows, tm), lambda i: (i // pos_tiles, 0, i % pos_tiles))
        out_shape = ([bf(N_HEADS, T, LANES)] + [jax.ShapeDtypeStruct((nb, KV_WIDTH, seq), F32)] * 6
                     + [f32rows(KV_WIDTH)] * 2
                     + [bf(nb, 2 * LANES, seq), bf(nb, LANES, seq), bf(KV_HEADS, T, LANES), bf(KV_HEADS, T, LANES)]
                     + tail_shapes)
        heads = lambda n: pl.BlockSpec((n, tm, LANES), lambda i: (0, i, 0))
        out_specs = ([heads(N_HEADS)] + [tposed(KV_WIDTH)] * 6 + [row(KV_WIDTH)] * 2
                     + [tposed(2 * LANES), tposed(LANES), heads(KV_HEADS), heads(KV_HEADS)] + tail_specs)
    return pl.pallas_call(
        functools.partial(_inproj_body, tm, sample, pos_tiles),
        grid=(nt,),
        in_specs=in_specs,
        out_specs=out_specs,
        out_shape=out_shape,
        compiler_params=pltpu.CompilerParams(dimension_semantics=("arbitrary",), vmem_limit_bytes=VMEM_LIMIT),
        name="in_project_sample" if sample else "in_project_prompt",
    )(x2d, prm["norm_g"], prm["w_in"], *tables, prm["w_sp_s" if sample else "w_sp"],
      prm["b_sp_s" if sample else "b_sp"], prm["v_norm_g"], prm["v_norm_b"], prm["w_pb"])


def _strided_halfblocks(load_rows):
    return jnp.concatenate([load_rows(t) for t in range(CMP_STRIDE)], axis=1).astype(BF16)


def _compress_mlp(xcat, n, w1_ref, b1_ref, w2_ref, b2_ref):
    hh = _dot(xcat, w1_ref[...])
    hid = jnp.concatenate(
        [hh[:, 0:LANES] + pltpu.roll(hh[:, LANES:2 * LANES], n - 1, 0),
         hh[:, 2 * LANES:3 * LANES] + pltpu.roll(hh[:, 3 * LANES:4 * LANES], n - 1, 0)], axis=1) + b1_ref[...]
    return _dot(_gelu(hid).astype(BF16), w2_ref[...]) + b2_ref[...]


def _compress_prompt_body(n, kr_ref, vr_ref, w1k, b1k, w2k, b2k, w1v, b1v, w2v, b2v, cos_ref, s1_ref, s2_ref,
                          ovl_ref, kct_ref, vco_ref):
    kc = _compress_mlp(_strided_halfblocks(lambda t: kr_ref[0, pl.ds(t, n, stride=CMP_STRIDE), :]),
                       n, w1k, b1k, w2k, b2k)
    kct_ref[0] = _rope(kc, cos_ref[...], s1_ref[...], s2_ref[...]).T.astype(BF16)
    vc = _compress_mlp(_strided_halfblocks(lambda t: vr_ref[0, pl.ds(t, n, stride=CMP_STRIDE), :]),
                       n, w1v, b1v, w2v, b2v)
    vco_ref[0, :, 0:LANES] = vc.astype(BF16)
    vco_ref[0, :, LANES:2 * LANES] = ovl_ref[...]


def _compress_prompt(kr, vr, prm, tables, ovl):
    B, S, _ = kr.shape
    n = S // CMP_STRIDE
    const = lambda *shape: pl.BlockSpec(shape, lambda b: (0,) * len(shape))
    rows = pl.BlockSpec((1, S, LANES), lambda b: (b, 0, 0))
    wspecs = [const(CMP_STRIDE * LANES, 4 * LANES), const(1, 2 * LANES), const(2 * LANES, LANES), const(1, LANES)]
    return pl.pallas_call(
        functools.partial(_compress_prompt_body, n),
        grid=(B,),
        in_specs=[rows, rows] + wspecs + wspecs + [const(n, LANES)] * 3 + [const(n, LANES)],
        out_specs=[pl.BlockSpec((1, LANES, n), lambda b: (b, 0, 0)),
                   pl.BlockSpec((1, n, 2 * LANES), lambda b: (b, 0, 0))],
        out_shape=[jax.ShapeDtypeStruct((B, LANES, n), BF16),
                   jax.ShapeDtypeStruct((B, n, 2 * LANES), BF16)],
        compiler_params=pltpu.CompilerParams(dimension_semantics=("arbitrary",), vmem_limit_bytes=VMEM_LIMIT),
        name="compress_prompt",
    )(kr, vr, *prm["cmp_k"], *prm["cmp_v"], *tables, ovl)


def _start_pages(pt_ref, pools, bufs, sems, b, n_pages):
    def body(p, carry):
        page = pt_ref[b, p]
        for i, (pool, buf, sem) in enumerate(zip(pools, bufs, sems)):
            pltpu.make_async_copy(pool.at[page], buf.at[p], sem).start(priority=i % 2)
        return carry
    lax.fori_loop(0, n_pages, body, 0)


def _wait_pages(pools, bufs, sems, n_pages):
    for pool, buf, sem in zip(pools, bufs, sems):
        pltpu.make_async_copy(pool.at[pl.ds(0, n_pages)], buf, sem).wait()


def _compress_sample_body(n, n_pages, pt_ref, pk_ref, pv_ref, w1k, b1k, w2k, b2k, w1v, b1v, w2v, b2v,
                          cos_ref, s1_ref, s2_ref, perm_ref, kct_ref, vc_ref, kb0, kb1, vb0, vb1, xk, xv, sem):
    s = pl.program_id(0)
    bufs = ((kb0, vb0, 0), (kb1, vb1, 1))

    def start(b, slot):
        kb, vb, si = bufs[slot]
        _start_pages(pt_ref, (pk_ref, pv_ref), (kb, vb), (sem.at[2 * si], sem.at[2 * si + 1]), b, n_pages)

    hb_per_page = PAGE_SIZE // CMP_STRIDE

    def halfblocks(buf, xcat):
        pairs = jnp.concatenate([buf[pl.ds(0, n_pages // 2, stride=2)], buf[pl.ds(1, n_pages // 2, stride=2)]], axis=2)
        y = _dot(pairs.reshape(n_pages // 2 * LANES, 2 * PAGE_SIZE).astype(BF16), perm_ref[...])
        for p in range(n_pages):
            z = y[p // 2 * LANES:(p // 2 + 1) * LANES, p % 2 * PAGE_SIZE:(p % 2 + 1) * PAGE_SIZE].T
            for t in range(CMP_STRIDE):
                xcat[p * hb_per_page:(p + 1) * hb_per_page, t * LANES:(t + 1) * LANES] = (
                    z[t * hb_per_page:(t + 1) * hb_per_page])

    def finish(slot):
        kb, vb, si = bufs[slot]
        _wait_pages((pk_ref, pv_ref), (kb, vb), (sem.at[2 * si], sem.at[2 * si + 1]), n_pages)
        halfblocks(kb, xk)
        halfblocks(vb, xv)
        kc = _compress_mlp(xk[...].astype(BF16), n, w1k, b1k, w2k, b2k)
        kct_ref[slot] = _rope(kc, cos_ref[...], s1_ref[...], s2_ref[...]).T.astype(BF16)
        vc_ref[slot] = _compress_mlp(xv[...].astype(BF16), n, w1v, b1v, w2v, b2v).astype(BF16)

    @pl.when(s == 0)
    def _():
        start(0, 0)

    start(2 * s + 1, 1)
    finish(0)

    @pl.when(s + 1 < pl.num_programs(0))
    def _():
        start(2 * s + 2, 0)

    finish(1)


def _compress_sample(page_table, pool_k, pool_v, prm, tables):
    Bd, n_pages = page_table.shape
    past = n_pages * PAGE_SIZE
    n = past // CMP_STRIDE
    const = lambda *shape: pl.BlockSpec(shape, lambda s, pt: (0,) * len(shape))
    wspecs = [const(CMP_STRIDE * LANES, 4 * LANES), const(1, 2 * LANES), const(2 * LANES, LANES), const(1, LANES)]
    anyspec = pl.BlockSpec(memory_space=pl.ANY)
    grid_spec = pltpu.PrefetchScalarGridSpec(
        num_scalar_prefetch=1,
        grid=(Bd // 2,),
        in_specs=[anyspec, anyspec] + wspecs + wspecs + [const(n, LANES)] * 3 + [const(2 * PAGE_SIZE, 2 * PAGE_SIZE)],
        out_specs=[pl.BlockSpec((2, LANES, n), lambda s, pt: (s, 0, 0)),
                   pl.BlockSpec((2, n, LANES), lambda s, pt: (s, 0, 0))],
        scratch_shapes=[pltpu.VMEM((n_pages, LANES, PAGE_SIZE), F32)] * 4
        + [pltpu.VMEM((n, CMP_STRIDE * LANES), F32)] * 2 + [pltpu.SemaphoreType.DMA((4,))],
    )
    hb = PAGE_SIZE // CMP_STRIDE
    perm = np.zeros((2 * PAGE_SIZE, 2 * PAGE_SIZE), np.float32)
    for side in range(2):
        for jj in range(hb):
            for t in range(CMP_STRIDE):
                perm[side * PAGE_SIZE + CMP_STRIDE * jj + t, side * PAGE_SIZE + t * hb + jj] = 1.0
    return pl.pallas_call(
        functools.partial(_compress_sample_body, n, n_pages),
        grid_spec=grid_spec,
        out_shape=[jax.ShapeDtypeStruct((Bd, LANES, n), BF16), jax.ShapeDtypeStruct((Bd, n, LANES), BF16)],
        compiler_params=pltpu.CompilerParams(dimension_semantics=("arbitrary",), vmem_limit_bytes=VMEM_LIMIT),
        name="compress_sample",
    )(page_table, pool_k, pool_v, *prm["cmp_k"], *prm["cmp_v"], *tables, jnp.asarray(perm, BF16))


def _mixer_tail(o, sza, sra, gpb, x, wpa_ref, wo_ref, fg_ref):
    pa = _dot((o * sza).astype(BF16), wpa_ref[...])
    merged = sra * pa + gpb
    hn = x + _dot(merged.astype(BF16), wo_ref[...])
    r = lax.rsqrt(jnp.mean(hn * hn, axis=-1, keepdims=True) + EPS)
    return (hn * r) * fg_ref[...]


def _topk_mask(score, blk, n_blocks, axis):
    cnt = jnp.zeros(score.shape, jnp.int32)
    for sp in range(n_blocks):
        b = lax.slice_in_dim(score, sp, sp + 1, axis=axis)
        ge = jnp.where(b >= score, 1, 0)
        gt = jnp.where(b > score, 1, 0)
        cnt = cnt + jnp.where(blk > sp, ge, gt)
    return cnt < TOP_N


def _attn_prompt_body(n_sel, q_ref, ka_ref, vs_ref, kw_ref, vw_ref, kc_ref, vco_ref, gat_ref, sza_ref, sra_ref,
                      gpb_ref, x_ref, wpa_ref, wo_ref, fg_ref, y_ref):
    i = pl.program_id(1)
    M = HPG * TQ
    G = range(KV_HEADS)
    qpos = i * TQ + (_iota((M, 1), 0) & (TQ - 1))
    n_cmp_pad = kc_ref.shape[2]
    gt = gat_ref[...]
    qs = [q_ref[HPG * g:HPG * (g + 1)].reshape(M, LANES) for g in G]

    mk = (_iota((M, n_cmp_pad), 1) * CMP_STRIDE + (CMP_BLOCK - 1)) <= qpos
    o_cmp, qa = [], []
    blk = _iota((n_sel, TQ), 0)
    cur = (i * TQ + _iota((n_sel, TQ), 1)) // SEL_BLOCK
    valid = blk <= cur
    forced = (valid & (blk > cur - N_LOCAL_BLOCKS)) | (blk < N_INIT_BLOCKS)
    scores = []
    for g in G:
        s = jnp.where(mk, _dot(qs[g], kc_ref[0]), NEG)
        e = jnp.exp2(s - jnp.max(s, axis=-1, keepdims=True))
        p = jnp.where(mk, e / jnp.sum(e, axis=-1, keepdims=True), 0.0)
        r = _dot(p.astype(BF16), vco_ref[0])
        o_cmp.append(r[:, 0:LANES])
        impc = r[:, LANES:2 * LANES]
        imp = impc[0:TQ] + impc[TQ:2 * TQ] + impc[2 * TQ:3 * TQ] + impc[3 * TQ:4 * TQ]
        sc = imp.T[HEAD_DIM:HEAD_DIM + n_sel]
        scores.append(jnp.where(forced, BIG, jnp.where(valid, sc, NEG)))

    last_blk = ((i + 1) * TQ - 1) // SEL_BLOCK

    def count_group(k, cnts):
        out = []
        for g in G:
            c = cnts[g]
            for sp in range(k * RANK_GROUP, (k + 1) * RANK_GROUP):
                b = scores[g][sp:sp + 1]
                c = c + jnp.where(blk > sp, jnp.where(b >= scores[g], 1, 0), jnp.where(b > scores[g], 1, 0))
            out.append(c)
        return tuple(out)

    cnts = (jnp.zeros((n_sel, TQ), jnp.int32),) * KV_HEADS
    for k in range(n_sel // RANK_GROUP):
        cnts = lax.cond(k * RANK_GROUP <= last_blk, functools.partial(count_group, k), lambda c: c, cnts)
    for g in G:
        selneg = jnp.where(cnts[g] < TOP_N, 0.0, NEG)
        seln_t = jnp.concatenate([selneg, jnp.zeros((LANES - n_sel, TQ), F32)], axis=0).T.astype(BF16)
        qa.append(jnp.concatenate([qs[g], jnp.concatenate([seln_t] * HPG, axis=0)], axis=1))

    def update(carry, s, v):
        m, acc = carry
        mn = jnp.maximum(m, jnp.max(s, axis=-1, keepdims=True))
        acc = jnp.exp2(m - mn) * acc + _dot(jnp.exp2((s - mn).astype(BF16)), v)
        return mn, acc

    def step(kt, carries, causal):
        off = pl.multiple_of(kt * KT, KT)
        k = ka_ref[0, :, pl.ds(off, KT)]
        ss = [_dot(qa[g], k) for g in G]
        if causal:
            keep = kt * KT + _iota((M, KT), 1) <= qpos
            ss = [jnp.where(keep, s, NEG) for s in ss]
        return tuple(update(carries[g], ss[g], vs_ref[g, pl.ds(off, KT), :]) for g in G)

    n_full = i // (KT // TQ)
    init = (jnp.full((M, 1), NEG, F32), jnp.zeros((M, LANES), F32))
    carries = lax.fori_loop(0, n_full // 2, lambda kp, c: step(2 * kp + 1, step(2 * kp, c, False), False),
                            (init,) * KV_HEADS)
    carries = lax.cond(n_full % 2 == 1, lambda c: step(n_full - 1, c, False), lambda c: c, carries)
    carries = step(n_full, carries, True)
    sum_lane = [(KV_HEADS - 1 - g) * HEAD_DIM for g in G]
    o_sel = [carries[g][1] / carries[g][1][:, sum_lane[g]:sum_lane[g] + 1] for g in G]

    st = pl.multiple_of(jnp.maximum(i * TQ - WINDOW, 0), TQ)
    kpos = st + _iota((M, WIN_KEYS), 1)
    mkw = (kpos <= qpos) & (kpos > qpos - WINDOW)
    kw = kw_ref[0, :, pl.ds(st, WIN_KEYS)]
    o_groups = []
    for g in G:
        s = jnp.where(mkw, _dot(qs[g], kw), NEG)
        e = jnp.exp2(s - jnp.max(s, axis=-1, keepdims=True))
        r = _dot(e.astype(BF16), vw_ref[g, pl.ds(st, WIN_KEYS), :])
        o_win = r / r[:, sum_lane[g]:sum_lane[g] + 1]

        def gcol(c, g=g):
            return jnp.concatenate(
                [gt[:, 3 * (HPG * g + j) + c:3 * (HPG * g + j) + c + 1] for j in range(HPG)], axis=0)

        o_groups.append(gcol(0) * o_cmp[g] + gcol(1) * o_sel[g] + gcol(2) * o_win)

    low = _iota((TQ, LANES), 1) < HEAD_DIM
    slabs = []
    for pp in range(N_HEADS // 2):
        g, j0 = pp // 2, (2 * pp) % HPG
        a = o_groups[g][j0 * TQ:(j0 + 1) * TQ]
        b = o_groups[g][(j0 + 1) * TQ:(j0 + 2) * TQ]
        if g == 0:
            b = pltpu.roll(b, HEAD_DIM, 1)
        else:
            a = pltpu.roll(a, HEAD_DIM, 1)
        slabs.append(jnp.where(low, a, b))
    o = jnp.concatenate(slabs, axis=1)
    y_ref[...] = _mixer_tail(o, sza_ref[...], sra_ref[...], gpb_ref[...], x_ref[...], wpa_ref, wo_ref, fg_ref)


def _attn_prompt(B, S, q_hm, kaug, vsb, kwt, vwb, kct, vco, gates, sza, sra, gpb, x2d, prm):
    nq = S // TQ
    n_sel = S // SEL_BLOCK
    n_cmp_pad = kct.shape[2]
    row = lambda width: pl.BlockSpec((TQ, width), lambda b, i: (b * nq + i, 0))
    const = lambda *shape: pl.BlockSpec(shape, lambda b, i: (0,) * len(shape))
    batch = lambda *shape: pl.BlockSpec((1,) + shape, lambda b, i: (b,) + (0,) * len(shape))
    in_specs = [
        pl.BlockSpec((N_HEADS, TQ, LANES), lambda b, i: (0, b * nq + i, 0)),
        batch(2 * LANES, S),
        pl.BlockSpec((KV_HEADS, S, LANES), lambda b, i: (0, b, 0)),
        batch(LANES, S),
        pl.BlockSpec((KV_HEADS, S, LANES), lambda b, i: (0, b, 0)),
        batch(LANES, n_cmp_pad),
        batch(n_cmp_pad, 2 * LANES),
        row(LANES), row(ATTN_WIDTH), row(D_MODEL), row(D_MODEL), row(D_MODEL),
        const(ATTN_WIDTH, D_MODEL), const(D_MODEL, D_MODEL), const(1, D_MODEL),
    ]
    return pl.pallas_call(
        functools.partial(_attn_prompt_body, n_sel),
        grid=(B, nq),
        in_specs=in_specs,
        out_specs=row(D_MODEL),
        out_shape=jax.ShapeDtypeStruct((B * S, D_MODEL), F32),
        compiler_params=pltpu.CompilerParams(dimension_semantics=("arbitrary", "arbitrary"),
                                             vmem_limit_bytes=VMEM_LIMIT),
        name="attn_prompt",
    )(q_hm, kaug, vsb, kwt, vwb, kct, vco, gates, sza, sra, gpb, x2d, prm["w_pa"], prm["w_o"], prm["final_g"])


def _attn_sample_body(n_pages, tn, pt_ref, pk_ref, pv_ref, qa_ref, kc_ref, vc_ref, ovl_ref, oh_ref, kt_ref, vt_ref,
                      ckw_ref, cvw_ref, kwn_ref, vwn_ref, o_ref, kb0, kb1, vb0, vb1, sem):
    s_id = pl.program_id(0)
    past = n_pages * PAGE_SIZE
    n_past_blk = past // SEL_BLOCK
    n_blk_pad = ovl_ref.shape[1]
    R = N_HEADS * TOK_PAD
    GR = HPG * TOK_PAD
    wc = ckw_ref.shape[2]
    bufs = ((kb0, vb0, 0), (kb1, vb1, 1))
    t_row = _iota((R, 1), 0) & (TOK_PAD - 1)

    def start(b, slot):
        kb, vb, si = bufs[slot]
        _start_pages(pt_ref, (pk_ref, pv_ref), (kb, vb), (sem.at[2 * si], sem.at[2 * si + 1]), b, n_pages)

    def finish(slot):
        kb, vb, si = bufs[slot]
        qa = qa_ref[slot]

        n_cmp_pad = kc_ref.shape[2]
        s = _dot(qa, kc_ref[slot])
        mk = _iota((R, n_cmp_pad), 1) < n_cmp_pad - 1
        s = jnp.where(mk, s, NEG)
        e = jnp.exp(s - jnp.max(s, axis=-1, keepdims=True))
        p = jnp.where(mk, e / jnp.sum(e, axis=-1, keepdims=True), 0.0).astype(BF16)
        o_cmp = _dot(p, vc_ref[slot])
        impc = _dot(p, ovl_ref[...])
        imp = jnp.concatenate(
            [impc[g * GR:g * GR + TOK_PAD] + impc[g * GR + TOK_PAD:g * GR + 2 * TOK_PAD]
             + impc[g * GR + 2 * TOK_PAD:g * GR + 3 * TOK_PAD] + impc[g * GR + 3 * TOK_PAD:g * GR + 4 * TOK_PAD]
             for g in range(KV_HEADS)], axis=0)
        nr = KV_HEADS * TOK_PAD
        blk = _iota((nr, n_blk_pad), 1)
        cur = (past + jnp.minimum(_iota((nr, n_blk_pad), 0) & (TOK_PAD - 1), tn - 1)) // SEL_BLOCK
        valid = blk <= cur
        forced = (valid & (blk > cur - N_LOCAL_BLOCKS)) | (blk < N_INIT_BLOCKS)
        score = jnp.where(forced, BIG, jnp.where(valid, imp, NEG))
        sel = _topk_mask(score, blk, n_past_blk + 1, 1)
        selneg = jnp.where(sel, 0.0, NEG)
        selneg = jnp.concatenate(
            [selneg[g * TOK_PAD:(g + 1) * TOK_PAD] for g in range(KV_HEADS) for _ in range(HPG)], axis=0)
        low = _iota((R, LANES), 1) < SEL_HALF
        qaug = []
        for hf in range(n_past_blk // SEL_HALF):
            slab = selneg[:, (hf // 2) * LANES:(hf // 2 + 1) * LANES]
            if hf % 2:
                slab = pltpu.roll(slab, SEL_HALF, 1)
            qaug.append(jnp.concatenate([qa, jnp.where(low, slab, 0.0).astype(BF16)], axis=1))

        _wait_pages((pk_ref, pv_ref), (kb, vb), (sem.at[2 * si], sem.at[2 * si + 1]), n_pages)

        pages_per_half = SEL_HALF * SEL_BLOCK // PAGE_SIZE

        def half(buf, hf):
            return jnp.concatenate(
                [buf[hf * pages_per_half + j] for j in range(pages_per_half)], axis=1).astype(BF16)

        col = _iota((R, LANES), 1)
        ss = [_dot(qaug[hf], jnp.concatenate([half(kb, hf), oh_ref[...]], axis=0)) for hf in range(len(qaug))]
        ss.append(jnp.where(col <= t_row, _nt(qa, kt_ref[slot]), NEG))
        m = functools.reduce(jnp.maximum, [jnp.max(s, axis=-1, keepdims=True) for s in ss])
        es = [jnp.exp(s - m) for s in ss]
        l = functools.reduce(jnp.add, [jnp.sum(e, axis=-1, keepdims=True) for e in es])
        acc = _dot(es[-1].astype(BF16), vt_ref[slot])
        for hf in range(len(qaug)):
            acc = acc + _nt(es[hf].astype(BF16), half(vb, hf))
        o_sel = acc / l

        sa = jnp.where(_iota((R, wc), 1) > t_row, _dot(qa, ckw_ref[slot].astype(BF16)), NEG)
        sb = jnp.where(col <= t_row, _nt(qa, kwn_ref[slot]), NEG)
        m = jnp.maximum(jnp.max(sa, axis=-1, keepdims=True), jnp.max(sb, axis=-1, keepdims=True))
        ea, eb = jnp.exp(sa - m), jnp.exp(sb - m)
        den = jnp.sum(ea, axis=-1, keepdims=True) + jnp.sum(eb, axis=-1, keepdims=True)
        o_win = (_nt(ea.astype(BF16), cvw_ref[slot].astype(BF16)) + _dot(eb.astype(BF16), vwn_ref[slot])) / den

        o_ref[slot, 0] = o_cmp
        o_ref[slot, 1] = o_sel
        o_ref[slot, 2] = o_win

    @pl.when(s_id == 0)
    def _():
        start(0, 0)

    start(2 * s_id + 1, 1)
    finish(0)

    @pl.when(s_id + 1 < pl.num_programs(0))
    def _():
        start(2 * s_id + 2, 0)

    finish(1)


def _attn_sample(page_table, pool_k, pool_v, tn, qa, kct, vc, ovl, ktail, vtail, ckw, cvw, kwn, vwn):
    Bd, n_pages = page_table.shape
    R = N_HEADS * TOK_PAD
    wc = ckw.shape[2]
    const = lambda *shape: pl.BlockSpec(shape, lambda s, pt: (0,) * len(shape))
    pair = lambda *shape: pl.BlockSpec((2,) + shape, lambda s, pt: (s,) + (0,) * len(shape))
    anyspec = pl.BlockSpec(memory_space=pl.ANY)
    half_keys = SEL_HALF * SEL_BLOCK
    onehot = np.zeros((LANES, half_keys), np.float32)
    onehot[np.arange(half_keys) // SEL_BLOCK, np.arange(half_keys)] = 1.0
    grid_spec = pltpu.PrefetchScalarGridSpec(
        num_scalar_prefetch=1,
        grid=(Bd // 2,),
        in_specs=[anyspec, anyspec, pair(R, LANES), pair(LANES, kct.shape[2]), pair(vc.shape[1], LANES),
                  const(*ovl.shape), const(LANES, half_keys), pair(LANES, LANES), pair(LANES, LANES),
                  pair(LANES, wc), pair(LANES, wc), pair(LANES, LANES), pair(LANES, LANES)],
        out_specs=pair(3, R, LANES),
        scratch_shapes=[pltpu.VMEM((n_pages, LANES, PAGE_SIZE), F32)] * 4 + [pltpu.SemaphoreType.DMA((4,))],
    )
    return pl.pallas_call(
        functools.partial(_attn_sample_body, n_pages, tn),
        grid_spec=grid_spec,
        out_shape=jax.ShapeDtypeStruct((Bd, 3, R, LANES), F32),
        compiler_params=pltpu.CompilerParams(dimension_semantics=("arbitrary",), vmem_limit_bytes=VMEM_LIMIT),
        name="attn_sample",
    )(page_table, pool_k, pool_v, qa, kct, vc, ovl, jnp.asarray(onehot, BF16), ktail, vtail, ckw, cvw, kwn, vwn)


def _mixer_sample_body(o3_ref, g3_ref, sza_ref, sra_ref, gpb_ref, x_ref, wpa_ref, wo_ref, fg_ref, y_ref):
    o = g3_ref[0] * o3_ref[0] + g3_ref[1] * o3_ref[1] + g3_ref[2] * o3_ref[2]
    y_ref[...] = _mixer_tail(o, sza_ref[...], sra_ref[...], gpb_ref[...], x_ref[...], wpa_ref, wo_ref, fg_ref)


def _mixer_sample(o3, g3, sza, sra, gpb, x2d, prm):
    T = x2d.shape[0]
    full = lambda *shape: pl.BlockSpec(shape, lambda i: (0,) * len(shape))
    return pl.pallas_call(
        _mixer_sample_body,
        grid=(1,),
        in_specs=[full(3, T, ATTN_WIDTH), full(3, T, ATTN_WIDTH), full(T, ATTN_WIDTH), full(T, D_MODEL),
                  full(T, D_MODEL), full(T, D_MODEL), full(ATTN_WIDTH, D_MODEL), full(D_MODEL, D_MODEL),
                  full(1, D_MODEL)],
        out_specs=full(T, D_MODEL),
        out_shape=jax.ShapeDtypeStruct((T, D_MODEL), F32),
        compiler_params=pltpu.CompilerParams(dimension_semantics=("arbitrary",), vmem_limit_bytes=VMEM_LIMIT),
        name="mixer_sample",
    )(o3, g3, sza, sra, gpb, x2d, prm["w_pa"], prm["w_o"], prm["final_g"])


def _overlap(n_cmp, n_sel):
    cs = np.arange(n_cmp)[:, None] * CMP_STRIDE
    ss = np.arange(n_sel)[None, :] * SEL_BLOCK
    ov = np.minimum(cs + CMP_BLOCK, ss + SEL_BLOCK) - np.maximum(cs, ss)
    return np.clip(ov, 0, None).astype(np.float32) / CMP_BLOCK


def _cmp_weights(w1, b1, w2, b2):
    w1r = w1.reshape(2, CMP_STRIDE, HEAD_DIM, CMP_HIDDEN).transpose(1, 2, 0, 3)
    big = jnp.zeros((CMP_STRIDE, KV_HEADS, HEAD_DIM, KV_HEADS, 2, CMP_HIDDEN), F32)
    w2b = jnp.zeros((KV_HEADS, CMP_HIDDEN, KV_HEADS, HEAD_DIM), F32)
    for g in range(KV_HEADS):
        big = big.at[:, g, :, g].set(w1r)
        w2b = w2b.at[g, :, g].set(w2)
    return (big.reshape(CMP_STRIDE * LANES, 4 * LANES).astype(BF16), jnp.tile(b1, KV_HEADS)[None],
            w2b.reshape(KV_HEADS * CMP_HIDDEN, LANES).astype(BF16), jnp.tile(b2, KV_HEADS)[None])


def _seq_minor(t):
    lead = t.shape[:-3]
    n = len(lead)
    return t.transpose(*range(n), n + 1, n + 2, n).reshape(*lead, KV_WIDTH, t.shape[-3])


def _seq_major(t):
    lead = t.shape[:-2]
    n = len(lead)
    return t.reshape(*lead, KV_HEADS, HEAD_DIM, t.shape[-1]).transpose(*range(n), n + 2, n, n + 1)


def kernel(x_prompt, x_sample, cache_k_cmp, cache_v_cmp, cache_k_sel, cache_v_sel, cache_k_win, cache_v_win, page_table, norm_g, w_in, cmp_k_w1, cmp_k_b1, cmp_k_w2, cmp_k_b2, cmp_v_w1, cmp_v_b1, cmp_v_w2, cmp_v_b2, v_norm_g, v_norm_b, w_spatial, b_spatial, w_pa, w_pb, w_o, final_g):
    B, S, _ = x_prompt.shape
    Bd, tn, _ = x_sample.shape
    depth = w_in.shape[0]
    assert depth == 1, "single-layer step"
    assert Bd * tn == CHUNK, "the sample tokens form one 128-row tile"
    n_pages = page_table.shape[1]
    past = n_pages * PAGE_SIZE

    split = OFF_GL + 3 * N_HEADS
    w = w_in[0]
    prm = {
        "norm_g": norm_g,
        "w_in": jnp.concatenate([w[:, :split], jnp.zeros((D_MODEL, GL_PAD), F32), w[:, split:]], axis=1).astype(BF16),
        "w_sp": w_spatial[0],
        "b_sp": jnp.repeat(b_spatial[0].T, LANES, axis=1),
        "w_sp_s": jnp.stack([jnp.kron(jnp.eye(CHUNK // tn, dtype=F32), w_spatial[0, g, :tn, :tn])
                             for g in range(GMLP_GROUPS)]),
        "b_sp_s": jnp.tile(jnp.repeat(b_spatial[0, :, :tn].T, LANES, axis=1), (CHUNK // tn, 1)),
        "v_norm_g": v_norm_g, "v_norm_b": v_norm_b,
        "w_pb": w_pb[0].astype(BF16), "w_pa": w_pa[0].astype(BF16), "w_o": w_o[0].astype(BF16),
        "final_g": final_g[None],
        "cmp_k": _cmp_weights(cmp_k_w1[0], cmp_k_b1[0], cmp_k_w2[0], cmp_k_b2[0]),
        "cmp_v": _cmp_weights(cmp_v_w1[0], cmp_v_b1[0], cmp_v_w2[0], cmp_v_b2[0]),
    }

    xp = x_prompt.reshape(B * S, D_MODEL)
    (q_hm, p_kcmp, p_vcmp, p_ksel, p_vsel, p_kwin, p_vwin, kcr, vcr, kaug, kwt, vsb, vwb,
     gates, sza, sra, gpb) = _in_project(xp, _rope_tables(jnp.arange(S)), prm, TM_PROMPT, False, S)
    n_half = S // CMP_STRIDE
    n_cmp = (S - CMP_BLOCK) // CMP_STRIDE + 1
    n_sel = S // SEL_BLOCK
    assert n_sel <= SEL_HALF
    ovl = np.zeros((n_half, LANES), np.float32)
    ovl[:n_cmp, HEAD_DIM:HEAD_DIM + n_sel] = _overlap(n_cmp, n_sel)
    cend_tables = _rope_tables(jnp.arange(n_half) * CMP_STRIDE + CMP_BLOCK - 1)
    kct, vco = _compress_prompt(kcr.reshape(B, S, LANES), vcr.reshape(B, S, LANES), prm, cend_tables,
                                jnp.asarray(ovl, BF16))
    y_prompt = _attn_prompt(B, S, q_hm, kaug, vsb, kwt, vwb, kct, vco, gates, sza, sra, gpb, xp, prm)

    xs = x_sample.reshape(Bd * tn, D_MODEL)
    pos_s = jnp.tile(past + jnp.arange(tn), Bd)
    (q_s, s_kcmp, s_vcmp, s_ksel, s_vsel, s_kwin, s_vwin, gates_s, sza_s, sra_s, gpb_s, vn_s) = _in_project(
        xs, _rope_tables(pos_s), prm, Bd * tn, True, tn)
    n_half_s = past // CMP_STRIDE
    pools = [_seq_minor(c[0]) for c in (cache_k_cmp, cache_v_cmp, cache_k_sel, cache_v_sel)]
    cend_s = _rope_tables(jnp.arange(n_half_s) * CMP_STRIDE + CMP_BLOCK - 1)
    kct_s, vc_s = _compress_sample(page_table, pools[0], pools[1], prm, cend_s)

    n_cmp_s = (past + tn - CMP_BLOCK) // CMP_STRIDE + 1
    n_blk_s = past // SEL_BLOCK + -(-tn // SEL_BLOCK)
    assert n_cmp_s == n_half_s - 1 and tn <= TOK_PAD and (past // SEL_BLOCK) % SEL_HALF == 0
    ovl_s = np.zeros((n_half_s, 2 * LANES), np.float32)
    ovl_s[:n_cmp_s, :n_blk_s] = _overlap(n_cmp_s, n_blk_s)
    q5 = q_s.reshape(Bd, tn, KV_HEADS, HPG, HEAD_DIM).transpose(0, 2, 3, 1, 4)
    q5 = jnp.pad(q5, ((0, 0), (0, 0), (0, 0), (0, TOK_PAD - tn), (0, 0))).reshape(Bd, KV_HEADS, HPG * TOK_PAD, HEAD_DIM)
    qa = jnp.zeros((Bd, KV_HEADS, HPG * TOK_PAD, KV_HEADS, HEAD_DIM), F32)
    for g in range(KV_HEADS):
        qa = qa.at[:, g, :, g].set(q5[:, g])
    qa = qa.reshape(Bd, N_HEADS * TOK_PAD, LANES).astype(BF16)
    pad_rows = lambda t: jnp.pad(t.reshape(Bd, tn, LANES), ((0, 0), (0, LANES - tn), (0, 0))).astype(BF16)
    ckw = _seq_minor(cache_k_win[0])
    cvw = _seq_minor(cache_v_win[0])
    o3 = _attn_sample(page_table, pools[2], pools[3], tn, qa, kct_s, vc_s, jnp.asarray(ovl_s, BF16),
                      pad_rows(s_ksel), pad_rows(s_vsel), ckw, cvw, pad_rows(s_kwin), pad_rows(s_vwin))
    new_t = lambda t: t.reshape(Bd, tn, LANES).transpose(0, 2, 1)
    s_k_win = _seq_major(jnp.concatenate([ckw[:, :, tn:], new_t(s_kwin)], axis=2))[None]
    s_v_win = _seq_major(jnp.concatenate([cvw[:, :, tn:], new_t(s_vwin)], axis=2))[None]
    first_group = jnp.arange(N_HEADS * TOK_PAD)[:, None] < HPG * TOK_PAD
    o3r = jnp.where(first_group, o3[..., :HEAD_DIM], o3[..., HEAD_DIM:])
    o3r = o3r.reshape(Bd, 3, N_HEADS, TOK_PAD, HEAD_DIM)[:, :, :, :tn]
    o3r = o3r.transpose(1, 0, 3, 2, 4).reshape(3, Bd * tn, ATTN_WIDTH)
    g3 = gates_s[:, :3 * N_HEADS].reshape(Bd * tn, N_HEADS, 3).transpose(2, 0, 1)
    g3 = jnp.repeat(g3, HEAD_DIM, axis=2)
    y_sample = _mixer_sample(o3r, g3, sza_s, sra_s, gpb_s, xs, prm)

    kv5 = lambda t, b, n: t.reshape(1, b, n, KV_HEADS, HEAD_DIM)
    pw = min(WINDOW, S)
    return (y_prompt.reshape(B, S, D_MODEL), y_sample.reshape(Bd, tn, D_MODEL),
            _seq_major(p_kcmp)[None], _seq_major(p_vcmp)[None], _seq_major(p_ksel)[None], _seq_major(p_vsel)[None],
            _seq_major(p_kwin[:, :, S - pw:])[None], _seq_major(p_vwin[:, :, S - pw:])[None],
            kv5(s_kcmp, Bd, tn), kv5(s_vcmp, Bd, tn), kv5(s_ksel, Bd, tn), kv5(s_vsel, Bd, tn),
            s_k_win, s_v_win, vn_s.reshape(1, Bd, tn, GMLP_WIDTH))
```

```python
import functools

import numpy as np
import jax
import jax.numpy as jnp
from jax import lax
from jax.experimental import pallas as pl
from jax.experimental.pallas import tpu as pltpu

F32 = jnp.float32
BF16 = jnp.bfloat16

D_MODEL = 1024
HEAD_DIM = 64
N_HEADS = 8
KV_HEADS = 2
HPG = N_HEADS // KV_HEADS
ATTN_WIDTH = N_HEADS * HEAD_DIM
KV_WIDTH = KV_HEADS * HEAD_DIM
ROT_DIM = HEAD_DIM // 4
ROT_HALF = ROT_DIM // 2
ROPE_THETA = 500000.0
CMP_BLOCK = 32
CMP_STRIDE = 16
CMP_HIDDEN = 128
SEL_BLOCK = 64
TOP_N = 16
N_INIT_BLOCKS = 1
N_LOCAL_BLOCKS = 2
WINDOW = 512
CHUNK = 128
GMLP_GROUPS = 4
GMLP_WIDTH = 512
PAGE_SIZE = 128
NEG = -1e30
BIG = 1e30
EPS = 1e-6
LOG2E = 1.4426950408889634

LANES = 128
GL_PAD = LANES - 3 * N_HEADS

OFF_Q = 0
OFF_KV = ATTN_WIDTH
OFF_GL = OFF_KV + 6 * KV_WIDTH
OFF_ZA = OFF_GL + LANES
OFF_U = OFF_ZA + ATTN_WIDTH
OFF_V = OFF_U + GMLP_WIDTH
OFF_ZB = OFF_V + GMLP_WIDTH
OFF_RA = OFF_ZB + GMLP_WIDTH
OFF_RB = OFF_RA + D_MODEL
W_TOT = OFF_RB + D_MODEL

VMEM_LIMIT = 56 * 1024 * 1024

TM_PROMPT = 512
TQ = 128
KT = 512
WIN_KEYS = WINDOW + TQ
SEL_HALF = 64
TOK_PAD = 8
RANK_GROUP = 8


def _nt(a, b):
    return lax.dot_general(a, b, (((1,), (1,)), ((), ())), preferred_element_type=F32)


def _dot(a, b):
    return jnp.dot(a, b, preferred_element_type=F32)


def _iota(shape, dim):
    return lax.broadcasted_iota(jnp.int32, shape, dim)


def _rope(slab, cos, s1, s2):
    return slab * cos + pltpu.roll(slab, LANES - ROT_HALF, 1) * s1 + pltpu.roll(slab, ROT_HALF, 1) * s2


def _rope_tables(pos):
    pos = np.asarray(pos, np.float64)
    n = pos.shape[0]
    inv = np.power(np.float64(ROPE_THETA), -np.arange(0, ROT_DIM, 2, dtype=np.float64) / ROT_DIM)
    ang = pos[:, None] * inv[None, :]
    cos, sin = np.cos(ang), np.sin(ang)
    rest = HEAD_DIM - ROT_DIM
    c = np.concatenate([cos, cos, np.ones((n, rest))], axis=1)
    s1 = np.concatenate([-sin, np.zeros((n, HEAD_DIM - ROT_HALF))], axis=1)
    s2 = np.concatenate([np.zeros((n, ROT_HALF)), sin, np.zeros((n, rest))], axis=1)
    return tuple(jnp.asarray(np.tile(t, (1, LANES // HEAD_DIM)), F32) for t in (c, s1, s2))


def _sigmoid(x):
    return 1.0 / (1.0 + jnp.exp(-x))


def _gelu(x):
    return jax.nn.gelu(x, approximate=True)


def _inproj_body(tm, sample, pos_tiles, x_ref, ng_ref, w_ref, cos_ref, s1_ref, s2_ref, wsp_ref, bsp_ref,
                 vng_ref, vnb_ref, wpb_ref, *outs):
    if sample:
        (q_ref, kcmp_ref, vcmp_ref, ksel_ref, vsel_ref, kwin_ref, vwin_ref,
         gat_ref, sza_ref, sra_ref, gpb_ref, vn_ref) = outs
    else:
        (q_ref, kcmp_ref, vcmp_ref, ksel_ref, vsel_ref, kwin_ref, vwin_ref,
         kcr_ref, vcr_ref, kaug_ref, kwt_ref, vsb_ref, vwb_ref, gat_ref, sza_ref, sra_ref, gpb_ref) = outs

    x = x_ref[...]
    r = lax.rsqrt(jnp.mean(x * x, axis=-1, keepdims=True) + EPS)
    h = ((x * r) * ng_ref[...]).astype(BF16)

    def proj(lo, hi):
        return _nt(h, w_ref[lo:hi, :])

    cos, s1, s2 = cos_ref[...], s1_ref[...], s2_ref[...]
    low = _iota((tm, LANES), 1) < HEAD_DIM

    q = proj(OFF_Q, OFF_Q + ATTN_WIDTH)
    q_scale = HEAD_DIM ** -0.5 if sample else HEAD_DIM ** -0.5 * LOG2E
    for pp in range(N_HEADS // 2):
        slab = _rope(q[:, pp * LANES:(pp + 1) * LANES], cos, s1, s2) * q_scale
        if sample:
            q_ref[:, pp * LANES:(pp + 1) * LANES] = slab
        elif (2 * pp) // HPG == 0:
            q_ref[2 * pp] = jnp.where(low, slab, 0.0).astype(BF16)
            q_ref[2 * pp + 1] = jnp.where(low, pltpu.roll(slab, HEAD_DIM, 1), 0.0).astype(BF16)
        else:
            q_ref[2 * pp] = jnp.where(low, 0.0, pltpu.roll(slab, HEAD_DIM, 1)).astype(BF16)
            q_ref[2 * pp + 1] = jnp.where(low, 0.0, slab).astype(BF16)

    kv = proj(OFF_KV, OFF_KV + 6 * KV_WIDTH)
    kcmp = kv[:, 0:LANES]
    vcmp = kv[:, LANES:2 * LANES]
    ksel = _rope(kv[:, 2 * LANES:3 * LANES], cos, s1, s2)
    vsel = kv[:, 3 * LANES:4 * LANES]
    kwin = _rope(kv[:, 4 * LANES:5 * LANES], cos, s1, s2)
    vwin = kv[:, 5 * LANES:6 * LANES]
    if sample:
        for ref, val in ((kcmp_ref, kcmp), (vcmp_ref, vcmp), (ksel_ref, ksel), (vsel_ref, vsel),
                         (kwin_ref, kwin), (vwin_ref, vwin)):
            ref[...] = val
    else:
        ksel_t = ksel.T
        kwin_t = kwin.T
        for ref, val in ((kcmp_ref, kcmp.T), (vcmp_ref, vcmp.T), (ksel_ref, ksel_t), (vsel_ref, vsel.T),
                         (kwin_ref, kwin_t), (vwin_ref, vwin.T)):
            ref[0] = val
        kcr_ref[...] = kcmp
        vcr_ref[...] = vcmp
        base = (pl.program_id(0) % pos_tiles) * tm
        blk = (base + _iota((LANES, tm), 1)) // SEL_BLOCK
        onehot = jnp.where(_iota((LANES, tm), 0) == blk, 1.0, 0.0)
        kaug_ref[0, 0:LANES] = ksel_t.astype(BF16)
        kaug_ref[0, LANES:2 * LANES] = onehot.astype(BF16)
        kwt_ref[0] = kwin_t.astype(BF16)
        for ref, val in ((vsb_ref, vsel), (vwb_ref, vwin)):
            ref[0] = jnp.where(low, val, 1.0).astype(BF16)
            ref[1] = jnp.where(low, 1.0, val).astype(BF16)

    gat_ref[...] = _sigmoid(proj(OFF_GL, OFF_GL + LANES))
    za = proj(OFF_ZA, OFF_ZA + ATTN_WIDTH)
    sza_ref[...] = za * _sigmoid(za)
    sra_ref[...] = _sigmoid(proj(OFF_RA, OFF_RA + D_MODEL))

    v = proj(OFF_V, OFF_V + GMLP_WIDTH)
    gv = _gelu(v)
    mu = jnp.mean(gv, axis=-1, keepdims=True)
    var = jnp.mean(jnp.square(gv - mu), axis=-1, keepdims=True)
    vn = ((gv - mu) * lax.rsqrt(var + EPS)) * vng_ref[...] + vnb_ref[...]
    if sample:
        vn_ref[...] = vn
    vnb16 = vn.astype(BF16)
    n_chunk = tm // CHUNK
    tri = _iota((CHUNK, CHUNK), 0) >= _iota((CHUNK, CHUNK), 1)
    mixed = []
    for g in range(GMLP_GROUPS):
        wm = jnp.where(tri, wsp_ref[g], 0.0).astype(BF16)
        cat = jnp.concatenate(
            [vnb16[c * CHUNK:(c + 1) * CHUNK, g * LANES:(g + 1) * LANES] for c in range(n_chunk)], axis=1)
        mixed.append(_dot(wm, cat))
    bsp = bsp_ref[...]
    sg = jnp.concatenate(
        [jnp.concatenate([mixed[g][:, c * LANES:(c + 1) * LANES] for g in range(GMLP_GROUPS)], axis=1) + bsp
         for c in range(n_chunk)], axis=0)
    u = proj(OFF_U, OFF_U + GMLP_WIDTH)
    zb = proj(OFF_ZB, OFF_ZB + GMLP_WIDTH)
    t = (_gelu(u) * sg) * (zb * _sigmoid(zb))
    pb = _dot(t.astype(BF16), wpb_ref[...])
    gpb_ref[...] = _sigmoid(proj(OFF_RB, OFF_RB + D_MODEL)) * pb


def _in_project(x2d, tables, prm, tm, sample, seq):
    T = x2d.shape[0]
    nt = T // tm
    pos_tiles = tables[0].shape[0] // tm
    const = lambda *shape: pl.BlockSpec(shape, lambda i: (0,) * len(shape))
    row = lambda width: pl.BlockSpec((tm, width), lambda i: (i, 0))
    tab = pl.BlockSpec((tm, LANES), lambda i: (i % pos_tiles, 0))
    in_specs = [
        row(D_MODEL), const(1, D_MODEL),
        pl.BlockSpec((W_TOT, D_MODEL), lambda i: (0, 0), pipeline_mode=pl.Buffered(1)),
        tab, tab, tab,
        const(GMLP_GROUPS, CHUNK, CHUNK), const(CHUNK, GMLP_WIDTH), const(1, GMLP_WIDTH), const(1, GMLP_WIDTH),
        const(GMLP_WIDTH, D_MODEL),
    ]
    f32rows = lambda width: jax.ShapeDtypeStruct((T, width), F32)
    tail_shapes = [f32rows(LANES), f32rows(ATTN_WIDTH), f32rows(D_MODEL), f32rows(D_MODEL)]
    tail_specs = [row(LANES), row(ATTN_WIDTH), row(D_MODEL), row(D_MODEL)]
    if sample:
        out_shape = [f32rows(ATTN_WIDTH)] + [f32rows(KV_WIDTH)] * 6 + tail_shapes + [f32rows(GMLP_WIDTH)]
        out_specs = [row(ATTN_WIDTH)] + [row(KV_WIDTH)] * 6 + tail_specs + [row(GMLP_WIDTH)]
    else:
        nb = T // seq
        bf = lambda *shape: jax.ShapeDtypeStruct(shape, BF16)
        tposed = lambda rows: pl.BlockSpec((1, rows, tm), lambda i: (i // pos_tiles, 0, i % pos_tiles))
        out_shape = ([bf(N_HEADS, T, LANES)] + [jax.ShapeDtypeStruct((nb, KV_WIDTH, seq), F32)] * 6
                     + [f32rows(KV_WIDTH)] * 2
                     + [bf(nb, 2 * LANES, seq), bf(nb, LANES, seq), bf(KV_HEADS, T, LANES), bf(KV_HEADS, T, LANES)]
                     + tail_shapes)
        heads = lambda n: pl.BlockSpec((n, tm, LANES), lambda i: (0, i, 0))
        out_specs = ([heads(N_HEADS)] + [tposed(KV_WIDTH)] * 6 + [row(KV_WIDTH)] * 2
                     + [tposed(2 * LANES), tposed(LANES), heads(KV_HEADS), heads(KV_HEADS)] + tail_specs)
    return pl.pallas_call(
        functools.partial(_inproj_body, tm, sample, pos_tiles),
        grid=(nt,),
        in_specs=in_specs,
        out_specs=out_specs,
        out_shape=out_shape,
        compiler_params=pltpu.CompilerParams(dimension_semantics=("arbitrary",), vmem_limit_bytes=VMEM_LIMIT),
        name="in_project_sample" if sample else "in_project_prompt",
    )(x2d, prm["norm_g"], prm["w_in"], *tables, prm["w_sp_s" if sample else "w_sp"],
      prm["b_sp_s" if sample else "b_sp"], prm["v_norm_g"], prm["v_norm_b"], prm["w_pb"])


def _strided_halfblocks(load_rows):
    return jnp.concatenate([load_rows(t) for t in range(CMP_STRIDE)], axis=1).astype(BF16)


def _compress_mlp(xcat, n, w1_ref, b1_ref, w2_ref, b2_ref):
    hh = _dot(xcat, w1_ref[...])
    hid = jnp.concatenate(
        [hh[:, 0:LANES] + pltpu.roll(hh[:, LANES:2 * LANES], n - 1, 0),
         hh[:, 2 * LANES:3 * LANES] + pltpu.roll(hh[:, 3 * LANES:4 * LANES], n - 1, 0)], axis=1) + b1_ref[...]
    return _dot(_gelu(hid).astype(BF16), w2_ref[...]) + b2_ref[...]


def _compress_prompt_body(n, kr_ref, vr_ref, w1k, b1k, w2k, b2k, w1v, b1v, w2v, b2v, cos_ref, s1_ref, s2_ref,
                          ovl_ref, kct_ref, vco_ref):
    kc = _compress_mlp(_strided_halfblocks(lambda t: kr_ref[0, pl.ds(t, n, stride=CMP_STRIDE), :]),
                       n, w1k, b1k, w2k, b2k)
    kct_ref[0] = _rope(kc, cos_ref[...], s1_ref[...], s2_ref[...]).T.astype(BF16)
    vc = _compress_mlp(_strided_halfblocks(lambda t: vr_ref[0, pl.ds(t, n, stride=CMP_STRIDE), :]),
                       n, w1v, b1v, w2v, b2v)
    vco_ref[0, :, 0:LANES] = vc.astype(BF16)
    vco_ref[0, :, LANES:2 * LANES] = ovl_ref[...]


def _compress_prompt(kr, vr, prm, tables, ovl):
    B, S, _ = kr.shape
    n = S // CMP_STRIDE
    const = lambda *shape: pl.BlockSpec(shape, lambda b: (0,) * len(shape))
    rows = pl.BlockSpec((1, S, LANES), lambda b: (b, 0, 0))
    wspecs = [const(CMP_STRIDE * LANES, 4 * LANES), const(1, 2 * LANES), const(2 * LANES, LANES), const(1, LANES)]
    return pl.pallas_call(
        functools.partial(_compress_prompt_body, n),
        grid=(B,),
        in_specs=[rows, rows] + wspecs + wspecs + [const(n, LANES)] * 3 + [const(n, LANES)],
        out_specs=[pl.BlockSpec((1, LANES, n), lambda b: (b, 0, 0)),
                   pl.BlockSpec((1, n, 2 * LANES), lambda b: (b, 0, 0))],
        out_shape=[jax.ShapeDtypeStruct((B, LANES, n), BF16),
                   jax.ShapeDtypeStruct((B, n, 2 * LANES), BF16)],
        compiler_params=pltpu.CompilerParams(dimension_semantics=("arbitrary",), vmem_limit_bytes=VMEM_LIMIT),
        name="compress_prompt",
    )(kr, vr, *prm["cmp_k"], *prm["cmp_v"], *tables, ovl)


def _start_pages(pt_ref, pools, bufs, sems, b, n_pages):
    def body(p, carry):
        page = pt_ref[b, p]
        for i, (pool, buf, sem) in enumerate(zip(pools, bufs, sems)):
            pltpu.make_async_copy(pool.at[page], buf.at[p], sem).start(priority=i % 2)
        return carry
    lax.fori_loop(0, n_pages, body, 0)


def _wait_pages(pools, bufs, sems, n_pages):
    for pool, buf, sem in zip(pools, bufs, sems):
        pltpu.make_async_copy(pool.at[pl.ds(0, n_pages)], buf, sem).wait()


def _compress_sample_body(n, n_pages, pt_ref, pk_ref, pv_ref, w1k, b1k, w2k, b2k, w1v, b1v, w2v, b2v,
                          cos_ref, s1_ref, s2_ref, perm_ref, kct_ref, vc_ref, kb0, kb1, vb0, vb1, xk, xv, sem):
    s = pl.program_id(0)
    bufs = ((kb0, vb0, 0), (kb1, vb1, 1))

    def start(b, slot):
        kb, vb, si = bufs[slot]
        _start_pages(pt_ref, (pk_ref, pv_ref), (kb, vb), (sem.at[2 * si], sem.at[2 * si + 1]), b, n_pages)

    hb_per_page = PAGE_SIZE // CMP_STRIDE

    def halfblocks(buf, xcat):
        pairs = jnp.concatenate([buf[pl.ds(0, n_pages // 2, stride=2)], buf[pl.ds(1, n_pages // 2, stride=2)]], axis=2)
        y = _dot(pairs.reshape(n_pages // 2 * LANES, 2 * PAGE_SIZE).astype(BF16), perm_ref[...])
        for p in range(n_pages):
            z = y[p // 2 * LANES:(p // 2 + 1) * LANES, p % 2 * PAGE_SIZE:(p % 2 + 1) * PAGE_SIZE].T
            for t in range(CMP_STRIDE):
                xcat[p * hb_per_page:(p + 1) * hb_per_page, t * LANES:(t + 1) * LANES] = (
                    z[t * hb_per_page:(t + 1) * hb_per_page])

    def finish(slot):
        kb, vb, si = bufs[slot]
        _wait_pages((pk_ref, pv_ref), (kb, vb), (sem.at[2 * si], sem.at[2 * si + 1]), n_pages)
        halfblocks(kb, xk)
        halfblocks(vb, xv)
        kc = _compress_mlp(xk[...].astype(BF16), n, w1k, b1k, w2k, b2k)
        kct_ref[slot] = _rope(kc, cos_ref[...], s1_ref[...], s2_ref[...]).T.astype(BF16)
        vc_ref[slot] = _compress_mlp(xv[...].astype(BF16), n, w1v, b1v, w2v, b2v).astype(BF16)

    @pl.when(s == 0)
    def _():
        start(0, 0)

    start(2 * s + 1, 1)
    finish(0)

    @pl.when(s + 1 < pl.num_programs(0))
    def _():
        start(2 * s + 2, 0)

    finish(1)


def _compress_sample(page_table, pool_k, pool_v, prm, tables):
    Bd, n_pages = page_table.shape
    past = n_pages * PAGE_SIZE
    n = past // CMP_STRIDE
    const = lambda *shape: pl.BlockSpec(shape, lambda s, pt: (0,) * len(shape))
    wspecs = [const(CMP_STRIDE * LANES, 4 * LANES), const(1, 2 * LANES), const(2 * LANES, LANES), const(1, LANES)]
    anyspec = pl.BlockSpec(memory_space=pl.ANY)
    grid_spec = pltpu.PrefetchScalarGridSpec(
        num_scalar_prefetch=1,
        grid=(Bd // 2,),
        in_specs=[anyspec, anyspec] + wspecs + wspecs + [const(n, LANES)] * 3 + [const(2 * PAGE_SIZE, 2 * PAGE_SIZE)],
        out_specs=[pl.BlockSpec((2, LANES, n), lambda s, pt: (s, 0, 0)),
                   pl.BlockSpec((2, n, LANES), lambda s, pt: (s, 0, 0))],
        scratch_shapes=[pltpu.VMEM((n_pages, LANES, PAGE_SIZE), F32)] * 4
        + [pltpu.VMEM((n, CMP_STRIDE * LANES), F32)] * 2 + [pltpu.SemaphoreType.DMA((4,))],
    )
    hb = PAGE_SIZE // CMP_STRIDE
    perm = np.zeros((2 * PAGE_SIZE, 2 * PAGE_SIZE), np.float32)
    for side in range(2):
        for jj in range(hb):
            for t in range(CMP_STRIDE):
                perm[side * PAGE_SIZE + CMP_STRIDE * jj + t, side * PAGE_SIZE + t * hb + jj] = 1.0
    return pl.pallas_call(
        functools.partial(_compress_sample_body, n, n_pages),
        grid_spec=grid_spec,
        out_shape=[jax.ShapeDtypeStruct((Bd, LANES, n), BF16), jax.ShapeDtypeStruct((Bd, n, LANES), BF16)],
        compiler_params=pltpu.CompilerParams(dimension_semantics=("arbitrary",), vmem_limit_bytes=VMEM_LIMIT),
        name="compress_sample",
    )(page_table, pool_k, pool_v, *prm["cmp_k"], *prm["cmp_v"], *tables, jnp.asarray(perm, BF16))


def _mixer_tail(o, sza, sra, gpb, x, wpa_ref, wo_ref, fg_ref):
    pa = _dot((o * sza).astype(BF16), wpa_ref[...])
    merged = sra * pa + gpb
    hn = x + _dot(merged.astype(BF16), wo_ref[...])
    r = lax.rsqrt(jnp.mean(hn * hn, axis=-1, keepdims=True) + EPS)
    return (hn * r) * fg_ref[...]


def _topk_mask(score, blk, n_blocks, axis):
    cnt = jnp.zeros(score.shape, jnp.int32)
    for sp in range(n_blocks):
        b = lax.slice_in_dim(score, sp, sp + 1, axis=axis)
        ge = jnp.where(b >= score, 1, 0)
        gt = jnp.where(b > score, 1, 0)
        cnt = cnt + jnp.where(blk > sp, ge, gt)
    return cnt < TOP_N


def _attn_prompt_body(n_sel, q_ref, ka_ref, vs_ref, kw_ref, vw_ref, kc_ref, vco_ref, gat_ref, sza_ref, sra_ref,
                      gpb_ref, x_ref, wpa_ref, wo_ref, fg_ref, y_ref):
    i = pl.program_id(1)
    M = HPG * TQ
    G = range(KV_HEADS)
    qpos = i * TQ + (_iota((M, 1), 0) & (TQ - 1))
    n_cmp_pad = kc_ref.shape[2]
    gt = gat_ref[...]
    qs = [q_ref[HPG * g:HPG * (g + 1)].reshape(M, LANES) for g in G]

    mk = (_iota((M, n_cmp_pad), 1) * CMP_STRIDE + (CMP_BLOCK - 1)) <= qpos
    o_cmp, qa = [], []
    blk = _iota((n_sel, TQ), 0)
    cur = (i * TQ + _iota((n_sel, TQ), 1)) // SEL_BLOCK
    valid = blk <= cur
    forced = (valid & (blk > cur - N_LOCAL_BLOCKS)) | (blk < N_INIT_BLOCKS)
    scores = []
    for g in G:
        s = jnp.where(mk, _dot(qs[g], kc_ref[0]), NEG)
        e = jnp.exp2(s - jnp.max(s, axis=-1, keepdims=True))
        p = jnp.where(mk, e / jnp.sum(e, axis=-1, keepdims=True), 0.0)
        r = _dot(p.astype(BF16), vco_ref[0])
        o_cmp.append(r[:, 0:LANES])
        impc = r[:, LANES:2 * LANES]
        imp = impc[0:TQ] + impc[TQ:2 * TQ] + impc[2 * TQ:3 * TQ] + impc[3 * TQ:4 * TQ]
        sc = imp.T[HEAD_DIM:HEAD_DIM + n_sel]
        scores.append(jnp.where(forced, BIG, jnp.where(valid, sc, NEG)))

    last_blk = ((i + 1) * TQ - 1) // SEL_BLOCK

    def count_group(k, cnts):
        out = []
        for g in G:
            c = cnts[g]
            for sp in range(k * RANK_GROUP, (k + 1) * RANK_GROUP):
                b = scores[g][sp:sp + 1]
                c = c + jnp.where(blk > sp, jnp.where(b >= scores[g], 1, 0), jnp.where(b > scores[g], 1, 0))
            out.append(c)
        return tuple(out)

    cnts = (jnp.zeros((n_sel, TQ), jnp.int32),) * KV_HEADS
    for k in range(n_sel // RANK_GROUP):
        cnts = lax.cond(k * RANK_GROUP <= last_blk, functools.partial(count_group, k), lambda c: c, cnts)
    for g in G:
        selneg = jnp.where(cnts[g] < TOP_N, 0.0, NEG)
        seln_t = jnp.concatenate([selneg, jnp.zeros((LANES - n_sel, TQ), F32)], axis=0).T.astype(BF16)
        qa.append(jnp.concatenate([qs[g], jnp.concatenate([seln_t] * HPG, axis=0)], axis=1))

    def update(carry, s, v):
        m, acc = carry
        mn = jnp.maximum(m, jnp.max(s, axis=-1, keepdims=True))
        acc = jnp.exp2(m - mn) * acc + _dot(jnp.exp2((s - mn).astype(BF16)), v)
        return mn, acc

    def step(kt, carries, causal):
        off = pl.multiple_of(kt * KT, KT)
        k = ka_ref[0, :, pl.ds(off, KT)]
        ss = [_dot(qa[g], k) for g in G]
        if causal:
            keep = kt * KT + _iota((M, KT), 1) <= qpos
            ss = [jnp.where(keep, s, NEG) for s in ss]
        return tuple(update(carries[g], ss[g], vs_ref[g, pl.ds(off, KT), :]) for g in G)

    n_full = i // (KT // TQ)
    init = (jnp.full((M, 1), NEG, F32), jnp.zeros((M, LANES), F32))
    carries = lax.fori_loop(0, n_full // 2, lambda kp, c: step(2 * kp + 1, step(2 * kp, c, False), False),
                            (init,) * KV_HEADS)
    carries = lax.cond(n_full % 2 == 1, lambda c: step(n_full - 1, c, False), lambda c: c, carries)
    carries = step(n_full, carries, True)
    sum_lane = [(KV_HEADS - 1 - g) * HEAD_DIM for g in G]
    o_sel = [carries[g][1] / carries[g][1][:, sum_lane[g]:sum_lane[g] + 1] for g in G]

    st = pl.multiple_of(jnp.maximum(i * TQ - WINDOW, 0), TQ)
    kpos = st + _iota((M, WIN_KEYS), 1)
    mkw = (kpos <= qpos) & (kpos > qpos - WINDOW)
    kw = kw_ref[0, :, pl.ds(st, WIN_KEYS)]
    o_groups = []
    for g in G:
        s = jnp.where(mkw, _dot(qs[g], kw), NEG)
        e = jnp.exp2(s - jnp.max(s, axis=-1, keepdims=True))
        r = _dot(e.astype(BF16), vw_ref[g, pl.ds(st, WIN_KEYS), :])
        o_win = r / r[:, sum_lane[g]:sum_lane[g] + 1]

        def gcol(c, g=g):
            return jnp.concatenate(
                [gt[:, 3 * (HPG * g + j) + c:3 * (HPG * g + j) + c + 1] for j in range(HPG)], axis=0)

        o_groups.append(gcol(0) * o_cmp[g] + gcol(1) * o_sel[g] + gcol(2) * o_win)

    low = _iota((TQ, LANES), 1) < HEAD_DIM
    slabs = []
    for pp in range(N_HEADS // 2):
        g, j0 = pp // 2, (2 * pp) % HPG
        a = o_groups[g][j0 * TQ:(j0 + 1) * TQ]
        b = o_groups[g][(j0 + 1) * TQ:(j0 + 2) * TQ]
        if g == 0:
            b = pltpu.roll(b, HEAD_DIM, 1)
        else:
            a = pltpu.roll(a, HEAD_DIM, 1)
        slabs.append(jnp.where(low, a, b))
    o = jnp.concatenate(slabs, axis=1)
    y_ref[...] = _mixer_tail(o, sza_ref[...], sra_ref[...], gpb_ref[...], x_ref[...], wpa_ref, wo_ref, fg_ref)


def _attn_prompt(B, S, q_hm, kaug, vsb, kwt, vwb, kct, vco, gates, sza, sra, gpb, x2d, prm):
    nq = S // TQ
    n_sel = S // SEL_BLOCK
    n_cmp_pad = kct.shape[2]
    row = lambda width: pl.BlockSpec((TQ, width), lambda b, i: (b * nq + i, 0))
    const = lambda *shape: pl.BlockSpec(shape, lambda b, i: (0,) * len(shape))
    batch = lambda *shape: pl.BlockSpec((1,) + shape, lambda b, i: (b,) + (0,) * len(shape))
    in_specs = [
        pl.BlockSpec((N_HEADS, TQ, LANES), lambda b, i: (0, b * nq + i, 0)),
        batch(2 * LANES, S),
        pl.BlockSpec((KV_HEADS, S, LANES), lambda b, i: (0, b, 0)),
        batch(LANES, S),
        pl.BlockSpec((KV_HEADS, S, LANES), lambda b, i: (0, b, 0)),
        batch(LANES, n_cmp_pad),
        batch(n_cmp_pad, 2 * LANES),
        row(LANES), row(ATTN_WIDTH), row(D_MODEL), row(D_MODEL), row(D_MODEL),
        const(ATTN_WIDTH, D_MODEL), const(D_MODEL, D_MODEL), const(1, D_MODEL),
    ]
    return pl.pallas_call(
        functools.partial(_attn_prompt_body, n_sel),
        grid=(B, nq),
        in_specs=in_specs,
        out_specs=row(D_MODEL),
        out_shape=jax.ShapeDtypeStruct((B * S, D_MODEL), F32),
        compiler_params=pltpu.CompilerParams(dimension_semantics=("arbitrary", "arbitrary"),
                                             vmem_limit_bytes=VMEM_LIMIT),
        name="attn_prompt",
    )(q_hm, kaug, vsb, kwt, vwb, kct, vco, gates, sza, sra, gpb, x2d, prm["w_pa"], prm["w_o"], prm["final_g"])


def _attn_sample_body(n_pages, tn, pt_ref, pk_ref, pv_ref, qa_ref, kc_ref, vc_ref, ovl_ref, oh_ref, kt_ref, vt_ref,
                      ckw_ref, cvw_ref, kwn_ref, vwn_ref, o_ref, kb0, kb1, vb0, vb1, sem):
    s_id = pl.program_id(0)
    past = n_pages * PAGE_SIZE
    n_past_blk = past // SEL_BLOCK
    n_blk_pad = ovl_ref.shape[1]
    R = N_HEADS * TOK_PAD
    GR = HPG * TOK_PAD
    wc = ckw_ref.shape[2]
    bufs = ((kb0, vb0, 0), (kb1, vb1, 1))
    t_row = _iota((R, 1), 0) & (TOK_PAD - 1)

    def start(b, slot):
        kb, vb, si = bufs[slot]
        _start_pages(pt_ref, (pk_ref, pv_ref), (kb, vb), (sem.at[2 * si], sem.at[2 * si + 1]), b, n_pages)

    def finish(slot):
        kb, vb, si = bufs[slot]
        qa = qa_ref[slot]

        n_cmp_pad = kc_ref.shape[2]
        s = _dot(qa, kc_ref[slot])
        mk = _iota((R, n_cmp_pad), 1) < n_cmp_pad - 1
        s = jnp.where(mk, s, NEG)
        e = jnp.exp(s - jnp.max(s, axis=-1, keepdims=True))
        p = jnp.where(mk, e / jnp.sum(e, axis=-1, keepdims=True), 0.0).astype(BF16)
        o_cmp = _dot(p, vc_ref[slot])
        impc = _dot(p, ovl_ref[...])
        imp = jnp.concatenate(
            [impc[g * GR:g * GR + TOK_PAD] + impc[g * GR + TOK_PAD:g * GR + 2 * TOK_PAD]
             + impc[g * GR + 2 * TOK_PAD:g * GR + 3 * TOK_PAD] + impc[g * GR + 3 * TOK_PAD:g * GR + 4 * TOK_PAD]
             for g in range(KV_HEADS)], axis=0)
        nr = KV_HEADS * TOK_PAD
        blk = _iota((nr, n_blk_pad), 1)
        cur = (past + jnp.minimum(_iota((nr, n_blk_pad), 0) & (TOK_PAD - 1), tn - 1)) // SEL_BLOCK
        valid = blk <= cur
        forced = (valid & (blk > cur - N_LOCAL_BLOCKS)) | (blk < N_INIT_BLOCKS)
        score = jnp.where(forced, BIG, jnp.where(valid, imp, NEG))
        sel = _topk_mask(score, blk, n_past_blk + 1, 1)
        selneg = jnp.where(sel, 0.0, NEG)
        selneg = jnp.concatenate(
            [selneg[g * TOK_PAD:(g + 1) * TOK_PAD] for g in range(KV_HEADS) for _ in range(HPG)], axis=0)
        low = _iota((R, LANES), 1) < SEL_HALF
        qaug = []
        for hf in range(n_past_blk // SEL_HALF):
            slab = selneg[:, (hf // 2) * LANES:(hf // 2 + 1) * LANES]
            if hf % 2:
                slab = pltpu.roll(slab, SEL_HALF, 1)
            qaug.append(jnp.concatenate([qa, jnp.where(low, slab, 0.0).astype(BF16)], axis=1))

        _wait_pages((pk_ref, pv_ref), (kb, vb), (sem.at[2 * si], sem.at[2 * si + 1]), n_pages)

        pages_per_half = SEL_HALF * SEL_BLOCK // PAGE_SIZE

        def half(buf, hf):
            return jnp.concatenate(
                [buf[hf * pages_per_half + j] for j in range(pages_per_half)], axis=1).astype(BF16)

        col = _iota((R, LANES), 1)
        ss = [_dot(qaug[hf], jnp.concatenate([half(kb, hf), oh_ref[...]], axis=0)) for hf in range(len(qaug))]
        ss.append(jnp.where(col <= t_row, _nt(qa, kt_ref[slot]), NEG))
        m = functools.reduce(jnp.maximum, [jnp.max(s, axis=-1, keepdims=True) for s in ss])
        es = [jnp.exp(s - m) for s in ss]
        l = functools.reduce(jnp.add, [jnp.sum(e, axis=-1, keepdims=True) for e in es])
        acc = _dot(es[-1].astype(BF16), vt_ref[slot])
        for hf in range(len(qaug)):
            acc = acc + _nt(es[hf].astype(BF16), half(vb, hf))
        o_sel = acc / l

        sa = jnp.where(_iota((R, wc), 1) > t_row, _dot(qa, ckw_ref[slot].astype(BF16)), NEG)
        sb = jnp.where(col <= t_row, _nt(qa, kwn_ref[slot]), NEG)
        m = jnp.maximum(jnp.max(sa, axis=-1, keepdims=True), jnp.max(sb, axis=-1, keepdims=True))
        ea, eb = jnp.exp(sa - m), jnp.exp(sb - m)
        den = jnp.sum(ea, axis=-1, keepdims=True) + jnp.sum(eb, axis=-1, keepdims=True)
        o_win = (_nt(ea.astype(BF16), cvw_ref[slot].astype(BF16)) + _dot(eb.astype(BF16), vwn_ref[slot])) / den

        o_ref[slot, 0] = o_cmp
        o_ref[slot, 1] = o_sel
        o_ref[slot, 2] = o_win

    @pl.when(s_id == 0)
    def _():
        start(0, 0)

    start(2 * s_id + 1, 1)
    finish(0)

    @pl.when(s_id + 1 < pl.num_programs(0))
    def _():
        start(2 * s_id + 2, 0)

    finish(1)


def _attn_sample(page_table, pool_k, pool_v, tn, qa, kct, vc, ovl, ktail, vtail, ckw, cvw, kwn, vwn):
    Bd, n_pages = page_table.shape
    R = N_HEADS * TOK_PAD
    wc = ckw.shape[2]
    const = lambda *shape: pl.BlockSpec(shape, lambda s, pt: (0,) * len(shape))
    pair = lambda *shape: pl.BlockSpec((2,) + shape, lambda s, pt: (s,) + (0,) * len(shape))
    anyspec = pl.BlockSpec(memory_space=pl.ANY)
    half_keys = SEL_HALF * SEL_BLOCK
    onehot = np.zeros((LANES, half_keys), np.float32)
    onehot[np.arange(half_keys) // SEL_BLOCK, np.arange(half_keys)] = 1.0
    grid_spec = pltpu.PrefetchScalarGridSpec(
        num_scalar_prefetch=1,
        grid=(Bd // 2,),
        in_specs=[anyspec, anyspec, pair(R, LANES), pair(LANES, kct.shape[2]), pair(vc.shape[1], LANES),
                  const(*ovl.shape), const(LANES, half_keys), pair(LANES, LANES), pair(LANES, LANES),
                  pair(LANES, wc), pair(LANES, wc), pair(LANES, LANES), pair(LANES, LANES)],
        out_specs=pair(3, R, LANES),
        scratch_shapes=[pltpu.VMEM((n_pages, LANES, PAGE_SIZE), F32)] * 4 + [pltpu.SemaphoreType.DMA((4,))],
    )
    return pl.pallas_call(
        functools.partial(_attn_sample_body, n_pages, tn),
        grid_spec=grid_spec,
        out_shape=jax.ShapeDtypeStruct((Bd, 3, R, LANES), F32),
        compiler_params=pltpu.CompilerParams(dimension_semantics=("arbitrary",), vmem_limit_bytes=VMEM_LIMIT),
        name="attn_sample",
    )(page_table, pool_k, pool_v, qa, kct, vc, ovl, jnp.asarray(onehot, BF16), ktail, vtail, ckw, cvw, kwn, vwn)


def _mixer_sample_body(o3_ref, g3_ref, sza_ref, sra_ref, gpb_ref, x_ref, wpa_ref, wo_ref, fg_ref, y_ref):
    o = g3_ref[0] * o3_ref[0] + g3_ref[1] * o3_ref[1] + g3_ref[2] * o3_ref[2]
    y_ref[...] = _mixer_tail(o, sza_ref[...], sra_ref[...], gpb_ref[...], x_ref[...], wpa_ref, wo_ref, fg_ref)


def _mixer_sample(o3, g3, sza, sra, gpb, x2d, prm):
    T = x2d.shape[0]
    full = lambda *shape: pl.BlockSpec(shape, lambda i: (0,) * len(shape))
    return pl.pallas_call(
        _mixer_sample_body,
        grid=(1,),
        in_specs=[full(3, T, ATTN_WIDTH), full(3, T, ATTN_WIDTH), full(T, ATTN_WIDTH), full(T, D_MODEL),
                  full(T, D_MODEL), full(T, D_MODEL), full(ATTN_WIDTH, D_MODEL), full(D_MODEL, D_MODEL),
                  full(1, D_MODEL)],
        out_specs=full(T, D_MODEL),
        out_shape=jax.ShapeDtypeStruct((T, D_MODEL), F32),
        compiler_params=pltpu.CompilerParams(dimension_semantics=("arbitrary",), vmem_limit_bytes=VMEM_LIMIT),
        name="mixer_sample",
    )(o3, g3, sza, sra, gpb, x2d, prm["w_pa"], prm["w_o"], prm["final_g"])


def _overlap(n_cmp, n_sel):
    cs = np.arange(n_cmp)[:, None] * CMP_STRIDE
    ss = np.arange(n_sel)[None, :] * SEL_BLOCK
    ov = np.minimum(cs + CMP_BLOCK, ss + SEL_BLOCK) - np.maximum(cs, ss)
    return np.clip(ov, 0, None).astype(np.float32) / CMP_BLOCK


def _cmp_weights(w1, b1, w2, b2):
    w1r = w1.reshape(2, CMP_STRIDE, HEAD_DIM, CMP_HIDDEN).transpose(1, 2, 0, 3)
    w1r = w1r.reshape(CMP_STRIDE, 1, HEAD_DIM, 2 * CMP_HIDDEN).astype(BF16)
    w2r = w2.reshape(1, CMP_HIDDEN, HEAD_DIM).astype(BF16)
    big = jnp.concatenate(
        [jnp.pad(w1r, ((0, 0), (0, 0), (0, 0), (g * 2 * CMP_HIDDEN, (KV_HEADS - 1 - g) * 2 * CMP_HIDDEN)))
         for g in range(KV_HEADS)], axis=1)
    w2b = jnp.concatenate(
        [jnp.pad(w2r, ((0, 0), (0, 0), (g * HEAD_DIM, (KV_HEADS - 1 - g) * HEAD_DIM))) for g in range(KV_HEADS)],
        axis=0)
    return (big.reshape(CMP_STRIDE * LANES, 4 * LANES), jnp.tile(b1, KV_HEADS)[None],
            w2b.reshape(KV_HEADS * CMP_HIDDEN, LANES), jnp.tile(b2, KV_HEADS)[None])


def _seq_minor(t):
    lead = t.shape[:-3]
    n = len(lead)
    return t.transpose(*range(n), n + 1, n + 2, n).reshape(*lead, KV_WIDTH, t.shape[-3])


def _seq_major(t):
    lead = t.shape[:-2]
    n = len(lead)
    return t.reshape(*lead, KV_HEADS, HEAD_DIM, t.shape[-1]).transpose(*range(n), n + 2, n, n + 1)


def kernel(x_prompt, x_sample, cache_k_cmp, cache_v_cmp, cache_k_sel, cache_v_sel, cache_k_win, cache_v_win, page_table, norm_g, w_in, cmp_k_w1, cmp_k_b1, cmp_k_w2, cmp_k_b2, cmp_v_w1, cmp_v_b1, cmp_v_w2, cmp_v_b2, v_norm_g, v_norm_b, w_spatial, b_spatial, w_pa, w_pb, w_o, final_g):
    B, S, _ = x_prompt.shape
    Bd, tn, _ = x_sample.shape
    depth = w_in.shape[0]
    assert depth == 1, "single-layer step"
    assert Bd * tn == CHUNK, "the sample tokens form one 128-row tile"
    n_pages = page_table.shape[1]
    past = n_pages * PAGE_SIZE

    split = OFF_GL + 3 * N_HEADS
    w = w_in[0]
    prm = {
        "norm_g": norm_g,
        "w_in": jnp.concatenate([w.T[:split], jnp.zeros((GL_PAD, D_MODEL), F32), w.T[split:]], axis=0).astype(BF16),
        "w_sp": w_spatial[0],
        "b_sp": jnp.repeat(b_spatial[0].T, LANES, axis=1),
        "w_sp_s": jnp.stack([jnp.kron(jnp.eye(CHUNK // tn, dtype=F32), w_spatial[0, g, :tn, :tn])
                             for g in range(GMLP_GROUPS)]),
        "b_sp_s": jnp.tile(jnp.repeat(b_spatial[0, :, :tn].T, LANES, axis=1), (CHUNK // tn, 1)),
        "v_norm_g": v_norm_g, "v_norm_b": v_norm_b,
        "w_pb": w_pb[0].astype(BF16), "w_pa": w_pa[0].astype(BF16), "w_o": w_o[0].astype(BF16),
        "final_g": final_g[None],
        "cmp_k": _cmp_weights(cmp_k_w1[0], cmp_k_b1[0], cmp_k_w2[0], cmp_k_b2[0]),
        "cmp_v": _cmp_weights(cmp_v_w1[0], cmp_v_b1[0], cmp_v_w2[0], cmp_v_b2[0]),
    }

    xp = x_prompt.reshape(B * S, D_MODEL)
    (q_hm, p_kcmp, p_vcmp, p_ksel, p_vsel, p_kwin, p_vwin, kcr, vcr, kaug, kwt, vsb, vwb,
     gates, sza, sra, gpb) = _in_project(xp, _rope_tables(np.arange(S)), prm, TM_PROMPT, False, S)
    n_half = S // CMP_STRIDE
    n_cmp = (S - CMP_BLOCK) // CMP_STRIDE + 1
    n_sel = S // SEL_BLOCK
    assert n_sel <= SEL_HALF
    ovl = np.zeros((n_half, LANES), np.float32)
    ovl[:n_cmp, HEAD_DIM:HEAD_DIM + n_sel] = _overlap(n_cmp, n_sel)
    cend_tables = _rope_tables(np.arange(n_half) * CMP_STRIDE + CMP_BLOCK - 1)
    kct, vco = _compress_prompt(kcr.reshape(B, S, LANES), vcr.reshape(B, S, LANES), prm, cend_tables,
                                jnp.asarray(ovl, BF16))
    y_prompt = _attn_prompt(B, S, q_hm, kaug, vsb, kwt, vwb, kct, vco, gates, sza, sra, gpb, xp, prm)

    xs = x_sample.reshape(Bd * tn, D_MODEL)
    pos_s = np.tile(past + np.arange(tn), Bd)
    (q_s, s_kcmp, s_vcmp, s_ksel, s_vsel, s_kwin, s_vwin, gates_s, sza_s, sra_s, gpb_s, vn_s) = _in_project(
        xs, _rope_tables(pos_s), prm, Bd * tn, True, tn)
    n_half_s = past // CMP_STRIDE
    pools = [_seq_minor(c[0]) for c in (cache_k_cmp, cache_v_cmp, cache_k_sel, cache_v_sel)]
    cend_s = _rope_tables(np.arange(n_half_s) * CMP_STRIDE + CMP_BLOCK - 1)
    kct_s, vc_s = _compress_sample(page_table, pools[0], pools[1], prm, cend_s)

    n_cmp_s = (past + tn - CMP_BLOCK) // CMP_STRIDE + 1
    n_blk_s = past // SEL_BLOCK + -(-tn // SEL_BLOCK)
    assert n_cmp_s == n_half_s - 1 and tn <= TOK_PAD and (past // SEL_BLOCK) % SEL_HALF == 0
    ovl_s = np.zeros((n_half_s, 2 * LANES), np.float32)
    ovl_s[:n_cmp_s, :n_blk_s] = _overlap(n_cmp_s, n_blk_s)
    q5 = q_s.reshape(Bd, tn, KV_HEADS, HPG, HEAD_DIM).transpose(0, 2, 3, 1, 4)
    q5 = jnp.pad(q5, ((0, 0), (0, 0), (0, 0), (0, TOK_PAD - tn), (0, 0))).reshape(Bd, KV_HEADS, HPG * TOK_PAD, HEAD_DIM)
    qa = jnp.zeros((Bd, KV_HEADS, HPG * TOK_PAD, KV_HEADS, HEAD_DIM), F32)
    for g in range(KV_HEADS):
        qa = qa.at[:, g, :, g].set(q5[:, g])
    qa = qa.reshape(Bd, N_HEADS * TOK_PAD, LANES).astype(BF16)
    pad_rows = lambda t: jnp.pad(t.reshape(Bd, tn, LANES), ((0, 0), (0, LANES - tn), (0, 0))).astype(BF16)
    ckw = _seq_minor(cache_k_win[0])
    cvw = _seq_minor(cache_v_win[0])
    o3 = _attn_sample(page_table, pools[2], pools[3], tn, qa, kct_s, vc_s, jnp.asarray(ovl_s, BF16),
                      pad_rows(s_ksel), pad_rows(s_vsel), ckw, cvw, pad_rows(s_kwin), pad_rows(s_vwin))
    new_t = lambda t: t.reshape(Bd, tn, LANES).transpose(0, 2, 1)
    s_k_win = _seq_major(jnp.concatenate([ckw[:, :, tn:], new_t(s_kwin)], axis=2))[None]
    s_v_win = _seq_major(jnp.concatenate([cvw[:, :, tn:], new_t(s_vwin)], axis=2))[None]
    first_group = jnp.arange(N_HEADS * TOK_PAD)[:, None] < HPG * TOK_PAD
    o3r = jnp.where(first_group, o3[..., :HEAD_DIM], o3[..., HEAD_DIM:])
    o3r = o3r.reshape(Bd, 3, N_HEADS, TOK_PAD, HEAD_DIM)[:, :, :, :tn]
    o3r = o3r.transpose(1, 0, 3, 2, 4).reshape(3, Bd * tn, ATTN_WIDTH)
    g3 = gates_s[:, :3 * N_HEADS].reshape(Bd * tn, N_HEADS, 3).transpose(2, 0, 1)
    g3 = jnp.repeat(g3, HEAD_DIM, axis=2)
    y_sample = _mixer_sample(o3r, g3, sza_s, sra_s, gpb_s, xs, prm)

    kv5 = lambda t, b, n: t.reshape(1, b, n, KV_HEADS, HEAD_DIM)
    pw = min(WINDOW, S)
    return (y_prompt.reshape(B, S, D_MODEL), y_sample.reshape(Bd, tn, D_MODEL),
            _seq_major(p_kcmp)[None], _seq_major(p_vcmp)[None], _seq_major(p_ksel)[None], _seq_major(p_vsel)[None],
            _seq_major(p_kwin[:, :, S - pw:])[None], _seq_major(p_vwin[:, :, S - pw:])[None],
            kv5(s_kcmp, Bd, tn), kv5(s_vcmp, Bd, tn), kv5(s_ksel, Bd, tn), kv5(s_vsel, Bd, tn),
            s_k_win, s_v_win, vn_s.reshape(1, Bd, tn, GMLP_WIDTH))
```

```python
import functools

import numpy as np
import jax
import jax.numpy as jnp
from jax import lax
from jax.experimental import pallas as pl
from jax.experimental.pallas import tpu as pltpu

F32 = jnp.float32
BF16 = jnp.bfloat16

D_MODEL = 1024
HEAD_DIM = 64
N_HEADS = 8
KV_HEADS = 2
HPG = N_HEADS // KV_HEADS
ATTN_WIDTH = N_HEADS * HEAD_DIM
KV_WIDTH = KV_HEADS * HEAD_DIM
ROT_DIM = HEAD_DIM // 4
ROT_HALF = ROT_DIM // 2
ROPE_THETA = 500000.0
CMP_BLOCK = 32
CMP_STRIDE = 16
CMP_HIDDEN = 128
SEL_BLOCK = 64
TOP_N = 16
N_INIT_BLOCKS = 1
N_LOCAL_BLOCKS = 2
WINDOW = 512
CHUNK = 128
GMLP_GROUPS = 4
GMLP_WIDTH = 512
PAGE_SIZE = 128
NEG = -1e30
BIG = 1e30
EPS = 1e-6
LOG2E = 1.4426950408889634

LANES = 128
GL_PAD = LANES - 3 * N_HEADS

OFF_Q = 0
OFF_KV = ATTN_WIDTH
OFF_GL = OFF_KV + 6 * KV_WIDTH
OFF_ZA = OFF_GL + LANES
OFF_U = OFF_ZA + ATTN_WIDTH
OFF_V = OFF_U + GMLP_WIDTH
OFF_ZB = OFF_V + GMLP_WIDTH
OFF_RA = OFF_ZB + GMLP_WIDTH
OFF_RB = OFF_RA + D_MODEL
W_TOT = OFF_RB + D_MODEL

VMEM_LIMIT = 56 * 1024 * 1024

TM_PROMPT = 512
TQ = 128
KT = 512
WIN_KEYS = WINDOW + TQ
SEL_HALF = 64
TOK_PAD = 8
RANK_GROUP = 8


def _nt(a, b):
    return lax.dot_general(a, b, (((1,), (1,)), ((), ())), preferred_element_type=F32)


def _dot(a, b):
    return jnp.dot(a, b, preferred_element_type=F32)


def _iota(shape, dim):
    return lax.broadcasted_iota(jnp.int32, shape, dim)


def _rope(slab, cos, s1, s2):
    return slab * cos + pltpu.roll(slab, LANES - ROT_HALF, 1) * s1 + pltpu.roll(slab, ROT_HALF, 1) * s2


def _rope_tables(pos):
    pos = np.asarray(pos, np.float64)
    n = pos.shape[0]
    inv = np.power(np.float64(ROPE_THETA), -np.arange(0, ROT_DIM, 2, dtype=np.float64) / ROT_DIM)
    ang = pos[:, None] * inv[None, :]
    cos, sin = np.cos(ang), np.sin(ang)
    rest = HEAD_DIM - ROT_DIM
    c = np.concatenate([cos, cos, np.ones((n, rest))], axis=1)
    s1 = np.concatenate([-sin, np.zeros((n, HEAD_DIM - ROT_HALF))], axis=1)
    s2 = np.concatenate([np.zeros((n, ROT_HALF)), sin, np.zeros((n, rest))], axis=1)
    return tuple(jnp.asarray(np.tile(t, (1, LANES // HEAD_DIM)), F32) for t in (c, s1, s2))


def _sigmoid(x):
    return 1.0 / (1.0 + jnp.exp(-x))


def _gelu(x):
    return jax.nn.gelu(x, approximate=True)


def _inproj_body(tm, sample, pos_tiles, x_ref, ng_ref, w_ref, cos_ref, s1_ref, s2_ref, wsp_ref, bsp_ref,
                 vng_ref, vnb_ref, wpb_ref, *outs):
    if sample:
        (q_ref, kcmp_ref, vcmp_ref, ksel_ref, vsel_ref, kwin_ref, vwin_ref,
         gat_ref, sza_ref, sra_ref, gpb_ref, vn_ref) = outs
    else:
        (q_ref, kcmp_ref, vcmp_ref, ksel_ref, vsel_ref, kwin_ref, vwin_ref,
         kcr_ref, vcr_ref, kaug_ref, kwt_ref, vsb_ref, vwb_ref, gat_ref, sza_ref, sra_ref, gpb_ref) = outs

    x = x_ref[...]
    r = lax.rsqrt(jnp.mean(x * x, axis=-1, keepdims=True) + EPS)
    h = ((x * r) * ng_ref[...]).astype(BF16)

    def proj(lo, hi):
        return _nt(h, w_ref[lo:hi, :])

    cos, s1, s2 = cos_ref[...], s1_ref[...], s2_ref[...]
    low = _iota((tm, LANES), 1) < HEAD_DIM

    q = proj(OFF_Q, OFF_Q + ATTN_WIDTH)
    q_scale = HEAD_DIM ** -0.5 if sample else HEAD_DIM ** -0.5 * LOG2E
    for pp in range(N_HEADS // 2):
        slab = _rope(q[:, pp * LANES:(pp + 1) * LANES], cos, s1, s2) * q_scale
        if sample:
            q_ref[:, pp * LANES:(pp + 1) * LANES] = slab
        elif (2 * pp) // HPG == 0:
            q_ref[2 * pp] = jnp.where(low, slab, 0.0).astype(BF16)
            q_ref[2 * pp + 1] = jnp.where(low, pltpu.roll(slab, HEAD_DIM, 1), 0.0).astype(BF16)
        else:
            q_ref[2 * pp] = jnp.where(low, 0.0, pltpu.roll(slab, HEAD_DIM, 1)).astype(BF16)
            q_ref[2 * pp + 1] = jnp.where(low, 0.0, slab).astype(BF16)

    kv = proj(OFF_KV, OFF_KV + 6 * KV_WIDTH)
    kcmp = kv[:, 0:LANES]
    vcmp = kv[:, LANES:2 * LANES]
    ksel = _rope(kv[:, 2 * LANES:3 * LANES], cos, s1, s2)
    vsel = kv[:, 3 * LANES:4 * LANES]
    kwin = _rope(kv[:, 4 * LANES:5 * LANES], cos, s1, s2)
    vwin = kv[:, 5 * LANES:6 * LANES]
    if sample:
        for ref, val in ((kcmp_ref, kcmp), (vcmp_ref, vcmp), (ksel_ref, ksel), (vsel_ref, vsel),
                         (kwin_ref, kwin), (vwin_ref, vwin)):
            ref[...] = val
    else:
        ksel_t = ksel.T
        kwin_t = kwin.T
        for ref, val in ((kcmp_ref, kcmp.T), (vcmp_ref, vcmp.T), (ksel_ref, ksel_t), (vsel_ref, vsel.T),
                         (kwin_ref, kwin_t), (vwin_ref, vwin.T)):
            ref[0] = val
        kcr_ref[...] = kcmp
        vcr_ref[...] = vcmp
        base = (pl.program_id(0) % pos_tiles) * tm
        blk = (base + _iota((LANES, tm), 1)) // SEL_BLOCK
        onehot = jnp.where(_iota((LANES, tm), 0) == blk, 1.0, 0.0)
        kaug_ref[0, 0:LANES] = ksel_t.astype(BF16)
        kaug_ref[0, LANES:2 * LANES] = onehot.astype(BF16)
        kwt_ref[0] = kwin_t.astype(BF16)
        for ref, val in ((vsb_ref, vsel), (vwb_ref, vwin)):
            ref[0] = jnp.where(low, val, 1.0).astype(BF16)
            ref[1] = jnp.where(low, 1.0, val).astype(BF16)

    gat_ref[...] = _sigmoid(proj(OFF_GL, OFF_GL + LANES))
    za = proj(OFF_ZA, OFF_ZA + ATTN_WIDTH)
    sza_ref[...] = za * _sigmoid(za)
    sra_ref[...] = _sigmoid(proj(OFF_RA, OFF_RA + D_MODEL))

    v = proj(OFF_V, OFF_V + GMLP_WIDTH)
    gv = _gelu(v)
    mu = jnp.mean(gv, axis=-1, keepdims=True)
    var = jnp.mean(jnp.square(gv - mu), axis=-1, keepdims=True)
    vn = ((gv - mu) * lax.rsqrt(var + EPS)) * vng_ref[...] + vnb_ref[...]
    if sample:
        vn_ref[...] = vn
    vnb16 = vn.astype(BF16)
    n_chunk = tm // CHUNK
    tri = _iota((CHUNK, CHUNK), 0) >= _iota((CHUNK, CHUNK), 1)
    mixed = []
    for g in range(GMLP_GROUPS):
        wm = jnp.where(tri, wsp_ref[g], 0.0).astype(BF16)
        cat = jnp.concatenate(
            [vnb16[c * CHUNK:(c + 1) * CHUNK, g * LANES:(g + 1) * LANES] for c in range(n_chunk)], axis=1)
        mixed.append(_dot(wm, cat))
    bsp = bsp_ref[...]
    sg = jnp.concatenate(
        [jnp.concatenate([mixed[g][:, c * LANES:(c + 1) * LANES] for g in range(GMLP_GROUPS)], axis=1) + bsp
         for c in range(n_chunk)], axis=0)
    u = proj(OFF_U, OFF_U + GMLP_WIDTH)
    zb = proj(OFF_ZB, OFF_ZB + GMLP_WIDTH)
    t = (_gelu(u) * sg) * (zb * _sigmoid(zb))
    pb = _dot(t.astype(BF16), wpb_ref[...])
    gpb_ref[...] = _sigmoid(proj(OFF_RB, OFF_RB + D_MODEL)) * pb


def _in_project(x2d, tables, prm, tm, sample, seq):
    T = x2d.shape[0]
    nt = T // tm
    pos_tiles = tables[0].shape[0] // tm
    const = lambda *shape: pl.BlockSpec(shape, lambda i: (0,) * len(shape))
    row = lambda width: pl.BlockSpec((tm, width), lambda i: (i, 0))
    tab = pl.BlockSpec((tm, LANES), lambda i: (i % pos_tiles, 0))
    in_specs = [
        row(D_MODEL), const(1, D_MODEL),
        pl.BlockSpec((W_TOT, D_MODEL), lambda i: (0, 0), pipeline_mode=pl.Buffered(1)),
        tab, tab, tab,
        const(GMLP_GROUPS, CHUNK, CHUNK), const(CHUNK, GMLP_WIDTH), const(1, GMLP_WIDTH), const(1, GMLP_WIDTH),
        const(GMLP_WIDTH, D_MODEL),
    ]
    f32rows = lambda width: jax.ShapeDtypeStruct((T, width), F32)
    tail_shapes = [f32rows(LANES), f32rows(ATTN_WIDTH), f32rows(D_MODEL), f32rows(D_MODEL)]
    tail_specs = [row(LANES), row(ATTN_WIDTH), row(D_MODEL), row(D_MODEL)]
    if sample:
        out_shape = [f32rows(ATTN_WIDTH)] + [f32rows(KV_WIDTH)] * 6 + tail_shapes + [f32rows(GMLP_WIDTH)]
        out_specs = [row(ATTN_WIDTH)] + [row(KV_WIDTH)] * 6 + tail_specs + [row(GMLP_WIDTH)]
    else:
        nb = T // seq
        bf = lambda *shape: jax.ShapeDtypeStruct(shape, BF16)
        tposed = lambda rows: pl.BlockSpec((1, rows, tm), lambda i: (i // pos_tiles, 0, i % pos_tiles))
        out_shape = ([bf(N_HEADS, T, LANES)] + [jax.ShapeDtypeStruct((nb, KV_WIDTH, seq), F32)] * 6
                     + [f32rows(KV_WIDTH)] * 2
                     + [bf(nb, 2 * LANES, seq), bf(nb, LANES, seq), bf(KV_HEADS, T, LANES), bf(KV_HEADS, T, LANES)]
                     + tail_shapes)
        heads = lambda n: pl.BlockSpec((n, tm, LANES), lambda i: (0, i, 0))
        out_specs = ([heads(N_HEADS)] + [tposed(KV_WIDTH)] * 6 + [row(KV_WIDTH)] * 2
                     + [tposed(2 * LANES), tposed(LANES), heads(KV_HEADS), heads(KV_HEADS)] + tail_specs)
    return pl.pallas_call(
        functools.partial(_inproj_body, tm, sample, pos_tiles),
        grid=(nt,),
        in_specs=in_specs,
        out_specs=out_specs,
        out_shape=out_shape,
        compiler_params=pltpu.CompilerParams(dimension_semantics=("arbitrary",), vmem_limit_bytes=VMEM_LIMIT),
        name="in_project_sample" if sample else "in_project_prompt",
    )(x2d, prm["norm_g"], prm["w_in"], *tables, prm["w_sp_s" if sample else "w_sp"],
      prm["b_sp_s" if sample else "b_sp"], prm["v_norm_g"], prm["v_norm_b"], prm["w_pb"])


def _strided_halfblocks(load_rows):
    return jnp.concatenate([load_rows(t) for t in range(CMP_STRIDE)], axis=1).astype(BF16)


def _compress_mlp(xcat, n, w1_ref, b1_ref, w2_ref, b2_ref):
    hh = _dot(xcat, w1_ref[...])
    hid = jnp.concatenate(
        [hh[:, 0:LANES] + pltpu.roll(hh[:, LANES:2 * LANES], n - 1, 0),
         hh[:, 2 * LANES:3 * LANES] + pltpu.roll(hh[:, 3 * LANES:4 * LANES], n - 1, 0)], axis=1) + b1_ref[...]
    return _dot(_gelu(hid).astype(BF16), w2_ref[...]) + b2_ref[...]


def _compress_prompt_body(n, kr_ref, vr_ref, w1k, b1k, w2k, b2k, w1v, b1v, w2v, b2v, cos_ref, s1_ref, s2_ref,
                          ovl_ref, kct_ref, vco_ref):
    kc = _compress_mlp(_strided_halfblocks(lambda t: kr_ref[0, pl.ds(t, n, stride=CMP_STRIDE), :]),
                       n, w1k, b1k, w2k, b2k)
    kct_ref[0] = _rope(kc, cos_ref[...], s1_ref[...], s2_ref[...]).T.astype(BF16)
    vc = _compress_mlp(_strided_halfblocks(lambda t: vr_ref[0, pl.ds(t, n, stride=CMP_STRIDE), :]),
                       n, w1v, b1v, w2v, b2v)
    vco_ref[0, :, 0:LANES] = vc.astype(BF16)
    vco_ref[0, :, LANES:2 * LANES] = ovl_ref[...]


def _compress_prompt(kr, vr, prm, tables, ovl):
    B, S, _ = kr.shape
    n = S // CMP_STRIDE
    const = lambda *shape: pl.BlockSpec(shape, lambda b: (0,) * len(shape))
    rows = pl.BlockSpec((1, S, LANES), lambda b: (b, 0, 0))
    wspecs = [const(CMP_STRIDE * LANES, 4 * LANES), const(1, 2 * LANES), const(2 * LANES, LANES), const(1, LANES)]
    return pl.pallas_call(
        functools.partial(_compress_prompt_body, n),
        grid=(B,),
        in_specs=[rows, rows] + wspecs + wspecs + [const(n, LANES)] * 3 + [const(n, LANES)],
        out_specs=[pl.BlockSpec((1, LANES, n), lambda b: (b, 0, 0)),
                   pl.BlockSpec((1, n, 2 * LANES), lambda b: (b, 0, 0))],
        out_shape=[jax.ShapeDtypeStruct((B, LANES, n), BF16),
                   jax.ShapeDtypeStruct((B, n, 2 * LANES), BF16)],
        compiler_params=pltpu.CompilerParams(dimension_semantics=("arbitrary",), vmem_limit_bytes=VMEM_LIMIT),
        name="compress_prompt",
    )(kr, vr, *prm["cmp_k"], *prm["cmp_v"], *tables, ovl)


def _start_pages(pt_ref, pools, bufs, sems, b, n_pages):
    def body(p, carry):
        page = pt_ref[b, p]
        for i, (pool, buf, sem) in enumerate(zip(pools, bufs, sems)):
            pltpu.make_async_copy(pool.at[page], buf.at[p], sem).start(priority=i % 2)
        return carry
    lax.fori_loop(0, n_pages, body, 0)


def _wait_pages(pools, bufs, sems, n_pages):
    for pool, buf, sem in zip(pools, bufs, sems):
        pltpu.make_async_copy(pool.at[pl.ds(0, n_pages)], buf, sem).wait()


def _compress_sample_body(n, n_pages, pt_ref, pk_ref, pv_ref, w1k, b1k, w2k, b2k, w1v, b1v, w2v, b2v,
                          cos_ref, s1_ref, s2_ref, perm_ref, kct_ref, vc_ref, kb0, kb1, vb0, vb1, xk, xv, sem):
    s = pl.program_id(0)
    bufs = ((kb0, vb0, 0), (kb1, vb1, 1))

    def start(b, slot):
        kb, vb, si = bufs[slot]
        _start_pages(pt_ref, (pk_ref, pv_ref), (kb, vb), (sem.at[2 * si], sem.at[2 * si + 1]), b, n_pages)

    hb_per_page = PAGE_SIZE // CMP_STRIDE

    def halfblocks(buf, xcat):
        pairs = jnp.concatenate([buf[pl.ds(0, n_pages // 2, stride=2)], buf[pl.ds(1, n_pages // 2, stride=2)]], axis=2)
        y = _dot(pairs.reshape(n_pages // 2 * LANES, 2 * PAGE_SIZE).astype(BF16), perm_ref[...])
        for p in range(n_pages):
            z = y[p // 2 * LANES:(p // 2 + 1) * LANES, p % 2 * PAGE_SIZE:(p % 2 + 1) * PAGE_SIZE].T
            for t in range(CMP_STRIDE):
                xcat[p * hb_per_page:(p + 1) * hb_per_page, t * LANES:(t + 1) * LANES] = (
                    z[t * hb_per_page:(t + 1) * hb_per_page])

    def finish(slot):
        kb, vb, si = bufs[slot]
        _wait_pages((pk_ref, pv_ref), (kb, vb), (sem.at[2 * si], sem.at[2 * si + 1]), n_pages)
        halfblocks(kb, xk)
        halfblocks(vb, xv)
        kc = _compress_mlp(xk[...].astype(BF16), n, w1k, b1k, w2k, b2k)
        kct_ref[slot] = _rope(kc, cos_ref[...], s1_ref[...], s2_ref[...]).T.astype(BF16)
        vc_ref[slot] = _compress_mlp(xv[...].astype(BF16), n, w1v, b1v, w2v, b2v).astype(BF16)

    @pl.when(s == 0)
    def _():
        start(0, 0)

    start(2 * s + 1, 1)
    finish(0)

    @pl.when(s + 1 < pl.num_programs(0))
    def _():
        start(2 * s + 2, 0)

    finish(1)


def _compress_sample(page_table, pool_k, pool_v, prm, tables):
    Bd, n_pages = page_table.shape
    past = n_pages * PAGE_SIZE
    n = past // CMP_STRIDE
    const = lambda *shape: pl.BlockSpec(shape, lambda s, pt: (0,) * len(shape))
    wspecs = [const(CMP_STRIDE * LANES, 4 * LANES), const(1, 2 * LANES), const(2 * LANES, LANES), const(1, LANES)]
    anyspec = pl.BlockSpec(memory_space=pl.ANY)
    grid_spec = pltpu.PrefetchScalarGridSpec(
        num_scalar_prefetch=1,
        grid=(Bd // 2,),
        in_specs=[anyspec, anyspec] + wspecs + wspecs + [const(n, LANES)] * 3 + [const(2 * PAGE_SIZE, 2 * PAGE_SIZE)],
        out_specs=[pl.BlockSpec((2, LANES, n), lambda s, pt: (s, 0, 0)),
                   pl.BlockSpec((2, n, LANES), lambda s, pt: (s, 0, 0))],
        scratch_shapes=[pltpu.VMEM((n_pages, LANES, PAGE_SIZE), F32)] * 4
        + [pltpu.VMEM((n, CMP_STRIDE * LANES), F32)] * 2 + [pltpu.SemaphoreType.DMA((4,))],
    )
    hb = PAGE_SIZE // CMP_STRIDE
    perm = np.zeros((2 * PAGE_SIZE, 2 * PAGE_SIZE), np.float32)
    for side in range(2):
        for jj in range(hb):
            for t in range(CMP_STRIDE):
                perm[side * PAGE_SIZE + CMP_STRIDE * jj + t, side * PAGE_SIZE + t * hb + jj] = 1.0
    return pl.pallas_call(
        functools.partial(_compress_sample_body, n, n_pages),
        grid_spec=grid_spec,
        out_shape=[jax.ShapeDtypeStruct((Bd, LANES, n), BF16), jax.ShapeDtypeStruct((Bd, n, LANES), BF16)],
        compiler_params=pltpu.CompilerParams(dimension_semantics=("arbitrary",), vmem_limit_bytes=VMEM_LIMIT),
        name="compress_sample",
    )(page_table, pool_k, pool_v, *prm["cmp_k"], *prm["cmp_v"], *tables, jnp.asarray(perm, BF16))


def _mixer_tail(o, sza, sra, gpb, x, wpa_ref, wo_ref, fg_ref):
    pa = _dot((o * sza).astype(BF16), wpa_ref[...])
    merged = sra * pa + gpb
    hn = x + _dot(merged.astype(BF16), wo_ref[...])
    r = lax.rsqrt(jnp.mean(hn * hn, axis=-1, keepdims=True) + EPS)
    return (hn * r) * fg_ref[...]


def _topk_mask(score, blk, n_blocks, axis):
    cnt = jnp.zeros(score.shape, jnp.int32)
    for sp in range(n_blocks):
        b = lax.slice_in_dim(score, sp, sp + 1, axis=axis)
        ge = jnp.where(b >= score, 1, 0)
        gt = jnp.where(b > score, 1, 0)
        cnt = cnt + jnp.where(blk > sp, ge, gt)
    return cnt < TOP_N


def _attn_prompt_body(n_sel, q_ref, ka_ref, vs_ref, kw_ref, vw_ref, kc_ref, vco_ref, gat_ref, sza_ref, sra_ref,
                      gpb_ref, x_ref, wpa_ref, wo_ref, fg_ref, y_ref):
    i = pl.program_id(1)
    M = HPG * TQ
    G = range(KV_HEADS)
    qpos = i * TQ + (_iota((M, 1), 0) & (TQ - 1))
    n_cmp_pad = kc_ref.shape[2]
    gt = gat_ref[...]
    qs = [q_ref[HPG * g:HPG * (g + 1)].reshape(M, LANES) for g in G]

    mk = (_iota((M, n_cmp_pad), 1) * CMP_STRIDE + (CMP_BLOCK - 1)) <= qpos
    o_cmp, qa = [], []
    blk = _iota((n_sel, TQ), 0)
    cur = (i * TQ + _iota((n_sel, TQ), 1)) // SEL_BLOCK
    valid = blk <= cur
    forced = (valid & (blk > cur - N_LOCAL_BLOCKS)) | (blk < N_INIT_BLOCKS)
    scores = []
    for g in G:
        s = jnp.where(mk, _dot(qs[g], kc_ref[0]), NEG)
        e = jnp.exp2(s - jnp.max(s, axis=-1, keepdims=True))
        p = jnp.where(mk, e / jnp.sum(e, axis=-1, keepdims=True), 0.0)
        r = _dot(p.astype(BF16), vco_ref[0])
        o_cmp.append(r[:, 0:LANES])
        impc = r[:, LANES:2 * LANES]
        imp = impc[0:TQ] + impc[TQ:2 * TQ] + impc[2 * TQ:3 * TQ] + impc[3 * TQ:4 * TQ]
        sc = imp.T[HEAD_DIM:HEAD_DIM + n_sel]
        scores.append(jnp.where(forced, BIG, jnp.where(valid, sc, NEG)))

    last_blk = ((i + 1) * TQ - 1) // SEL_BLOCK

    def count_group(k, cnts):
        out = []
        for g in G:
            c = cnts[g]
            for sp in range(k * RANK_GROUP, (k + 1) * RANK_GROUP):
                b = scores[g][sp:sp + 1]
                c = c + jnp.where(blk > sp, jnp.where(b >= scores[g], 1, 0), jnp.where(b > scores[g], 1, 0))
            out.append(c)
        return tuple(out)

    cnts = (jnp.zeros((n_sel, TQ), jnp.int32),) * KV_HEADS
    for k in range(n_sel // RANK_GROUP):
        cnts = lax.cond(k * RANK_GROUP <= last_blk, functools.partial(count_group, k), lambda c: c, cnts)
    for g in G:
        selneg = jnp.where(cnts[g] < TOP_N, 0.0, NEG)
        seln_t = jnp.concatenate([selneg, jnp.zeros((LANES - n_sel, TQ), F32)], axis=0).T.astype(BF16)
        qa.append(jnp.concatenate([qs[g], jnp.concatenate([seln_t] * HPG, axis=0)], axis=1))

    def update(carry, s, v):
        m, acc = carry
        mn = jnp.maximum(m, jnp.max(s, axis=-1, keepdims=True))
        acc = jnp.exp2(m - mn) * acc + _dot(jnp.exp2((s - mn).astype(BF16)), v)
        return mn, acc

    def step(kt, carries, causal):
        off = pl.multiple_of(kt * KT, KT)
        k = ka_ref[0, :, pl.ds(off, KT)]
        ss = [_dot(qa[g], k) for g in G]
        if causal:
            keep = kt * KT + _iota((M, KT), 1) <= qpos
            ss = [jnp.where(keep, s, NEG) for s in ss]
        return tuple(update(carries[g], ss[g], vs_ref[g, pl.ds(off, KT), :]) for g in G)

    n_full = i // (KT // TQ)
    init = (jnp.full((M, 1), NEG, F32), jnp.zeros((M, LANES), F32))
    carries = lax.fori_loop(0, n_full // 2, lambda kp, c: step(2 * kp + 1, step(2 * kp, c, False), False),
                            (init,) * KV_HEADS)
    carries = lax.cond(n_full % 2 == 1, lambda c: step(n_full - 1, c, False), lambda c: c, carries)
    carries = step(n_full, carries, True)
    sum_lane = [(KV_HEADS - 1 - g) * HEAD_DIM for g in G]
    o_sel = [carries[g][1] / carries[g][1][:, sum_lane[g]:sum_lane[g] + 1] for g in G]

    st = pl.multiple_of(jnp.maximum(i * TQ - WINDOW, 0), TQ)
    kpos = st + _iota((M, WIN_KEYS), 1)
    mkw = (kpos <= qpos) & (kpos > qpos - WINDOW)
    kw = kw_ref[0, :, pl.ds(st, WIN_KEYS)]
    o_groups = []
    for g in G:
        s = jnp.where(mkw, _dot(qs[g], kw), NEG)
        e = jnp.exp2(s - jnp.max(s, axis=-1, keepdims=True))
        r = _dot(e.astype(BF16), vw_ref[g, pl.ds(st, WIN_KEYS), :])
        o_win = r / r[:, sum_lane[g]:sum_lane[g] + 1]

        def gcol(c, g=g):
            return jnp.concatenate(
                [gt[:, 3 * (HPG * g + j) + c:3 * (HPG * g + j) + c + 1] for j in range(HPG)], axis=0)

        o_groups.append(gcol(0) * o_cmp[g] + gcol(1) * o_sel[g] + gcol(2) * o_win)

    low = _iota((TQ, LANES), 1) < HEAD_DIM
    slabs = []
    for pp in range(N_HEADS // 2):
        g, j0 = pp // 2, (2 * pp) % HPG
        a = o_groups[g][j0 * TQ:(j0 + 1) * TQ]
        b = o_groups[g][(j0 + 1) * TQ:(j0 + 2) * TQ]
        if g == 0:
            b = pltpu.roll(b, HEAD_DIM, 1)
        else:
            a = pltpu.roll(a, HEAD_DIM, 1)
        slabs.append(jnp.where(low, a, b))
    o = jnp.concatenate(slabs, axis=1)
    y_ref[...] = _mixer_tail(o, sza_ref[...], sra_ref[...], gpb_ref[...], x_ref[...], wpa_ref, wo_ref, fg_ref)


def _attn_prompt(B, S, q_hm, kaug, vsb, kwt, vwb, kct, vco, gates, sza, sra, gpb, x2d, prm):
    nq = S // TQ
    n_sel = S // SEL_BLOCK
    n_cmp_pad = kct.shape[2]
    row = lambda width: pl.BlockSpec((TQ, width), lambda b, i: (b * nq + i, 0))
    const = lambda *shape: pl.BlockSpec(shape, lambda b, i: (0,) * len(shape))
    batch = lambda *shape: pl.BlockSpec((1,) + shape, lambda b, i: (b,) + (0,) * len(shape))
    in_specs = [
        pl.BlockSpec((N_HEADS, TQ, LANES), lambda b, i: (0, b * nq + i, 0)),
        batch(2 * LANES, S),
        pl.BlockSpec((KV_HEADS, S, LANES), lambda b, i: (0, b, 0)),
        batch(LANES, S),
        pl.BlockSpec((KV_HEADS, S, LANES), lambda b, i: (0, b, 0)),
        batch(LANES, n_cmp_pad),
        batch(n_cmp_pad, 2 * LANES),
        row(LANES), row(ATTN_WIDTH), row(D_MODEL), row(D_MODEL), row(D_MODEL),
        const(ATTN_WIDTH, D_MODEL), const(D_MODEL, D_MODEL), const(1, D_MODEL),
    ]
    return pl.pallas_call(
        functools.partial(_attn_prompt_body, n_sel),
        grid=(B, nq),
        in_specs=in_specs,
        out_specs=row(D_MODEL),
        out_shape=jax.ShapeDtypeStruct((B * S, D_MODEL), F32),
        compiler_params=pltpu.CompilerParams(dimension_semantics=("arbitrary", "arbitrary"),
                                             vmem_limit_bytes=VMEM_LIMIT),
        name="attn_prompt",
    )(q_hm, kaug, vsb, kwt, vwb, kct, vco, gates, sza, sra, gpb, x2d, prm["w_pa"], prm["w_o"], prm["final_g"])


def _attn_sample_body(n_pages, tn, pt_ref, pk_ref, pv_ref, qa_ref, kc_ref, vc_ref, ovl_ref, oh_ref, kt_ref, vt_ref,
                      ckw_ref, cvw_ref, kwn_ref, vwn_ref, kwt_ref, vwt_ref, o_ref, okw_ref, ovw_ref,
                      kb0, kb1, vb0, vb1, sem):
    s_id = pl.program_id(0)
    past = n_pages * PAGE_SIZE
    n_past_blk = past // SEL_BLOCK
    n_blk_pad = ovl_ref.shape[1]
    R = N_HEADS * TOK_PAD
    GR = HPG * TOK_PAD
    wc = ckw_ref.shape[2]
    bufs = ((kb0, vb0, 0), (kb1, vb1, 1))
    t_row = _iota((R, 1), 0) & (TOK_PAD - 1)

    def start(b, slot):
        kb, vb, si = bufs[slot]
        _start_pages(pt_ref, (pk_ref, pv_ref), (kb, vb), (sem.at[2 * si], sem.at[2 * si + 1]), b, n_pages)

    def finish(slot):
        kb, vb, si = bufs[slot]
        qa = qa_ref[slot]

        n_cmp_pad = kc_ref.shape[2]
        s = _dot(qa, kc_ref[slot])
        mk = _iota((R, n_cmp_pad), 1) < n_cmp_pad - 1
        s = jnp.where(mk, s, NEG)
        e = jnp.exp(s - jnp.max(s, axis=-1, keepdims=True))
        p = jnp.where(mk, e / jnp.sum(e, axis=-1, keepdims=True), 0.0).astype(BF16)
        o_cmp = _dot(p, vc_ref[slot])
        impc = _dot(p, ovl_ref[...])
        imp = jnp.concatenate(
            [impc[g * GR:g * GR + TOK_PAD] + impc[g * GR + TOK_PAD:g * GR + 2 * TOK_PAD]
             + impc[g * GR + 2 * TOK_PAD:g * GR + 3 * TOK_PAD] + impc[g * GR + 3 * TOK_PAD:g * GR + 4 * TOK_PAD]
             for g in range(KV_HEADS)], axis=0)
        nr = KV_HEADS * TOK_PAD
        blk = _iota((nr, n_blk_pad), 1)
        cur = (past + jnp.minimum(_iota((nr, n_blk_pad), 0) & (TOK_PAD - 1), tn - 1)) // SEL_BLOCK
        valid = blk <= cur
        forced = (valid & (blk > cur - N_LOCAL_BLOCKS)) | (blk < N_INIT_BLOCKS)
        score = jnp.where(forced, BIG, jnp.where(valid, imp, NEG))
        sel = _topk_mask(score, blk, n_past_blk + 1, 1)
        selneg = jnp.where(sel, 0.0, NEG)
        selneg = jnp.concatenate(
            [selneg[g * TOK_PAD:(g + 1) * TOK_PAD] for g in range(KV_HEADS) for _ in range(HPG)], axis=0)
        low = _iota((R, LANES), 1) < SEL_HALF
        qaug = []
        for hf in range(n_past_blk // SEL_HALF):
            slab = selneg[:, (hf // 2) * LANES:(hf // 2 + 1) * LANES]
            if hf % 2:
                slab = pltpu.roll(slab, SEL_HALF, 1)
            qaug.append(jnp.concatenate([qa, jnp.where(low, slab, 0.0).astype(BF16)], axis=1))

        _wait_pages((pk_ref, pv_ref), (kb, vb), (sem.at[2 * si], sem.at[2 * si + 1]), n_pages)

        pages_per_half = SEL_HALF * SEL_BLOCK // PAGE_SIZE

        def half(buf, hf):
            return jnp.concatenate(
                [buf[hf * pages_per_half + j] for j in range(pages_per_half)], axis=1).astype(BF16)

        col = _iota((R, LANES), 1)
        ss = [_dot(qaug[hf], jnp.concatenate([half(kb, hf), oh_ref[...]], axis=0)) for hf in range(len(qaug))]
        ss.append(jnp.where(col <= t_row, _nt(qa, kt_ref[slot]), NEG))
        m = functools.reduce(jnp.maximum, [jnp.max(s, axis=-1, keepdims=True) for s in ss])
        es = [jnp.exp(s - m) for s in ss]
        l = functools.reduce(jnp.add, [jnp.sum(e, axis=-1, keepdims=True) for e in es])
        acc = _dot(es[-1].astype(BF16), vt_ref[slot])
        for hf in range(len(qaug)):
            acc = acc + _nt(es[hf].astype(BF16), half(vb, hf))
        o_sel = acc / l

        sa = jnp.where(_iota((R, wc), 1) > t_row, _dot(qa, ckw_ref[slot].astype(BF16)), NEG)
        sb = jnp.where(col <= t_row, _nt(qa, kwn_ref[slot]), NEG)
        m = jnp.maximum(jnp.max(sa, axis=-1, keepdims=True), jnp.max(sb, axis=-1, keepdims=True))
        ea, eb = jnp.exp(sa - m), jnp.exp(sb - m)
        den = jnp.sum(ea, axis=-1, keepdims=True) + jnp.sum(eb, axis=-1, keepdims=True)
        o_win = (_nt(ea.astype(BF16), cvw_ref[slot].astype(BF16)) + _dot(eb.astype(BF16), vwn_ref[slot])) / den

        o_ref[slot, 0] = o_cmp
        o_ref[slot, 1] = o_sel
        o_ref[slot, 2] = o_win

        newest = _iota((LANES, LANES), 1) >= LANES - tn
        for src, new, dst in ((ckw_ref, kwt_ref, okw_ref), (cvw_ref, vwt_ref, ovw_ref)):
            shifted = pltpu.roll(src[slot], wc - tn, 1)
            dst[slot, :, 0:wc - LANES] = shifted[:, 0:wc - LANES]
            dst[slot, :, wc - LANES:wc] = jnp.where(newest, new[slot], shifted[:, wc - LANES:wc])

    @pl.when(s_id == 0)
    def _():
        start(0, 0)

    start(2 * s_id + 1, 1)
    finish(0)

    @pl.when(s_id + 1 < pl.num_programs(0))
    def _():
        start(2 * s_id + 2, 0)

    finish(1)


def _attn_sample(page_table, pool_k, pool_v, tn, qa, kct, vc, ovl, ktail, vtail, ckw, cvw, kwn, vwn, kwt, vwt):
    Bd, n_pages = page_table.shape
    R = N_HEADS * TOK_PAD
    wc = ckw.shape[2]
    const = lambda *shape: pl.BlockSpec(shape, lambda s, pt: (0,) * len(shape))
    pair = lambda *shape: pl.BlockSpec((2,) + shape, lambda s, pt: (s,) + (0,) * len(shape))
    anyspec = pl.BlockSpec(memory_space=pl.ANY)
    half_keys = SEL_HALF * SEL_BLOCK
    onehot = np.zeros((LANES, half_keys), np.float32)
    onehot[np.arange(half_keys) // SEL_BLOCK, np.arange(half_keys)] = 1.0
    grid_spec = pltpu.PrefetchScalarGridSpec(
        num_scalar_prefetch=1,
        grid=(Bd // 2,),
        in_specs=[anyspec, anyspec, pair(R, LANES), pair(LANES, kct.shape[2]), pair(vc.shape[1], LANES),
                  const(*ovl.shape), const(LANES, half_keys), pair(LANES, LANES), pair(LANES, LANES),
                  pair(LANES, wc), pair(LANES, wc), pair(LANES, LANES), pair(LANES, LANES),
                  pair(LANES, LANES), pair(LANES, LANES)],
        out_specs=[pair(3, R, LANES), pair(LANES, wc), pair(LANES, wc)],
        scratch_shapes=[pltpu.VMEM((n_pages, LANES, PAGE_SIZE), F32)] * 4 + [pltpu.SemaphoreType.DMA((4,))],
    )
    return pl.pallas_call(
        functools.partial(_attn_sample_body, n_pages, tn),
        grid_spec=grid_spec,
        out_shape=[jax.ShapeDtypeStruct((Bd, 3, R, LANES), F32)] + [jax.ShapeDtypeStruct((Bd, LANES, wc), F32)] * 2,
        compiler_params=pltpu.CompilerParams(dimension_semantics=("arbitrary",), vmem_limit_bytes=VMEM_LIMIT),
        name="attn_sample",
    )(page_table, pool_k, pool_v, qa, kct, vc, ovl, jnp.asarray(onehot, BF16), ktail, vtail, ckw, cvw, kwn, vwn,
      kwt, vwt)


def _mixer_sample_body(o3_ref, g3_ref, sza_ref, sra_ref, gpb_ref, x_ref, wpa_ref, wo_ref, fg_ref, y_ref):
    o = g3_ref[0] * o3_ref[0] + g3_ref[1] * o3_ref[1] + g3_ref[2] * o3_ref[2]
    y_ref[...] = _mixer_tail(o, sza_ref[...], sra_ref[...], gpb_ref[...], x_ref[...], wpa_ref, wo_ref, fg_ref)


def _mixer_sample(o3, g3, sza, sra, gpb, x2d, prm):
    T = x2d.shape[0]
    full = lambda *shape: pl.BlockSpec(shape, lambda i: (0,) * len(shape))
    return pl.pallas_call(
        _mixer_sample_body,
        grid=(1,),
        in_specs=[full(3, T, ATTN_WIDTH), full(3, T, ATTN_WIDTH), full(T, ATTN_WIDTH), full(T, D_MODEL),
                  full(T, D_MODEL), full(T, D_MODEL), full(ATTN_WIDTH, D_MODEL), full(D_MODEL, D_MODEL),
                  full(1, D_MODEL)],
        out_specs=full(T, D_MODEL),
        out_shape=jax.ShapeDtypeStruct((T, D_MODEL), F32),
        compiler_params=pltpu.CompilerParams(dimension_semantics=("arbitrary",), vmem_limit_bytes=VMEM_LIMIT),
        name="mixer_sample",
    )(o3, g3, sza, sra, gpb, x2d, prm["w_pa"], prm["w_o"], prm["final_g"])


def _overlap(n_cmp, n_sel):
    cs = np.arange(n_cmp)[:, None] * CMP_STRIDE
    ss = np.arange(n_sel)[None, :] * SEL_BLOCK
    ov = np.minimum(cs + CMP_BLOCK, ss + SEL_BLOCK) - np.maximum(cs, ss)
    return np.clip(ov, 0, None).astype(np.float32) / CMP_BLOCK


def _cmp_weights(w1, b1, w2, b2):
    w1r = w1.reshape(2, CMP_STRIDE, HEAD_DIM, CMP_HIDDEN).transpose(1, 2, 0, 3)
    w1r = w1r.reshape(CMP_STRIDE, 1, HEAD_DIM, 2 * CMP_HIDDEN).astype(BF16)
    w2r = w2.reshape(1, CMP_HIDDEN, HEAD_DIM).astype(BF16)
    big = jnp.concatenate(
        [jnp.pad(w1r, ((0, 0), (0, 0), (0, 0), (g * 2 * CMP_HIDDEN, (KV_HEADS - 1 - g) * 2 * CMP_HIDDEN)))
         for g in range(KV_HEADS)], axis=1)
    w2b = jnp.concatenate(
        [jnp.pad(w2r, ((0, 0), (0, 0), (g * HEAD_DIM, (KV_HEADS - 1 - g) * HEAD_DIM))) for g in range(KV_HEADS)],
        axis=0)
    return (big.reshape(CMP_STRIDE * LANES, 4 * LANES), jnp.tile(b1, KV_HEADS)[None],
            w2b.reshape(KV_HEADS * CMP_HIDDEN, LANES), jnp.tile(b2, KV_HEADS)[None])


def _seq_minor(t):
    lead = t.shape[:-3]
    n = len(lead)
    return t.transpose(*range(n), n + 1, n + 2, n).reshape(*lead, KV_WIDTH, t.shape[-3])


def _seq_major(t):
    lead = t.shape[:-2]
    n = len(lead)
    return t.reshape(*lead, KV_HEADS, HEAD_DIM, t.shape[-1]).transpose(*range(n), n + 2, n, n + 1)


def kernel(x_prompt, x_sample, cache_k_cmp, cache_v_cmp, cache_k_sel, cache_v_sel, cache_k_win, cache_v_win, page_table, norm_g, w_in, cmp_k_w1, cmp_k_b1, cmp_k_w2, cmp_k_b2, cmp_v_w1, cmp_v_b1, cmp_v_w2, cmp_v_b2, v_norm_g, v_norm_b, w_spatial, b_spatial, w_pa, w_pb, w_o, final_g):
    B, S, _ = x_prompt.shape
    Bd, tn, _ = x_sample.shape
    depth = w_in.shape[0]
    assert depth == 1, "single-layer step"
    assert Bd * tn == CHUNK, "the sample tokens form one 128-row tile"
    n_pages = page_table.shape[1]
    past = n_pages * PAGE_SIZE

    split = OFF_GL + 3 * N_HEADS
    w = w_in[0]
    prm = {
        "norm_g": norm_g,
        "w_in": jnp.concatenate([w.T[:split], jnp.zeros((GL_PAD, D_MODEL), F32), w.T[split:]], axis=0).astype(BF16),
        "w_sp": w_spatial[0],
        "b_sp": jnp.repeat(b_spatial[0].T, LANES, axis=1),
        "w_sp_s": jnp.stack([jnp.kron(jnp.eye(CHUNK // tn, dtype=F32), w_spatial[0, g, :tn, :tn])
                             for g in range(GMLP_GROUPS)]),
        "b_sp_s": jnp.tile(jnp.repeat(b_spatial[0, :, :tn].T, LANES, axis=1), (CHUNK // tn, 1)),
        "v_norm_g": v_norm_g, "v_norm_b": v_norm_b,
        "w_pb": w_pb[0].astype(BF16), "w_pa": w_pa[0].astype(BF16), "w_o": w_o[0].astype(BF16),
        "final_g": final_g[None],
        "cmp_k": _cmp_weights(cmp_k_w1[0], cmp_k_b1[0], cmp_k_w2[0], cmp_k_b2[0]),
        "cmp_v": _cmp_weights(cmp_v_w1[0], cmp_v_b1[0], cmp_v_w2[0], cmp_v_b2[0]),
    }

    xp = x_prompt.reshape(B * S, D_MODEL)
    (q_hm, p_kcmp, p_vcmp, p_ksel, p_vsel, p_kwin, p_vwin, kcr, vcr, kaug, kwt, vsb, vwb,
     gates, sza, sra, gpb) = _in_project(xp, _rope_tables(np.arange(S)), prm, TM_PROMPT, False, S)
    n_half = S // CMP_STRIDE
    n_cmp = (S - CMP_BLOCK) // CMP_STRIDE + 1
    n_sel = S // SEL_BLOCK
    assert n_sel <= SEL_HALF
    ovl = np.zeros((n_half, LANES), np.float32)
    ovl[:n_cmp, HEAD_DIM:HEAD_DIM + n_sel] = _overlap(n_cmp, n_sel)
    cend_tables = _rope_tables(np.arange(n_half) * CMP_STRIDE + CMP_BLOCK - 1)
    kct, vco = _compress_prompt(kcr.reshape(B, S, LANES), vcr.reshape(B, S, LANES), prm, cend_tables,
                                jnp.asarray(ovl, BF16))
    y_prompt = _attn_prompt(B, S, q_hm, kaug, vsb, kwt, vwb, kct, vco, gates, sza, sra, gpb, xp, prm)

    xs = x_sample.reshape(Bd * tn, D_MODEL)
    pos_s = np.tile(past + np.arange(tn), Bd)
    (q_s, s_kcmp, s_vcmp, s_ksel, s_vsel, s_kwin, s_vwin, gates_s, sza_s, sra_s, gpb_s, vn_s) = _in_project(
        xs, _rope_tables(pos_s), prm, Bd * tn, True, tn)
    n_half_s = past // CMP_STRIDE
    pools = [_seq_minor(c[0]) for c in (cache_k_cmp, cache_v_cmp, cache_k_sel, cache_v_sel)]
    cend_s = _rope_tables(np.arange(n_half_s) * CMP_STRIDE + CMP_BLOCK - 1)
    kct_s, vc_s = _compress_sample(page_table, pools[0], pools[1], prm, cend_s)

    n_cmp_s = (past + tn - CMP_BLOCK) // CMP_STRIDE + 1
    n_blk_s = past // SEL_BLOCK + -(-tn // SEL_BLOCK)
    assert n_cmp_s == n_half_s - 1 and tn <= TOK_PAD and (past // SEL_BLOCK) % SEL_HALF == 0
    ovl_s = np.zeros((n_half_s, 2 * LANES), np.float32)
    ovl_s[:n_cmp_s, :n_blk_s] = _overlap(n_cmp_s, n_blk_s)
    q5 = q_s.reshape(Bd, tn, KV_HEADS, HPG, HEAD_DIM).transpose(0, 2, 3, 1, 4)
    q5 = jnp.pad(q5, ((0, 0), (0, 0), (0, 0), (0, TOK_PAD - tn), (0, 0))).reshape(Bd, KV_HEADS, HPG * TOK_PAD, HEAD_DIM)
    qa = jnp.concatenate(
        [jnp.pad(q5[:, g], ((0, 0), (0, 0), (g * HEAD_DIM, (KV_HEADS - 1 - g) * HEAD_DIM))) for g in range(KV_HEADS)],
        axis=1).astype(BF16)
    pad_rows = lambda t: jnp.pad(t.reshape(Bd, tn, LANES), ((0, 0), (0, LANES - tn), (0, 0))).astype(BF16)
    ckw = _seq_minor(cache_k_win[0])
    cvw = _seq_minor(cache_v_win[0])
    new_t = lambda t: jnp.pad(t.reshape(Bd, tn, LANES).transpose(0, 2, 1), ((0, 0), (0, 0), (LANES - tn, 0)))
    o3, s_k_win, s_v_win = _attn_sample(
        page_table, pools[2], pools[3], tn, qa, kct_s, vc_s, jnp.asarray(ovl_s, BF16),
        pad_rows(s_ksel), pad_rows(s_vsel), ckw, cvw, pad_rows(s_kwin), pad_rows(s_vwin), new_t(s_kwin), new_t(s_vwin))
    s_k_win = _seq_major(s_k_win)[None]
    s_v_win = _seq_major(s_v_win)[None]
    first_group = jnp.arange(N_HEADS * TOK_PAD)[:, None] < HPG * TOK_PAD
    o3r = jnp.where(first_group, o3[..., :HEAD_DIM], o3[..., HEAD_DIM:])
    o3r = o3r.reshape(Bd, 3, N_HEADS, TOK_PAD, HEAD_DIM)[:, :, :, :tn]
    o3r = o3r.transpose(1, 0, 3, 2, 4).reshape(3, Bd * tn, ATTN_WIDTH)
    g3 = gates_s[:, :3 * N_HEADS].reshape(Bd * tn, N_HEADS, 3).transpose(2, 0, 1)
    g3 = jnp.repeat(g3, HEAD_DIM, axis=2)
    y_sample = _mixer_sample(o3r, g3, sza_s, sra_s, gpb_s, xs, prm)

    kv5 = lambda t, b, n: t.reshape(1, b, n, KV_HEADS, HEAD_DIM)
    pw = min(WINDOW, S)
    return (y_prompt.reshape(B, S, D_MODEL), y_sample.reshape(Bd, tn, D_MODEL),
            _seq_major(p_kcmp)[None], _seq_major(p_vcmp)[None], _seq_major(p_ksel)[None], _seq_major(p_vsel)[None],
            _seq_major(p_kwin[:, :, S - pw:])[None], _seq_major(p_vwin[:, :, S - pw:])[None],
            kv5(s_kcmp, Bd, tn), kv5(s_vcmp, Bd, tn), kv5(s_ksel, Bd, tn), kv5(s_vsel, Bd, tn),
            s_k_win, s_v_win, vn_s.reshape(1, Bd, tn, GMLP_WIDTH))
```

```python
import functools

import numpy as np
import jax
import jax.numpy as jnp
from jax import lax
from jax.experimental import pallas as pl
from jax.experimental.pallas import tpu as pltpu

F32 = jnp.float32
BF16 = jnp.bfloat16

D_MODEL = 1024
HEAD_DIM = 64
N_HEADS = 8
KV_HEADS = 2
HPG = N_HEADS // KV_HEADS
ATTN_WIDTH = N_HEADS * HEAD_DIM
KV_WIDTH = KV_HEADS * HEAD_DIM
ROT_DIM = HEAD_DIM // 4
ROT_HALF = ROT_DIM // 2
ROPE_THETA = 500000.0
CMP_BLOCK = 32
CMP_STRIDE = 16
CMP_HIDDEN = 128
SEL_BLOCK = 64
TOP_N = 16
N_INIT_BLOCKS = 1
N_LOCAL_BLOCKS = 2
WINDOW = 512
CHUNK = 128
GMLP_GROUPS = 4
GMLP_WIDTH = 512
PAGE_SIZE = 128
NEG = -1e30
BIG = 1e30
EPS = 1e-6
LOG2E = 1.4426950408889634

LANES = 128
GL_PAD = LANES - 3 * N_HEADS

OFF_Q = 0
OFF_KV = ATTN_WIDTH
OFF_GL = OFF_KV + 6 * KV_WIDTH
OFF_ZA = OFF_GL + LANES
OFF_U = OFF_ZA + ATTN_WIDTH
OFF_V = OFF_U + GMLP_WIDTH
OFF_ZB = OFF_V + GMLP_WIDTH
OFF_RA = OFF_ZB + GMLP_WIDTH
OFF_RB = OFF_RA + D_MODEL
W_TOT = OFF_RB + D_MODEL

VMEM_LIMIT = 56 * 1024 * 1024

TM_PROMPT = 512
TQ = 128
KT = 512
WIN_KEYS = WINDOW + TQ
SEL_HALF = 64
TOK_PAD = 8
RANK_GROUP = 8


def _nt(a, b):
    return lax.dot_general(a, b, (((1,), (1,)), ((), ())), preferred_element_type=F32)


def _dot(a, b):
    return jnp.dot(a, b, preferred_element_type=F32)


def _iota(shape, dim):
    return lax.broadcasted_iota(jnp.int32, shape, dim)


def _rope(slab, cos, s1, s2):
    return slab * cos + pltpu.roll(slab, LANES - ROT_HALF, 1) * s1 + pltpu.roll(slab, ROT_HALF, 1) * s2


def _rope_tables(pos):
    pos = np.asarray(pos, np.float64)
    n = pos.shape[0]
    inv = np.power(np.float64(ROPE_THETA), -np.arange(0, ROT_DIM, 2, dtype=np.float64) / ROT_DIM)
    ang = pos[:, None] * inv[None, :]
    cos, sin = np.cos(ang), np.sin(ang)
    rest = HEAD_DIM - ROT_DIM
    c = np.concatenate([cos, cos, np.ones((n, rest))], axis=1)
    s1 = np.concatenate([-sin, np.zeros((n, HEAD_DIM - ROT_HALF))], axis=1)
    s2 = np.concatenate([np.zeros((n, ROT_HALF)), sin, np.zeros((n, rest))], axis=1)
    return tuple(jnp.asarray(np.tile(t, (1, LANES // HEAD_DIM)), F32) for t in (c, s1, s2))


def _sigmoid(x):
    return 1.0 / (1.0 + jnp.exp(-x))


def _gelu(x):
    return jax.nn.gelu(x, approximate=True)


def _inproj_body(tm, sample, pos_tiles, x_ref, ng_ref, w_ref, cos_ref, s1_ref, s2_ref, wsp_ref, bsp_ref,
                 vng_ref, vnb_ref, wpb_ref, *outs):
    if sample:
        (q_ref, kcmp_ref, vcmp_ref, ksel_ref, vsel_ref, kwin_ref, vwin_ref,
         gat_ref, sza_ref, sra_ref, gpb_ref, vn_ref) = outs
    else:
        (q_ref, kcmp_ref, vcmp_ref, ksel_ref, vsel_ref, kwin_ref, vwin_ref,
         kcr_ref, vcr_ref, kaug_ref, kwt_ref, vsb_ref, vwb_ref, gat_ref, sza_ref, sra_ref, gpb_ref) = outs

    x = x_ref[...]
    r = lax.rsqrt(jnp.mean(x * x, axis=-1, keepdims=True) + EPS)
    h = ((x * r) * ng_ref[...]).astype(BF16)

    def proj(lo, hi):
        return _nt(h, w_ref[lo:hi, :])

    cos, s1, s2 = cos_ref[...], s1_ref[...], s2_ref[...]
    low = _iota((tm, LANES), 1) < HEAD_DIM

    q = proj(OFF_Q, OFF_Q + ATTN_WIDTH)
    q_scale = HEAD_DIM ** -0.5 if sample else HEAD_DIM ** -0.5 * LOG2E
    for pp in range(N_HEADS // 2):
        slab = _rope(q[:, pp * LANES:(pp + 1) * LANES], cos, s1, s2) * q_scale
        if sample:
            q_ref[:, pp * LANES:(pp + 1) * LANES] = slab
        elif (2 * pp) // HPG == 0:
            q_ref[2 * pp] = jnp.where(low, slab, 0.0).astype(BF16)
            q_ref[2 * pp + 1] = jnp.where(low, pltpu.roll(slab, HEAD_DIM, 1), 0.0).astype(BF16)
        else:
            q_ref[2 * pp] = jnp.where(low, 0.0, pltpu.roll(slab, HEAD_DIM, 1)).astype(BF16)
            q_ref[2 * pp + 1] = jnp.where(low, 0.0, slab).astype(BF16)

    kv = proj(OFF_KV, OFF_KV + 6 * KV_WIDTH)
    kcmp = kv[:, 0:LANES]
    vcmp = kv[:, LANES:2 * LANES]
    ksel = _rope(kv[:, 2 * LANES:3 * LANES], cos, s1, s2)
    vsel = kv[:, 3 * LANES:4 * LANES]
    kwin = _rope(kv[:, 4 * LANES:5 * LANES], cos, s1, s2)
    vwin = kv[:, 5 * LANES:6 * LANES]
    if sample:
        for ref, val in ((kcmp_ref, kcmp), (vcmp_ref, vcmp), (ksel_ref, ksel), (vsel_ref, vsel),
                         (kwin_ref, kwin), (vwin_ref, vwin)):
            ref[...] = val
    else:
        ksel_t = ksel.T
        kwin_t = kwin.T
        for ref, val in ((kcmp_ref, kcmp.T), (vcmp_ref, vcmp.T), (ksel_ref, ksel_t), (vsel_ref, vsel.T),
                         (kwin_ref, kwin_t), (vwin_ref, vwin.T)):
            ref[0] = val
        kcr_ref[...] = kcmp
        vcr_ref[...] = vcmp
        base = (pl.program_id(0) % pos_tiles) * tm
        blk = (base + _iota((LANES, tm), 1)) // SEL_BLOCK
        onehot = jnp.where(_iota((LANES, tm), 0) == blk, 1.0, 0.0)
        kaug_ref[0, 0:LANES] = ksel_t.astype(BF16)
        kaug_ref[0, LANES:2 * LANES] = onehot.astype(BF16)
        kwt_ref[0] = kwin_t.astype(BF16)
        for ref, val in ((vsb_ref, vsel), (vwb_ref, vwin)):
            ref[0] = jnp.where(low, val, 1.0).astype(BF16)
            ref[1] = jnp.where(low, 1.0, val).astype(BF16)

    gat_ref[...] = _sigmoid(proj(OFF_GL, OFF_GL + LANES))
    za = proj(OFF_ZA, OFF_ZA + ATTN_WIDTH)
    sza_ref[...] = za * _sigmoid(za)
    sra_ref[...] = _sigmoid(proj(OFF_RA, OFF_RA + D_MODEL))

    v = proj(OFF_V, OFF_V + GMLP_WIDTH)
    gv = _gelu(v)
    mu = jnp.mean(gv, axis=-1, keepdims=True)
    var = jnp.mean(jnp.square(gv - mu), axis=-1, keepdims=True)
    vn = ((gv - mu) * lax.rsqrt(var + EPS)) * vng_ref[...] + vnb_ref[...]
    if sample:
        vn_ref[...] = vn
    vnb16 = vn.astype(BF16)
    n_chunk = tm // CHUNK
    tri = _iota((CHUNK, CHUNK), 0) >= _iota((CHUNK, CHUNK), 1)
    mixed = []
    for g in range(GMLP_GROUPS):
        wm = jnp.where(tri, wsp_ref[g], 0.0).astype(BF16)
        cat = jnp.concatenate(
            [vnb16[c * CHUNK:(c + 1) * CHUNK, g * LANES:(g + 1) * LANES] for c in range(n_chunk)], axis=1)
        mixed.append(_dot(wm, cat))
    bsp = bsp_ref[...]
    sg = jnp.concatenate(
        [jnp.concatenate([mixed[g][:, c * LANES:(c + 1) * LANES] for g in range(GMLP_GROUPS)], axis=1) + bsp
         for c in range(n_chunk)], axis=0)
    u = proj(OFF_U, OFF_U + GMLP_WIDTH)
    zb = proj(OFF_ZB, OFF_ZB + GMLP_WIDTH)
    t = (_gelu(u) * sg) * (zb * _sigmoid(zb))
    pb = _dot(t.astype(BF16), wpb_ref[...])
    gpb_ref[...] = _sigmoid(proj(OFF_RB, OFF_RB + D_MODEL)) * pb


def _in_project(x2d, tables, prm, tm, sample, seq):
    T = x2d.shape[0]
    nt = T // tm
    pos_tiles = tables[0].shape[0] // tm
    const = lambda *shape: pl.BlockSpec(shape, lambda i: (0,) * len(shape))
    row = lambda width: pl.BlockSpec((tm, width), lambda i: (i, 0))
    tab = pl.BlockSpec((tm, LANES), lambda i: (i % pos_tiles, 0))
    in_specs = [
        row(D_MODEL), const(1, D_MODEL),
        pl.BlockSpec((W_TOT, D_MODEL), lambda i: (0, 0), pipeline_mode=pl.Buffered(1)),
        tab, tab, tab,
        const(GMLP_GROUPS, CHUNK, CHUNK), const(CHUNK, GMLP_WIDTH), const(1, GMLP_WIDTH), const(1, GMLP_WIDTH),
        const(GMLP_WIDTH, D_MODEL),
    ]
    f32rows = lambda width: jax.ShapeDtypeStruct((T, width), F32)
    tail_shapes = [f32rows(LANES), f32rows(ATTN_WIDTH), f32rows(D_MODEL), f32rows(D_MODEL)]
    tail_specs = [row(LANES), row(ATTN_WIDTH), row(D_MODEL), row(D_MODEL)]
    if sample:
        out_shape = [f32rows(ATTN_WIDTH)] + [f32rows(KV_WIDTH)] * 6 + tail_shapes + [f32rows(GMLP_WIDTH)]
        out_specs = [row(ATTN_WIDTH)] + [row(KV_WIDTH)] * 6 + tail_specs + [row(GMLP_WIDTH)]
    else:
        nb = T // seq
        bf = lambda *shape: jax.ShapeDtypeStruct(shape, BF16)
        tposed = lambda rows: pl.BlockSpec((1, rows, tm), lambda i: (i // pos_tiles, 0, i % pos_tiles))
        out_shape = ([bf(N_HEADS, T, LANES)] + [jax.ShapeDtypeStruct((nb, KV_WIDTH, seq), F32)] * 6
                     + [f32rows(KV_WIDTH)] * 2
                     + [bf(nb, 2 * LANES, seq), bf(nb, LANES, seq), bf(KV_HEADS, T, LANES), bf(KV_HEADS, T, LANES)]
                     + tail_shapes)
        heads = lambda n: pl.BlockSpec((n, tm, LANES), lambda i: (0, i, 0))
        out_specs = ([heads(N_HEADS)] + [tposed(KV_WIDTH)] * 6 + [row(KV_WIDTH)] * 2
                     + [tposed(2 * LANES), tposed(LANES), heads(KV_HEADS), heads(KV_HEADS)] + tail_specs)
    return pl.pallas_call(
        functools.partial(_inproj_body, tm, sample, pos_tiles),
        grid=(nt,),
        in_specs=in_specs,
        out_specs=out_specs,
        out_shape=out_shape,
        compiler_params=pltpu.CompilerParams(dimension_semantics=("arbitrary",), vmem_limit_bytes=VMEM_LIMIT),
        name="in_project_sample" if sample else "in_project_prompt",
    )(x2d, prm["norm_g"], prm["w_in"], *tables, prm["w_sp_s" if sample else "w_sp"],
      prm["b_sp_s" if sample else "b_sp"], prm["v_norm_g"], prm["v_norm_b"], prm["w_pb"])


def _strided_halfblocks(load_rows):
    return jnp.concatenate([load_rows(t) for t in range(CMP_STRIDE)], axis=1).astype(BF16)


def _compress_mlp(xcat, n, w1_ref, b1_ref, w2_ref, b2_ref):
    hh = _dot(xcat, w1_ref[...])
    hid = jnp.concatenate(
        [hh[:, 0:LANES] + pltpu.roll(hh[:, LANES:2 * LANES], n - 1, 0),
         hh[:, 2 * LANES:3 * LANES] + pltpu.roll(hh[:, 3 * LANES:4 * LANES], n - 1, 0)], axis=1) + b1_ref[...]
    return _dot(_gelu(hid).astype(BF16), w2_ref[...]) + b2_ref[...]


def _compress_prompt_body(n, kr_ref, vr_ref, w1k, b1k, w2k, b2k, w1v, b1v, w2v, b2v, cos_ref, s1_ref, s2_ref,
                          ovl_ref, kct_ref, vco_ref):
    kc = _compress_mlp(_strided_halfblocks(lambda t: kr_ref[0, pl.ds(t, n, stride=CMP_STRIDE), :]),
                       n, w1k, b1k, w2k, b2k)
    kct_ref[0] = _rope(kc, cos_ref[...], s1_ref[...], s2_ref[...]).T.astype(BF16)
    vc = _compress_mlp(_strided_halfblocks(lambda t: vr_ref[0, pl.ds(t, n, stride=CMP_STRIDE), :]),
                       n, w1v, b1v, w2v, b2v)
    vco_ref[0, :, 0:LANES] = vc.astype(BF16)
    vco_ref[0, :, LANES:2 * LANES] = ovl_ref[...]


def _compress_prompt(kr, vr, prm, tables, ovl):
    B, S, _ = kr.shape
    n = S // CMP_STRIDE
    const = lambda *shape: pl.BlockSpec(shape, lambda b: (0,) * len(shape))
    rows = pl.BlockSpec((1, S, LANES), lambda b: (b, 0, 0))
    wspecs = [const(CMP_STRIDE * LANES, 4 * LANES), const(1, 2 * LANES), const(2 * LANES, LANES), const(1, LANES)]
    return pl.pallas_call(
        functools.partial(_compress_prompt_body, n),
        grid=(B,),
        in_specs=[rows, rows] + wspecs + wspecs + [const(n, LANES)] * 3 + [const(n, LANES)],
        out_specs=[pl.BlockSpec((1, LANES, n), lambda b: (b, 0, 0)),
                   pl.BlockSpec((1, n, 2 * LANES), lambda b: (b, 0, 0))],
        out_shape=[jax.ShapeDtypeStruct((B, LANES, n), BF16),
                   jax.ShapeDtypeStruct((B, n, 2 * LANES), BF16)],
        compiler_params=pltpu.CompilerParams(dimension_semantics=("arbitrary",), vmem_limit_bytes=VMEM_LIMIT),
        name="compress_prompt",
    )(kr, vr, *prm["cmp_k"], *prm["cmp_v"], *tables, ovl)


def _start_pages(pt_ref, pools, bufs, sems, b, n_pages):
    def body(p, carry):
        page = pt_ref[b, p]
        for i, (pool, buf, sem) in enumerate(zip(pools, bufs, sems)):
            pltpu.make_async_copy(pool.at[page], buf.at[p], sem).start(priority=i % 2)
        return carry
    lax.fori_loop(0, n_pages, body, 0)


def _wait_pages(pools, bufs, sems, n_pages):
    for pool, buf, sem in zip(pools, bufs, sems):
        pltpu.make_async_copy(pool.at[pl.ds(0, n_pages)], buf, sem).wait()


def _compress_sample_body(n, n_pages, pt_ref, pk_ref, pv_ref, w1k, b1k, w2k, b2k, w1v, b1v, w2v, b2v,
                          cos_ref, s1_ref, s2_ref, perm_ref, kct_ref, vc_ref, kb0, kb1, vb0, vb1, xk, xv, sem):
    s = pl.program_id(0)
    bufs = ((kb0, vb0, 0), (kb1, vb1, 1))

    def start(b, slot):
        kb, vb, si = bufs[slot]
        _start_pages(pt_ref, (pk_ref, pv_ref), (kb, vb), (sem.at[2 * si], sem.at[2 * si + 1]), b, n_pages)

    hb_per_page = PAGE_SIZE // CMP_STRIDE

    def halfblocks(buf, xcat):
        pairs = jnp.concatenate([buf[pl.ds(0, n_pages // 2, stride=2)], buf[pl.ds(1, n_pages // 2, stride=2)]], axis=2)
        y = _dot(pairs.reshape(n_pages // 2 * LANES, 2 * PAGE_SIZE).astype(BF16), perm_ref[...])
        for p in range(n_pages):
            z = y[p // 2 * LANES:(p // 2 + 1) * LANES, p % 2 * PAGE_SIZE:(p % 2 + 1) * PAGE_SIZE].T
            for t in range(CMP_STRIDE):
                xcat[p * hb_per_page:(p + 1) * hb_per_page, t * LANES:(t + 1) * LANES] = (
                    z[t * hb_per_page:(t + 1) * hb_per_page])

    def finish(slot):
        kb, vb, si = bufs[slot]
        _wait_pages((pk_ref, pv_ref), (kb, vb), (sem.at[2 * si], sem.at[2 * si + 1]), n_pages)
        halfblocks(kb, xk)
        halfblocks(vb, xv)
        kc = _compress_mlp(xk[...].astype(BF16), n, w1k, b1k, w2k, b2k)
        kct_ref[slot] = _rope(kc, cos_ref[...], s1_ref[...], s2_ref[...]).T.astype(BF16)
        vc_ref[slot] = _compress_mlp(xv[...].astype(BF16), n, w1v, b1v, w2v, b2v).astype(BF16)

    @pl.when(s == 0)
    def _():
        start(0, 0)

    start(2 * s + 1, 1)
    finish(0)

    @pl.when(s + 1 < pl.num_programs(0))
    def _():
        start(2 * s + 2, 0)

    finish(1)


def _compress_sample(page_table, pool_k, pool_v, prm, tables):
    Bd, n_pages = page_table.shape
    past = n_pages * PAGE_SIZE
    n = past // CMP_STRIDE
    const = lambda *shape: pl.BlockSpec(shape, lambda s, pt: (0,) * len(shape))
    wspecs = [const(CMP_STRIDE * LANES, 4 * LANES), const(1, 2 * LANES), const(2 * LANES, LANES), const(1, LANES)]
    anyspec = pl.BlockSpec(memory_space=pl.ANY)
    grid_spec = pltpu.PrefetchScalarGridSpec(
        num_scalar_prefetch=1,
        grid=(Bd // 2,),
        in_specs=[anyspec, anyspec] + wspecs + wspecs + [const(n, LANES)] * 3 + [const(2 * PAGE_SIZE, 2 * PAGE_SIZE)],
        out_specs=[pl.BlockSpec((2, LANES, n), lambda s, pt: (s, 0, 0)),
                   pl.BlockSpec((2, n, LANES), lambda s, pt: (s, 0, 0))],
        scratch_shapes=[pltpu.VMEM((n_pages, LANES, PAGE_SIZE), F32)] * 4
        + [pltpu.VMEM((n, CMP_STRIDE * LANES), F32)] * 2 + [pltpu.SemaphoreType.DMA((4,))],
    )
    hb = PAGE_SIZE // CMP_STRIDE
    perm = np.zeros((2 * PAGE_SIZE, 2 * PAGE_SIZE), np.float32)
    for side in range(2):
        for jj in range(hb):
            for t in range(CMP_STRIDE):
                perm[side * PAGE_SIZE + CMP_STRIDE * jj + t, side * PAGE_SIZE + t * hb + jj] = 1.0
    return pl.pallas_call(
        functools.partial(_compress_sample_body, n, n_pages),
        grid_spec=grid_spec,
        out_shape=[jax.ShapeDtypeStruct((Bd, LANES, n), BF16), jax.ShapeDtypeStruct((Bd, n, LANES), BF16)],
        compiler_params=pltpu.CompilerParams(dimension_semantics=("arbitrary",), vmem_limit_bytes=VMEM_LIMIT),
        name="compress_sample",
    )(page_table, pool_k, pool_v, *prm["cmp_k"], *prm["cmp_v"], *tables, jnp.asarray(perm, BF16))


def _mixer_tail(o, sza, sra, gpb, x, wpa_ref, wo_ref, fg_ref):
    pa = _dot((o * sza).astype(BF16), wpa_ref[...])
    merged = sra * pa + gpb
    hn = x + _dot(merged.astype(BF16), wo_ref[...])
    r = lax.rsqrt(jnp.mean(hn * hn, axis=-1, keepdims=True) + EPS)
    return (hn * r) * fg_ref[...]


def _topk_mask(score, blk, n_blocks, axis):
    cnt = jnp.zeros(score.shape, jnp.int32)
    for sp in range(n_blocks):
        b = lax.slice_in_dim(score, sp, sp + 1, axis=axis)
        ge = jnp.where(b >= score, 1, 0)
        gt = jnp.where(b > score, 1, 0)
        cnt = cnt + jnp.where(blk > sp, ge, gt)
    return cnt < TOP_N


def _attn_prompt_body(n_sel, q_ref, ka_ref, vs_ref, kw_ref, vw_ref, kc_ref, vco_ref, gat_ref, sza_ref, sra_ref,
                      gpb_ref, x_ref, wpa_ref, wo_ref, fg_ref, y_ref):
    i = pl.program_id(1)
    M = HPG * TQ
    G = range(KV_HEADS)
    qpos = i * TQ + (_iota((M, 1), 0) & (TQ - 1))
    n_cmp_pad = kc_ref.shape[2]
    gt = gat_ref[...]
    qs = [q_ref[HPG * g:HPG * (g + 1)].reshape(M, LANES) for g in G]

    last_blk = ((i + 1) * TQ - 1) // SEL_BLOCK

    def select(n_c, n_s):
        mk = (_iota((M, n_c), 1) * CMP_STRIDE + (CMP_BLOCK - 1)) <= qpos
        blk = _iota((n_s, TQ), 0)
        cur = (i * TQ + _iota((n_s, TQ), 1)) // SEL_BLOCK
        valid = blk <= cur
        forced = (valid & (blk > cur - N_LOCAL_BLOCKS)) | (blk < N_INIT_BLOCKS)
        o_cmp, scores = [], []
        for g in G:
            s = jnp.where(mk, _dot(qs[g], kc_ref[0, :, 0:n_c]), NEG)
            e = jnp.exp2(s - jnp.max(s, axis=-1, keepdims=True))
            p = jnp.where(mk, e / jnp.sum(e, axis=-1, keepdims=True), 0.0)
            r = _dot(p.astype(BF16), vco_ref[0, 0:n_c, :])
            o_cmp.append(r[:, 0:LANES])
            impc = r[:, LANES:2 * LANES]
            imp = impc[0:TQ] + impc[TQ:2 * TQ] + impc[2 * TQ:3 * TQ] + impc[3 * TQ:4 * TQ]
            sc = imp.T[HEAD_DIM:HEAD_DIM + n_s]
            scores.append(jnp.where(forced, BIG, jnp.where(valid, sc, NEG)))

        def count_group(k, cnts):
            out = []
            for g in G:
                c = cnts[g]
                for sp in range(k * RANK_GROUP, (k + 1) * RANK_GROUP):
                    b = scores[g][sp:sp + 1]
                    c = c + jnp.where(blk > sp, jnp.where(b >= scores[g], 1, 0), jnp.where(b > scores[g], 1, 0))
                out.append(c)
            return tuple(out)

        cnts = (jnp.zeros((n_s, TQ), jnp.int32),) * KV_HEADS
        for k in range(n_s // RANK_GROUP):
            cnts = lax.cond(k * RANK_GROUP <= last_blk, functools.partial(count_group, k), lambda c: c, cnts)
        later = [jnp.full((n_sel - n_s, TQ), NEG, F32)] if n_s < n_sel else []
        seln = [jnp.concatenate([jnp.where(c < TOP_N, 0.0, NEG)] + later + [jnp.zeros((LANES - n_sel, TQ), F32)],
                                axis=0) for c in cnts]
        return tuple(o_cmp), tuple(seln)

    nq = pl.num_programs(1)
    o_cmp, seln = lax.cond(2 * i >= nq, lambda: select(n_cmp_pad, n_sel), lambda: select(n_cmp_pad // 2, n_sel // 2))
    qa = []
    for g in G:
        seln_t = seln[g].T.astype(BF16)
        qa.append(jnp.concatenate([qs[g], jnp.concatenate([seln_t] * HPG, axis=0)], axis=1))

    def update(carry, s, v):
        m, acc = carry
        mn = jnp.maximum(m, jnp.max(s, axis=-1, keepdims=True))
        acc = jnp.exp2(m - mn) * acc + _dot(jnp.exp2((s - mn).astype(BF16)), v)
        return mn, acc

    def step(kt, carries, causal):
        off = pl.multiple_of(kt * KT, KT)
        k = ka_ref[0, :, pl.ds(off, KT)]
        ss = [_dot(qa[g], k) for g in G]
        if causal:
            keep = kt * KT + _iota((M, KT), 1) <= qpos
            ss = [jnp.where(keep, s, NEG) for s in ss]
        return tuple(update(carries[g], ss[g], vs_ref[g, pl.ds(off, KT), :]) for g in G)

    n_full = i // (KT // TQ)
    init = (jnp.full((M, 1), NEG, F32), jnp.zeros((M, LANES), F32))
    carries = lax.fori_loop(0, n_full // 2, lambda kp, c: step(2 * kp + 1, step(2 * kp, c, False), False),
                            (init,) * KV_HEADS)
    carries = lax.cond(n_full % 2 == 1, lambda c: step(n_full - 1, c, False), lambda c: c, carries)
    carries = step(n_full, carries, True)
    sum_lane = [(KV_HEADS - 1 - g) * HEAD_DIM for g in G]
    o_sel = [carries[g][1] / carries[g][1][:, sum_lane[g]:sum_lane[g] + 1] for g in G]

    st = pl.multiple_of(jnp.maximum(i * TQ - WINDOW, 0), TQ)
    kpos = st + _iota((M, WIN_KEYS), 1)
    mkw = (kpos <= qpos) & (kpos > qpos - WINDOW)
    kw = kw_ref[0, :, pl.ds(st, WIN_KEYS)]
    o_groups = []
    for g in G:
        s = jnp.where(mkw, _dot(qs[g], kw), NEG)
        e = jnp.exp2(s - jnp.max(s, axis=-1, keepdims=True))
        r = _dot(e.astype(BF16), vw_ref[g, pl.ds(st, WIN_KEYS), :])
        o_win = r / r[:, sum_lane[g]:sum_lane[g] + 1]

        def gcol(c, g=g):
            return jnp.concatenate(
                [gt[:, 3 * (HPG * g + j) + c:3 * (HPG * g + j) + c + 1] for j in range(HPG)], axis=0)

        o_groups.append(gcol(0) * o_cmp[g] + gcol(1) * o_sel[g] + gcol(2) * o_win)

    low = _iota((TQ, LANES), 1) < HEAD_DIM
    slabs = []
    for pp in range(N_HEADS // 2):
        g, j0 = pp // 2, (2 * pp) % HPG
        a = o_groups[g][j0 * TQ:(j0 + 1) * TQ]
        b = o_groups[g][(j0 + 1) * TQ:(j0 + 2) * TQ]
        if g == 0:
            b = pltpu.roll(b, HEAD_DIM, 1)
        else:
            a = pltpu.roll(a, HEAD_DIM, 1)
        slabs.append(jnp.where(low, a, b))
    o = jnp.concatenate(slabs, axis=1)
    y_ref[...] = _mixer_tail(o, sza_ref[...], sra_ref[...], gpb_ref[...], x_ref[...], wpa_ref, wo_ref, fg_ref)


def _attn_prompt(B, S, q_hm, kaug, vsb, kwt, vwb, kct, vco, gates, sza, sra, gpb, x2d, prm):
    nq = S // TQ
    n_sel = S // SEL_BLOCK
    n_cmp_pad = kct.shape[2]
    row = lambda width: pl.BlockSpec((TQ, width), lambda b, i: (b * nq + i, 0))
    const = lambda *shape: pl.BlockSpec(shape, lambda b, i: (0,) * len(shape))
    batch = lambda *shape: pl.BlockSpec((1,) + shape, lambda b, i: (b,) + (0,) * len(shape))
    in_specs = [
        pl.BlockSpec((N_HEADS, TQ, LANES), lambda b, i: (0, b * nq + i, 0)),
        batch(2 * LANES, S),
        pl.BlockSpec((KV_HEADS, S, LANES), lambda b, i: (0, b, 0)),
        batch(LANES, S),
        pl.BlockSpec((KV_HEADS, S, LANES), lambda b, i: (0, b, 0)),
        batch(LANES, n_cmp_pad),
        batch(n_cmp_pad, 2 * LANES),
        row(LANES), row(ATTN_WIDTH), row(D_MODEL), row(D_MODEL), row(D_MODEL),
        const(ATTN_WIDTH, D_MODEL), const(D_MODEL, D_MODEL), const(1, D_MODEL),
    ]
    return pl.pallas_call(
        functools.partial(_attn_prompt_body, n_sel),
        grid=(B, nq),
        in_specs=in_specs,
        out_specs=row(D_MODEL),
        out_shape=jax.ShapeDtypeStruct((B * S, D_MODEL), F32),
        compiler_params=pltpu.CompilerParams(dimension_semantics=("arbitrary", "arbitrary"),
                                             vmem_limit_bytes=VMEM_LIMIT),
        name="attn_prompt",
    )(q_hm, kaug, vsb, kwt, vwb, kct, vco, gates, sza, sra, gpb, x2d, prm["w_pa"], prm["w_o"], prm["final_g"])


def _attn_sample_body(n_pages, tn, pt_ref, pk_ref, pv_ref, qa_ref, kc_ref, vc_ref, ovl_ref, oh_ref, kt_ref, vt_ref,
                      ckw_ref, cvw_ref, kwn_ref, vwn_ref, kwt_ref, vwt_ref, o_ref, okw_ref, ovw_ref,
                      kb0, kb1, vb0, vb1, sem):
    s_id = pl.program_id(0)
    past = n_pages * PAGE_SIZE
    n_past_blk = past // SEL_BLOCK
    n_blk_pad = ovl_ref.shape[1]
    R = N_HEADS * TOK_PAD
    GR = HPG * TOK_PAD
    wc = ckw_ref.shape[2]
    bufs = ((kb0, vb0, 0), (kb1, vb1, 1))
    t_row = _iota((R, 1), 0) & (TOK_PAD - 1)

    def start(b, slot):
        kb, vb, si = bufs[slot]
        _start_pages(pt_ref, (pk_ref, pv_ref), (kb, vb), (sem.at[2 * si], sem.at[2 * si + 1]), b, n_pages)

    def finish(slot):
        kb, vb, si = bufs[slot]
        qa = qa_ref[slot]

        n_cmp_pad = kc_ref.shape[2]
        s = _dot(qa, kc_ref[slot])
        mk = _iota((R, n_cmp_pad), 1) < n_cmp_pad - 1
        s = jnp.where(mk, s, NEG)
        e = jnp.exp(s - jnp.max(s, axis=-1, keepdims=True))
        p = jnp.where(mk, e / jnp.sum(e, axis=-1, keepdims=True), 0.0).astype(BF16)
        o_cmp = _dot(p, vc_ref[slot])
        impc = _dot(p, ovl_ref[...])
        imp = jnp.concatenate(
            [impc[g * GR:g * GR + TOK_PAD] + impc[g * GR + TOK_PAD:g * GR + 2 * TOK_PAD]
             + impc[g * GR + 2 * TOK_PAD:g * GR + 3 * TOK_PAD] + impc[g * GR + 3 * TOK_PAD:g * GR + 4 * TOK_PAD]
             for g in range(KV_HEADS)], axis=0)
        nr = KV_HEADS * TOK_PAD
        blk = _iota((nr, n_blk_pad), 1)
        cur = (past + jnp.minimum(_iota((nr, n_blk_pad), 0) & (TOK_PAD - 1), tn - 1)) // SEL_BLOCK
        valid = blk <= cur
        forced = (valid & (blk > cur - N_LOCAL_BLOCKS)) | (blk < N_INIT_BLOCKS)
        score = jnp.where(forced, BIG, jnp.where(valid, imp, NEG))
        sel = _topk_mask(score, blk, n_past_blk + 1, 1)
        selneg = jnp.where(sel, 0.0, NEG)
        selneg = jnp.concatenate(
            [selneg[g * TOK_PAD:(g + 1) * TOK_PAD] for g in range(KV_HEADS) for _ in range(HPG)], axis=0)
        low = _iota((R, LANES), 1) < SEL_HALF
        qaug = []
        for hf in range(n_past_blk // SEL_HALF):
            slab = selneg[:, (hf // 2) * LANES:(hf // 2 + 1) * LANES]
            if hf % 2:
                slab = pltpu.roll(slab, SEL_HALF, 1)
            qaug.append(jnp.concatenate([qa, jnp.where(low, slab, 0.0).astype(BF16)], axis=1))

        _wait_pages((pk_ref, pv_ref), (kb, vb), (sem.at[2 * si], sem.at[2 * si + 1]), n_pages)

        pages_per_half = SEL_HALF * SEL_BLOCK // PAGE_SIZE

        def half(buf, hf):
            return jnp.concatenate(
                [buf[hf * pages_per_half + j] for j in range(pages_per_half)], axis=1).astype(BF16)

        col = _iota((R, LANES), 1)
        ss = [_dot(qaug[hf], jnp.concatenate([half(kb, hf), oh_ref[...]], axis=0)) for hf in range(len(qaug))]
        ss.append(jnp.where(col <= t_row, _nt(qa, kt_ref[slot]), NEG))
        m = functools.reduce(jnp.maximum, [jnp.max(s, axis=-1, keepdims=True) for s in ss])
        es = [jnp.exp(s - m) for s in ss]
        l = functools.reduce(jnp.add, [jnp.sum(e, axis=-1, keepdims=True) for e in es])
        acc = _dot(es[-1].astype(BF16), vt_ref[slot])
        for hf in range(len(qaug)):
            acc = acc + _nt(es[hf].astype(BF16), half(vb, hf))
        o_sel = acc / l

        sa = jnp.where(_iota((R, wc), 1) > t_row, _dot(qa, ckw_ref[slot].astype(BF16)), NEG)
        sb = jnp.where(col <= t_row, _nt(qa, kwn_ref[slot]), NEG)
        m = jnp.maximum(jnp.max(sa, axis=-1, keepdims=True), jnp.max(sb, axis=-1, keepdims=True))
        ea, eb = jnp.exp(sa - m), jnp.exp(sb - m)
        den = jnp.sum(ea, axis=-1, keepdims=True) + jnp.sum(eb, axis=-1, keepdims=True)
        o_win = (_nt(ea.astype(BF16), cvw_ref[slot].astype(BF16)) + _dot(eb.astype(BF16), vwn_ref[slot])) / den

        o_ref[slot, 0] = o_cmp
        o_ref[slot, 1] = o_sel
        o_ref[slot, 2] = o_win

        newest = _iota((LANES, LANES), 1) >= LANES - tn
        for src, new, dst in ((ckw_ref, kwt_ref, okw_ref), (cvw_ref, vwt_ref, ovw_ref)):
            shifted = pltpu.roll(src[slot], wc - tn, 1)
            dst[slot, :, 0:wc - LANES] = shifted[:, 0:wc - LANES]
            dst[slot, :, wc - LANES:wc] = jnp.where(newest, new[slot], shifted[:, wc - LANES:wc])

    @pl.when(s_id == 0)
    def _():
        start(0, 0)

    start(2 * s_id + 1, 1)
    finish(0)

    @pl.when(s_id + 1 < pl.num_programs(0))
    def _():
        start(2 * s_id + 2, 0)

    finish(1)


def _attn_sample(page_table, pool_k, pool_v, tn, qa, kct, vc, ovl, ktail, vtail, ckw, cvw, kwn, vwn, kwt, vwt):
    Bd, n_pages = page_table.shape
    R = N_HEADS * TOK_PAD
    wc = ckw.shape[2]
    const = lambda *shape: pl.BlockSpec(shape, lambda s, pt: (0,) * len(shape))
    pair = lambda *shape: pl.BlockSpec((2,) + shape, lambda s, pt: (s,) + (0,) * len(shape))
    anyspec = pl.BlockSpec(memory_space=pl.ANY)
    half_keys = SEL_HALF * SEL_BLOCK
    onehot = np.zeros((LANES, half_keys), np.float32)
    onehot[np.arange(half_keys) // SEL_BLOCK, np.arange(half_keys)] = 1.0
    grid_spec = pltpu.PrefetchScalarGridSpec(
        num_scalar_prefetch=1,
        grid=(Bd // 2,),
        in_specs=[anyspec, anyspec, pair(R, LANES), pair(LANES, kct.shape[2]), pair(vc.shape[1], LANES),
                  const(*ovl.shape), const(LANES, half_keys), pair(LANES, LANES), pair(LANES, LANES),
                  pair(LANES, wc), pair(LANES, wc), pair(LANES, LANES), pair(LANES, LANES),
                  pair(LANES, LANES), pair(LANES, LANES)],
        out_specs=[pair(3, R, LANES), pair(LANES, wc), pair(LANES, wc)],
        scratch_shapes=[pltpu.VMEM((n_pages, LANES, PAGE_SIZE), F32)] * 4 + [pltpu.SemaphoreType.DMA((4,))],
    )
    return pl.pallas_call(
        functools.partial(_attn_sample_body, n_pages, tn),
        grid_spec=grid_spec,
        out_shape=[jax.ShapeDtypeStruct((Bd, 3, R, LANES), F32)] + [jax.ShapeDtypeStruct((Bd, LANES, wc), F32)] * 2,
        compiler_params=pltpu.CompilerParams(dimension_semantics=("arbitrary",), vmem_limit_bytes=VMEM_LIMIT),
        name="attn_sample",
    )(page_table, pool_k, pool_v, qa, kct, vc, ovl, jnp.asarray(onehot, BF16), ktail, vtail, ckw, cvw, kwn, vwn,
      kwt, vwt)


def _mixer_sample_body(o3_ref, g3_ref, sza_ref, sra_ref, gpb_ref, x_ref, wpa_ref, wo_ref, fg_ref, y_ref):
    o = g3_ref[0] * o3_ref[0] + g3_ref[1] * o3_ref[1] + g3_ref[2] * o3_ref[2]
    y_ref[...] = _mixer_tail(o, sza_ref[...], sra_ref[...], gpb_ref[...], x_ref[...], wpa_ref, wo_ref, fg_ref)


def _mixer_sample(o3, g3, sza, sra, gpb, x2d, prm):
    T = x2d.shape[0]
    full = lambda *shape: pl.BlockSpec(shape, lambda i: (0,) * len(shape))
    return pl.pallas_call(
        _mixer_sample_body,
        grid=(1,),
        in_specs=[full(3, T, ATTN_WIDTH), full(3, T, ATTN_WIDTH), full(T, ATTN_WIDTH), full(T, D_MODEL),
                  full(T, D_MODEL), full(T, D_MODEL), full(ATTN_WIDTH, D_MODEL), full(D_MODEL, D_MODEL),
                  full(1, D_MODEL)],
        out_specs=full(T, D_MODEL),
        out_shape=jax.ShapeDtypeStruct((T, D_MODEL), F32),
        compiler_params=pltpu.CompilerParams(dimension_semantics=("arbitrary",), vmem_limit_bytes=VMEM_LIMIT),
        name="mixer_sample",
    )(o3, g3, sza, sra, gpb, x2d, prm["w_pa"], prm["w_o"], prm["final_g"])


def _overlap(n_cmp, n_sel):
    cs = np.arange(n_cmp)[:, None] * CMP_STRIDE
    ss = np.arange(n_sel)[None, :] * SEL_BLOCK
    ov = np.minimum(cs + CMP_BLOCK, ss + SEL_BLOCK) - np.maximum(cs, ss)
    return np.clip(ov, 0, None).astype(np.float32) / CMP_BLOCK


def _cmp_weights(w1, b1, w2, b2):
    w1r = w1.reshape(2, CMP_STRIDE, HEAD_DIM, CMP_HIDDEN).transpose(1, 2, 0, 3)
    w1r = w1r.reshape(CMP_STRIDE, 1, HEAD_DIM, 2 * CMP_HIDDEN).astype(BF16)
    w2r = w2.reshape(1, CMP_HIDDEN, HEAD_DIM).astype(BF16)
    big = jnp.concatenate(
        [jnp.pad(w1r, ((0, 0), (0, 0), (0, 0), (g * 2 * CMP_HIDDEN, (KV_HEADS - 1 - g) * 2 * CMP_HIDDEN)))
         for g in range(KV_HEADS)], axis=1)
    w2b = jnp.concatenate(
        [jnp.pad(w2r, ((0, 0), (0, 0), (g * HEAD_DIM, (KV_HEADS - 1 - g) * HEAD_DIM))) for g in range(KV_HEADS)],
        axis=0)
    return (big.reshape(CMP_STRIDE * LANES, 4 * LANES), jnp.tile(b1, KV_HEADS)[None],
            w2b.reshape(KV_HEADS * CMP_HIDDEN, LANES), jnp.tile(b2, KV_HEADS)[None])


def _seq_minor(t):
    lead = t.shape[:-3]
    n = len(lead)
    return t.transpose(*range(n), n + 1, n + 2, n).reshape(*lead, KV_WIDTH, t.shape[-3])


def _seq_major(t):
    lead = t.shape[:-2]
    n = len(lead)
    return t.reshape(*lead, KV_HEADS, HEAD_DIM, t.shape[-1]).transpose(*range(n), n + 2, n, n + 1)


def kernel(x_prompt, x_sample, cache_k_cmp, cache_v_cmp, cache_k_sel, cache_v_sel, cache_k_win, cache_v_win, page_table, norm_g, w_in, cmp_k_w1, cmp_k_b1, cmp_k_w2, cmp_k_b2, cmp_v_w1, cmp_v_b1, cmp_v_w2, cmp_v_b2, v_norm_g, v_norm_b, w_spatial, b_spatial, w_pa, w_pb, w_o, final_g):
    B, S, _ = x_prompt.shape
    Bd, tn, _ = x_sample.shape
    depth = w_in.shape[0]
    assert depth == 1, "single-layer step"
    assert Bd * tn == CHUNK, "the sample tokens form one 128-row tile"
    n_pages = page_table.shape[1]
    past = n_pages * PAGE_SIZE

    split = OFF_GL + 3 * N_HEADS
    w = w_in[0]
    prm = {
        "norm_g": norm_g,
        "w_in": jnp.concatenate([w.T[:split], jnp.zeros((GL_PAD, D_MODEL), F32), w.T[split:]], axis=0).astype(BF16),
        "w_sp": w_spatial[0],
        "b_sp": jnp.repeat(b_spatial[0].T, LANES, axis=1),
        "w_sp_s": jnp.stack([jnp.kron(jnp.eye(CHUNK // tn, dtype=F32), w_spatial[0, g, :tn, :tn])
                             for g in range(GMLP_GROUPS)]),
        "b_sp_s": jnp.tile(jnp.repeat(b_spatial[0, :, :tn].T, LANES, axis=1), (CHUNK // tn, 1)),
        "v_norm_g": v_norm_g, "v_norm_b": v_norm_b,
        "w_pb": w_pb[0].astype(BF16), "w_pa": w_pa[0].astype(BF16), "w_o": w_o[0].astype(BF16),
        "final_g": final_g[None],
        "cmp_k": _cmp_weights(cmp_k_w1[0], cmp_k_b1[0], cmp_k_w2[0], cmp_k_b2[0]),
        "cmp_v": _cmp_weights(cmp_v_w1[0], cmp_v_b1[0], cmp_v_w2[0], cmp_v_b2[0]),
    }

    xp = x_prompt.reshape(B * S, D_MODEL)
    (q_hm, p_kcmp, p_vcmp, p_ksel, p_vsel, p_kwin, p_vwin, kcr, vcr, kaug, kwt, vsb, vwb,
     gates, sza, sra, gpb) = _in_project(xp, _rope_tables(np.arange(S)), prm, TM_PROMPT, False, S)
    n_half = S // CMP_STRIDE
    n_cmp = (S - CMP_BLOCK) // CMP_STRIDE + 1
    n_sel = S // SEL_BLOCK
    assert n_sel <= SEL_HALF
    ovl = np.zeros((n_half, LANES), np.float32)
    ovl[:n_cmp, HEAD_DIM:HEAD_DIM + n_sel] = _overlap(n_cmp, n_sel)
    cend_tables = _rope_tables(np.arange(n_half) * CMP_STRIDE + CMP_BLOCK - 1)
    kct, vco = _compress_prompt(kcr.reshape(B, S, LANES), vcr.reshape(B, S, LANES), prm, cend_tables,
                                jnp.asarray(ovl, BF16))
    y_prompt = _attn_prompt(B, S, q_hm, kaug, vsb, kwt, vwb, kct, vco, gates, sza, sra, gpb, xp, prm)

    xs = x_sample.reshape(Bd * tn, D_MODEL)
    pos_s = np.tile(past + np.arange(tn), Bd)
    (q_s, s_kcmp, s_vcmp, s_ksel, s_vsel, s_kwin, s_vwin, gates_s, sza_s, sra_s, gpb_s, vn_s) = _in_project(
        xs, _rope_tables(pos_s), prm, Bd * tn, True, tn)
    n_half_s = past // CMP_STRIDE
    pools = [_seq_minor(c[0]) for c in (cache_k_cmp, cache_v_cmp, cache_k_sel, cache_v_sel)]
    cend_s = _rope_tables(np.arange(n_half_s) * CMP_STRIDE + CMP_BLOCK - 1)
    kct_s, vc_s = _compress_sample(page_table, pools[0], pools[1], prm, cend_s)

    n_cmp_s = (past + tn - CMP_BLOCK) // CMP_STRIDE + 1
    n_blk_s = past // SEL_BLOCK + -(-tn // SEL_BLOCK)
    assert n_cmp_s == n_half_s - 1 and tn <= TOK_PAD and (past // SEL_BLOCK) % SEL_HALF == 0
    ovl_s = np.zeros((n_half_s, 2 * LANES), np.float32)
    ovl_s[:n_cmp_s, :n_blk_s] = _overlap(n_cmp_s, n_blk_s)
    q5 = q_s.reshape(Bd, tn, KV_HEADS, HPG, HEAD_DIM).transpose(0, 2, 3, 1, 4)
    q5 = jnp.pad(q5, ((0, 0), (0, 0), (0, 0), (0, TOK_PAD - tn), (0, 0))).reshape(Bd, KV_HEADS, HPG * TOK_PAD, HEAD_DIM)
    qa = jnp.concatenate(
        [jnp.pad(q5[:, g], ((0, 0), (0, 0), (g * HEAD_DIM, (KV_HEADS - 1 - g) * HEAD_DIM))) for g in range(KV_HEADS)],
        axis=1).astype(BF16)
    pad_rows = lambda t: jnp.pad(t.reshape(Bd, tn, LANES), ((0, 0), (0, LANES - tn), (0, 0))).astype(BF16)
    ckw = _seq_minor(cache_k_win[0])
    cvw = _seq_minor(cache_v_win[0])
    new_t = lambda t: jnp.pad(t.reshape(Bd, tn, LANES).transpose(0, 2, 1), ((0, 0), (0, 0), (LANES - tn, 0)))
    o3, s_k_win, s_v_win = _attn_sample(
        page_table, pools[2], pools[3], tn, qa, kct_s, vc_s, jnp.asarray(ovl_s, BF16),
        pad_rows(s_ksel), pad_rows(s_vsel), ckw, cvw, pad_rows(s_kwin), pad_rows(s_vwin), new_t(s_kwin), new_t(s_vwin))
    s_k_win = _seq_major(s_k_win)[None]
    s_v_win = _seq_major(s_v_win)[None]
    first_group = jnp.arange(N_HEADS * TOK_PAD)[:, None] < HPG * TOK_PAD
    o3r = jnp.where(first_group, o3[..., :HEAD_DIM], o3[..., HEAD_DIM:])
    o3r = o3r.reshape(Bd, 3, N_HEADS, TOK_PAD, HEAD_DIM)[:, :, :, :tn]
    o3r = o3r.transpose(1, 0, 3, 2, 4).reshape(3, Bd * tn, ATTN_WIDTH)
    g3 = gates_s[:, :3 * N_HEADS].reshape(Bd * tn, N_HEADS, 3).transpose(2, 0, 1)
    g3 = jnp.repeat(g3, HEAD_DIM, axis=2)
    y_sample = _mixer_sample(o3r, g3, sza_s, sra_s, gpb_s, xs, prm)

    kv5 = lambda t, b, n: t.reshape(1, b, n, KV_HEADS, HEAD_DIM)
    pw = min(WINDOW, S)
    return (y_prompt.reshape(B, S, D_MODEL), y_sample.reshape(Bd, tn, D_MODEL),
            _seq_major(p_kcmp)[None], _seq_major(p_vcmp)[None], _seq_major(p_ksel)[None], _seq_major(p_vsel)[None],
            _seq_major(p_kwin[:, :, S - pw:])[None], _seq_major(p_vwin[:, :, S - pw:])[None],
            kv5(s_kcmp, Bd, tn), kv5(s_vcmp, Bd, tn), kv5(s_ksel, Bd, tn), kv5(s_vsel, Bd, tn),
            s_k_win, s_v_win, vn_s.reshape(1, Bd, tn, GMLP_WIDTH))
```

```python
import functools

import numpy as np
import jax
import jax.numpy as jnp
from jax import lax
from jax.experimental import pallas as pl
from jax.experimental.pallas import tpu as pltpu

F32 = jnp.float32
BF16 = jnp.bfloat16

D_MODEL = 1024
HEAD_DIM = 64
N_HEADS = 8
KV_HEADS = 2
HPG = N_HEADS // KV_HEADS
ATTN_WIDTH = N_HEADS * HEAD_DIM
KV_WIDTH = KV_HEADS * HEAD_DIM
ROT_DIM = HEAD_DIM // 4
ROT_HALF = ROT_DIM // 2
ROPE_THETA = 500000.0
CMP_BLOCK = 32
CMP_STRIDE = 16
CMP_HIDDEN = 128
SEL_BLOCK = 64
TOP_N = 16
N_INIT_BLOCKS = 1
N_LOCAL_BLOCKS = 2
WINDOW = 512
CHUNK = 128
GMLP_GROUPS = 4
GMLP_WIDTH = 512
PAGE_SIZE = 128
NEG = -1e30
BIG = 1e30
EPS = 1e-6
LOG2E = 1.4426950408889634

LANES = 128
GL_PAD = LANES - 3 * N_HEADS

OFF_Q = 0
OFF_KV = ATTN_WIDTH
OFF_GL = OFF_KV + 6 * KV_WIDTH
OFF_ZA = OFF_GL + LANES
OFF_U = OFF_ZA + ATTN_WIDTH
OFF_V = OFF_U + GMLP_WIDTH
OFF_ZB = OFF_V + GMLP_WIDTH
OFF_RA = OFF_ZB + GMLP_WIDTH
OFF_RB = OFF_RA + D_MODEL
W_TOT = OFF_RB + D_MODEL

VMEM_LIMIT = 56 * 1024 * 1024

TM_PROMPT = 512
TQ = 128
KT = 512
WIN_KEYS = WINDOW + TQ
SEL_HALF = 64
TOK_PAD = 8
RANK_GROUP = 8


def _nt(a, b):
    return lax.dot_general(a, b, (((1,), (1,)), ((), ())), preferred_element_type=F32)


def _dot(a, b):
    return jnp.dot(a, b, preferred_element_type=F32)


def _iota(shape, dim):
    return lax.broadcasted_iota(jnp.int32, shape, dim)


def _rope(slab, cos, s1, s2):
    return slab * cos + pltpu.roll(slab, LANES - ROT_HALF, 1) * s1 + pltpu.roll(slab, ROT_HALF, 1) * s2


def _rope_tables(pos):
    pos = np.asarray(pos, np.float64)
    n = pos.shape[0]
    inv = np.power(np.float64(ROPE_THETA), -np.arange(0, ROT_DIM, 2, dtype=np.float64) / ROT_DIM)
    ang = pos[:, None] * inv[None, :]
    cos, sin = np.cos(ang), np.sin(ang)
    rest = HEAD_DIM - ROT_DIM
    c = np.concatenate([cos, cos, np.ones((n, rest))], axis=1)
    s1 = np.concatenate([-sin, np.zeros((n, HEAD_DIM - ROT_HALF))], axis=1)
    s2 = np.concatenate([np.zeros((n, ROT_HALF)), sin, np.zeros((n, rest))], axis=1)
    return tuple(jnp.asarray(np.tile(t, (1, LANES // HEAD_DIM)), F32) for t in (c, s1, s2))


def _sigmoid(x):
    return 1.0 / (1.0 + jnp.exp(-x))


def _gelu(x):
    return jax.nn.gelu(x, approximate=True)


def _inproj_body(tm, sample, pos_tiles, x_ref, ng_ref, w_ref, cos_ref, s1_ref, s2_ref, wsp_ref, bsp_ref,
                 vng_ref, vnb_ref, wpb_ref, *outs):
    if sample:
        (q_ref, kcmp_ref, vcmp_ref, ksel_ref, vsel_ref, kwin_ref, vwin_ref,
         gat_ref, sza_ref, sra_ref, gpb_ref, vn_ref) = outs
    else:
        (q_ref, kcmp_ref, vcmp_ref, ksel_ref, vsel_ref, kwin_ref, vwin_ref,
         kcr_ref, vcr_ref, kaug_ref, kwt_ref, vsb_ref, vwb_ref, gat_ref, sza_ref, sra_ref, gpb_ref) = outs

    x = x_ref[...]
    r = lax.rsqrt(jnp.mean(x * x, axis=-1, keepdims=True) + EPS)
    h = ((x * r) * ng_ref[...]).astype(BF16)

    def proj(lo, hi):
        return _nt(h, w_ref[lo:hi, :])

    cos, s1, s2 = cos_ref[...], s1_ref[...], s2_ref[...]
    low = _iota((tm, LANES), 1) < HEAD_DIM

    q = proj(OFF_Q, OFF_Q + ATTN_WIDTH)
    q_scale = HEAD_DIM ** -0.5 if sample else HEAD_DIM ** -0.5 * LOG2E
    for pp in range(N_HEADS // 2):
        slab = _rope(q[:, pp * LANES:(pp + 1) * LANES], cos, s1, s2) * q_scale
        if sample:
            q_ref[:, pp * LANES:(pp + 1) * LANES] = slab
        elif (2 * pp) // HPG == 0:
            q_ref[2 * pp] = jnp.where(low, slab, 0.0).astype(BF16)
            q_ref[2 * pp + 1] = jnp.where(low, pltpu.roll(slab, HEAD_DIM, 1), 0.0).astype(BF16)
        else:
            q_ref[2 * pp] = jnp.where(low, 0.0, pltpu.roll(slab, HEAD_DIM, 1)).astype(BF16)
            q_ref[2 * pp + 1] = jnp.where(low, 0.0, slab).astype(BF16)

    kv = proj(OFF_KV, OFF_KV + 6 * KV_WIDTH)
    kcmp = kv[:, 0:LANES]
    vcmp = kv[:, LANES:2 * LANES]
    ksel = _rope(kv[:, 2 * LANES:3 * LANES], cos, s1, s2)
    vsel = kv[:, 3 * LANES:4 * LANES]
    kwin = _rope(kv[:, 4 * LANES:5 * LANES], cos, s1, s2)
    vwin = kv[:, 5 * LANES:6 * LANES]
    if sample:
        for ref, val in ((kcmp_ref, kcmp), (vcmp_ref, vcmp), (ksel_ref, ksel), (vsel_ref, vsel),
                         (kwin_ref, kwin), (vwin_ref, vwin)):
            ref[...] = val
    else:
        ksel_t = ksel.T
        kwin_t = kwin.T
        for ref, val in ((kcmp_ref, kcmp.T), (vcmp_ref, vcmp.T), (ksel_ref, ksel_t), (vsel_ref, vsel.T),
                         (kwin_ref, kwin_t), (vwin_ref, vwin.T)):
            ref[0] = val
        kcr_ref[...] = kcmp
        vcr_ref[...] = vcmp
        base = (pl.program_id(0) % pos_tiles) * tm
        blk = (base + _iota((LANES, tm), 1)) // SEL_BLOCK
        onehot = jnp.where(_iota((LANES, tm), 0) == blk, 1.0, 0.0)
        kaug_ref[0, 0:LANES] = ksel_t.astype(BF16)
        kaug_ref[0, LANES:2 * LANES] = onehot.astype(BF16)
        kwt_ref[0] = kwin_t.astype(BF16)
        for ref, val in ((vsb_ref, vsel), (vwb_ref, vwin)):
            ref[0] = jnp.where(low, val, 1.0).astype(BF16)
            ref[1] = jnp.where(low, 1.0, val).astype(BF16)

    gat_ref[...] = _sigmoid(proj(OFF_GL, OFF_GL + LANES))
    za = proj(OFF_ZA, OFF_ZA + ATTN_WIDTH)
    sza_ref[...] = za * _sigmoid(za)
    sra_ref[...] = _sigmoid(proj(OFF_RA, OFF_RA + D_MODEL))

    v = proj(OFF_V, OFF_V + GMLP_WIDTH)
    gv = _gelu(v)
    mu = jnp.mean(gv, axis=-1, keepdims=True)
    var = jnp.mean(jnp.square(gv - mu), axis=-1, keepdims=True)
    vn = ((gv - mu) * lax.rsqrt(var + EPS)) * vng_ref[...] + vnb_ref[...]
    if sample:
        vn_ref[...] = vn
    vnb16 = vn.astype(BF16)
    n_chunk = tm // CHUNK
    tri = _iota((CHUNK, CHUNK), 0) >= _iota((CHUNK, CHUNK), 1)
    mixed = []
    for g in range(GMLP_GROUPS):
        wm = jnp.where(tri, wsp_ref[g], 0.0).astype(BF16)
        cat = jnp.concatenate(
            [vnb16[c * CHUNK:(c + 1) * CHUNK, g * LANES:(g + 1) * LANES] for c in range(n_chunk)], axis=1)
        mixed.append(_dot(wm, cat))
    bsp = bsp_ref[...]
    sg = jnp.concatenate(
        [jnp.concatenate([mixed[g][:, c * LANES:(c + 1) * LANES] for g in range(GMLP_GROUPS)], axis=1) + bsp
         for c in range(n_chunk)], axis=0)
    u = proj(OFF_U, OFF_U + GMLP_WIDTH)
    zb = proj(OFF_ZB, OFF_ZB + GMLP_WIDTH)
    t = (_gelu(u) * sg) * (zb * _sigmoid(zb))
    pb = _dot(t.astype(BF16), wpb_ref[...])
    gpb_ref[...] = _sigmoid(proj(OFF_RB, OFF_RB + D_MODEL)) * pb


def _in_project(x2d, tables, prm, tm, sample, seq):
    T = x2d.shape[0]
    nt = T // tm
    pos_tiles = tables[0].shape[0] // tm
    const = lambda *shape: pl.BlockSpec(shape, lambda i: (0,) * len(shape))
    row = lambda width: pl.BlockSpec((tm, width), lambda i: (i, 0))
    tab = pl.BlockSpec((tm, LANES), lambda i: (i % pos_tiles, 0))
    in_specs = [
        row(D_MODEL), const(1, D_MODEL),
        pl.BlockSpec((W_TOT, D_MODEL), lambda i: (0, 0), pipeline_mode=pl.Buffered(1)),
        tab, tab, tab,
        const(GMLP_GROUPS, CHUNK, CHUNK), const(CHUNK, GMLP_WIDTH), const(1, GMLP_WIDTH), const(1, GMLP_WIDTH),
        const(GMLP_WIDTH, D_MODEL),
    ]
    f32rows = lambda width: jax.ShapeDtypeStruct((T, width), F32)
    tail_shapes = [f32rows(LANES), f32rows(ATTN_WIDTH), f32rows(D_MODEL), f32rows(D_MODEL)]
    tail_specs = [row(LANES), row(ATTN_WIDTH), row(D_MODEL), row(D_MODEL)]
    if sample:
        out_shape = [f32rows(ATTN_WIDTH)] + [f32rows(KV_WIDTH)] * 6 + tail_shapes + [f32rows(GMLP_WIDTH)]
        out_specs = [row(ATTN_WIDTH)] + [row(KV_WIDTH)] * 6 + tail_specs + [row(GMLP_WIDTH)]
    else:
        nb = T // seq
        bf = lambda *shape: jax.ShapeDtypeStruct(shape, BF16)
        tposed = lambda rows: pl.BlockSpec((1, rows, tm), lambda i: (i // pos_tiles, 0, i % pos_tiles))
        out_shape = ([bf(N_HEADS, T, LANES)] + [jax.ShapeDtypeStruct((nb, KV_WIDTH, seq), F32)] * 6
                     + [f32rows(KV_WIDTH)] * 2
                     + [bf(nb, 2 * LANES, seq), bf(nb, LANES, seq), bf(KV_HEADS, T, LANES), bf(KV_HEADS, T, LANES)]
                     + tail_shapes)
        heads = lambda n: pl.BlockSpec((n, tm, LANES), lambda i: (0, i, 0))
        out_specs = ([heads(N_HEADS)] + [tposed(KV_WIDTH)] * 6 + [row(KV_WIDTH)] * 2
                     + [tposed(2 * LANES), tposed(LANES), heads(KV_HEADS), heads(KV_HEADS)] + tail_specs)
    return pl.pallas_call(
        functools.partial(_inproj_body, tm, sample, pos_tiles),
        grid=(nt,),
        in_specs=in_specs,
        out_specs=out_specs,
        out_shape=out_shape,
        compiler_params=pltpu.CompilerParams(dimension_semantics=("arbitrary",), vmem_limit_bytes=VMEM_LIMIT),
        name="in_project_sample" if sample else "in_project_prompt",
    )(x2d, prm["norm_g"], prm["w_in"], *tables, prm["w_sp_s" if sample else "w_sp"],
      prm["b_sp_s" if sample else "b_sp"], prm["v_norm_g"], prm["v_norm_b"], prm["w_pb"])


def _strided_halfblocks(load_rows):
    return jnp.concatenate([load_rows(t) for t in range(CMP_STRIDE)], axis=1).astype(BF16)


def _compress_mlp(xcat, n, w1_ref, b1_ref, w2_ref, b2_ref):
    hh = _dot(xcat, w1_ref[...])
    hid = jnp.concatenate(
        [hh[:, 0:LANES] + pltpu.roll(hh[:, LANES:2 * LANES], n - 1, 0),
         hh[:, 2 * LANES:3 * LANES] + pltpu.roll(hh[:, 3 * LANES:4 * LANES], n - 1, 0)], axis=1) + b1_ref[...]
    return _dot(_gelu(hid).astype(BF16), w2_ref[...]) + b2_ref[...]


def _compress_prompt_body(n, kr_ref, vr_ref, w1k, b1k, w2k, b2k, w1v, b1v, w2v, b2v, cos_ref, s1_ref, s2_ref,
                          ovl_ref, kct_ref, vco_ref):
    kc = _compress_mlp(_strided_halfblocks(lambda t: kr_ref[0, pl.ds(t, n, stride=CMP_STRIDE), :]),
                       n, w1k, b1k, w2k, b2k)
    kct_ref[0] = _rope(kc, cos_ref[...], s1_ref[...], s2_ref[...]).T.astype(BF16)
    vc = _compress_mlp(_strided_halfblocks(lambda t: vr_ref[0, pl.ds(t, n, stride=CMP_STRIDE), :]),
                       n, w1v, b1v, w2v, b2v)
    vco_ref[0, :, 0:LANES] = vc.astype(BF16)
    vco_ref[0, :, LANES:2 * LANES] = ovl_ref[...]


def _compress_prompt(kr, vr, prm, tables, ovl):
    B, S, _ = kr.shape
    n = S // CMP_STRIDE
    const = lambda *shape: pl.BlockSpec(shape, lambda b: (0,) * len(shape))
    rows = pl.BlockSpec((1, S, LANES), lambda b: (b, 0, 0))
    wspecs = [const(CMP_STRIDE * LANES, 4 * LANES), const(1, 2 * LANES), const(2 * LANES, LANES), const(1, LANES)]
    return pl.pallas_call(
        functools.partial(_compress_prompt_body, n),
        grid=(B,),
        in_specs=[rows, rows] + wspecs + wspecs + [const(n, LANES)] * 3 + [const(n, LANES)],
        out_specs=[pl.BlockSpec((1, LANES, n), lambda b: (b, 0, 0)),
                   pl.BlockSpec((1, n, 2 * LANES), lambda b: (b, 0, 0))],
        out_shape=[jax.ShapeDtypeStruct((B, LANES, n), BF16),
                   jax.ShapeDtypeStruct((B, n, 2 * LANES), BF16)],
        compiler_params=pltpu.CompilerParams(dimension_semantics=("arbitrary",), vmem_limit_bytes=VMEM_LIMIT),
        name="compress_prompt",
    )(kr, vr, *prm["cmp_k"], *prm["cmp_v"], *tables, ovl)


def _start_pages(pt_ref, pools, bufs, sems, b, n_pages):
    def body(p, carry):
        page = pt_ref[b, p]
        for i, (pool, buf, sem) in enumerate(zip(pools, bufs, sems)):
            pltpu.make_async_copy(pool.at[page], buf.at[p], sem).start(priority=i % 2)
        return carry
    lax.fori_loop(0, n_pages, body, 0)


def _wait_pages(pools, bufs, sems, n_pages):
    for pool, buf, sem in zip(pools, bufs, sems):
        pltpu.make_async_copy(pool.at[pl.ds(0, n_pages)], buf, sem).wait()


def _compress_sample_body(n, n_pages, pt_ref, pk_ref, pv_ref, w1k, b1k, w2k, b2k, w1v, b1v, w2v, b2v,
                          cos_ref, s1_ref, s2_ref, perm_ref, kct_ref, vc_ref, kb0, kb1, vb0, vb1, xk, xv, sem):
    s = pl.program_id(0)
    bufs = ((kb0, vb0, 0), (kb1, vb1, 1))

    def start(b, slot):
        kb, vb, si = bufs[slot]
        _start_pages(pt_ref, (pk_ref, pv_ref), (kb, vb), (sem.at[2 * si], sem.at[2 * si + 1]), b, n_pages)

    hb_per_page = PAGE_SIZE // CMP_STRIDE

    def halfblocks(buf, xcat):
        pairs =jnp.concatenate([buf[pl.ds(0, n_pages // 2, stride=2)], buf[pl.ds(1, n_pages // 2, stride=2)]], axis=2)
        y = _dot(pairs.reshape(n_pages // 2 * LANES, 2 * PAGE_SIZE).astype(BF16), perm_ref[...])
        for p in range(n_pages):
            z = y[p // 2 * LANES:(p // 2 + 1) * LANES, p % 2 * PAGE_SIZE:(p % 2 + 1) * PAGE_SIZE].T
            for t in range(CMP_STRIDE):
                xcat[p * hb_per_page:(p + 1) * hb_per_page, t * LANES:(t + 1) * LANES] = (
                    z[t * hb_per_page:(t + 1) * hb_per_page])

    def finish(slot):
        kb, vb, si = bufs[slot]
        _wait_pages((pk_ref, pv_ref), (kb, vb), (sem.at[2 * si], sem.at[2 * si + 1]), n_pages)
        halfblocks(kb, xk)
        halfblocks(vb, xv)
        kc = _compress_mlp(xk[...].astype(BF16), n, w1k, b1k, w2k, b2k)
        kct_ref[slot] = _rope(kc, cos_ref[...], s1_ref[...], s2_ref[...]).T.astype(BF16)
        vc_ref[slot] = _compress_mlp(xv[...].astype(BF16), n, w1v, b1v, w2v, b2v).astype(BF16)

    @pl.when(s == 0)
    def _():
        start(0, 0)

    start(2 * s + 1, 1)
    finish(0)

    @pl.when(s + 1 < pl.num_programs(0))
    def _():
        start(2 * s + 2, 0)

    finish(1)


def _compress_sample(page_table, pool_k, pool_v, prm, tables):
    Bd, n_pages = page_table.shape
    past = n_pages * PAGE_SIZE
    n = past // CMP_STRIDE
    const = lambda *shape: pl.BlockSpec(shape, lambda s, pt: (0,) * len(shape))
    wspecs = [const(CMP_STRIDE * LANES, 4 * LANES), const(1, 2 * LANES), const(2 * LANES, LANES), const(1, LANES)]
    anyspec = pl.BlockSpec(memory_space=pl.ANY)
    grid_spec = pltpu.PrefetchScalarGridSpec(
        num_scalar_prefetch=1,
        grid=(Bd // 2,),
        in_specs=[anyspec, anyspec] + wspecs + wspecs + [const(n, LANES)] * 3 + [const(2 * PAGE_SIZE, 2 * PAGE_SIZE)],
        out_specs=[pl.BlockSpec((2, LANES, n), lambda s, pt: (s, 0, 0)),
                   pl.BlockSpec((2, n, LANES), lambda s, pt: (s, 0, 0))],
        scratch_shapes=[pltpu.VMEM((n_pages, LANES, PAGE_SIZE), F32)] * 4
        + [pltpu.VMEM((n, CMP_STRIDE * LANES), F32)] * 2 + [pltpu.SemaphoreType.DMA((4,))],
    )
    hb = PAGE_SIZE // CMP_STRIDE
    perm = np.zeros((2 * PAGE_SIZE, 2 * PAGE_SIZE), np.float32)
    for side in range(2):
        for jj in range(hb):
            for t in range(CMP_STRIDE):
                perm[side * PAGE_SIZE + CMP_STRIDE * jj + t, side * PAGE_SIZE + t * hb + jj] = 1.0
    return pl.pallas_call(
        functools.partial(_compress_sample_body, n, n_pages),
        grid_spec=grid_spec,
        out_shape=[jax.ShapeDtypeStruct((Bd, LANES, n), BF16), jax.ShapeDtypeStruct((Bd, n, LANES), BF16)],
        compiler_params=pltpu.CompilerParams(dimension_semantics=("arbitrary",), vmem_limit_bytes=VMEM_LIMIT),
        name="compress_sample",
    )(page_table, pool_k, pool_v, *prm["cmp_k"], *prm["cmp_v"], *tables, jnp.asarray(perm, BF16))


def _mixer_tail(o, sza, sra, gpb, x, wpa_ref, wo_ref, fg_ref):
    pa = _dot((o * sza).astype(BF16), wpa_ref[...])
    merged = sra * pa + gpb
    hn = x + _dot(merged.astype(BF16), wo_ref[...])
    r = lax.rsqrt(jnp.mean(hn * hn, axis=-1, keepdims=True) + EPS)
    return (hn * r) * fg_ref[...]


def _topk_mask(score, blk, n_blocks, axis):
    cnt = jnp.zeros(score.shape, jnp.int32)
    for sp in range(n_blocks):
        b = lax.slice_in_dim(score, sp, sp + 1, axis=axis)
        ge = jnp.where(b >= score, 1, 0)
        gt = jnp.where(b > score, 1, 0)
        cnt = cnt + jnp.where(blk > sp, ge, gt)
    return cnt < TOP_N


def _attn_prompt_body(n_sel, q_ref, ka_ref, vs_ref, kw_ref, vw_ref, kc_ref, vco_ref, band_ref, rowhot_ref, gat_ref,
                      sza_ref, sra_ref, gpb_ref, x_ref, wpa_ref, wo_ref, fg_ref, y_ref):
    i = pl.program_id(1)
    M = HPG * TQ
    G = range(KV_HEADS)
    qpos = i * TQ + (_iota((M, 1), 0) & (TQ - 1))
    n_cmp_pad = kc_ref.shape[2]
    gt = gat_ref[...]
    qs = [q_ref[HPG * g:HPG * (g + 1)].reshape(M, LANES) for g in G]

    mk = (_iota((M, n_cmp_pad), 1) * CMP_STRIDE + (CMP_BLOCK - 1)) <= qpos
    o_cmp, qa = [], []
    blk = _iota((n_sel, TQ), 0)
    cur = (i * TQ + _iota((n_sel, TQ), 1)) // SEL_BLOCK
    valid = blk <= cur
    forced = (valid & (blk > cur - N_LOCAL_BLOCKS)) | (blk < N_INIT_BLOCKS)
    scores = []
    for g in G:
        s = jnp.where(mk, _dot(qs[g], kc_ref[0]), NEG)
        e = jnp.exp2(s - jnp.max(s, axis=-1, keepdims=True))
        p = jnp.where(mk, e / jnp.sum(e, axis=-1, keepdims=True), 0.0)
        r = _dot(p.astype(BF16), vco_ref[0])
        o_cmp.append(r[:, 0:LANES])
        impc = r[:, LANES:2 * LANES]
        imp = impc[0:TQ] + impc[TQ:2 * TQ] + impc[2 * TQ:3 * TQ] + impc[3 * TQ:4 * TQ]
        sc = imp.T[HEAD_DIM:HEAD_DIM + n_sel]
        scores.append(jnp.where(forced, BIG, jnp.where(valid, sc, NEG)))

    last_blk = ((i + 1) * TQ - 1) // SEL_BLOCK

    def count_group(k, cnts):
        out = []
        for g in G:
            c = cnts[g]
            for sp in range(k * RANK_GROUP, (k + 1) * RANK_GROUP):
                b = scores[g][sp:sp + 1]
                c = c + jnp.where(blk > sp, jnp.where(b >= scores[g], 1, 0), jnp.where(b > scores[g], 1, 0))
            out.append(c)
        return tuple(out)

    cnts = (jnp.zeros((n_sel, TQ), jnp.int32),) * KV_HEADS
    for k in range(n_sel // RANK_GROUP):
        cnts = lax.cond(k * RANK_GROUP <= last_blk, functools.partial(count_group, k), lambda c: c, cnts)
    for g in G:
        selneg = jnp.where(cnts[g] < TOP_N, 0.0, NEG)
        seln_t = jnp.concatenate([selneg, jnp.zeros((LANES - n_sel, TQ), F32)], axis=0).T.astype(BF16)
        qa.append(jnp.concatenate([qs[g], jnp.concatenate([seln_t] * HPG, axis=0)], axis=1))

    def update(carry, s, v):
        m, acc = carry
        mn = jnp.maximum(m, jnp.max(s, axis=-1, keepdims=True))
        acc = jnp.exp2(m - mn) * acc + _dot(jnp.exp2((s - mn).astype(BF16)), v)
        return mn, acc

    def step(kt, carries, causal):
        off = pl.multiple_of(kt * KT, KT)
        k = ka_ref[0, :, pl.ds(off, KT)]
        ss = [_dot(qa[g], k) for g in G]
        if causal:
            keep = kt * KT + _iota((M, KT), 1) <= qpos
            ss = [jnp.where(keep, s, NEG) for s in ss]
        return tuple(update(carries[g], ss[g], vs_ref[g, pl.ds(off, KT), :]) for g in G)

    n_full = i // (KT // TQ)
    init = (jnp.full((M, 1), NEG, F32), jnp.zeros((M, LANES), F32))
    carries = lax.fori_loop(0, n_full // 2, lambda kp, c: step(2 * kp + 1, step(2 * kp, c, False), False),
                            (init,) * KV_HEADS)
    carries = lax.cond(n_full % 2 == 1, lambda c: step(n_full - 1, c, False), lambda c: c, carries)
    carries = step(n_full, carries, True)
    sum_lane = [(KV_HEADS - 1 - g) * HEAD_DIM for g in G]
    o_sel = [carries[g][1] / carries[g][1][:, sum_lane[g]:sum_lane[g] + 1] for g in G]

    st = pl.multiple_of(jnp.maximum(i * TQ - WINDOW, 0), TQ)
    kw = jnp.concatenate([kw_ref[0, :, pl.ds(st, WIN_KEYS)], band_ref[jnp.minimum(i, WINDOW // TQ)]], axis=0)
    o_groups = []
    for g in G:
        s = _dot(jnp.concatenate([qs[g], rowhot_ref[...]], axis=1), kw)
        e = jnp.exp2(s - jnp.max(s, axis=-1, keepdims=True))
        r = _dot(e.astype(BF16), vw_ref[g, pl.ds(st, WIN_KEYS), :])
        o_win = r / r[:, sum_lane[g]:sum_lane[g] + 1]

        def gcol(c, g=g):
            return jnp.concatenate(
                [gt[:, 3 * (HPG * g + j) + c:3 * (HPG * g + j) + c + 1] for j in range(HPG)], axis=0)

        o_groups.append(gcol(0) * o_cmp[g] + gcol(1) * o_sel[g] + gcol(2) * o_win)

    low = _iota((TQ, LANES), 1) < HEAD_DIM
    slabs = []
    for pp in range(N_HEADS // 2):
        g, j0 = pp // 2, (2 * pp) % HPG
        a = o_groups[g][j0 * TQ:(j0 + 1) * TQ]
        b = o_groups[g][(j0 + 1) * TQ:(j0 + 2) * TQ]
        if g == 0:
            b = pltpu.roll(b, HEAD_DIM, 1)
        else:
            a = pltpu.roll(a, HEAD_DIM, 1)
        slabs.append(jnp.where(low, a, b))
    o = jnp.concatenate(slabs, axis=1)
    y_ref[...] = _mixer_tail(o, sza_ref[...], sra_ref[...], gpb_ref[...], x_ref[...], wpa_ref, wo_ref, fg_ref)


def _attn_prompt(B, S, q_hm, kaug, vsb, kwt, vwb, kct, vco, gates, sza, sra, gpb, x2d, prm):
    nq = S // TQ
    n_sel = S // SEL_BLOCK
    n_cmp_pad = kct.shape[2]
    row = lambda width: pl.BlockSpec((TQ, width), lambda b, i: (b * nq + i, 0))
    const = lambda *shape: pl.BlockSpec(shape, lambda b, i: (0,) * len(shape))
    batch = lambda *shape: pl.BlockSpec((1,) + shape, lambda b, i: (b,) + (0,) * len(shape))
    in_specs = [
        pl.BlockSpec((N_HEADS, TQ, LANES), lambda b, i: (0, b * nq + i, 0)),
        batch(2 * LANES, S),
        pl.BlockSpec((KV_HEADS, S, LANES), lambda b, i: (0, b, 0)),
        batch(LANES, S),
        pl.BlockSpec((KV_HEADS, S, LANES), lambda b, i: (0, b, 0)),
        batch(LANES, n_cmp_pad),
        batch(n_cmp_pad, 2 * LANES),
        const(WINDOW // TQ + 1, TQ, WIN_KEYS), const(HPG * TQ, TQ),
        row(LANES), row(ATTN_WIDTH), row(D_MODEL), row(D_MODEL), row(D_MODEL),
        const(ATTN_WIDTH, D_MODEL), const(D_MODEL, D_MODEL), const(1, D_MODEL),
    ]
    d = np.arange(WIN_KEYS)[None, :] - np.arange(TQ)[:, None]
    band = np.stack([np.where((d <= min(WINDOW, v * TQ)) & (d > min(WINDOW, v * TQ) - WINDOW), 0.0, NEG)
                     for v in range(WINDOW // TQ + 1)]).astype(np.float32)
    rowhot = np.tile(np.eye(TQ, dtype=np.float32), (HPG, 1))
    return pl.pallas_call(
        functools.partial(_attn_prompt_body, n_sel),
        grid=(B, nq),
        in_specs=in_specs,
        out_specs=row(D_MODEL),
        out_shape=jax.ShapeDtypeStruct((B * S, D_MODEL), F32),
        compiler_params=pltpu.CompilerParams(dimension_semantics=("arbitrary", "arbitrary"),
                                             vmem_limit_bytes=VMEM_LIMIT),
        name="attn_prompt",
    )(q_hm, kaug, vsb, kwt, vwb, kct, vco, jnp.asarray(band, BF16), jnp.asarray(rowhot, BF16), gates, sza, sra, gpb,
      x2d, prm["w_pa"], prm["w_o"], prm["final_g"])


def _attn_sample_body(n_pages, tn, pt_ref, pk_ref, pv_ref, qa_ref, kc_ref, vc_ref, ovl_ref, oh_ref, kt_ref, vt_ref,
                      ckw_ref, cvw_ref, kwn_ref, vwn_ref, kwt_ref, vwt_ref, o_ref, okw_ref, ovw_ref,
                      kb0, kb1, vb0, vb1, sem):
    s_id = pl.program_id(0)
    past = n_pages * PAGE_SIZE
    n_past_blk = past // SEL_BLOCK
    n_blk_pad = ovl_ref.shape[1]
    R = N_HEADS * TOK_PAD
    GR = HPG * TOK_PAD
    wc = ckw_ref.shape[2]
    bufs = ((kb0, vb0, 0), (kb1, vb1, 1))
    t_row = _iota((R, 1), 0) & (TOK_PAD - 1)

    def start(b, slot):
        kb, vb, si = bufs[slot]
        _start_pages(pt_ref, (pk_ref, pv_ref), (kb, vb), (sem.at[2 * si], sem.at[2 * si + 1]), b, n_pages)

    def finish(slot):
        kb, vb, si = bufs[slot]
        qa = qa_ref[slot]

        n_cmp_pad = kc_ref.shape[2]
        s = _dot(qa, kc_ref[slot])
        mk = _iota((R, n_cmp_pad), 1) < n_cmp_pad - 1
        s = jnp.where(mk, s, NEG)
        e = jnp.exp(s - jnp.max(s, axis=-1, keepdims=True))
        p = jnp.where(mk, e / jnp.sum(e, axis=-1, keepdims=True), 0.0).astype(BF16)
        o_cmp = _dot(p, vc_ref[slot])
        impc = _dot(p, ovl_ref[...])
        imp = jnp.concatenate(
            [impc[g * GR:g * GR + TOK_PAD] + impc[g * GR + TOK_PAD:g * GR + 2 * TOK_PAD]
             + impc[g * GR + 2 * TOK_PAD:g * GR + 3 * TOK_PAD] + impc[g * GR + 3 * TOK_PAD:g * GR + 4 * TOK_PAD]
             for g in range(KV_HEADS)], axis=0)
        nr = KV_HEADS * TOK_PAD
        blk = _iota((nr, n_blk_pad), 1)
        cur = (past + jnp.minimum(_iota((nr, n_blk_pad), 0) & (TOK_PAD - 1), tn - 1)) // SEL_BLOCK
        valid = blk <= cur
        forced = (valid & (blk > cur - N_LOCAL_BLOCKS)) | (blk < N_INIT_BLOCKS)
        score = jnp.where(forced, BIG, jnp.where(valid, imp, NEG))
        sel = _topk_mask(score, blk, n_past_blk + 1, 1)
        selneg = jnp.where(sel, 0.0, NEG)
        selneg = jnp.concatenate(
            [selneg[g * TOK_PAD:(g + 1) * TOK_PAD] for g in range(KV_HEADS) for _ in range(HPG)], axis=0)
        low = _iota((R, LANES), 1) < SEL_HALF
        qaug = []
        for hf in range(n_past_blk // SEL_HALF):
            slab = selneg[:, (hf // 2) * LANES:(hf // 2 + 1) * LANES]
            if hf % 2:
                slab = pltpu.roll(slab, SEL_HALF, 1)
            qaug.append(jnp.concatenate([qa, jnp.where(low, slab, 0.0).astype(BF16)], axis=1))

        _wait_pages((pk_ref, pv_ref), (kb, vb), (sem.at[2 * si], sem.at[2 * si + 1]), n_pages)

        pages_per_half = SEL_HALF * SEL_BLOCK // PAGE_SIZE

        def half(buf, hf):
            return jnp.concatenate(
                [buf[hf * pages_per_half + j] for j in range(pages_per_half)], axis=1).astype(BF16)

        col = _iota((R, LANES), 1)
        ss = [_dot(qaug[hf], jnp.concatenate([half(kb, hf), oh_ref[...]], axis=0)) for hf in range(len(qaug))]
        ss.append(jnp.where(col <= t_row, _nt(qa, kt_ref[slot]), NEG))
        m = functools.reduce(jnp.maximum, [jnp.max(s, axis=-1, keepdims=True) for s in ss])
        es = [jnp.exp(s - m) for s in ss]
        l = functools.reduce(jnp.add, [jnp.sum(e, axis=-1, keepdims=True) for e in es])
        acc = _dot(es[-1].astype(BF16), vt_ref[slot])
        for hf in range(len(qaug)):
            acc = acc + _nt(es[hf].astype(BF16), half(vb, hf))
        o_sel = acc / l

        sa = jnp.where(_iota((R, wc), 1) > t_row, _dot(qa, ckw_ref[slot].astype(BF16)), NEG)
        sb = jnp.where(col <= t_row, _nt(qa, kwn_ref[slot]), NEG)
        m = jnp.maximum(jnp.max(sa, axis=-1, keepdims=True), jnp.max(sb, axis=-1, keepdims=True))
        ea, eb = jnp.exp(sa - m), jnp.exp(sb - m)
        den = jnp.sum(ea, axis=-1, keepdims=True) + jnp.sum(eb, axis=-1, keepdims=True)
        o_win = (_nt(ea.astype(BF16), cvw_ref[slot].astype(BF16)) + _dot(eb.astype(BF16), vwn_ref[slot])) / den

        o_ref[slot, 0] = o_cmp
        o_ref[slot, 1] = o_sel
        o_ref[slot, 2] = o_win

        newest = _iota((LANES, LANES), 1) >= LANES - tn
        for src, new, dst in ((ckw_ref, kwt_ref, okw_ref), (cvw_ref, vwt_ref, ovw_ref)):
            shifted = pltpu.roll(src[slot], wc - tn, 1)
            dst[slot, :, 0:wc - LANES] = shifted[:, 0:wc - LANES]
            dst[slot, :, wc - LANES:wc] = jnp.where(newest, new[slot], shifted[:, wc - LANES:wc])

    @pl.when(s_id == 0)
    def _():
        start(0, 0)

    start(2 * s_id + 1, 1)
    finish(0)

    @pl.when(s_id + 1 < pl.num_programs(0))
    def _():
        start(2 * s_id + 2, 0)

    finish(1)


def _attn_sample(page_table, pool_k, pool_v, tn, qa, kct, vc, ovl, ktail, vtail, ckw, cvw, kwn, vwn, kwt, vwt):
    Bd, n_pages = page_table.shape
    R = N_HEADS * TOK_PAD
    wc = ckw.shape[2]
    const = lambda *shape: pl.BlockSpec(shape, lambda s, pt: (0,) * len(shape))
    pair = lambda *shape: pl.BlockSpec((2,) + shape, lambda s, pt: (s,) + (0,) * len(shape))
    anyspec = pl.BlockSpec(memory_space=pl.ANY)
    half_keys = SEL_HALF * SEL_BLOCK
    onehot = np.zeros((LANES, half_keys), np.float32)
    onehot[np.arange(half_keys) // SEL_BLOCK, np.arange(half_keys)] = 1.0
    grid_spec = pltpu.PrefetchScalarGridSpec(
        num_scalar_prefetch=1,
        grid=(Bd // 2,),
        in_specs=[anyspec, anyspec, pair(R, LANES), pair(LANES, kct.shape[2]), pair(vc.shape[1], LANES),
                  const(*ovl.shape), const(LANES, half_keys), pair(LANES, LANES), pair(LANES, LANES),
                  pair(LANES, wc), pair(LANES, wc), pair(LANES, LANES), pair(LANES, LANES),
                  pair(LANES, LANES), pair(LANES, LANES)],
        out_specs=[pair(3, R, LANES), pair(LANES, wc), pair(LANES, wc)],
        scratch_shapes=[pltpu.VMEM((n_pages, LANES, PAGE_SIZE), F32)] * 4 + [pltpu.SemaphoreType.DMA((4,))],
    )
    return pl.pallas_call(
        functools.partial(_attn_sample_body, n_pages, tn),
        grid_spec=grid_spec,
        out_shape=[jax.ShapeDtypeStruct((Bd, 3, R, LANES), F32)] + [jax.ShapeDtypeStruct((Bd, LANES, wc), F32)] * 2,
        compiler_params=pltpu.CompilerParams(dimension_semantics=("arbitrary",), vmem_limit_bytes=VMEM_LIMIT),
        name="attn_sample",
    )(page_table, pool_k, pool_v, qa, kct, vc, ovl, jnp.asarray(onehot, BF16), ktail, vtail, ckw, cvw, kwn, vwn,
      kwt, vwt)


def _mixer_sample_body(o3_ref, g3_ref, sza_ref, sra_ref, gpb_ref, x_ref, wpa_ref, wo_ref, fg_ref, y_ref):
    o = g3_ref[0] * o3_ref[0] + g3_ref[1] * o3_ref[1] + g3_ref[2] * o3_ref[2]
    y_ref[...] = _mixer_tail(o, sza_ref[...], sra_ref[...], gpb_ref[...], x_ref[...], wpa_ref, wo_ref, fg_ref)


def _mixer_sample(o3, g3, sza, sra, gpb, x2d, prm):
    T = x2d.shape[0]
    full = lambda *shape: pl.BlockSpec(shape, lambda i: (0,) * len(shape))
    return pl.pallas_call(
        _mixer_sample_body,
        grid=(1,),
        in_specs=[full(3, T, ATTN_WIDTH), full(3, T, ATTN_WIDTH), full(T, ATTN_WIDTH), full(T, D_MODEL),
                  full(T, D_MODEL), full(T, D_MODEL), full(ATTN_WIDTH, D_MODEL), full(D_MODEL, D_MODEL),
                  full(1, D_MODEL)],
        out_specs=full(T, D_MODEL),
        out_shape=jax.ShapeDtypeStruct((T, D_MODEL), F32),
        compiler_params=pltpu.CompilerParams(dimension_semantics=("arbitrary",), vmem_limit_bytes=VMEM_LIMIT),
        name="mixer_sample",
    )(o3, g3, sza, sra, gpb, x2d, prm["w_pa"], prm["w_o"], prm["final_g"])


def _overlap(n_cmp, n_sel):
    cs = np.arange(n_cmp)[:, None] * CMP_STRIDE
    ss = np.arange(n_sel)[None, :] * SEL_BLOCK
    ov = np.minimum(cs + CMP_BLOCK, ss + SEL_BLOCK) - np.maximum(cs, ss)
    return np.clip(ov, 0, None).astype(np.float32) / CMP_BLOCK


def _cmp_weights(w1, b1, w2, b2):
    w1r = w1.reshape(2, CMP_STRIDE, HEAD_DIM, CMP_HIDDEN).transpose(1, 2, 0, 3)
    w1r = w1r.reshape(CMP_STRIDE, 1, HEAD_DIM, 2 * CMP_HIDDEN).astype(BF16)
    w2r = w2.reshape(1, CMP_HIDDEN, HEAD_DIM).astype(BF16)
    big = jnp.concatenate(
        [jnp.pad(w1r, ((0, 0), (0, 0), (0, 0), (g * 2 * CMP_HIDDEN, (KV_HEADS - 1 - g) * 2 * CMP_HIDDEN)))
         for g in range(KV_HEADS)], axis=1)
    w2b = jnp.concatenate(
        [jnp.pad(w2r, ((0, 0), (0, 0), (g * HEAD_DIM, (KV_HEADS - 1 - g) * HEAD_DIM))) for g in range(KV_HEADS)],
        axis=0)
    return (big.reshape(CMP_STRIDE * LANES, 4 * LANES), jnp.tile(b1, KV_HEADS)[None],
            w2b.reshape(KV_HEADS * CMP_HIDDEN, LANES), jnp.tile(b2, KV_HEADS)[None])


def _seq_minor(t):
    lead = t.shape[:-3]
    n = len(lead)
    return t.transpose(*range(n), n + 1, n + 2, n).reshape(*lead, KV_WIDTH, t.shape[-3])


def _seq_major(t):
    lead = t.shape[:-2]
    n = len(lead)
    return t.reshape(*lead, KV_HEADS, HEAD_DIM, t.shape[-1]).transpose(*range(n), n + 2, n, n + 1)


def kernel(x_prompt, x_sample, cache_k_cmp, cache_v_cmp, cache_k_sel, cache_v_sel, cache_k_win, cache_v_win, page_table, norm_g, w_in, cmp_k_w1, cmp_k_b1, cmp_k_w2, cmp_k_b2, cmp_v_w1, cmp_v_b1, cmp_v_w2, cmp_v_b2, v_norm_g, v_norm_b, w_spatial, b_spatial, w_pa, w_pb, w_o, final_g):
    B, S, _ = x_prompt.shape
    Bd, tn, _ = x_sample.shape
    depth = w_in.shape[0]
    assert depth == 1, "single-layer step"
    assert Bd * tn == CHUNK, "the sample tokens form one 128-row tile"
    n_pages = page_table.shape[1]
    past = n_pages * PAGE_SIZE

    split = OFF_GL + 3 * N_HEADS
    w = w_in[0]
    prm = {
        "norm_g": norm_g,
        "w_in": jnp.concatenate([w.T[:split], jnp.zeros((GL_PAD, D_MODEL), F32), w.T[split:]], axis=0).astype(BF16),
        "w_sp": w_spatial[0],
        "b_sp": jnp.repeat(b_spatial[0].T, LANES, axis=1),
        "w_sp_s": jnp.stack([jnp.kron(jnp.eye(CHUNK // tn, dtype=F32), w_spatial[0, g, :tn, :tn])
                             for g in range(GMLP_GROUPS)]),
        "b_sp_s": jnp.tile(jnp.repeat(b_spatial[0, :, :tn].T, LANES, axis=1), (CHUNK // tn, 1)),
        "v_norm_g": v_norm_g, "v_norm_b": v_norm_b,
        "w_pb": w_pb[0].astype(BF16), "w_pa": w_pa[0].astype(BF16), "w_o": w_o[0].astype(BF16),
        "final_g": final_g[None],
        "cmp_k": _cmp_weights(cmp_k_w1[0], cmp_k_b1[0], cmp_k_w2[0], cmp_k_b2[0]),
        "cmp_v": _cmp_weights(cmp_v_w1[0], cmp_v_b1[0], cmp_v_w2[0], cmp_v_b2[0]),
    }

    xp = x_prompt.reshape(B * S, D_MODEL)
    (q_hm, p_kcmp, p_vcmp, p_ksel, p_vsel, p_kwin, p_vwin, kcr, vcr, kaug, kwt, vsb, vwb,
     gates, sza, sra, gpb) = _in_project(xp, _rope_tables(np.arange(S)), prm, TM_PROMPT, False, S)
    n_half = S // CMP_STRIDE
    n_cmp = (S - CMP_BLOCK) // CMP_STRIDE + 1
    n_sel = S // SEL_BLOCK
    assert n_sel <= SEL_HALF
    ovl = np.zeros((n_half, LANES), np.float32)
    ovl[:n_cmp, HEAD_DIM:HEAD_DIM + n_sel] = _overlap(n_cmp, n_sel)
    cend_tables = _rope_tables(np.arange(n_half) * CMP_STRIDE + CMP_BLOCK - 1)
    kct, vco = _compress_prompt(kcr.reshape(B, S, LANES), vcr.reshape(B, S, LANES), prm, cend_tables,
                                jnp.asarray(ovl, BF16))
    y_prompt = _attn_prompt(B, S, q_hm, kaug, vsb, kwt, vwb, kct, vco, gates, sza, sra, gpb, xp, prm)

    xs = x_sample.reshape(Bd * tn, D_MODEL)
    pos_s = np.tile(past + np.arange(tn), Bd)
    (q_s, s_kcmp, s_vcmp, s_ksel, s_vsel, s_kwin, s_vwin, gates_s, sza_s, sra_s, gpb_s, vn_s) = _in_project(
        xs, _rope_tables(pos_s), prm, Bd * tn, True, tn)
    n_half_s = past // CMP_STRIDE
    pools = [_seq_minor(c[0]) for c in (cache_k_cmp, cache_v_cmp, cache_k_sel, cache_v_sel)]
    cend_s = _rope_tables(np.arange(n_half_s) * CMP_STRIDE + CMP_BLOCK - 1)
    kct_s, vc_s = _compress_sample(page_table, pools[0], pools[1], prm, cend_s)

    n_cmp_s = (past + tn - CMP_BLOCK) // CMP_STRIDE + 1
    n_blk_s = past // SEL_BLOCK + -(-tn // SEL_BLOCK)
    assert n_cmp_s == n_half_s - 1 and tn <= TOK_PAD and (past // SEL_BLOCK) % SEL_HALF == 0
    ovl_s = np.zeros((n_half_s, 2 * LANES), np.float32)
    ovl_s[:n_cmp_s, :n_blk_s] = _overlap(n_cmp_s, n_blk_s)
    q5 = q_s.reshape(Bd, tn, KV_HEADS, HPG, HEAD_DIM).transpose(0, 2, 3, 1, 4)
    q5 = jnp.pad(q5, ((0, 0), (0, 0), (0, 0), (0, TOK_PAD - tn), (0, 0))).reshape(Bd, KV_HEADS, HPG * TOK_PAD, HEAD_DIM)
    qa = jnp.concatenate(
        [jnp.pad(q5[:, g], ((0, 0), (0, 0), (g * HEAD_DIM, (KV_HEADS - 1 - g) * HEAD_DIM))) for g in range(KV_HEADS)],
        axis=1).astype(BF16)
    pad_rows = lambda t: jnp.pad(t.reshape(Bd, tn, LANES), ((0, 0), (0, LANES - tn), (0, 0))).astype(BF16)
    ckw = _seq_minor(cache_k_win[0])
    cvw = _seq_minor(cache_v_win[0])
    new_t = lambda t: jnp.pad(t.reshape(Bd, tn, LANES).transpose(0, 2, 1), ((0, 0), (0, 0), (LANES - tn, 0)))
    o3, s_k_win, s_v_win = _attn_sample(
        page_table, pools[2], pools[3], tn, qa, kct_s, vc_s, jnp.asarray(ovl_s, BF16),
        pad_rows(s_ksel), pad_rows(s_vsel), ckw, cvw, pad_rows(s_kwin), pad_rows(s_vwin), new_t(s_kwin), new_t(s_vwin))
    s_k_win = _seq_major(s_k_win)[None]
    s_v_win = _seq_major(s_v_win)[None]
    first_group = jnp.arange(N_HEADS * TOK_PAD)[:, None] < HPG * TOK_PAD
    o3r = jnp.where(first_group, o3[..., :HEAD_DIM], o3[..., HEAD_DIM:])
    o3r = o3r.reshape(Bd, 3, N_HEADS, TOK_PAD, HEAD_DIM)[:, :, :, :tn]
    o3r = o3r.transpose(1, 0, 3, 2, 4).reshape(3, Bd * tn, ATTN_WIDTH)
    g3 = gates_s[:, :3 * N_HEADS].reshape(Bd * tn, N_HEADS, 3).transpose(2, 0, 1)
    g3 = jnp.repeat(g3, HEAD_DIM, axis=2)
    y_sample = _mixer_sample(o3r, g3, sza_s, sra_s, gpb_s, xs, prm)

    kv5 = lambda t, b, n: t.reshape(1, b, n, KV_HEADS, HEAD_DIM)
    pw = min(WINDOW, S)
    return (y_prompt.reshape(B, S, D_MODEL), y_sample.reshape(Bd, tn, D_MODEL),
            _seq_major(p_kcmp)[None], _seq_major(p_vcmp)[None], _seq_major(p_ksel)[None], _seq_major(p_vsel)[None],
            _seq_major(p_kwin[:, :, S - pw:])[None], _seq_major(p_vwin[:, :, S - pw:])[None],
            kv5(s_kcmp, Bd, tn), kv5(s_vcmp, Bd, tn), kv5(s_ksel, Bd, tn), kv5(s_vsel, Bd, tn),
            s_k_win, s_v_win, vn_s.reshape(1, Bd, tn, GMLP_WIDTH))
```

```python
import functools

import numpy as np
import jax
import jax.numpy as jnp
from jax import lax
from jax.experimental import pallas as pl
from jax.experimental.pallas import tpu as pltpu

F32 = jnp.float32
BF16 = jnp.bfloat16

D_MODEL = 1024
HEAD_DIM = 64
N_HEADS = 8
KV_HEADS = 2
HPG = N_HEADS // KV_HEADS
ATTN_WIDTH = N_HEADS * HEAD_DIM
KV_WIDTH = KV_HEADS * HEAD_DIM
ROT_DIM = HEAD_DIM // 4
ROT_HALF = ROT_DIM // 2
ROPE_THETA = 500000.0
CMP_BLOCK = 32
CMP_STRIDE = 16
CMP_HIDDEN = 128
SEL_BLOCK = 64
TOP_N = 16
N_INIT_BLOCKS = 1
N_LOCAL_BLOCKS = 2
WINDOW = 512
CHUNK = 128
GMLP_GROUPS = 4
GMLP_WIDTH = 512
PAGE_SIZE = 128
NEG = -1e30
BIG = 1e30
EPS = 1e-6
LOG2E = 1.4426950408889634

LANES = 128
GL_PAD = LANES - 3 * N_HEADS

OFF_Q = 0
OFF_KV = ATTN_WIDTH
OFF_GL = OFF_KV + 6 * KV_WIDTH
OFF_ZA = OFF_GL + LANES
OFF_U = OFF_ZA + ATTN_WIDTH
OFF_V = OFF_U + GMLP_WIDTH
OFF_ZB = OFF_V + GMLP_WIDTH
OFF_RA = OFF_ZB + GMLP_WIDTH
OFF_RB = OFF_RA + D_MODEL
W_TOT = OFF_RB + D_MODEL

VMEM_LIMIT = 56 * 1024 * 1024

TM_PROMPT = 512
TQ = 128
KT = 512
WIN_KEYS = WINDOW + TQ
SEL_HALF = 64
TOK_PAD = 8
RANK_GROUP = 8


def _nt(a, b):
    return lax.dot_general(a, b, (((1,), (1,)), ((), ())), preferred_element_type=F32)


def _dot(a, b):
    return jnp.dot(a, b, preferred_element_type=F32)


def _iota(shape, dim):
    return lax.broadcasted_iota(jnp.int32, shape, dim)


def _rope(slab, cos, s1, s2):
    return slab * cos + pltpu.roll(slab, LANES - ROT_HALF, 1) * s1 + pltpu.roll(slab, ROT_HALF, 1) * s2


def _rope_tables(pos):
    pos = np.asarray(pos, np.float64)
    n = pos.shape[0]
    inv = np.power(np.float64(ROPE_THETA), -np.arange(0, ROT_DIM, 2, dtype=np.float64) / ROT_DIM)
    ang = pos[:, None] * inv[None, :]
    cos, sin = np.cos(ang), np.sin(ang)
    rest = HEAD_DIM - ROT_DIM
    c = np.concatenate([cos, cos, np.ones((n, rest))], axis=1)
    s1 = np.concatenate([-sin, np.zeros((n, HEAD_DIM - ROT_HALF))], axis=1)
    s2 = np.concatenate([np.zeros((n, ROT_HALF)), sin, np.zeros((n, rest))], axis=1)
    return tuple(jnp.asarray(np.tile(t, (1, LANES // HEAD_DIM)), F32) for t in (c, s1, s2))


def _sigmoid(x):
    return 1.0 / (1.0 + jnp.exp(-x))


def _gelu(x):
    return jax.nn.gelu(x, approximate=True)


def _inproj_body(tm, sample, pos_tiles, x_ref, ng_ref, w_ref, cos_ref, s1_ref, s2_ref, wsp_ref, bsp_ref,
                 vng_ref, vnb_ref, wpb_ref, *outs):
    if sample:
        (q_ref, kcmp_ref, vcmp_ref, ksel_ref, vsel_ref, kwin_ref, vwin_ref,
         gat_ref, sza_ref, sra_ref, gpb_ref, vn_ref) = outs
    else:
        (q_ref, kcmp_ref, vcmp_ref, ksel_ref, vsel_ref, kwin_ref, vwin_ref,
         kcr_ref, vcr_ref, kaug_ref, kwt_ref, vsb_ref, vwb_ref, gat_ref, sza_ref, sra_ref, gpb_ref) = outs

    x = x_ref[...]
    r = lax.rsqrt(jnp.mean(x * x, axis=-1, keepdims=True) + EPS)
    h = ((x * r) * ng_ref[...]).astype(BF16)

    def proj(lo, hi):
        return _nt(h, w_ref[lo:hi, :])

    cos, s1, s2 = cos_ref[...], s1_ref[...], s2_ref[...]
    low = _iota((tm, LANES), 1) < HEAD_DIM

    q = proj(OFF_Q, OFF_Q + ATTN_WIDTH)
    q_scale = HEAD_DIM ** -0.5 if sample else HEAD_DIM ** -0.5 * LOG2E
    for pp in range(N_HEADS // 2):
        slab = _rope(q[:, pp * LANES:(pp + 1) * LANES], cos, s1, s2) * q_scale
        if sample:
            q_ref[:, pp * LANES:(pp + 1) * LANES] = slab
        elif (2 * pp) // HPG == 0:
            q_ref[2 * pp] = jnp.where(low, slab, 0.0).astype(BF16)
            q_ref[2 * pp + 1] = jnp.where(low, pltpu.roll(slab, HEAD_DIM, 1), 0.0).astype(BF16)
        else:
            q_ref[2 * pp] = jnp.where(low, 0.0, pltpu.roll(slab, HEAD_DIM, 1)).astype(BF16)
            q_ref[2 * pp + 1] = jnp.where(low, 0.0, slab).astype(BF16)

    kv = proj(OFF_KV, OFF_KV + 6 * KV_WIDTH)
    kcmp = kv[:, 0:LANES]
    vcmp = kv[:, LANES:2 * LANES]
    ksel = _rope(kv[:, 2 * LANES:3 * LANES], cos, s1, s2)
    vsel = kv[:, 3 * LANES:4 * LANES]
    kwin = _rope(kv[:, 4 * LANES:5 * LANES], cos, s1, s2)
    vwin = kv[:, 5 * LANES:6 * LANES]
    if sample:
        for ref, val in ((kcmp_ref, kcmp), (vcmp_ref, vcmp), (ksel_ref, ksel), (vsel_ref, vsel),
                         (kwin_ref, kwin), (vwin_ref, vwin)):
            ref[...] = val
    else:
        ksel_t = ksel.T
        kwin_t = kwin.T
        for ref, val in ((kcmp_ref, kcmp.T), (vcmp_ref, vcmp.T), (ksel_ref, ksel_t), (vsel_ref, vsel.T),
                         (kwin_ref, kwin_t), (vwin_ref, vwin.T)):
            ref[0] = val
        kcr_ref[...] = kcmp
        vcr_ref[...] = vcmp
        base = (pl.program_id(0) % pos_tiles) * tm
        blk = (base + _iota((LANES, tm), 1)) // SEL_BLOCK
        onehot = jnp.where(_iota((LANES, tm), 0) == blk, 1.0, 0.0)
        kaug_ref[0, 0:LANES] = ksel_t.astype(BF16)
        kaug_ref[0, LANES:2 * LANES] = onehot.astype(BF16)
        kwt_ref[0] = kwin_t.astype(BF16)
        for ref, val in ((vsb_ref, vsel), (vwb_ref, vwin)):
            ref[0] = jnp.where(low, val, 1.0).astype(BF16)
            ref[1] = jnp.where(low, 1.0, val).astype(BF16)

    gat_ref[...] = _sigmoid(proj(OFF_GL, OFF_GL + LANES))
    za = proj(OFF_ZA, OFF_ZA + ATTN_WIDTH)
    sza_ref[...] = za * _sigmoid(za)
    sra_ref[...] = _sigmoid(proj(OFF_RA, OFF_RA + D_MODEL))

    v = proj(OFF_V, OFF_V + GMLP_WIDTH)
    gv = _gelu(v)
    mu = jnp.mean(gv, axis=-1, keepdims=True)
    var = jnp.mean(jnp.square(gv - mu), axis=-1, keepdims=True)
    vn = ((gv - mu) * lax.rsqrt(var + EPS)) * vng_ref[...] + vnb_ref[...]
    if sample:
        vn_ref[...] = vn
    vnb16 = vn.astype(BF16)
    n_chunk = tm // CHUNK
    tri = _iota((CHUNK, CHUNK), 0) >= _iota((CHUNK, CHUNK), 1)
    mixed = []
    for g in range(GMLP_GROUPS):
        wm = jnp.where(tri, wsp_ref[g], 0.0).astype(BF16)
        cat = jnp.concatenate(
            [vnb16[c * CHUNK:(c + 1) * CHUNK, g * LANES:(g + 1) * LANES] for c in range(n_chunk)], axis=1)
        mixed.append(_dot(wm, cat))
    bsp = bsp_ref[...]
    sg = jnp.concatenate(
        [jnp.concatenate([mixed[g][:, c * LANES:(c + 1) * LANES] for g in range(GMLP_GROUPS)], axis=1) + bsp
         for c in range(n_chunk)], axis=0)
    u = proj(OFF_U, OFF_U + GMLP_WIDTH)
    zb = proj(OFF_ZB, OFF_ZB + GMLP_WIDTH)
    t = (_gelu(u) * sg) * (zb * _sigmoid(zb))
    pb = _dot(t.astype(BF16), wpb_ref[...])
    gpb_ref[...] = _sigmoid(proj(OFF_RB, OFF_RB + D_MODEL)) * pb


def _in_project(x2d, tables, prm, tm, sample, seq):
    T = x2d.shape[0]
    nt = T // tm
    pos_tiles = tables[0].shape[0] // tm
    const = lambda *shape: pl.BlockSpec(shape, lambda i: (0,) * len(shape))
    row = lambda width: pl.BlockSpec((tm, width), lambda i: (i, 0))
    tab = pl.BlockSpec((tm, LANES), lambda i: (i % pos_tiles, 0))
    in_specs = [
        row(D_MODEL), const(1, D_MODEL),
        pl.BlockSpec((W_TOT, D_MODEL), lambda i: (0, 0), pipeline_mode=pl.Buffered(1)),
        tab, tab, tab,
        const(GMLP_GROUPS, CHUNK, CHUNK), const(CHUNK, GMLP_WIDTH), const(1, GMLP_WIDTH), const(1, GMLP_WIDTH),
        const(GMLP_WIDTH, D_MODEL),
    ]
    f32rows = lambda width: jax.ShapeDtypeStruct((T, width), F32)
    tail_shapes = [f32rows(LANES), f32rows(ATTN_WIDTH), f32rows(D_MODEL), f32rows(D_MODEL)]
    tail_specs = [row(LANES), row(ATTN_WIDTH), row(D_MODEL), row(D_MODEL)]
    if sample:
        out_shape = [f32rows(ATTN_WIDTH)] + [f32rows(KV_WIDTH)] * 6 + tail_shapes + [f32rows(GMLP_WIDTH)]
        out_specs = [row(ATTN_WIDTH)] + [row(KV_WIDTH)] * 6 + tail_specs + [row(GMLP_WIDTH)]
    else:
        nb = T // seq
        bf = lambda *shape: jax.ShapeDtypeStruct(shape, BF16)
        tposed = lambda rows: pl.BlockSpec((1, rows, tm), lambda i: (i // pos_tiles, 0, i % pos_tiles))
        out_shape = ([bf(N_HEADS, T, LANES)] + [jax.ShapeDtypeStruct((nb, KV_WIDTH, seq), F32)] * 6
                     + [f32rows(KV_WIDTH)] * 2
                     + [bf(nb, 2 * LANES, seq), bf(nb, LANES, seq), bf(KV_HEADS, T, LANES), bf(KV_HEADS, T, LANES)]
                     + tail_shapes)
        heads = lambda n: pl.BlockSpec((n, tm, LANES), lambda i: (0, i, 0))
        out_specs = ([heads(N_HEADS)] + [tposed(KV_WIDTH)] * 6 + [row(KV_WIDTH)] * 2
                     + [tposed(2 * LANES), tposed(LANES), heads(KV_HEADS), heads(KV_HEADS)] + tail_specs)
    return pl.pallas_call(
        functools.partial(_inproj_body, tm, sample, pos_tiles),
        grid=(nt,),
        in_specs=in_specs,
        out_specs=out_specs,
        out_shape=out_shape,
        compiler_params=pltpu.CompilerParams(dimension_semantics=("arbitrary",), vmem_limit_bytes=VMEM_LIMIT),
        name="in_project_sample" if sample else "in_project_prompt",
    )(x2d, prm["norm_g"], prm["w_in"], *tables, prm["w_sp_s" if sample else "w_sp"],
      prm["b_sp_s" if sample else "b_sp"], prm["v_norm_g"], prm["v_norm_b"], prm["w_pb"])


def _strided_halfblocks(load_rows):
    return jnp.concatenate([load_rows(t) for t in range(CMP_STRIDE)], axis=1).astype(BF16)


def _compress_mlp(xcat, n, w1_ref, b1_ref, w2_ref, b2_ref):
    hh = _dot(xcat, w1_ref[...])
    hid = jnp.concatenate(
        [hh[:, 0:LANES] + pltpu.roll(hh[:, LANES:2 * LANES], n - 1, 0),
         hh[:, 2 * LANES:3 * LANES] + pltpu.roll(hh[:, 3 * LANES:4 * LANES], n - 1, 0)], axis=1) + b1_ref[...]
    return _dot(_gelu(hid).astype(BF16), w2_ref[...]) + b2_ref[...]


def _compress_prompt_body(n, kr_ref, vr_ref, w1k, b1k, w2k, b2k, w1v, b1v, w2v, b2v, cos_ref, s1_ref, s2_ref,
                          ovl_ref, kct_ref, vco_ref):
    kc = _compress_mlp(_strided_halfblocks(lambda t: kr_ref[0, pl.ds(t, n, stride=CMP_STRIDE), :]),
                       n, w1k, b1k, w2k, b2k)
    kct_ref[0] = _rope(kc, cos_ref[...], s1_ref[...], s2_ref[...]).T.astype(BF16)
    vc = _compress_mlp(_strided_halfblocks(lambda t: vr_ref[0, pl.ds(t, n, stride=CMP_STRIDE), :]),
                       n, w1v, b1v, w2v, b2v)
    vco_ref[0, :, 0:LANES] = vc.astype(BF16)
    vco_ref[0, :, LANES:2 * LANES] = ovl_ref[...]


def _compress_prompt(kr, vr, prm, tables, ovl):
    B, S, _ = kr.shape
    n = S // CMP_STRIDE
    const = lambda *shape: pl.BlockSpec(shape, lambda b: (0,) * len(shape))
    rows = pl.BlockSpec((1, S, LANES), lambda b: (b, 0, 0))
    wspecs = [const(CMP_STRIDE * LANES, 4 * LANES), const(1, 2 * LANES), const(2 * LANES, LANES), const(1, LANES)]
    return pl.pallas_call(
        functools.partial(_compress_prompt_body, n),
        grid=(B,),
        in_specs=[rows, rows] + wspecs + wspecs + [const(n, LANES)] * 3 + [const(n, LANES)],
        out_specs=[pl.BlockSpec((1, LANES, n), lambda b: (b, 0, 0)),
                   pl.BlockSpec((1, n, 2 * LANES), lambda b: (b, 0, 0))],
        out_shape=[jax.ShapeDtypeStruct((B, LANES, n), BF16),
                   jax.ShapeDtypeStruct((B, n, 2 * LANES), BF16)],
        compiler_params=pltpu.CompilerParams(dimension_semantics=("arbitrary",), vmem_limit_bytes=VMEM_LIMIT),
        name="compress_prompt",
    )(kr, vr, *prm["cmp_k"], *prm["cmp_v"], *tables, ovl)


def _start_pages(pt_ref, pools, bufs, sems, b, n_pages):
    def body(p, carry):
        page = pt_ref[b, p]
        for i, (pool, buf, sem) in enumerate(zip(pools, bufs, sems)):
            pltpu.make_async_copy(pool.at[page], buf.at[p], sem).start(priority=i % 2)
        return carry
    lax.fori_loop(0, n_pages, body, 0)


def _wait_pages(pools, bufs, sems, n_pages):
    for pool, buf, sem in zip(pools, bufs, sems):
        pltpu.make_async_copy(pool.at[pl.ds(0, n_pages)], buf, sem).wait()


def _compress_sample_body(n, n_pages, pt_ref, pk_ref, pv_ref, w1k, b1k, w2k, b2k, w1v, b1v, w2v, b2v,
                          cos_ref, s1_ref, s2_ref, perm_ref, kct_ref, vc_ref, kb0, kb1, vb0, vb1, xk, xv, sem):
    s = pl.program_id(0)
    bufs = ((kb0, vb0, 0), (kb1, vb1, 1))

    def start(b, slot):
        kb, vb, si = bufs[slot]
        _start_pages(pt_ref, (pk_ref, pv_ref), (kb, vb), (sem.at[2 * si], sem.at[2 * si + 1]), b, n_pages)

    hb_per_page = PAGE_SIZE // CMP_STRIDE

    def halfblocks(buf, xcat):
        pairs =jnp.concatenate([buf[pl.ds(0, n_pages // 2, stride=2)], buf[pl.ds(1, n_pages // 2, stride=2)]], axis=2)
        y = _dot(pairs.reshape(n_pages // 2 * LANES, 2 * PAGE_SIZE).astype(BF16), perm_ref[...])
        for p in range(n_pages):
            z = y[p // 2 * LANES:(p // 2 + 1) * LANES, p % 2 * PAGE_SIZE:(p % 2 + 1) * PAGE_SIZE].T
            for t in range(CMP_STRIDE):
                xcat[p * hb_per_page:(p + 1) * hb_per_page, t * LANES:(t + 1) * LANES] = (
                    z[t * hb_per_page:(t + 1) * hb_per_page])

    def finish(slot):
        kb, vb, si = bufs[slot]
        _wait_pages((pk_ref, pv_ref), (kb, vb), (sem.at[2 * si], sem.at[2 * si + 1]), n_pages)
        halfblocks(kb, xk)
        halfblocks(vb, xv)
        kc = _compress_mlp(xk[...].astype(BF16), n, w1k, b1k, w2k, b2k)
        kct_ref[slot] = _rope(kc, cos_ref[...], s1_ref[...], s2_ref[...]).T.astype(BF16)
        vc_ref[slot] = _compress_mlp(xv[...].astype(BF16), n, w1v, b1v, w2v, b2v).astype(BF16)

    @pl.when(s == 0)
    def _():
        start(0, 0)

    start(2 * s + 1, 1)
    finish(0)

    @pl.when(s + 1 < pl.num_programs(0))
    def _():
        start(2 * s + 2, 0)

    finish(1)


def _compress_sample(page_table, pool_k, pool_v, prm, tables):
    Bd, n_pages = page_table.shape
    past = n_pages * PAGE_SIZE
    n = past // CMP_STRIDE
    const = lambda *shape: pl.BlockSpec(shape, lambda s, pt: (0,) * len(shape))
    wspecs = [const(CMP_STRIDE * LANES, 4 * LANES), const(1, 2 * LANES), const(2 * LANES, LANES), const(1, LANES)]
    anyspec = pl.BlockSpec(memory_space=pl.ANY)
    grid_spec = pltpu.PrefetchScalarGridSpec(
        num_scalar_prefetch=1,
        grid=(Bd // 2,),
        in_specs=[anyspec, anyspec] + wspecs + wspecs + [const(n, LANES)] * 3 + [const(2 * PAGE_SIZE, 2 * PAGE_SIZE)],
        out_specs=[pl.BlockSpec((2, LANES, n), lambda s, pt: (s, 0, 0)),
                   pl.BlockSpec((2, n, LANES), lambda s, pt: (s, 0, 0))],
        scratch_shapes=[pltpu.VMEM((n_pages, LANES, PAGE_SIZE), F32)] * 4
        + [pltpu.VMEM((n, CMP_STRIDE * LANES), F32)] * 2 + [pltpu.SemaphoreType.DMA((4,))],
    )
    hb = PAGE_SIZE // CMP_STRIDE
    perm = np.zeros((2 * PAGE_SIZE, 2 * PAGE_SIZE), np.float32)
    for side in range(2):
        for jj in range(hb):
            for t in range(CMP_STRIDE):
                perm[side * PAGE_SIZE + CMP_STRIDE * jj + t, side * PAGE_SIZE + t * hb + jj] = 1.0
    return pl.pallas_call(
        functools.partial(_compress_sample_body, n, n_pages),
        grid_spec=grid_spec,
        out_shape=[jax.ShapeDtypeStruct((Bd, LANES, n), BF16), jax.ShapeDtypeStruct((Bd, n, LANES), BF16)],
        compiler_params=pltpu.CompilerParams(dimension_semantics=("arbitrary",), vmem_limit_bytes=VMEM_LIMIT),
        name="compress_sample",
    )(page_table, pool_k, pool_v, *prm["cmp_k"], *prm["cmp_v"], *tables, jnp.asarray(perm, BF16))


def _mixer_tail(o, sza, sra, gpb, x, wpa_ref, wo_ref, fg_ref):
    pa = _dot((o * sza).astype(BF16), wpa_ref[...])
    merged = sra * pa + gpb
    hn = x + _dot(merged.astype(BF16), wo_ref[...])
    r = lax.rsqrt(jnp.mean(hn * hn, axis=-1, keepdims=True) + EPS)
    return (hn * r) * fg_ref[...]


def _topk_mask(score, blk, n_blocks, axis, top):
    cnt = jnp.zeros(score.shape, jnp.int32)
    for sp in range(n_blocks):
        b = lax.slice_in_dim(score, sp, sp + 1, axis=axis)
        ge = jnp.where(b >= score, 1, 0)
        gt = jnp.where(b > score, 1, 0)
        cnt = cnt + jnp.where(blk > sp, ge, gt)
    return cnt < top


def _attn_prompt_body(n_sel, q_ref, ka_ref, vs_ref, kw_ref, vw_ref, kc_ref, vco_ref, band_ref, rowhot_ref, gat_ref,
                      sza_ref, sra_ref, gpb_ref, x_ref, wpa_ref, wo_ref, fg_ref, y_ref):
    i = pl.program_id(1)
    M = HPG * TQ
    G = range(KV_HEADS)
    qpos = i * TQ + (_iota((M, 1), 0) & (TQ - 1))
    n_cmp_pad = kc_ref.shape[2]
    gt = gat_ref[...]
    qs = [q_ref[HPG * g:HPG * (g + 1)].reshape(M, LANES) for g in G]

    mk = (_iota((M, n_cmp_pad), 1) * CMP_STRIDE + (CMP_BLOCK - 1)) <= qpos
    o_cmp, qa = [], []
    blk = _iota((n_sel, TQ), 0)
    cur = (i * TQ + _iota((n_sel, TQ), 1)) // SEL_BLOCK
    valid = blk <= cur
    forced = (valid & (blk > cur - N_LOCAL_BLOCKS)) | (blk < N_INIT_BLOCKS)
    scores = []
    for g in G:
        s = jnp.where(mk, _dot(qs[g], kc_ref[0]), NEG)
        e = jnp.exp2(s - jnp.max(s, axis=-1, keepdims=True))
        p = jnp.where(mk, e / jnp.sum(e, axis=-1, keepdims=True), 0.0)
        r = _dot(p.astype(BF16), vco_ref[0])
        o_cmp.append(r[:, 0:LANES])
        impc = r[:, LANES:2 * LANES]
        imp = impc[0:TQ] + impc[TQ:2 * TQ] + impc[2 * TQ:3 * TQ] + impc[3 * TQ:4 * TQ]
        sc = imp.T[HEAD_DIM:HEAD_DIM + n_sel]
        scores.append(jnp.where(forced, BIG, jnp.where(valid, sc, NEG)))

    last_blk = ((i + 1) * TQ - 1) // SEL_BLOCK

    def count_group(k, cnts):
        out = []
        for g in G:
            c = cnts[g]
            for sp in range(k * RANK_GROUP, (k + 1) * RANK_GROUP):
                b = scores[g][sp:sp + 1]
                c = c + jnp.where(blk > sp, jnp.where(b >= scores[g], 1, 0), jnp.where(b > scores[g], 1, 0))
            out.append(c)
        return tuple(out)

    cnts = (jnp.zeros((n_sel, TQ), jnp.int32),) * KV_HEADS
    for k in range(n_sel // RANK_GROUP):
        cnts = lax.cond(k * RANK_GROUP <= last_blk, functools.partial(count_group, k), lambda c: c, cnts)
    for g in G:
        selneg = jnp.where(cnts[g] < TOP_N, 0.0, NEG)
        seln_t = jnp.concatenate([selneg, jnp.zeros((LANES - n_sel, TQ), F32)], axis=0).T.astype(BF16)
        qa.append(jnp.concatenate([qs[g], jnp.concatenate([seln_t] * HPG, axis=0)], axis=1))

    def update(carry, s, v):
        m, acc = carry
        mn = jnp.maximum(m, jnp.max(s, axis=-1, keepdims=True))
        acc = jnp.exp2(m - mn) * acc + _dot(jnp.exp2((s - mn).astype(BF16)), v)
        return mn, acc

    def step(kt, carries, causal):
        off = pl.multiple_of(kt * KT, KT)
        k = ka_ref[0, :, pl.ds(off, KT)]
        ss = [_dot(qa[g], k) for g in G]
        if causal:
            keep = kt * KT + _iota((M, KT), 1) <= qpos
            ss = [jnp.where(keep, s, NEG) for s in ss]
        return tuple(update(carries[g], ss[g], vs_ref[g, pl.ds(off, KT), :]) for g in G)

    n_full = i // (KT // TQ)
    init = (jnp.full((M, 1), NEG, F32), jnp.zeros((M, LANES), F32))
    carries = lax.fori_loop(0, n_full // 2, lambda kp, c: step(2 * kp + 1, step(2 * kp, c, False), False),
                            (init,) * KV_HEADS)
    carries = lax.cond(n_full % 2 == 1, lambda c: step(n_full - 1, c, False), lambda c: c, carries)
    carries = step(n_full, carries, True)
    sum_lane = [(KV_HEADS - 1 - g) * HEAD_DIM for g in G]
    o_sel = [carries[g][1] / carries[g][1][:, sum_lane[g]:sum_lane[g] + 1] for g in G]

    st = pl.multiple_of(jnp.maximum(i * TQ - WINDOW, 0), TQ)
    kw = jnp.concatenate([kw_ref[0, :, pl.ds(st, WIN_KEYS)], band_ref[jnp.minimum(i, WINDOW // TQ)]], axis=0)
    o_groups = []
    for g in G:
        s = _dot(jnp.concatenate([qs[g], rowhot_ref[...]], axis=1), kw)
        e = jnp.exp2(s - jnp.max(s, axis=-1, keepdims=True))
        r = _dot(e.astype(BF16), vw_ref[g, pl.ds(st, WIN_KEYS), :])
        o_win = r / r[:, sum_lane[g]:sum_lane[g] + 1]

        def gcol(c, g=g):
            return jnp.concatenate(
                [gt[:, 3 * (HPG * g + j) + c:3 * (HPG * g + j) + c + 1] for j in range(HPG)], axis=0)

        o_groups.append(gcol(0) * o_cmp[g] + gcol(1) * o_sel[g] + gcol(2) * o_win)

    low = _iota((TQ, LANES), 1) < HEAD_DIM
    slabs = []
    for pp in range(N_HEADS // 2):
        g, j0 = pp // 2, (2 * pp) % HPG
        a = o_groups[g][j0 * TQ:(j0 + 1) * TQ]
        b = o_groups[g][(j0 + 1) * TQ:(j0 + 2) * TQ]
        if g == 0:
            b = pltpu.roll(b, HEAD_DIM, 1)
        else:
            a = pltpu.roll(a, HEAD_DIM, 1)
        slabs.append(jnp.where(low, a, b))
    o = jnp.concatenate(slabs, axis=1)
    y_ref[...] = _mixer_tail(o, sza_ref[...], sra_ref[...], gpb_ref[...], x_ref[...], wpa_ref, wo_ref, fg_ref)


def _attn_prompt(B, S, q_hm, kaug, vsb, kwt, vwb, kct, vco, gates, sza, sra, gpb, x2d, prm):
    nq = S // TQ
    n_sel = S // SEL_BLOCK
    n_cmp_pad = kct.shape[2]
    row = lambda width: pl.BlockSpec((TQ, width), lambda b, i: (b * nq + i, 0))
    const = lambda *shape: pl.BlockSpec(shape, lambda b, i: (0,) * len(shape))
    batch = lambda *shape: pl.BlockSpec((1,) + shape, lambda b, i: (b,) + (0,) * len(shape))
    in_specs = [
        pl.BlockSpec((N_HEADS, TQ, LANES), lambda b, i: (0, b * nq + i, 0)),
        batch(2 * LANES, S),
        pl.BlockSpec((KV_HEADS, S, LANES), lambda b, i: (0, b, 0)),
        batch(LANES, S),
        pl.BlockSpec((KV_HEADS, S, LANES), lambda b, i: (0, b, 0)),
        batch(LANES, n_cmp_pad),
        batch(n_cmp_pad, 2 * LANES),
        const(WINDOW // TQ + 1, TQ, WIN_KEYS), const(HPG * TQ, TQ),
        row(LANES), row(ATTN_WIDTH), row(D_MODEL), row(D_MODEL), row(D_MODEL),
        const(ATTN_WIDTH, D_MODEL), const(D_MODEL, D_MODEL), const(1, D_MODEL),
    ]
    d = np.arange(WIN_KEYS)[None, :] - np.arange(TQ)[:, None]
    band = np.stack([np.where((d <= min(WINDOW, v * TQ)) & (d > min(WINDOW, v * TQ) - WINDOW), 0.0, NEG)
                     for v in range(WINDOW // TQ + 1)]).astype(np.float32)
    rowhot = np.tile(np.eye(TQ, dtype=np.float32), (HPG, 1))
    return pl.pallas_call(
        functools.partial(_attn_prompt_body, n_sel),
        grid=(B, nq),
        in_specs=in_specs,
        out_specs=row(D_MODEL),
        out_shape=jax.ShapeDtypeStruct((B * S, D_MODEL), F32),
        compiler_params=pltpu.CompilerParams(dimension_semantics=("arbitrary", "arbitrary"),
                                             vmem_limit_bytes=VMEM_LIMIT),
        name="attn_prompt",
    )(q_hm, kaug, vsb, kwt, vwb, kct, vco, jnp.asarray(band, BF16), jnp.asarray(rowhot, BF16), gates, sza, sra, gpb,
      x2d, prm["w_pa"], prm["w_o"], prm["final_g"])


def _attn_sample_body(n_pages, tn, pt_ref, pk_ref, pv_ref, qa_ref, kc_ref, vc_ref, ovl_ref, oh_ref, kt_ref, vt_ref,
                      ckw_ref, cvw_ref, kwn_ref, vwn_ref, kwt_ref, vwt_ref, o_ref, okw_ref, ovw_ref,
                      kb0, kb1, vb0, vb1, sem):
    s_id = pl.program_id(0)
    past = n_pages * PAGE_SIZE
    n_past_blk = past // SEL_BLOCK
    n_blk_pad = ovl_ref.shape[1]
    R = N_HEADS * TOK_PAD
    GR = HPG * TOK_PAD
    wc = ckw_ref.shape[2]
    bufs = ((kb0, vb0, 0), (kb1, vb1, 1))
    t_row = _iota((R, 1), 0) & (TOK_PAD - 1)

    def start(b, slot):
        kb, vb, si = bufs[slot]
        _start_pages(pt_ref, (pk_ref, pv_ref), (kb, vb), (sem.at[2 * si], sem.at[2 * si + 1]), b, n_pages)

    def finish(slot):
        kb, vb, si = bufs[slot]
        qa = qa_ref[slot]

        n_cmp_pad = kc_ref.shape[2]
        s = _dot(qa, kc_ref[slot])
        mk = _iota((R, n_cmp_pad), 1) < n_cmp_pad - 1
        s = jnp.where(mk, s, NEG)
        e = jnp.exp(s - jnp.max(s, axis=-1, keepdims=True))
        p = jnp.where(mk, e / jnp.sum(e, axis=-1, keepdims=True), 0.0).astype(BF16)
        o_cmp = _dot(p, vc_ref[slot])
        impc = _dot(p, ovl_ref[...])
        imp = jnp.concatenate(
            [impc[g * GR:g * GR + TOK_PAD] + impc[g * GR + TOK_PAD:g * GR + 2 * TOK_PAD]
             + impc[g * GR + 2 * TOK_PAD:g * GR + 3 * TOK_PAD] + impc[g * GR + 3 * TOK_PAD:g * GR + 4 * TOK_PAD]
             for g in range(KV_HEADS)], axis=0)
        nr = KV_HEADS * TOK_PAD
        n_rank = -(-n_past_blk // LANES) * LANES
        blk = _iota((nr, n_rank), 1)
        cur = (past + jnp.minimum(_iota((nr, n_rank), 0) & (TOK_PAD - 1), tn - 1)) // SEL_BLOCK
        forced = (blk > cur - N_LOCAL_BLOCKS) | (blk < N_INIT_BLOCKS)
        score = jnp.where(blk < n_past_blk, jnp.where(forced, BIG, imp[:, 0:n_rank]), NEG)
        sel = _topk_mask(score, blk, n_past_blk, 1, TOP_N - 1)
        selneg = jnp.where(sel, 0.0, NEG)
        selneg = jnp.concatenate(
            [selneg[g * TOK_PAD:(g + 1) * TOK_PAD] for g in range(KV_HEADS) for _ in range(HPG)], axis=0)
        low = _iota((R, LANES), 1) < SEL_HALF
        qaug = []
        for hf in range(n_past_blk // SEL_HALF):
            slab = selneg[:, (hf // 2) * LANES:(hf // 2 + 1) * LANES]
            if hf % 2:
                slab = pltpu.roll(slab, SEL_HALF, 1)
            qaug.append(jnp.concatenate([qa, jnp.where(low, slab, 0.0).astype(BF16)], axis=1))

        _wait_pages((pk_ref, pv_ref), (kb, vb), (sem.at[2 * si], sem.at[2 * si + 1]), n_pages)

        pages_per_half = SEL_HALF * SEL_BLOCK // PAGE_SIZE

        def half(buf, hf):
            return jnp.concatenate(
                [buf[hf * pages_per_half + j] for j in range(pages_per_half)], axis=1).astype(BF16)

        col = _iota((R, LANES), 1)
        ss = [_dot(qaug[hf], jnp.concatenate([half(kb, hf), oh_ref[...]], axis=0)) for hf in range(len(qaug))]
        ss.append(jnp.where(col <= t_row, _nt(qa, kt_ref[slot]), NEG))
        m = functools.reduce(jnp.maximum, [jnp.max(s, axis=-1, keepdims=True) for s in ss])
        es = [jnp.exp(s - m) for s in ss]
        l = functools.reduce(jnp.add, [jnp.sum(e, axis=-1, keepdims=True) for e in es])
        acc = _dot(es[-1].astype(BF16), vt_ref[slot])
        for hf in range(len(qaug)):
            acc = acc + _nt(es[hf].astype(BF16), half(vb, hf))
        o_sel = acc / l

        sa = jnp.where(_iota((R, wc), 1) > t_row, _dot(qa, ckw_ref[slot].astype(BF16)), NEG)
        sb = jnp.where(col <= t_row, _nt(qa, kwn_ref[slot]), NEG)
        m = jnp.maximum(jnp.max(sa, axis=-1, keepdims=True), jnp.max(sb, axis=-1, keepdims=True))
        ea, eb = jnp.exp(sa - m), jnp.exp(sb - m)
        den = jnp.sum(ea, axis=-1, keepdims=True) + jnp.sum(eb, axis=-1, keepdims=True)
        o_win = (_nt(ea.astype(BF16), cvw_ref[slot].astype(BF16)) + _dot(eb.astype(BF16), vwn_ref[slot])) / den

        o_ref[slot, 0] = o_cmp
        o_ref[slot, 1] = o_sel
        o_ref[slot, 2] = o_win

        newest = _iota((LANES, LANES), 1) >= LANES - tn
        for src, new, dst in ((ckw_ref, kwt_ref, okw_ref), (cvw_ref, vwt_ref, ovw_ref)):
            shifted = pltpu.roll(src[slot], wc - tn, 1)
            dst[slot, :, 0:wc - LANES] = shifted[:, 0:wc - LANES]
            dst[slot, :, wc - LANES:wc] = jnp.where(newest, new[slot], shifted[:, wc - LANES:wc])

    @pl.when(s_id == 0)
    def _():
        start(0, 0)

    start(2 * s_id + 1, 1)
    finish(0)

    @pl.when(s_id + 1 < pl.num_programs(0))
    def _():
        start(2 * s_id + 2, 0)

    finish(1)


def _attn_sample(page_table, pool_k, pool_v, tn, qa, kct, vc, ovl, ktail, vtail, ckw, cvw, kwn, vwn, kwt, vwt):
    Bd, n_pages = page_table.shape
    R = N_HEADS * TOK_PAD
    wc = ckw.shape[2]
    const = lambda *shape: pl.BlockSpec(shape, lambda s, pt: (0,) * len(shape))
    pair = lambda *shape: pl.BlockSpec((2,) + shape, lambda s, pt: (s,) + (0,) * len(shape))
    anyspec = pl.BlockSpec(memory_space=pl.ANY)
    half_keys = SEL_HALF * SEL_BLOCK
    onehot = np.zeros((LANES, half_keys), np.float32)
    onehot[np.arange(half_keys) // SEL_BLOCK, np.arange(half_keys)] = 1.0
    grid_spec = pltpu.PrefetchScalarGridSpec(
        num_scalar_prefetch=1,
        grid=(Bd // 2,),
        in_specs=[anyspec, anyspec, pair(R, LANES), pair(LANES, kct.shape[2]), pair(vc.shape[1], LANES),
                  const(*ovl.shape), const(LANES, half_keys), pair(LANES, LANES), pair(LANES, LANES),
                  pair(LANES, wc), pair(LANES, wc), pair(LANES, LANES), pair(LANES, LANES),
                  pair(LANES, LANES), pair(LANES, LANES)],
        out_specs=[pair(3, R, LANES), pair(LANES, wc), pair(LANES, wc)],
        scratch_shapes=[pltpu.VMEM((n_pages, LANES, PAGE_SIZE), F32)] * 4 + [pltpu.SemaphoreType.DMA((4,))],
    )
    return pl.pallas_call(
        functools.partial(_attn_sample_body, n_pages, tn),
        grid_spec=grid_spec,
        out_shape=[jax.ShapeDtypeStruct((Bd, 3, R, LANES), F32)] + [jax.ShapeDtypeStruct((Bd, LANES, wc), F32)] * 2,
        compiler_params=pltpu.CompilerParams(dimension_semantics=("arbitrary",), vmem_limit_bytes=VMEM_LIMIT),
        name="attn_sample",
    )(page_table, pool_k, pool_v, qa, kct, vc, ovl, jnp.asarray(onehot, BF16), ktail, vtail, ckw, cvw, kwn, vwn,
      kwt, vwt)


def _mixer_sample_body(o3_ref, g3_ref, sza_ref, sra_ref, gpb_ref, x_ref, wpa_ref, wo_ref, fg_ref, y_ref):
    o = g3_ref[0] * o3_ref[0] + g3_ref[1] * o3_ref[1] + g3_ref[2] * o3_ref[2]
    y_ref[...] = _mixer_tail(o, sza_ref[...], sra_ref[...], gpb_ref[...], x_ref[...], wpa_ref, wo_ref, fg_ref)


def _mixer_sample(o3, g3, sza, sra, gpb, x2d, prm):
    T = x2d.shape[0]
    full = lambda *shape: pl.BlockSpec(shape, lambda i: (0,) * len(shape))
    return pl.pallas_call(
        _mixer_sample_body,
        grid=(1,),
        in_specs=[full(3, T, ATTN_WIDTH), full(3, T, ATTN_WIDTH), full(T, ATTN_WIDTH), full(T, D_MODEL),
                  full(T, D_MODEL), full(T, D_MODEL), full(ATTN_WIDTH, D_MODEL), full(D_MODEL, D_MODEL),
                  full(1, D_MODEL)],
        out_specs=full(T, D_MODEL),
        out_shape=jax.ShapeDtypeStruct((T, D_MODEL), F32),
        compiler_params=pltpu.CompilerParams(dimension_semantics=("arbitrary",), vmem_limit_bytes=VMEM_LIMIT),
        name="mixer_sample",
    )(o3, g3, sza, sra, gpb, x2d, prm["w_pa"], prm["w_o"], prm["final_g"])


def _overlap(n_cmp, n_sel):
    cs = np.arange(n_cmp)[:, None] * CMP_STRIDE
    ss = np.arange(n_sel)[None, :] * SEL_BLOCK
    ov = np.minimum(cs + CMP_BLOCK, ss + SEL_BLOCK) - np.maximum(cs, ss)
    return np.clip(ov, 0, None).astype(np.float32) / CMP_BLOCK


def _cmp_weights(w1, b1, w2, b2):
    w1r = w1.reshape(2, CMP_STRIDE, HEAD_DIM, CMP_HIDDEN).transpose(1, 2, 0, 3)
    w1r = w1r.reshape(CMP_STRIDE, 1, HEAD_DIM, 2 * CMP_HIDDEN).astype(BF16)
    w2r = w2.reshape(1, CMP_HIDDEN, HEAD_DIM).astype(BF16)
    big = jnp.concatenate(
        [jnp.pad(w1r, ((0, 0), (0, 0), (0, 0), (g * 2 * CMP_HIDDEN, (KV_HEADS - 1 - g) * 2 * CMP_HIDDEN)))
         for g in range(KV_HEADS)], axis=1)
    w2b = jnp.concatenate(
        [jnp.pad(w2r, ((0, 0), (0, 0), (g * HEAD_DIM, (KV_HEADS - 1 - g) * HEAD_DIM))) for g in range(KV_HEADS)],
        axis=0)
    return (big.reshape(CMP_STRIDE * LANES, 4 * LANES), jnp.tile(b1, KV_HEADS)[None],
            w2b.reshape(KV_HEADS * CMP_HIDDEN, LANES), jnp.tile(b2, KV_HEADS)[None])


def _seq_minor(t):
    lead = t.shape[:-3]
    n = len(lead)
    return t.transpose(*range(n), n + 1, n + 2, n).reshape(*lead, KV_WIDTH, t.shape[-3])


def _seq_major(t):
    lead = t.shape[:-2]
    n = len(lead)
    return t.reshape(*lead, KV_HEADS, HEAD_DIM, t.shape[-1]).transpose(*range(n), n + 2, n, n + 1)


def kernel(x_prompt, x_sample, cache_k_cmp, cache_v_cmp, cache_k_sel, cache_v_sel, cache_k_win, cache_v_win, page_table, norm_g, w_in, cmp_k_w1, cmp_k_b1, cmp_k_w2, cmp_k_b2, cmp_v_w1, cmp_v_b1, cmp_v_w2, cmp_v_b2, v_norm_g, v_norm_b, w_spatial, b_spatial, w_pa, w_pb, w_o, final_g):
    B, S, _ = x_prompt.shape
    Bd, tn, _ = x_sample.shape
    depth = w_in.shape[0]
    assert depth == 1, "single-layer step"
    assert Bd * tn == CHUNK, "the sample tokens form one 128-row tile"
    n_pages = page_table.shape[1]
    past = n_pages * PAGE_SIZE

    split = OFF_GL + 3 * N_HEADS
    w = w_in[0]
    prm = {
        "norm_g": norm_g,
        "w_in": jnp.concatenate([w.T[:split], jnp.zeros((GL_PAD, D_MODEL), F32), w.T[split:]], axis=0).astype(BF16),
        "w_sp": w_spatial[0],
        "b_sp": jnp.repeat(b_spatial[0].T, LANES, axis=1),
        "w_sp_s": jnp.stack([jnp.kron(jnp.eye(CHUNK // tn, dtype=F32), w_spatial[0, g, :tn, :tn])
                             for g in range(GMLP_GROUPS)]),
        "b_sp_s": jnp.tile(jnp.repeat(b_spatial[0, :, :tn].T, LANES, axis=1), (CHUNK // tn, 1)),
        "v_norm_g": v_norm_g, "v_norm_b": v_norm_b,
        "w_pb": w_pb[0].astype(BF16), "w_pa": w_pa[0].astype(BF16), "w_o": w_o[0].astype(BF16),
        "final_g": final_g[None],
        "cmp_k": _cmp_weights(cmp_k_w1[0], cmp_k_b1[0], cmp_k_w2[0], cmp_k_b2[0]),
        "cmp_v": _cmp_weights(cmp_v_w1[0], cmp_v_b1[0], cmp_v_w2[0], cmp_v_b2[0]),
    }

    xp = x_prompt.reshape(B * S, D_MODEL)
    (q_hm, p_kcmp, p_vcmp, p_ksel, p_vsel, p_kwin, p_vwin, kcr, vcr, kaug, kwt, vsb, vwb,
     gates, sza, sra, gpb) = _in_project(xp, _rope_tables(np.arange(S)), prm, TM_PROMPT, False, S)
    n_half = S // CMP_STRIDE
    n_cmp = (S - CMP_BLOCK) // CMP_STRIDE + 1
    n_sel = S // SEL_BLOCK
    assert n_sel <= SEL_HALF
    ovl = np.zeros((n_half, LANES), np.float32)
    ovl[:n_cmp, HEAD_DIM:HEAD_DIM + n_sel] = _overlap(n_cmp, n_sel)
    cend_tables = _rope_tables(np.arange(n_half) * CMP_STRIDE + CMP_BLOCK - 1)
    kct, vco = _compress_prompt(kcr.reshape(B, S, LANES), vcr.reshape(B, S, LANES), prm, cend_tables,
                                jnp.asarray(ovl, BF16))
    y_prompt = _attn_prompt(B, S, q_hm, kaug, vsb, kwt, vwb, kct, vco, gates, sza, sra, gpb, xp, prm)

    xs = x_sample.reshape(Bd * tn, D_MODEL)
    pos_s = np.tile(past + np.arange(tn), Bd)
    (q_s, s_kcmp, s_vcmp, s_ksel, s_vsel, s_kwin, s_vwin, gates_s, sza_s, sra_s, gpb_s, vn_s) = _in_project(
        xs, _rope_tables(pos_s), prm, Bd * tn, True, tn)
    n_half_s = past // CMP_STRIDE
    pools = [_seq_minor(c[0]) for c in (cache_k_cmp, cache_v_cmp, cache_k_sel, cache_v_sel)]
    cend_s = _rope_tables(np.arange(n_half_s) * CMP_STRIDE + CMP_BLOCK - 1)
    kct_s, vc_s = _compress_sample(page_table, pools[0], pools[1], prm, cend_s)

    n_cmp_s = (past + tn - CMP_BLOCK) // CMP_STRIDE + 1
    n_blk_s = past // SEL_BLOCK + -(-tn // SEL_BLOCK)
    assert n_cmp_s == n_half_s - 1 and tn <= TOK_PAD and (past // SEL_BLOCK) % SEL_HALF == 0
    ovl_s = np.zeros((n_half_s, 2 * LANES), np.float32)
    ovl_s[:n_cmp_s, :n_blk_s] = _overlap(n_cmp_s, n_blk_s)
    q5 = q_s.reshape(Bd, tn, KV_HEADS, HPG, HEAD_DIM).transpose(0, 2, 3, 1, 4)
    q5 = jnp.pad(q5, ((0, 0), (0, 0), (0, 0), (0, TOK_PAD - tn), (0, 0))).reshape(Bd, KV_HEADS, HPG * TOK_PAD, HEAD_DIM)
    qa = jnp.concatenate(
        [jnp.pad(q5[:, g], ((0, 0), (0, 0), (g * HEAD_DIM, (KV_HEADS - 1 - g) * HEAD_DIM))) for g in range(KV_HEADS)],
        axis=1).astype(BF16)
    pad_rows = lambda t: jnp.pad(t.reshape(Bd, tn, LANES), ((0, 0), (0, LANES - tn), (0, 0))).astype(BF16)
    ckw = _seq_minor(cache_k_win[0])
    cvw = _seq_minor(cache_v_win[0])
    new_t = lambda t: jnp.pad(t.reshape(Bd, tn, LANES).transpose(0, 2, 1), ((0, 0), (0, 0), (LANES - tn, 0)))
    o3, s_k_win, s_v_win = _attn_sample(
        page_table, pools[2], pools[3], tn, qa, kct_s, vc_s, jnp.asarray(ovl_s, BF16),
        pad_rows(s_ksel), pad_rows(s_vsel), ckw, cvw, pad_rows(s_kwin), pad_rows(s_vwin), new_t(s_kwin), new_t(s_vwin))
    s_k_win = _seq_major(s_k_win)[None]
    s_v_win = _seq_major(s_v_win)[None]
    first_group = jnp.arange(N_HEADS * TOK_PAD)[:, None] < HPG * TOK_PAD
    o3r = jnp.where(first_group, o3[..., :HEAD_DIM], o3[..., HEAD_DIM:])
    o3r = o3r.reshape(Bd, 3, N_HEADS, TOK_PAD, HEAD_DIM)[:, :, :, :tn]
    o3r = o3r.transpose(1, 0, 3, 2, 4).reshape(3, Bd * tn, ATTN_WIDTH)
    g3 = gates_s[:, :3 * N_HEADS].reshape(Bd * tn, N_HEADS, 3).transpose(2, 0, 1)
    g3 = jnp.repeat(g3, HEAD_DIM, axis=2)
    y_sample = _mixer_sample(o3r, g3, sza_s, sra_s, gpb_s, xs, prm)

    kv5 = lambda t, b, n: t.reshape(1, b, n, KV_HEADS, HEAD_DIM)
    pw = min(WINDOW, S)
    return (y_prompt.reshape(B, S, D_MODEL), y_sample.reshape(Bd, tn, D_MODEL),
            _seq_major(p_kcmp)[None], _seq_major(p_vcmp)[None], _seq_major(p_ksel)[None], _seq_major(p_vsel)[None],
            _seq_major(p_kwin[:, :, S - pw:])[None], _seq_major(p_vwin[:, :, S - pw:])[None],
            kv5(s_kcmp, Bd, tn), kv5(s_vcmp, Bd, tn), kv5(s_ksel, Bd, tn), kv5(s_vsel, Bd, tn),
            s_k_win, s_v_win, vn_s.reshape(1, Bd, tn, GMLP_WIDTH))
```

```python
import functools

import numpy as np
import jax
import jax.numpy as jnp
from jax import lax
from jax.experimental import pallas as pl
from jax.experimental.pallas import tpu as pltpu

F32 = jnp.float32
BF16 = jnp.bfloat16

D_MODEL = 1024
HEAD_DIM = 64
N_HEADS = 8
KV_HEADS = 2
HPG = N_HEADS // KV_HEADS
ATTN_WIDTH = N_HEADS * HEAD_DIM
KV_WIDTH = KV_HEADS * HEAD_DIM
ROT_DIM = HEAD_DIM // 4
ROT_HALF = ROT_DIM // 2
ROPE_THETA = 500000.0
CMP_BLOCK = 32
CMP_STRIDE = 16
CMP_HIDDEN = 128
SEL_BLOCK = 64
TOP_N = 16
N_INIT_BLOCKS = 1
N_LOCAL_BLOCKS = 2
WINDOW = 512
CHUNK = 128
GMLP_GROUPS = 4
GMLP_WIDTH = 512
PAGE_SIZE = 128
NEG = -1e30
BIG = 1e30
EPS = 1e-6
LOG2E = 1.4426950408889634

LANES = 128
GL_PAD = LANES - 3 * N_HEADS

OFF_Q = 0
OFF_KV = ATTN_WIDTH
OFF_GL = OFF_KV + 6 * KV_WIDTH
OFF_ZA = OFF_GL + LANES
OFF_U = OFF_ZA + ATTN_WIDTH
OFF_V = OFF_U + GMLP_WIDTH
OFF_ZB = OFF_V + GMLP_WIDTH
OFF_RA = OFF_ZB + GMLP_WIDTH
OFF_RB = OFF_RA + D_MODEL
W_TOT = OFF_RB + D_MODEL

VMEM_LIMIT = 56 * 1024 * 1024

TM_PROMPT = 512
TQ = 128
KT = 512
WIN_KEYS = WINDOW + TQ
SEL_HALF = 64
TOK_PAD = 8
RANK_GROUP = 8


def _nt(a, b):
    return lax.dot_general(a, b, (((1,), (1,)), ((), ())), preferred_element_type=F32)


def _dot(a, b):
    return jnp.dot(a, b, preferred_element_type=F32)


def _iota(shape, dim):
    return lax.broadcasted_iota(jnp.int32, shape, dim)


def _rope(slab, cos, s1, s2):
    return slab * cos + pltpu.roll(slab, LANES - ROT_HALF, 1) * s1 + pltpu.roll(slab, ROT_HALF, 1) * s2


def _rope_tables(pos):
    pos = np.asarray(pos, np.float64)
    n = pos.shape[0]
    inv = np.power(np.float64(ROPE_THETA), -np.arange(0, ROT_DIM, 2, dtype=np.float64) / ROT_DIM)
    ang = pos[:, None] * inv[None, :]
    cos, sin = np.cos(ang), np.sin(ang)
    rest = HEAD_DIM - ROT_DIM
    c = np.concatenate([cos, cos, np.ones((n, rest))], axis=1)
    s1 = np.concatenate([-sin, np.zeros((n, HEAD_DIM - ROT_HALF))], axis=1)
    s2 = np.concatenate([np.zeros((n, ROT_HALF)), sin, np.zeros((n, rest))], axis=1)
    return tuple(jnp.asarray(np.tile(t, (1, LANES // HEAD_DIM)), F32) for t in (c, s1, s2))


def _sigmoid(x):
    return 1.0 / (1.0 + jnp.exp(-x))


def _gelu(x):
    return jax.nn.gelu(x, approximate=True)


def _inproj_body(tm, sample, pos_tiles, x_ref, ng_ref, w_ref, cos_ref, s1_ref, s2_ref, wsp_ref, bsp_ref,
                 vng_ref, vnb_ref, wpb_ref, *outs):
    if sample:
        (q_ref, kcmp_ref, vcmp_ref, ksel_ref, vsel_ref, kwin_ref, vwin_ref,
         gat_ref, sza_ref, sra_ref, gpb_ref, vn_ref) = outs
    else:
        (q_ref, kcmp_ref, vcmp_ref, ksel_ref, vsel_ref, kwin_ref, vwin_ref,
         kcr_ref, vcr_ref, kaug_ref, kwt_ref, vsb_ref, vwb_ref, gat_ref, sza_ref, sra_ref, gpb_ref) = outs

    x = x_ref[...]
    r = lax.rsqrt(jnp.mean(x * x, axis=-1, keepdims=True) + EPS)
    h = ((x * r) * ng_ref[...]).astype(BF16)

    def proj(lo, hi):
        return _nt(h, w_ref[lo:hi, :])

    cos, s1, s2 = cos_ref[...], s1_ref[...], s2_ref[...]
    low = _iota((tm, LANES), 1) < HEAD_DIM

    q = proj(OFF_Q, OFF_Q + ATTN_WIDTH)
    q_scale = HEAD_DIM ** -0.5 if sample else HEAD_DIM ** -0.5 * LOG2E
    for pp in range(N_HEADS // 2):
        slab = _rope(q[:, pp * LANES:(pp + 1) * LANES], cos, s1, s2) * q_scale
        if sample:
            q_ref[:, pp * LANES:(pp + 1) * LANES] = slab
        elif (2 * pp) // HPG == 0:
            q_ref[2 * pp] = jnp.where(low, slab, 0.0).astype(BF16)
            q_ref[2 * pp + 1] = jnp.where(low, pltpu.roll(slab, HEAD_DIM, 1), 0.0).astype(BF16)
        else:
            q_ref[2 * pp] = jnp.where(low, 0.0, pltpu.roll(slab, HEAD_DIM, 1)).astype(BF16)
            q_ref[2 * pp + 1] = jnp.where(low, 0.0, slab).astype(BF16)

    kv = proj(OFF_KV, OFF_KV + 6 * KV_WIDTH)
    kcmp = kv[:, 0:LANES]
    vcmp = kv[:, LANES:2 * LANES]
    ksel = _rope(kv[:, 2 * LANES:3 * LANES], cos, s1, s2)
    vsel = kv[:, 3 * LANES:4 * LANES]
    kwin = _rope(kv[:, 4 * LANES:5 * LANES], cos, s1, s2)
    vwin = kv[:, 5 * LANES:6 * LANES]
    if sample:
        for ref, val in ((kcmp_ref, kcmp), (vcmp_ref, vcmp), (ksel_ref, ksel), (vsel_ref, vsel),
                         (kwin_ref, kwin), (vwin_ref, vwin)):
            ref[...] = val
    else:
        ksel_t = ksel.T
        kwin_t = kwin.T
        for ref, val in ((kcmp_ref, kcmp.T), (vcmp_ref, vcmp.T), (ksel_ref, ksel_t), (vsel_ref, vsel.T),
                         (kwin_ref, kwin_t), (vwin_ref, vwin.T)):
            ref[0] = val
        kcr_ref[...] = kcmp
        vcr_ref[...] = vcmp
        base = (pl.program_id(0) % pos_tiles) * tm
        blk = (base + _iota((LANES, tm), 1)) // SEL_BLOCK
        onehot = jnp.where(_iota((LANES, tm), 0) == blk, 1.0, 0.0)
        kaug_ref[0, 0:LANES] = ksel_t.astype(BF16)
        kaug_ref[0, LANES:2 * LANES] = onehot.astype(BF16)
        kwt_ref[0] = kwin_t.astype(BF16)
        for ref, val in ((vsb_ref, vsel), (vwb_ref, vwin)):
            ref[0] = jnp.where(low, val, 1.0).astype(BF16)
            ref[1] = jnp.where(low, 1.0, val).astype(BF16)

    gat_ref[...] = _sigmoid(proj(OFF_GL, OFF_GL + LANES))
    za = proj(OFF_ZA, OFF_ZA + ATTN_WIDTH)
    sza_ref[...] = za * _sigmoid(za)
    sra_ref[...] = _sigmoid(proj(OFF_RA, OFF_RA + D_MODEL))

    v = proj(OFF_V, OFF_V + GMLP_WIDTH)
    gv = _gelu(v)
    mu = jnp.mean(gv, axis=-1, keepdims=True)
    var = jnp.mean(jnp.square(gv - mu), axis=-1, keepdims=True)
    vn = ((gv - mu) * lax.rsqrt(var + EPS)) * vng_ref[...] + vnb_ref[...]
    if sample:
        vn_ref[...] = vn
    vnb16 = vn.astype(BF16)
    n_chunk = tm // CHUNK
    tri = _iota((CHUNK, CHUNK), 0) >= _iota((CHUNK, CHUNK), 1)
    mixed = []
    for g in range(GMLP_GROUPS):
        wm = jnp.where(tri, wsp_ref[g], 0.0).astype(BF16)
        cat = jnp.concatenate(
            [vnb16[c * CHUNK:(c + 1) * CHUNK, g * LANES:(g + 1) * LANES] for c in range(n_chunk)], axis=1)
        mixed.append(_dot(wm, cat))
    bsp = bsp_ref[...]
    sg = jnp.concatenate(
        [jnp.concatenate([mixed[g][:, c * LANES:(c + 1) * LANES] for g in range(GMLP_GROUPS)], axis=1) + bsp
         for c in range(n_chunk)], axis=0)
    u = proj(OFF_U, OFF_U + GMLP_WIDTH)
    zb = proj(OFF_ZB, OFF_ZB + GMLP_WIDTH)
    t = (_gelu(u) * sg) * (zb * _sigmoid(zb))
    pb = _dot(t.astype(BF16), wpb_ref[...])
    gpb_ref[...] = _sigmoid(proj(OFF_RB, OFF_RB + D_MODEL)) * pb


def _in_project(x2d, tables, prm, tm, sample, seq):
    T = x2d.shape[0]
    nt = T // tm
    pos_tiles = tables[0].shape[0] // tm
    const = lambda *shape: pl.BlockSpec(shape, lambda i: (0,) * len(shape))
    row = lambda width: pl.BlockSpec((tm, width), lambda i: (i, 0))
    tab = pl.BlockSpec((tm, LANES), lambda i: (i % pos_tiles, 0))
    in_specs = [
        row(D_MODEL), const(1, D_MODEL),
        pl.BlockSpec((W_TOT, D_MODEL), lambda i: (0, 0), pipeline_mode=pl.Buffered(1)),
        tab, tab, tab,
        const(GMLP_GROUPS, CHUNK, CHUNK), const(CHUNK, GMLP_WIDTH), const(1, GMLP_WIDTH), const(1, GMLP_WIDTH),
        const(GMLP_WIDTH, D_MODEL),
    ]
    f32rows = lambda width: jax.ShapeDtypeStruct((T, width), F32)
    tail_shapes = [f32rows(LANES), f32rows(ATTN_WIDTH), f32rows(D_MODEL), f32rows(D_MODEL)]
    tail_specs = [row(LANES), row(ATTN_WIDTH), row(D_MODEL), row(D_MODEL)]
    if sample:
        out_shape = [f32rows(ATTN_WIDTH)] + [f32rows(KV_WIDTH)] * 6 + tail_shapes + [f32rows(GMLP_WIDTH)]
        out_specs = [row(ATTN_WIDTH)] + [row(KV_WIDTH)] * 6 + tail_specs + [row(GMLP_WIDTH)]
    else:
        nb = T // seq
        bf = lambda *shape: jax.ShapeDtypeStruct(shape, BF16)
        tposed = lambda rows: pl.BlockSpec((1, rows, tm), lambda i: (i // pos_tiles, 0, i % pos_tiles))
        out_shape = ([bf(N_HEADS, T, LANES)] + [jax.ShapeDtypeStruct((nb, KV_WIDTH, seq), F32)] * 6
                     + [f32rows(KV_WIDTH)] * 2
                     + [bf(nb, 2 * LANES, seq), bf(nb, LANES, seq), bf(KV_HEADS, T, LANES), bf(KV_HEADS, T, LANES)]
                     + tail_shapes)
        heads = lambda n: pl.BlockSpec((n, tm, LANES), lambda i: (0, i, 0))
        out_specs = ([heads(N_HEADS)] + [tposed(KV_WIDTH)] * 6 + [row(KV_WIDTH)] * 2
                     + [tposed(2 * LANES), tposed(LANES), heads(KV_HEADS), heads(KV_HEADS)] + tail_specs)
    return pl.pallas_call(
        functools.partial(_inproj_body, tm, sample, pos_tiles),
        grid=(nt,),
        in_specs=in_specs,
        out_specs=out_specs,
        out_shape=out_shape,
        compiler_params=pltpu.CompilerParams(dimension_semantics=("arbitrary",), vmem_limit_bytes=VMEM_LIMIT),
        name="in_project_sample" if sample else "in_project_prompt",
    )(x2d, prm["norm_g"], prm["w_in"], *tables, prm["w_sp_s" if sample else "w_sp"],
      prm["b_sp_s" if sample else "b_sp"], prm["v_norm_g"], prm["v_norm_b"], prm["w_pb"])


def _strided_halfblocks(load_rows):
    return jnp.concatenate([load_rows(t) for t in range(CMP_STRIDE)], axis=1).astype(BF16)


def _compress_mlp(xcat, n, w1_ref, b1_ref, w2_ref, b2_ref):
    hh = _dot(xcat, w1_ref[...])
    hid = jnp.concatenate(
        [hh[:, 0:LANES] + pltpu.roll(hh[:, LANES:2 * LANES], n - 1, 0),
         hh[:, 2 * LANES:3 * LANES] + pltpu.roll(hh[:, 3 * LANES:4 * LANES], n - 1, 0)], axis=1) + b1_ref[...]
    return _dot(_gelu(hid).astype(BF16), w2_ref[...]) + b2_ref[...]


def _compress_prompt_body(n, kr_ref, vr_ref, w1k, b1k, w2k, b2k, w1v, b1v, w2v, b2v, cos_ref, s1_ref, s2_ref,
                          ovl_ref, kct_ref, vco_ref):
    kc = _compress_mlp(_strided_halfblocks(lambda t: kr_ref[0, pl.ds(t, n, stride=CMP_STRIDE), :]),
                       n, w1k, b1k, w2k, b2k)
    kct_ref[0] = _rope(kc, cos_ref[...], s1_ref[...], s2_ref[...]).T.astype(BF16)
    vc = _compress_mlp(_strided_halfblocks(lambda t: vr_ref[0, pl.ds(t, n, stride=CMP_STRIDE), :]),
                       n, w1v, b1v, w2v, b2v)
    vco_ref[0, :, 0:LANES] = vc.astype(BF16)
    vco_ref[0, :, LANES:2 * LANES] = ovl_ref[...]


def _compress_prompt(kr, vr, prm, tables, ovl):
    B, S, _ = kr.shape
    n = S // CMP_STRIDE
    const = lambda *shape: pl.BlockSpec(shape, lambda b: (0,) * len(shape))
    rows = pl.BlockSpec((1, S, LANES), lambda b: (b, 0, 0))
    wspecs = [const(CMP_STRIDE * LANES, 4 * LANES), const(1, 2 * LANES), const(2 * LANES, LANES), const(1, LANES)]
    return pl.pallas_call(
        functools.partial(_compress_prompt_body, n),
        grid=(B,),
        in_specs=[rows, rows] + wspecs + wspecs + [const(n, LANES)] * 3 + [const(n, LANES)],
        out_specs=[pl.BlockSpec((1, LANES, n), lambda b: (b, 0, 0)),
                   pl.BlockSpec((1, n, 2 * LANES), lambda b: (b, 0, 0))],
        out_shape=[jax.ShapeDtypeStruct((B, LANES, n), BF16),
                   jax.ShapeDtypeStruct((B, n, 2 * LANES), BF16)],
        compiler_params=pltpu.CompilerParams(dimension_semantics=("arbitrary",), vmem_limit_bytes=VMEM_LIMIT),
        name="compress_prompt",
    )(kr, vr, *prm["cmp_k"], *prm["cmp_v"], *tables, ovl)


def _start_pages(pt_ref, pools, bufs, sems, b, n_pages):
    def body(p, carry):
        page = pt_ref[b, p]
        for i, (pool, buf, sem) in enumerate(zip(pools, bufs, sems)):
            pltpu.make_async_copy(pool.at[page], buf.at[p], sem).start(priority=i % 2)
        return carry
    lax.fori_loop(0, n_pages, body, 0)


def _wait_pages(pools, bufs, sems, n_pages):
    for pool, buf, sem in zip(pools, bufs, sems):
        pltpu.make_async_copy(pool.at[pl.ds(0, n_pages)], buf, sem).wait()


PAGE_SLOTS = 4


def _two_batches_per_step(s, start, finish):
    base = 2 * (s % 2)

    @pl.when(s == 0)
    def _():
        start(0, 0)
        start(1, 1)

    @pl.when(s + 1 < pl.num_programs(0))
    def _():
        start(2 * s + 2, 2 - base)
        start(2 * s + 3, 3 - base)

    finish(0, base)
    finish(1, base + 1)


def _compress_sample_body(n, n_pages, pt_ref, pk_ref, pv_ref, w1k, b1k, w2k, b2k, w1v, b1v, w2v, b2v,
                          cos_ref, s1_ref, s2_ref, perm_ref, kct_ref, vc_ref, kbuf, vbuf, xk, xv, sem):
    s = pl.program_id(0)

    def slot_refs(slot):
        return (kbuf.at[slot], vbuf.at[slot]), (sem.at[2 * slot], sem.at[2 * slot + 1])

    def start(b, slot):
        bufs, sems = slot_refs(slot)
        _start_pages(pt_ref, (pk_ref, pv_ref), bufs, sems, b, n_pages)

    hb_per_page = PAGE_SIZE // CMP_STRIDE

    def halfblocks(buf, xcat):
        pairs =jnp.concatenate([buf[pl.ds(0, n_pages // 2, stride=2)], buf[pl.ds(1, n_pages // 2, stride=2)]], axis=2)
        y = _dot(pairs.reshape(n_pages // 2 * LANES, 2 * PAGE_SIZE).astype(BF16), perm_ref[...])
        for p in range(n_pages):
            z = y[p // 2 * LANES:(p // 2 + 1) * LANES, p % 2 * PAGE_SIZE:(p % 2 + 1) * PAGE_SIZE].T
            for t in range(CMP_STRIDE):
                xcat[p * hb_per_page:(p + 1) * hb_per_page, t * LANES:(t + 1) * LANES] = (
                    z[t * hb_per_page:(t + 1) * hb_per_page])

    def finish(j, slot):
        (kb, vb), sems = slot_refs(slot)
        _wait_pages((pk_ref, pv_ref), (kb, vb), sems, n_pages)
        halfblocks(kb, xk)
        halfblocks(vb, xv)
        kc = _compress_mlp(xk[...].astype(BF16), n, w1k, b1k, w2k, b2k)
        kct_ref[j] = _rope(kc, cos_ref[...], s1_ref[...], s2_ref[...]).T.astype(BF16)
        vc_ref[j] = _compress_mlp(xv[...].astype(BF16), n, w1v, b1v, w2v, b2v).astype(BF16)

    _two_batches_per_step(s, start, finish)


def _compress_sample(page_table, pool_k, pool_v, prm, tables):
    Bd, n_pages = page_table.shape
    past = n_pages * PAGE_SIZE
    n = past // CMP_STRIDE
    const = lambda *shape: pl.BlockSpec(shape, lambda s, pt: (0,) * len(shape))
    wspecs = [const(CMP_STRIDE * LANES, 4 * LANES), const(1, 2 * LANES), const(2 * LANES, LANES), const(1, LANES)]
    anyspec = pl.BlockSpec(memory_space=pl.ANY)
    grid_spec = pltpu.PrefetchScalarGridSpec(
        num_scalar_prefetch=1,
        grid=(Bd // 2,),
        in_specs=[anyspec, anyspec] + wspecs + wspecs + [const(n, LANES)] * 3 + [const(2 * PAGE_SIZE, 2 * PAGE_SIZE)],
        out_specs=[pl.BlockSpec((2, LANES, n), lambda s, pt: (s, 0, 0)),
                   pl.BlockSpec((2, n, LANES), lambda s, pt: (s, 0, 0))],
        scratch_shapes=[pltpu.VMEM((PAGE_SLOTS, n_pages, LANES, PAGE_SIZE), F32)] * 2
        + [pltpu.VMEM((n, CMP_STRIDE * LANES), F32)] * 2 + [pltpu.SemaphoreType.DMA((2 * PAGE_SLOTS,))],
    )
    hb = PAGE_SIZE // CMP_STRIDE
    perm = np.zeros((2 * PAGE_SIZE, 2 * PAGE_SIZE), np.float32)
    for side in range(2):
        for jj in range(hb):
            for t in range(CMP_STRIDE):
                perm[side * PAGE_SIZE + CMP_STRIDE * jj + t, side * PAGE_SIZE + t * hb + jj] = 1.0
    return pl.pallas_call(
        functools.partial(_compress_sample_body, n, n_pages),
        grid_spec=grid_spec,
        out_shape=[jax.ShapeDtypeStruct((Bd, LANES, n), BF16), jax.ShapeDtypeStruct((Bd, n, LANES), BF16)],
        compiler_params=pltpu.CompilerParams(dimension_semantics=("arbitrary",), vmem_limit_bytes=VMEM_LIMIT),
        name="compress_sample",
    )(page_table, pool_k, pool_v, *prm["cmp_k"], *prm["cmp_v"], *tables, jnp.asarray(perm, BF16))


def _mixer_tail(o, sza, sra, gpb, x, wpa_ref, wo_ref, fg_ref):
    pa = _dot((o * sza).astype(BF16), wpa_ref[...])
    merged = sra * pa + gpb
    hn = x + _dot(merged.astype(BF16), wo_ref[...])
    r = lax.rsqrt(jnp.mean(hn * hn, axis=-1, keepdims=True) + EPS)
    return (hn * r) * fg_ref[...]


def _topk_mask(score, blk, n_blocks, axis):
    cnt = jnp.zeros(score.shape, jnp.int32)
    for sp in range(n_blocks):
        b = lax.slice_in_dim(score, sp, sp + 1, axis=axis)
        ge = jnp.where(b >= score, 1, 0)
        gt = jnp.where(b > score, 1, 0)
        cnt = cnt + jnp.where(blk > sp, ge, gt)
    return cnt < TOP_N


def _attn_prompt_body(n_sel, q_ref, ka_ref, vs_ref, kw_ref, vw_ref, kc_ref, vco_ref, band_ref, rowhot_ref, gat_ref,
                      sza_ref, sra_ref, gpb_ref, x_ref, wpa_ref, wo_ref, fg_ref, y_ref):
    i = pl.program_id(1)
    M = HPG * TQ
    G = range(KV_HEADS)
    qpos = i * TQ + (_iota((M, 1), 0) & (TQ - 1))
    n_cmp_pad = kc_ref.shape[2]
    gt = gat_ref[...]
    qs = [q_ref[HPG * g:HPG * (g + 1)].reshape(M, LANES) for g in G]

    mk = (_iota((M, n_cmp_pad), 1) * CMP_STRIDE + (CMP_BLOCK - 1)) <= qpos
    o_cmp, qa = [], []
    blk = _iota((n_sel, TQ), 0)
    cur = (i * TQ + _iota((n_sel, TQ), 1)) // SEL_BLOCK
    valid = blk <= cur
    forced = (valid & (blk > cur - N_LOCAL_BLOCKS)) | (blk < N_INIT_BLOCKS)
    scores = []
    for g in G:
        s = jnp.where(mk, _dot(qs[g], kc_ref[0]), NEG)
        e = jnp.exp2(s - jnp.max(s, axis=-1, keepdims=True))
        p = jnp.where(mk, e / jnp.sum(e, axis=-1, keepdims=True), 0.0)
        r = _dot(p.astype(BF16), vco_ref[0])
        o_cmp.append(r[:, 0:LANES])
        impc = r[:, LANES:2 * LANES]
        imp = impc[0:TQ] + impc[TQ:2 * TQ] + impc[2 * TQ:3 * TQ] + impc[3 * TQ:4 * TQ]
        sc = imp.T[HEAD_DIM:HEAD_DIM + n_sel]
        scores.append(jnp.where(forced, BIG, jnp.where(valid, sc, NEG)))

    last_blk = ((i + 1) * TQ - 1) // SEL_BLOCK

    def count_group(k, cnts):
        out = []
        for g in G:
            c = cnts[g]
            for sp in range(k * RANK_GROUP, (k + 1) * RANK_GROUP):
                b = scores[g][sp:sp + 1]
                c = c + jnp.where(blk > sp, jnp.where(b >= scores[g], 1, 0), jnp.where(b > scores[g], 1, 0))
            out.append(c)
        return tuple(out)

    cnts = (jnp.zeros((n_sel, TQ), jnp.int32),) * KV_HEADS
    for k in range(n_sel // RANK_GROUP):
        cnts = lax.cond(k * RANK_GROUP <= last_blk, functools.partial(count_group, k), lambda c: c, cnts)
    for g in G:
        selneg = jnp.where(cnts[g] < TOP_N, 0.0, NEG)
        seln_t = jnp.concatenate([selneg, jnp.zeros((LANES - n_sel, TQ), F32)], axis=0).T.astype(BF16)
        qa.append(jnp.concatenate([qs[g], jnp.concatenate([seln_t] * HPG, axis=0)], axis=1))

    def update(carry, s, v):
        m, acc = carry
        mn = jnp.maximum(m, jnp.max(s, axis=-1, keepdims=True))
        acc = jnp.exp2(m - mn) * acc + _dot(jnp.exp2((s - mn).astype(BF16)), v)
        return mn, acc

    def step(kt, carries, causal):
        off = pl.multiple_of(kt * KT, KT)
        k = ka_ref[0, :, pl.ds(off, KT)]
        ss = [_dot(qa[g], k) for g in G]
        if causal:
            keep = kt * KT + _iota((M, KT), 1) <= qpos
            ss = [jnp.where(keep, s, NEG) for s in ss]
        return tuple(update(carries[g], ss[g], vs_ref[g, pl.ds(off, KT), :]) for g in G)

    n_full = i // (KT // TQ)
    init = (jnp.full((M, 1), NEG, F32), jnp.zeros((M, LANES), F32))
    carries = lax.fori_loop(0, n_full // 2, lambda kp, c: step(2 * kp + 1, step(2 * kp, c, False), False),
                            (init,) * KV_HEADS)
    carries = lax.cond(n_full % 2 == 1, lambda c: step(n_full - 1, c, False), lambda c: c, carries)
    carries = step(n_full, carries, True)
    sum_lane = [(KV_HEADS - 1 - g) * HEAD_DIM for g in G]
    o_sel = [carries[g][1] / carries[g][1][:, sum_lane[g]:sum_lane[g] + 1] for g in G]

    st = pl.multiple_of(jnp.maximum(i * TQ - WINDOW, 0), TQ)
    kw = jnp.concatenate([kw_ref[0, :, pl.ds(st, WIN_KEYS)], band_ref[jnp.minimum(i, WINDOW // TQ)]], axis=0)
    o_groups = []
    for g in G:
        s = _dot(jnp.concatenate([qs[g], rowhot_ref[...]], axis=1), kw)
        e = jnp.exp2(s - jnp.max(s, axis=-1, keepdims=True))
        r = _dot(e.astype(BF16), vw_ref[g, pl.ds(st, WIN_KEYS), :])
        o_win = r / r[:, sum_lane[g]:sum_lane[g] + 1]

        def gcol(c, g=g):
            return jnp.concatenate(
                [gt[:, 3 * (HPG * g + j) + c:3 * (HPG * g + j) + c + 1] for j in range(HPG)], axis=0)

        o_groups.append(gcol(0) * o_cmp[g] + gcol(1) * o_sel[g] + gcol(2) * o_win)

    low = _iota((TQ, LANES), 1) < HEAD_DIM
    slabs = []
    for pp in range(N_HEADS // 2):
        g, j0 = pp // 2, (2 * pp) % HPG
        a = o_groups[g][j0 * TQ:(j0 + 1) * TQ]
        b = o_groups[g][(j0 + 1) * TQ:(j0 + 2) * TQ]
        if g == 0:
            b = pltpu.roll(b, HEAD_DIM, 1)
        else:
            a = pltpu.roll(a, HEAD_DIM, 1)
        slabs.append(jnp.where(low, a, b))
    o = jnp.concatenate(slabs, axis=1)
    y_ref[...] = _mixer_tail(o, sza_ref[...], sra_ref[...], gpb_ref[...], x_ref[...], wpa_ref, wo_ref, fg_ref)


def _attn_prompt(B, S, q_hm, kaug, vsb, kwt, vwb, kct, vco, gates, sza, sra, gpb, x2d, prm):
    nq = S // TQ
    n_sel = S // SEL_BLOCK
    n_cmp_pad = kct.shape[2]
    row = lambda width: pl.BlockSpec((TQ, width), lambda b, i: (b * nq + i, 0))
    const = lambda *shape: pl.BlockSpec(shape, lambda b, i: (0,) * len(shape))
    batch = lambda *shape: pl.BlockSpec((1,) + shape, lambda b, i: (b,) + (0,) * len(shape))
    in_specs = [
        pl.BlockSpec((N_HEADS, TQ, LANES), lambda b, i: (0, b * nq + i, 0)),
        batch(2 * LANES, S),
        pl.BlockSpec((KV_HEADS, S, LANES), lambda b, i: (0, b, 0)),
        batch(LANES, S),
        pl.BlockSpec((KV_HEADS, S, LANES), lambda b, i: (0, b, 0)),
        batch(LANES, n_cmp_pad),
        batch(n_cmp_pad, 2 * LANES),
        const(WINDOW // TQ + 1, TQ, WIN_KEYS), const(HPG * TQ, TQ),
        row(LANES), row(ATTN_WIDTH), row(D_MODEL), row(D_MODEL), row(D_MODEL),
        const(ATTN_WIDTH, D_MODEL), const(D_MODEL, D_MODEL), const(1, D_MODEL),
    ]
    d = np.arange(WIN_KEYS)[None, :] - np.arange(TQ)[:, None]
    band = np.stack([np.where((d <= min(WINDOW, v * TQ)) & (d > min(WINDOW, v * TQ) - WINDOW), 0.0, NEG)
                     for v in range(WINDOW // TQ + 1)]).astype(np.float32)
    rowhot = np.tile(np.eye(TQ, dtype=np.float32), (HPG, 1))
    return pl.pallas_call(
        functools.partial(_attn_prompt_body, n_sel),
        grid=(B, nq),
        in_specs=in_specs,
        out_specs=row(D_MODEL),
        out_shape=jax.ShapeDtypeStruct((B * S, D_MODEL), F32),
        compiler_params=pltpu.CompilerParams(dimension_semantics=("arbitrary", "arbitrary"),
                                             vmem_limit_bytes=VMEM_LIMIT),
        name="attn_prompt",
    )(q_hm, kaug, vsb, kwt, vwb, kct, vco, jnp.asarray(band, BF16), jnp.asarray(rowhot, BF16), gates, sza, sra, gpb,
      x2d, prm["w_pa"], prm["w_o"], prm["final_g"])


def _attn_sample_body(n_pages, tn, pt_ref, pk_ref, pv_ref, qa_ref, kc_ref, vc_ref, ovl_ref, oh_ref, kt_ref, vt_ref,
                      ckw_ref, cvw_ref, kwn_ref, vwn_ref, kwt_ref, vwt_ref, o_ref, okw_ref, ovw_ref,
                      kbuf, vbuf, sem):
    s_id = pl.program_id(0)
    past = n_pages * PAGE_SIZE
    n_past_blk = past // SEL_BLOCK
    n_blk_pad = ovl_ref.shape[1]
    R = N_HEADS * TOK_PAD
    GR = HPG * TOK_PAD
    wc = ckw_ref.shape[2]
    t_row = _iota((R, 1), 0) & (TOK_PAD - 1)

    def slot_refs(slot):
        return (kbuf.at[slot], vbuf.at[slot]), (sem.at[2 * slot], sem.at[2 * slot + 1])

    def start(b, slot):
        bufs, sems = slot_refs(slot)
        _start_pages(pt_ref, (pk_ref, pv_ref), bufs, sems, b, n_pages)

    def finish(j, page_slot):
        (kb, vb), sems = slot_refs(page_slot)
        slot = j
        qa = qa_ref[slot]

        n_cmp_pad = kc_ref.shape[2]
        s = _dot(qa, kc_ref[slot])
        mk = _iota((R, n_cmp_pad), 1) < n_cmp_pad - 1
        s = jnp.where(mk, s, NEG)
        e = jnp.exp(s - jnp.max(s, axis=-1, keepdims=True))
        p = jnp.where(mk, e / jnp.sum(e, axis=-1, keepdims=True), 0.0).astype(BF16)
        o_cmp = _dot(p, vc_ref[slot])
        impc = _dot(p, ovl_ref[...])
        imp = jnp.concatenate(
            [impc[g * GR:g * GR + TOK_PAD] + impc[g * GR + TOK_PAD:g * GR + 2 * TOK_PAD]
             + impc[g * GR + 2 * TOK_PAD:g * GR + 3 * TOK_PAD] + impc[g * GR + 3 * TOK_PAD:g * GR + 4 * TOK_PAD]
             for g in range(KV_HEADS)], axis=0)
        nr = KV_HEADS * TOK_PAD
        blk = _iota((nr, n_blk_pad), 1)
        cur = (past + jnp.minimum(_iota((nr, n_blk_pad), 0) & (TOK_PAD - 1), tn - 1)) // SEL_BLOCK
        valid = blk <= cur
        forced = (valid & (blk > cur - N_LOCAL_BLOCKS)) | (blk < N_INIT_BLOCKS)
        score = jnp.where(forced, BIG, jnp.where(valid, imp, NEG))
        sel = _topk_mask(score, blk, n_past_blk + 1, 1)
        selneg = jnp.where(sel, 0.0, NEG)
        selneg = jnp.concatenate(
            [selneg[g * TOK_PAD:(g + 1) * TOK_PAD] for g in range(KV_HEADS) for _ in range(HPG)], axis=0)
        low = _iota((R, LANES), 1) < SEL_HALF
        qaug = []
        for hf in range(n_past_blk // SEL_HALF):
            slab = selneg[:, (hf // 2) * LANES:(hf // 2 + 1) * LANES]
            if hf % 2:
                slab = pltpu.roll(slab, SEL_HALF, 1)
            qaug.append(jnp.concatenate([qa, jnp.where(low, slab, 0.0).astype(BF16)], axis=1))

        _wait_pages((pk_ref, pv_ref), (kb, vb), sems, n_pages)

        pages_per_half = SEL_HALF * SEL_BLOCK // PAGE_SIZE

        def half(buf, hf):
            return jnp.concatenate(
                [buf[hf * pages_per_half + j] for j in range(pages_per_half)], axis=1).astype(BF16)

        col = _iota((R, LANES), 1)
        ss = [_dot(qaug[hf], jnp.concatenate([half(kb, hf), oh_ref[...]], axis=0)) for hf in range(len(qaug))]
        ss.append(jnp.where(col <= t_row, _nt(qa, kt_ref[slot]), NEG))
        m = functools.reduce(jnp.maximum, [jnp.max(s, axis=-1, keepdims=True) for s in ss])
        es = [jnp.exp(s - m) for s in ss]
        l = functools.reduce(jnp.add, [jnp.sum(e, axis=-1, keepdims=True) for e in es])
        acc = _dot(es[-1].astype(BF16), vt_ref[slot])
        for hf in range(len(qaug)):
            acc = acc + _nt(es[hf].astype(BF16), half(vb, hf))
        o_sel = acc / l

        sa = jnp.where(_iota((R, wc), 1) > t_row, _dot(qa, ckw_ref[slot].astype(BF16)), NEG)
        sb = jnp.where(col <= t_row, _nt(qa, kwn_ref[slot]), NEG)
        m = jnp.maximum(jnp.max(sa, axis=-1, keepdims=True), jnp.max(sb, axis=-1, keepdims=True))
        ea, eb = jnp.exp(sa - m), jnp.exp(sb - m)
        den = jnp.sum(ea, axis=-1, keepdims=True) + jnp.sum(eb, axis=-1, keepdims=True)
        o_win = (_nt(ea.astype(BF16), cvw_ref[slot].astype(BF16)) + _dot(eb.astype(BF16), vwn_ref[slot])) / den

        o_ref[slot, 0] = o_cmp
        o_ref[slot, 1] = o_sel
        o_ref[slot, 2] = o_win

        newest = _iota((LANES, LANES), 1) >= LANES - tn
        for src, new, dst in ((ckw_ref, kwt_ref, okw_ref), (cvw_ref, vwt_ref, ovw_ref)):
            shifted = pltpu.roll(src[slot], wc - tn, 1)
            dst[slot, :, 0:wc - LANES] = shifted[:, 0:wc - LANES]
            dst[slot, :, wc - LANES:wc] = jnp.where(newest, new[slot], shifted[:, wc - LANES:wc])

    _two_batches_per_step(s_id, start, finish)


def _attn_sample(page_table, pool_k, pool_v, tn, qa, kct, vc, ovl, ktail, vtail, ckw, cvw, kwn, vwn, kwt, vwt):
    Bd, n_pages = page_table.shape
    R = N_HEADS * TOK_PAD
    wc = ckw.shape[2]
    const = lambda *shape: pl.BlockSpec(shape, lambda s, pt: (0,) * len(shape))
    pair = lambda *shape: pl.BlockSpec((2,) + shape, lambda s, pt: (s,) + (0,) * len(shape))
    anyspec = pl.BlockSpec(memory_space=pl.ANY)
    half_keys = SEL_HALF * SEL_BLOCK
    onehot = np.zeros((LANES, half_keys), np.float32)
    onehot[np.arange(half_keys) // SEL_BLOCK, np.arange(half_keys)] = 1.0
    grid_spec = pltpu.PrefetchScalarGridSpec(
        num_scalar_prefetch=1,
        grid=(Bd // 2,),
        in_specs=[anyspec, anyspec, pair(R, LANES), pair(LANES, kct.shape[2]), pair(vc.shape[1], LANES),
                  const(*ovl.shape), const(LANES, half_keys), pair(LANES, LANES), pair(LANES, LANES),
                  pair(LANES, wc), pair(LANES, wc), pair(LANES, LANES), pair(LANES, LANES),
                  pair(LANES, LANES), pair(LANES, LANES)],
        out_specs=[pair(3, R, LANES), pair(LANES, wc), pair(LANES, wc)],
        scratch_shapes=[pltpu.VMEM((PAGE_SLOTS, n_pages, LANES, PAGE_SIZE), F32)] * 2
        + [pltpu.SemaphoreType.DMA((2 * PAGE_SLOTS,))],
    )
    return pl.pallas_call(
        functools.partial(_attn_sample_body, n_pages, tn),
        grid_spec=grid_spec,
        out_shape=[jax.ShapeDtypeStruct((Bd, 3, R, LANES), F32)] + [jax.ShapeDtypeStruct((Bd, LANES, wc), F32)] * 2,
        compiler_params=pltpu.CompilerParams(dimension_semantics=("arbitrary",), vmem_limit_bytes=VMEM_LIMIT),
        name="attn_sample",
    )(page_table, pool_k, pool_v, qa, kct, vc, ovl, jnp.asarray(onehot, BF16), ktail, vtail, ckw, cvw, kwn, vwn,
      kwt, vwt)


def _mixer_sample_body(o3_ref, g3_ref, sza_ref, sra_ref, gpb_ref, x_ref, wpa_ref, wo_ref, fg_ref, y_ref):
    o = g3_ref[0] * o3_ref[0] + g3_ref[1] * o3_ref[1] + g3_ref[2] * o3_ref[2]
    y_ref[...] = _mixer_tail(o, sza_ref[...], sra_ref[...], gpb_ref[...], x_ref[...], wpa_ref, wo_ref, fg_ref)


def _mixer_sample(o3, g3, sza, sra, gpb, x2d, prm):
    T = x2d.shape[0]
    full = lambda *shape: pl.BlockSpec(shape, lambda i: (0,) * len(shape))
    return pl.pallas_call(
        _mixer_sample_body,
        grid=(1,),
        in_specs=[full(3, T, ATTN_WIDTH), full(3, T, ATTN_WIDTH), full(T, ATTN_WIDTH), full(T, D_MODEL),
                  full(T, D_MODEL), full(T, D_MODEL), full(ATTN_WIDTH, D_MODEL), full(D_MODEL, D_MODEL),
                  full(1, D_MODEL)],
        out_specs=full(T, D_MODEL),
        out_shape=jax.ShapeDtypeStruct((T, D_MODEL), F32),
        compiler_params=pltpu.CompilerParams(dimension_semantics=("arbitrary",), vmem_limit_bytes=VMEM_LIMIT),
        name="mixer_sample",
    )(o3, g3, sza, sra, gpb, x2d, prm["w_pa"], prm["w_o"], prm["final_g"])


def _overlap(n_cmp, n_sel):
    cs = np.arange(n_cmp)[:, None] * CMP_STRIDE
    ss = np.arange(n_sel)[None, :] * SEL_BLOCK
    ov = np.minimum(cs + CMP_BLOCK, ss + SEL_BLOCK) - np.maximum(cs, ss)
    return np.clip(ov, 0, None).astype(np.float32) / CMP_BLOCK


def _cmp_weights(w1, b1, w2, b2):
    w1r = w1.reshape(2, CMP_STRIDE, HEAD_DIM, CMP_HIDDEN).transpose(1, 2, 0, 3)
    w1r = w1r.reshape(CMP_STRIDE, 1, HEAD_DIM, 2 * CMP_HIDDEN).astype(BF16)
    w2r = w2.reshape(1, CMP_HIDDEN, HEAD_DIM).astype(BF16)
    big = jnp.concatenate(
        [jnp.pad(w1r, ((0, 0), (0, 0), (0, 0), (g * 2 * CMP_HIDDEN, (KV_HEADS - 1 - g) * 2 * CMP_HIDDEN)))
         for g in range(KV_HEADS)], axis=1)
    w2b = jnp.concatenate(
        [jnp.pad(w2r, ((0, 0), (0, 0), (g * HEAD_DIM, (KV_HEADS - 1 - g) * HEAD_DIM))) for g in range(KV_HEADS)],
        axis=0)
    return (big.reshape(CMP_STRIDE * LANES, 4 * LANES), jnp.tile(b1, KV_HEADS)[None],
            w2b.reshape(KV_HEADS * CMP_HIDDEN, LANES), jnp.tile(b2, KV_HEADS)[None])


def _seq_minor(t):
    lead = t.shape[:-3]
    n = len(lead)
    return t.transpose(*range(n), n + 1, n + 2, n).reshape(*lead, KV_WIDTH, t.shape[-3])


def _seq_major(t):
    lead = t.shape[:-2]
    n = len(lead)
    return t.reshape(*lead, KV_HEADS, HEAD_DIM, t.shape[-1]).transpose(*range(n), n + 2, n, n + 1)


def kernel(x_prompt, x_sample, cache_k_cmp, cache_v_cmp, cache_k_sel, cache_v_sel, cache_k_win, cache_v_win, page_table, norm_g, w_in, cmp_k_w1, cmp_k_b1, cmp_k_w2, cmp_k_b2, cmp_v_w1, cmp_v_b1, cmp_v_w2, cmp_v_b2, v_norm_g, v_norm_b, w_spatial, b_spatial, w_pa, w_pb, w_o, final_g):
    B, S, _ = x_prompt.shape
    Bd, tn, _ = x_sample.shape
    depth = w_in.shape[0]
    assert depth == 1, "single-layer step"
    assert Bd * tn == CHUNK, "the sample tokens form one 128-row tile"
    n_pages = page_table.shape[1]
    past = n_pages * PAGE_SIZE

    split = OFF_GL + 3 * N_HEADS
    w = w_in[0]
    prm = {
        "norm_g": norm_g,
        "w_in": jnp.concatenate([w.T[:split], jnp.zeros((GL_PAD, D_MODEL), F32), w.T[split:]], axis=0).astype(BF16),
        "w_sp": w_spatial[0],
        "b_sp": jnp.repeat(b_spatial[0].T, LANES, axis=1),
        "w_sp_s": jnp.stack([jnp.kron(jnp.eye(CHUNK // tn, dtype=F32), w_spatial[0, g, :tn, :tn])
                             for g in range(GMLP_GROUPS)]),
        "b_sp_s": jnp.tile(jnp.repeat(b_spatial[0, :, :tn].T, LANES, axis=1), (CHUNK // tn, 1)),
        "v_norm_g": v_norm_g, "v_norm_b": v_norm_b,
        "w_pb": w_pb[0].astype(BF16), "w_pa": w_pa[0].astype(BF16), "w_o": w_o[0].astype(BF16),
        "final_g": final_g[None],
        "cmp_k": _cmp_weights(cmp_k_w1[0], cmp_k_b1[0], cmp_k_w2[0], cmp_k_b2[0]),
        "cmp_v": _cmp_weights(cmp_v_w1[0], cmp_v_b1[0], cmp_v_w2[0], cmp_v_b2[0]),
    }

    xp = x_prompt.reshape(B * S, D_MODEL)
    (q_hm, p_kcmp, p_vcmp, p_ksel, p_vsel, p_kwin, p_vwin, kcr, vcr, kaug, kwt, vsb, vwb,
     gates, sza, sra, gpb) = _in_project(xp, _rope_tables(np.arange(S)), prm, TM_PROMPT, False, S)
    n_half = S // CMP_STRIDE
    n_cmp = (S - CMP_BLOCK) // CMP_STRIDE + 1
    n_sel = S // SEL_BLOCK
    assert n_sel <= SEL_HALF
    ovl = np.zeros((n_half, LANES), np.float32)
    ovl[:n_cmp, HEAD_DIM:HEAD_DIM + n_sel] = _overlap(n_cmp, n_sel)
    cend_tables = _rope_tables(np.arange(n_half) * CMP_STRIDE + CMP_BLOCK - 1)
    kct, vco = _compress_prompt(kcr.reshape(B, S, LANES), vcr.reshape(B, S, LANES), prm, cend_tables,
                                jnp.asarray(ovl, BF16))
    y_prompt = _attn_prompt(B, S, q_hm, kaug, vsb, kwt, vwb, kct, vco, gates, sza, sra, gpb, xp, prm)

    xs = x_sample.reshape(Bd * tn, D_MODEL)
    pos_s = np.tile(past + np.arange(tn), Bd)
    (q_s, s_kcmp, s_vcmp, s_ksel, s_vsel, s_kwin, s_vwin, gates_s, sza_s, sra_s, gpb_s, vn_s) = _in_project(
        xs, _rope_tables(pos_s), prm, Bd * tn, True, tn)
    n_half_s = past // CMP_STRIDE
    pools = [_seq_minor(c[0]) for c in (cache_k_cmp, cache_v_cmp, cache_k_sel, cache_v_sel)]
    cend_s = _rope_tables(np.arange(n_half_s) * CMP_STRIDE + CMP_BLOCK - 1)
    kct_s, vc_s = _compress_sample(page_table, pools[0], pools[1], prm, cend_s)

    n_cmp_s = (past + tn - CMP_BLOCK) // CMP_STRIDE + 1
    n_blk_s = past // SEL_BLOCK + -(-tn // SEL_BLOCK)
    assert n_cmp_s == n_half_s - 1 and tn <= TOK_PAD and (past // SEL_BLOCK) % SEL_HALF == 0
    ovl_s = np.zeros((n_half_s, 2 * LANES), np.float32)
    ovl_s[:n_cmp_s, :n_blk_s] = _overlap(n_cmp_s, n_blk_s)
    q5 = q_s.reshape(Bd, tn, KV_HEADS, HPG, HEAD_DIM).transpose(0, 2, 3, 1, 4)
    q5 = jnp.pad(q5, ((0, 0), (0, 0), (0, 0), (0, TOK_PAD - tn), (0, 0))).reshape(Bd, KV_HEADS, HPG * TOK_PAD, HEAD_DIM)
    qa = jnp.concatenate(
        [jnp.pad(q5[:, g], ((0, 0), (0, 0), (g * HEAD_DIM, (KV_HEADS - 1 - g) * HEAD_DIM))) for g in range(KV_HEADS)],
        axis=1).astype(BF16)
    pad_rows = lambda t: jnp.pad(t.reshape(Bd, tn, LANES), ((0, 0), (0, LANES - tn), (0, 0))).astype(BF16)
    ckw = _seq_minor(cache_k_win[0])
    cvw = _seq_minor(cache_v_win[0])
    new_t = lambda t: jnp.pad(t.reshape(Bd, tn, LANES).transpose(0, 2, 1), ((0, 0), (0, 0), (LANES - tn, 0)))
    o3, s_k_win, s_v_win = _attn_sample(
        page_table, pools[2], pools[3], tn, qa, kct_s, vc_s, jnp.asarray(ovl_s, BF16),
        pad_rows(s_ksel), pad_rows(s_vsel), ckw, cvw, pad_rows(s_kwin), pad_rows(s_vwin), new_t(s_kwin), new_t(s_vwin))
    s_k_win = _seq_major(s_k_win)[None]
    s_v_win = _seq_major(s_v_win)[None]
    first_group = jnp.arange(N_HEADS * TOK_PAD)[:, None] < HPG * TOK_PAD
    o3r = jnp.where(first_group, o3[..., :HEAD_DIM], o3[..., HEAD_DIM:])
    o3r = o3r.reshape(Bd, 3, N_HEADS, TOK_PAD, HEAD_DIM)[:, :, :, :tn]
    o3r = o3r.transpose(1, 0, 3, 2, 4).reshape(3, Bd * tn, ATTN_WIDTH)
    g3 = gates_s[:, :3 * N_HEADS].reshape(Bd * tn, N_HEADS, 3).transpose(2, 0, 1)
    g3 = jnp.repeat(g3, HEAD_DIM, axis=2)
    y_sample = _mixer_sample(o3r, g3, sza_s, sra_s, gpb_s, xs, prm)

    kv5 = lambda t, b, n: t.reshape(1, b, n, KV_HEADS, HEAD_DIM)
    pw = min(WINDOW, S)
    return (y_prompt.reshape(B, S, D_MODEL), y_sample.reshape(Bd, tn, D_MODEL),
            _seq_major(p_kcmp)[None], _seq_major(p_vcmp)[None], _seq_major(p_ksel)[None], _seq_major(p_vsel)[None],
            _seq_major(p_kwin[:, :, S - pw:])[None], _seq_major(p_vwin[:, :, S - pw:])[None],
            kv5(s_kcmp, Bd, tn), kv5(s_vcmp, Bd, tn), kv5(s_ksel, Bd, tn), kv5(s_vsel, Bd, tn),
            s_k_win, s_v_win, vn_s.reshape(1, Bd, tn, GMLP_WIDTH))
```

```python
import functools

import numpy as np
import jax
import jax.numpy as jnp
from jax import lax
from jax.experimental import pallas as pl
from jax.experimental.pallas import tpu as pltpu

F32 = jnp.float32
BF16 = jnp.bfloat16

D_MODEL = 1024
HEAD_DIM = 64
N_HEADS = 8
KV_HEADS = 2
HPG = N_HEADS // KV_HEADS
ATTN_WIDTH = N_HEADS * HEAD_DIM
KV_WIDTH = KV_HEADS * HEAD_DIM
ROT_DIM = HEAD_DIM // 4
ROT_HALF = ROT_DIM // 2
ROPE_THETA = 500000.0
CMP_BLOCK = 32
CMP_STRIDE = 16
CMP_HIDDEN = 128
SEL_BLOCK = 64
TOP_N = 16
N_INIT_BLOCKS = 1
N_LOCAL_BLOCKS = 2
WINDOW = 512
CHUNK = 128
GMLP_GROUPS = 4
GMLP_WIDTH = 512
PAGE_SIZE = 128
NEG = -1e30
BIG = 1e30
EPS = 1e-6
LOG2E = 1.4426950408889634

LANES = 128
GL_PAD = LANES - 3 * N_HEADS

OFF_Q = 0
OFF_KV = ATTN_WIDTH
OFF_GL = OFF_KV + 6 * KV_WIDTH
OFF_ZA = OFF_GL + LANES
OFF_U = OFF_ZA + ATTN_WIDTH
OFF_V = OFF_U + GMLP_WIDTH
OFF_ZB = OFF_V + GMLP_WIDTH
OFF_RA = OFF_ZB + GMLP_WIDTH
OFF_RB = OFF_RA + D_MODEL
W_TOT = OFF_RB + D_MODEL

VMEM_LIMIT = 56 * 1024 * 1024

TM_PROMPT = 512
TQ = 256
HU = 2
KT = 512
WIN_KEYS = WINDOW + TQ
SEL_HALF = 64
TOK_PAD = 8
RANK_GROUP = 8


def _nt(a, b):
    return lax.dot_general(a, b, (((1,), (1,)), ((), ())), preferred_element_type=F32)


def _dot(a, b):
    return jnp.dot(a, b, preferred_element_type=F32)


def _iota(shape, dim):
    return lax.broadcasted_iota(jnp.int32, shape, dim)


def _rope(slab, cos, s1, s2):
    return slab * cos + pltpu.roll(slab, LANES - ROT_HALF, 1) * s1 + pltpu.roll(slab, ROT_HALF, 1) * s2


def _rope_tables(pos):
    pos = np.asarray(pos, np.float64)
    n = pos.shape[0]
    inv = np.power(np.float64(ROPE_THETA), -np.arange(0, ROT_DIM, 2, dtype=np.float64) / ROT_DIM)
    ang = pos[:, None] * inv[None, :]
    cos, sin = np.cos(ang), np.sin(ang)
    rest = HEAD_DIM - ROT_DIM
    c = np.concatenate([cos, cos, np.ones((n, rest))], axis=1)
    s1 = np.concatenate([-sin, np.zeros((n, HEAD_DIM - ROT_HALF))], axis=1)
    s2 = np.concatenate([np.zeros((n, ROT_HALF)), sin, np.zeros((n, rest))], axis=1)
    return tuple(jnp.asarray(np.tile(t, (1, LANES // HEAD_DIM)), F32) for t in (c, s1, s2))


def _sigmoid(x):
    return 1.0 / (1.0 + jnp.exp(-x))


def _gelu(x):
    return jax.nn.gelu(x, approximate=True)


def _inproj_body(tm, sample, pos_tiles, x_ref, ng_ref, w_ref, cos_ref, s1_ref, s2_ref, wsp_ref, bsp_ref,
                 vng_ref, vnb_ref, wpb_ref, *outs):
    if sample:
        (q_ref, kcmp_ref, vcmp_ref, ksel_ref, vsel_ref, kwin_ref, vwin_ref,
         gat_ref, sza_ref, sra_ref, gpb_ref, vn_ref) = outs
    else:
        (q_ref, kcmp_ref, vcmp_ref, ksel_ref, vsel_ref, kwin_ref, vwin_ref,
         kcr_ref, vcr_ref, kaug_ref, kwt_ref, vsb_ref, vwb_ref, gat_ref, sza_ref, sra_ref, gpb_ref) = outs

    x = x_ref[...]
    r = lax.rsqrt(jnp.mean(x * x, axis=-1, keepdims=True) + EPS)
    h = ((x * r) * ng_ref[...]).astype(BF16)

    def proj(lo, hi):
        return _nt(h, w_ref[lo:hi, :])

    cos, s1, s2 = cos_ref[...], s1_ref[...], s2_ref[...]
    low = _iota((tm, LANES), 1) < HEAD_DIM

    q = proj(OFF_Q, OFF_Q + ATTN_WIDTH)
    q_scale = HEAD_DIM ** -0.5 if sample else HEAD_DIM ** -0.5 * LOG2E
    for pp in range(N_HEADS // 2):
        slab = _rope(q[:, pp * LANES:(pp + 1) * LANES], cos, s1, s2) * q_scale
        if sample:
            q_ref[:, pp * LANES:(pp + 1) * LANES] = slab
        elif (2 * pp) // HPG == 0:
            q_ref[2 * pp] = jnp.where(low, slab, 0.0).astype(BF16)
            q_ref[2 * pp + 1] = jnp.where(low, pltpu.roll(slab, HEAD_DIM, 1), 0.0).astype(BF16)
        else:
            q_ref[2 * pp] = jnp.where(low, 0.0, pltpu.roll(slab, HEAD_DIM, 1)).astype(BF16)
            q_ref[2 * pp + 1] = jnp.where(low, 0.0, slab).astype(BF16)

    kv = proj(OFF_KV, OFF_KV + 6 * KV_WIDTH)
    kcmp = kv[:, 0:LANES]
    vcmp = kv[:, LANES:2 * LANES]
    ksel = _rope(kv[:, 2 * LANES:3 * LANES], cos, s1, s2)
    vsel = kv[:, 3 * LANES:4 * LANES]
    kwin = _rope(kv[:, 4 * LANES:5 * LANES], cos, s1, s2)
    vwin = kv[:, 5 * LANES:6 * LANES]
    if sample:
        for ref, val in ((kcmp_ref, kcmp), (vcmp_ref, vcmp), (ksel_ref, ksel), (vsel_ref, vsel),
                         (kwin_ref, kwin), (vwin_ref, vwin)):
            ref[...] = val
    else:
        ksel_t = ksel.T
        kwin_t = kwin.T
        for ref, val in ((kcmp_ref, kcmp.T), (vcmp_ref, vcmp.T), (ksel_ref, ksel_t), (vsel_ref, vsel.T),
                         (kwin_ref, kwin_t), (vwin_ref, vwin.T)):
            ref[0] = val
        kcr_ref[...] = kcmp
        vcr_ref[...] = vcmp
        base = (pl.program_id(0) % pos_tiles) * tm
        blk = (base + _iota((LANES, tm), 1)) // SEL_BLOCK
        onehot = jnp.where(_iota((LANES, tm), 0) == blk, 1.0, 0.0)
        kaug_ref[0, 0:LANES] = ksel_t.astype(BF16)
        kaug_ref[0, LANES:2 * LANES] = onehot.astype(BF16)
        kwt_ref[0] = kwin_t.astype(BF16)
        for ref, val in ((vsb_ref, vsel), (vwb_ref, vwin)):
            ref[0] = jnp.where(low, val, 1.0).astype(BF16)
            ref[1] = jnp.where(low, 1.0, val).astype(BF16)

    gat_ref[...] = _sigmoid(proj(OFF_GL, OFF_GL + LANES))
    za = proj(OFF_ZA, OFF_ZA + ATTN_WIDTH)
    sza_ref[...] = za * _sigmoid(za)
    sra_ref[...] = _sigmoid(proj(OFF_RA, OFF_RA + D_MODEL))

    v = proj(OFF_V, OFF_V + GMLP_WIDTH)
    gv = _gelu(v)
    mu = jnp.mean(gv, axis=-1, keepdims=True)
    var = jnp.mean(jnp.square(gv - mu), axis=-1, keepdims=True)
    vn = ((gv - mu) * lax.rsqrt(var + EPS)) * vng_ref[...] + vnb_ref[...]
    if sample:
        vn_ref[...] = vn
    vnb16 = vn.astype(BF16)
    n_chunk = tm // CHUNK
    tri = _iota((CHUNK, CHUNK), 0) >= _iota((CHUNK, CHUNK), 1)
    mixed = []
    for g in range(GMLP_GROUPS):
        wm = jnp.where(tri, wsp_ref[g], 0.0).astype(BF16)
        cat = jnp.concatenate(
            [vnb16[c * CHUNK:(c + 1) * CHUNK, g * LANES:(g + 1) * LANES] for c in range(n_chunk)], axis=1)
        mixed.append(_dot(wm, cat))
    bsp = bsp_ref[...]
    sg = jnp.concatenate(
        [jnp.concatenate([mixed[g][:, c * LANES:(c + 1) * LANES] for g in range(GMLP_GROUPS)], axis=1) + bsp
         for c in range(n_chunk)], axis=0)
    u = proj(OFF_U, OFF_U + GMLP_WIDTH)
    zb = proj(OFF_ZB, OFF_ZB + GMLP_WIDTH)
    t = (_gelu(u) * sg) * (zb * _sigmoid(zb))
    pb = _dot(t.astype(BF16), wpb_ref[...])
    gpb_ref[...] = _sigmoid(proj(OFF_RB, OFF_RB + D_MODEL)) * pb


def _in_project(x2d, tables, prm, tm, sample, seq):
    T = x2d.shape[0]
    nt = T // tm
    pos_tiles = tables[0].shape[0] // tm
    const = lambda *shape: pl.BlockSpec(shape, lambda i: (0,) * len(shape))
    row = lambda width: pl.BlockSpec((tm, width), lambda i: (i, 0))
    tab = pl.BlockSpec((tm, LANES), lambda i: (i % pos_tiles, 0))
    in_specs = [
        row(D_MODEL), const(1, D_MODEL),
        pl.BlockSpec((W_TOT, D_MODEL), lambda i: (0, 0), pipeline_mode=pl.Buffered(1)),
        tab, tab, tab,
        const(GMLP_GROUPS, CHUNK, CHUNK), const(CHUNK, GMLP_WIDTH), const(1, GMLP_WIDTH), const(1, GMLP_WIDTH),
        const(GMLP_WIDTH, D_MODEL),
    ]
    f32rows = lambda width: jax.ShapeDtypeStruct((T, width), F32)
    tail_shapes = [f32rows(LANES), f32rows(ATTN_WIDTH), f32rows(D_MODEL), f32rows(D_MODEL)]
    tail_specs = [row(LANES), row(ATTN_WIDTH), row(D_MODEL), row(D_MODEL)]
    if sample:
        out_shape = [f32rows(ATTN_WIDTH)] + [f32rows(KV_WIDTH)] * 6 + tail_shapes + [f32rows(GMLP_WIDTH)]
        out_specs = [row(ATTN_WIDTH)] + [row(KV_WIDTH)] * 6 + tail_specs + [row(GMLP_WIDTH)]
    else:
        nb = T // seq
        bf = lambda *shape: jax.ShapeDtypeStruct(shape, BF16)
        tposed = lambda rows: pl.BlockSpec((1, rows, tm), lambda i: (i // pos_tiles, 0, i % pos_tiles))
        out_shape = ([bf(N_HEADS, T, LANES)] + [jax.ShapeDtypeStruct((nb, KV_WIDTH, seq), F32)] * 6
                     + [f32rows(KV_WIDTH)] * 2
                     + [bf(nb, 2 * LANES, seq), bf(nb, LANES, seq), bf(KV_HEADS, T, LANES), bf(KV_HEADS, T, LANES)]
                     + tail_shapes)
        heads = lambda n: pl.BlockSpec((n, tm, LANES), lambda i: (0, i, 0))
        out_specs = ([heads(N_HEADS)] + [tposed(KV_WIDTH)] * 6 + [row(KV_WIDTH)] * 2
                     + [tposed(2 * LANES), tposed(LANES), heads(KV_HEADS), heads(KV_HEADS)] + tail_specs)
    return pl.pallas_call(
        functools.partial(_inproj_body, tm, sample, pos_tiles),
        grid=(nt,),
        in_specs=in_specs,
        out_specs=out_specs,
        out_shape=out_shape,
        compiler_params=pltpu.CompilerParams(dimension_semantics=("arbitrary",), vmem_limit_bytes=VMEM_LIMIT),
        name="in_project_sample" if sample else "in_project_prompt",
    )(x2d, prm["norm_g"], prm["w_in"], *tables, prm["w_sp_s" if sample else "w_sp"],
      prm["b_sp_s" if sample else "b_sp"], prm["v_norm_g"], prm["v_norm_b"], prm["w_pb"])


def _strided_halfblocks(load_rows):
    return jnp.concatenate([load_rows(t) for t in range(CMP_STRIDE)], axis=1).astype(BF16)


def _compress_mlp(xcat, n, w1_ref, b1_ref, w2_ref, b2_ref):
    hh = _dot(xcat, w1_ref[...])
    hid = jnp.concatenate(
        [hh[:, 0:LANES] + pltpu.roll(hh[:, LANES:2 * LANES], n - 1, 0),
         hh[:, 2 * LANES:3 * LANES] + pltpu.roll(hh[:, 3 * LANES:4 * LANES], n - 1, 0)], axis=1) + b1_ref[...]
    return _dot(_gelu(hid).astype(BF16), w2_ref[...]) + b2_ref[...]


def _compress_prompt_body(n, kr_ref, vr_ref, w1k, b1k, w2k, b2k, w1v, b1v, w2v, b2v, cos_ref, s1_ref, s2_ref,
                          ovl_ref, kct_ref, vco_ref):
    kc = _compress_mlp(_strided_halfblocks(lambda t: kr_ref[0, pl.ds(t, n, stride=CMP_STRIDE), :]),
                       n, w1k, b1k, w2k, b2k)
    kct_ref[0] = _rope(kc, cos_ref[...], s1_ref[...], s2_ref[...]).T.astype(BF16)
    vc = _compress_mlp(_strided_halfblocks(lambda t: vr_ref[0, pl.ds(t, n, stride=CMP_STRIDE), :]),
                       n, w1v, b1v, w2v, b2v)
    vco_ref[0, :, 0:LANES] = vc.astype(BF16)
    vco_ref[0, :, LANES:2 * LANES] = ovl_ref[...]


def _compress_prompt(kr, vr, prm, tables, ovl):
    B, S, _ = kr.shape
    n = S // CMP_STRIDE
    const = lambda *shape: pl.BlockSpec(shape, lambda b: (0,) * len(shape))
    rows = pl.BlockSpec((1, S, LANES), lambda b: (b, 0, 0))
    wspecs = [const(CMP_STRIDE * LANES, 4 * LANES), const(1, 2 * LANES), const(2 * LANES, LANES), const(1, LANES)]
    return pl.pallas_call(
        functools.partial(_compress_prompt_body, n),
        grid=(B,),
        in_specs=[rows, rows] + wspecs + wspecs + [const(n, LANES)] * 3 + [const(n, LANES)],
        out_specs=[pl.BlockSpec((1, LANES, n), lambda b: (b, 0, 0)),
                   pl.BlockSpec((1, n, 2 * LANES), lambda b: (b, 0, 0))],
        out_shape=[jax.ShapeDtypeStruct((B, LANES, n), BF16),
                   jax.ShapeDtypeStruct((B, n, 2 * LANES), BF16)],
        compiler_params=pltpu.CompilerParams(dimension_semantics=("arbitrary",), vmem_limit_bytes=VMEM_LIMIT),
        name="compress_prompt",
    )(kr, vr, *prm["cmp_k"], *prm["cmp_v"], *tables, ovl)


def _start_pages(pt_ref, pools, bufs, sems, b, n_pages):
    def body(p, carry):
        page = pt_ref[b, p]
        for i, (pool, buf, sem) in enumerate(zip(pools, bufs, sems)):
            pltpu.make_async_copy(pool.at[page], buf.at[p], sem).start(priority=i % 2)
        return carry
    lax.fori_loop(0, n_pages, body, 0)


def _wait_pages(pools, bufs, sems, n_pages):
    for pool, buf, sem in zip(pools, bufs, sems):
        pltpu.make_async_copy(pool.at[pl.ds(0, n_pages)], buf, sem).wait()


def _compress_sample_body(n, n_pages, pt_ref, pk_ref, pv_ref, w1k, b1k, w2k, b2k, w1v, b1v, w2v, b2v,
                          cos_ref, s1_ref, s2_ref, perm_ref, kct_ref, vc_ref, kb0, kb1, vb0, vb1, xk, xv, sem):
    s = pl.program_id(0)
    bufs = ((kb0, vb0, 0), (kb1, vb1, 1))

    def start(b, slot):
        kb, vb, si = bufs[slot]
        _start_pages(pt_ref, (pk_ref, pv_ref), (kb, vb), (sem.at[2 * si], sem.at[2 * si + 1]), b, n_pages)

    hb_per_page = PAGE_SIZE // CMP_STRIDE

    def halfblocks(buf, xcat):
        pairs =jnp.concatenate([buf[pl.ds(0, n_pages // 2, stride=2)], buf[pl.ds(1, n_pages // 2, stride=2)]], axis=2)
        y = _dot(pairs.reshape(n_pages // 2 * LANES, 2 * PAGE_SIZE).astype(BF16), perm_ref[...])
        for p in range(n_pages):
            z = y[p // 2 * LANES:(p // 2 + 1) * LANES, p % 2 * PAGE_SIZE:(p % 2 + 1) * PAGE_SIZE].T
            for t in range(CMP_STRIDE):
                xcat[p * hb_per_page:(p + 1) * hb_per_page, t * LANES:(t + 1) * LANES] = (
                    z[t * hb_per_page:(t + 1) * hb_per_page])

    def finish(slot):
        kb, vb, si = bufs[slot]
        _wait_pages((pk_ref, pv_ref), (kb, vb), (sem.at[2 * si], sem.at[2 * si + 1]), n_pages)
        halfblocks(kb, xk)
        halfblocks(vb, xv)
        kc = _compress_mlp(xk[...].astype(BF16), n, w1k, b1k, w2k, b2k)
        kct_ref[slot] = _rope(kc, cos_ref[...], s1_ref[...], s2_ref[...]).T.astype(BF16)
        vc_ref[slot] = _compress_mlp(xv[...].astype(BF16), n, w1v, b1v, w2v, b2v).astype(BF16)

    @pl.when(s == 0)
    def _():
        start(0, 0)

    start(2 * s + 1, 1)
    finish(0)

    @pl.when(s + 1 < pl.num_programs(0))
    def _():
        start(2 * s + 2, 0)

    finish(1)


def _compress_sample(page_table, pool_k, pool_v, prm, tables):
    Bd, n_pages = page_table.shape
    past = n_pages * PAGE_SIZE
    n = past // CMP_STRIDE
    const = lambda *shape: pl.BlockSpec(shape, lambda s, pt: (0,) * len(shape))
    wspecs = [const(CMP_STRIDE * LANES, 4 * LANES), const(1, 2 * LANES), const(2 * LANES, LANES), const(1, LANES)]
    anyspec = pl.BlockSpec(memory_space=pl.ANY)
    grid_spec = pltpu.PrefetchScalarGridSpec(
        num_scalar_prefetch=1,
        grid=(Bd // 2,),
        in_specs=[anyspec, anyspec] + wspecs + wspecs + [const(n, LANES)] * 3 + [const(2 * PAGE_SIZE, 2 * PAGE_SIZE)],
        out_specs=[pl.BlockSpec((2, LANES, n), lambda s, pt: (s, 0, 0)),
                   pl.BlockSpec((2, n, LANES), lambda s, pt: (s, 0, 0))],
        scratch_shapes=[pltpu.VMEM((n_pages, LANES, PAGE_SIZE), F32)] * 4
        + [pltpu.VMEM((n, CMP_STRIDE * LANES), F32)] * 2 + [pltpu.SemaphoreType.DMA((4,))],
    )
    hb = PAGE_SIZE // CMP_STRIDE
    perm = np.zeros((2 * PAGE_SIZE, 2 * PAGE_SIZE), np.float32)
    for side in range(2):
        for jj in range(hb):
            for t in range(CMP_STRIDE):
                perm[side * PAGE_SIZE + CMP_STRIDE * jj + t, side * PAGE_SIZE + t * hb + jj] = 1.0
    return pl.pallas_call(
        functools.partial(_compress_sample_body, n, n_pages),
        grid_spec=grid_spec,
        out_shape=[jax.ShapeDtypeStruct((Bd, LANES, n), BF16), jax.ShapeDtypeStruct((Bd, n, LANES), BF16)],
        compiler_params=pltpu.CompilerParams(dimension_semantics=("arbitrary",), vmem_limit_bytes=VMEM_LIMIT),
        name="compress_sample",
    )(page_table, pool_k, pool_v, *prm["cmp_k"], *prm["cmp_v"], *tables, jnp.asarray(perm, BF16))


def _mixer_tail(o, sza, sra, gpb, x, wpa_ref, wo_ref, fg_ref):
    pa = _dot((o * sza).astype(BF16), wpa_ref[...])
    merged = sra * pa + gpb
    hn = x + _dot(merged.astype(BF16), wo_ref[...])
    r = lax.rsqrt(jnp.mean(hn * hn, axis=-1, keepdims=True) + EPS)
    return (hn * r) * fg_ref[...]


def _topk_mask(score, blk, n_blocks, axis):
    cnt = jnp.zeros(score.shape, jnp.int32)
    for sp in range(n_blocks):
        b = lax.slice_in_dim(score, sp, sp + 1, axis=axis)
        ge = jnp.where(b >= score, 1, 0)
        gt = jnp.where(b > score, 1, 0)
        cnt = cnt + jnp.where(blk > sp, ge, gt)
    return cnt < TOP_N


def _attn_prompt_body(n_sel, q_ref, ka_ref, vs_ref, kw_ref, vw_ref, kc_ref, vco_ref, band_ref, rowhot_ref, gat_ref,
                      sza_ref, sra_ref, gpb_ref, x_ref, wpa_ref, wo_ref, fg_ref, y_ref):
    i = pl.program_id(1)
    M = HU * TQ
    G = range(KV_HEADS)
    units = [(g, u) for g in G for u in range(HPG // HU)]
    qpos = i * TQ + (_iota((M, 1), 0) & (TQ - 1))
    n_cmp_pad = kc_ref.shape[2]
    gt = gat_ref[...]
    qs = [q_ref[HPG * g + HU * u:HPG * g + HU * (u + 1)].reshape(M, LANES) for g, u in units]

    mk = (_iota((M, n_cmp_pad), 1) * CMP_STRIDE + (CMP_BLOCK - 1)) <= qpos
    o_cmp = []
    imps = [None] * KV_HEADS
    for n, (g, u) in enumerate(units):
        s = jnp.where(mk, _dot(qs[n], kc_ref[0]), NEG)
        e = jnp.exp2(s - jnp.max(s, axis=-1, keepdims=True))
        p = jnp.where(mk, e / jnp.sum(e, axis=-1, keepdims=True), 0.0)
        r = _dot(p.astype(BF16), vco_ref[0])
        o_cmp.append(r[:, 0:LANES])
        impc = r[:, LANES:2 * LANES]
        for j in range(HU):
            part = impc[j * TQ:(j + 1) * TQ]
            imps[g] = part if imps[g] is None else imps[g] + part
    blk = _iota((n_sel, TQ), 0)
    cur = (i * TQ + _iota((n_sel, TQ), 1)) // SEL_BLOCK
    valid = blk <= cur
    forced = (valid & (blk > cur - N_LOCAL_BLOCKS)) | (blk < N_INIT_BLOCKS)
    scores = [jnp.where(forced, BIG, jnp.where(valid, imps[g].T[HEAD_DIM:HEAD_DIM + n_sel], NEG)) for g in G]

    last_blk = ((i + 1) * TQ - 1) // SEL_BLOCK

    def count_group(k, cnts):
        out = []
        for g in G:
            c = cnts[g]
            for sp in range(k * RANK_GROUP, (k + 1) * RANK_GROUP):
                b = scores[g][sp:sp + 1]
                c = c + jnp.where(blk > sp, jnp.where(b >= scores[g], 1, 0), jnp.where(b > scores[g], 1, 0))
            out.append(c)
        return tuple(out)

    cnts = (jnp.zeros((n_sel, TQ), jnp.int32),) * KV_HEADS
    for k in range(n_sel // RANK_GROUP):
        cnts = lax.cond(k * RANK_GROUP <= last_blk, functools.partial(count_group, k), lambda c: c, cnts)
    seln = []
    for g in G:
        selneg = jnp.where(cnts[g] < TOP_N, 0.0, NEG)
        seln_t = jnp.concatenate([selneg, jnp.zeros((LANES - n_sel, TQ), F32)], axis=0).T.astype(BF16)
        seln.append(jnp.concatenate([seln_t] * HU, axis=0))
    qa = [jnp.concatenate([qs[n], seln[g]], axis=1) for n, (g, u) in enumerate(units)]

    def update(carry, s, v):
        m, acc = carry
        mn = jnp.maximum(m, jnp.max(s, axis=-1, keepdims=True))
        acc = jnp.exp2(m - mn) * acc + _dot(jnp.exp2((s - mn).astype(BF16)), v)
        return mn, acc

    def step(kt, carries, causal):
        off = pl.multiple_of(kt * KT, KT)
        k = ka_ref[0, :, pl.ds(off, KT)]
        ss = [_dot(q, k) for q in qa]
        if causal:
            keep = kt * KT + _iota((M, KT), 1) <= qpos
            ss = [jnp.where(keep, s, NEG) for s in ss]
        return tuple(update(carries[n], ss[n], vs_ref[g, pl.ds(off, KT), :]) for n, (g, u) in enumerate(units))

    n_full = i // (KT // TQ)
    init = (jnp.full((M, 1), NEG, F32), jnp.zeros((M, LANES), F32))
    carries = lax.fori_loop(0, n_full, lambda kt, c: step(kt, c, False), (init,) * len(units))
    carries = step(n_full, carries, True)
    sum_lane = [(KV_HEADS - 1 - g) * HEAD_DIM for g in G]
    o_sel = [carries[n][1] / carries[n][1][:, sum_lane[g]:sum_lane[g] + 1] for n, (g, u) in enumerate(units)]

    st = pl.multiple_of(jnp.maximum(i * TQ - WINDOW, 0), TQ)
    kw = jnp.concatenate([kw_ref[0, :, pl.ds(st, WIN_KEYS)], band_ref[jnp.minimum(i, WINDOW // TQ)]], axis=0)
    o_units = []
    for n, (g, u) in enumerate(units):
        s = _dot(jnp.concatenate([qs[n], rowhot_ref[...]], axis=1), kw)
        e = jnp.exp2(s - jnp.max(s, axis=-1, keepdims=True))
        r = _dot(e.astype(BF16), vw_ref[g, pl.ds(st, WIN_KEYS), :])
        o_win = r / r[:, sum_lane[g]:sum_lane[g] + 1]

        def gcol(c, g=g, u=u):
            return jnp.concatenate(
                [gt[:, 3 * (HPG * g + HU * u + j) + c:3 * (HPG * g + HU * u + j) + c + 1] for j in range(HU)], axis=0)

        o_units.append(gcol(0) * o_cmp[n] + gcol(1) * o_sel[n] + gcol(2) * o_win)

    low = _iota((TQ, LANES), 1) < HEAD_DIM
    slabs = []
    for pp in range(N_HEADS // 2):
        h0 = 2 * pp
        g, n, j0 = h0 // HPG, h0 // HU, h0 % HU
        a = o_units[n][j0 * TQ:(j0 + 1) * TQ]
        b = o_units[n][(j0 + 1) * TQ:(j0 + 2) * TQ]
        if g == 0:
            b = pltpu.roll(b, HEAD_DIM, 1)
        else:
            a = pltpu.roll(a, HEAD_DIM, 1)
        slabs.append(jnp.where(low, a, b))
    o = jnp.concatenate(slabs, axis=1)
    y_ref[...] = _mixer_tail(o, sza_ref[...], sra_ref[...], gpb_ref[...], x_ref[...], wpa_ref, wo_ref, fg_ref)


def _attn_prompt(B, S, q_hm, kaug, vsb, kwt, vwb, kct, vco, gates, sza, sra, gpb, x2d, prm):
    nq = S // TQ
    n_sel = S // SEL_BLOCK
    n_cmp_pad = kct.shape[2]
    row = lambda width: pl.BlockSpec((TQ, width), lambda b, i: (b * nq + i, 0))
    const = lambda *shape: pl.BlockSpec(shape, lambda b, i: (0,) * len(shape))
    batch = lambda *shape: pl.BlockSpec((1,) + shape, lambda b, i: (b,) + (0,) * len(shape))
    in_specs = [
        pl.BlockSpec((N_HEADS, TQ, LANES), lambda b, i: (0, b * nq + i, 0)),
        batch(2 * LANES, S),
        pl.BlockSpec((KV_HEADS, S, LANES), lambda b, i: (0, b, 0)),
        batch(LANES, S),
        pl.BlockSpec((KV_HEADS, S, LANES), lambda b, i: (0, b, 0)),
        batch(LANES, n_cmp_pad),
        batch(n_cmp_pad, 2 * LANES),
        const(WINDOW // TQ + 1, TQ, WIN_KEYS), const(HU * TQ, TQ),
        row(LANES), row(ATTN_WIDTH), row(D_MODEL), row(D_MODEL), row(D_MODEL),
        const(ATTN_WIDTH, D_MODEL), const(D_MODEL, D_MODEL), const(1, D_MODEL),
    ]
    d = np.arange(WIN_KEYS)[None, :] - np.arange(TQ)[:, None]
    band = np.stack([np.where((d <= min(WINDOW, v * TQ)) & (d > min(WINDOW, v * TQ) - WINDOW), 0.0, NEG)
                     for v in range(WINDOW // TQ + 1)]).astype(np.float32)
    rowhot = np.tile(np.eye(TQ, dtype=np.float32), (HU, 1))
    return pl.pallas_call(
        functools.partial(_attn_prompt_body, n_sel),
        grid=(B, nq),
        in_specs=in_specs,
        out_specs=row(D_MODEL),
        out_shape=jax.ShapeDtypeStruct((B * S, D_MODEL), F32),
        compiler_params=pltpu.CompilerParams(dimension_semantics=("arbitrary", "arbitrary"),
                                             vmem_limit_bytes=VMEM_LIMIT),
        name="attn_prompt",
    )(q_hm, kaug, vsb, kwt, vwb, kct, vco, jnp.asarray(band, BF16), jnp.asarray(rowhot, BF16), gates, sza, sra, gpb,
      x2d, prm["w_pa"], prm["w_o"], prm["final_g"])


def _attn_sample_body(n_pages, tn, pt_ref, pk_ref, pv_ref, qa_ref, kc_ref, vc_ref, ovl_ref, oh_ref, kt_ref, vt_ref,
                      ckw_ref, cvw_ref, kwn_ref, vwn_ref, kwt_ref, vwt_ref, o_ref, okw_ref, ovw_ref,
                      kb0, kb1, vb0, vb1, sem):
    s_id = pl.program_id(0)
    past = n_pages * PAGE_SIZE
    n_past_blk = past // SEL_BLOCK
    n_blk_pad = ovl_ref.shape[1]
    R = N_HEADS * TOK_PAD
    GR = HPG * TOK_PAD
    wc = ckw_ref.shape[2]
    bufs = ((kb0, vb0, 0), (kb1, vb1, 1))
    t_row = _iota((R, 1), 0) & (TOK_PAD - 1)

    def start(b, slot):
        kb, vb, si = bufs[slot]
        _start_pages(pt_ref, (pk_ref, pv_ref), (kb, vb), (sem.at[2 * si], sem.at[2 * si + 1]), b, n_pages)

    def finish(slot):
        kb, vb, si = bufs[slot]
        qa = qa_ref[slot]

        n_cmp_pad = kc_ref.shape[2]
        s = _dot(qa, kc_ref[slot])
        mk = _iota((R, n_cmp_pad), 1) < n_cmp_pad - 1
        s = jnp.where(mk, s, NEG)
        e = jnp.exp(s - jnp.max(s, axis=-1, keepdims=True))
        p = jnp.where(mk, e / jnp.sum(e, axis=-1, keepdims=True), 0.0).astype(BF16)
        o_cmp = _dot(p, vc_ref[slot])
        impc = _dot(p, ovl_ref[...])
        imp = jnp.concatenate(
            [impc[g * GR:g * GR + TOK_PAD] + impc[g * GR + TOK_PAD:g * GR + 2 * TOK_PAD]
             + impc[g * GR + 2 * TOK_PAD:g * GR + 3 * TOK_PAD] + impc[g * GR + 3 * TOK_PAD:g * GR + 4 * TOK_PAD]
             for g in range(KV_HEADS)], axis=0)
        nr = KV_HEADS * TOK_PAD
        blk = _iota((nr, n_blk_pad), 1)
        cur = (past + jnp.minimum(_iota((nr, n_blk_pad), 0) & (TOK_PAD - 1), tn - 1)) // SEL_BLOCK
        valid = blk <= cur
        forced = (valid & (blk > cur - N_LOCAL_BLOCKS)) | (blk < N_INIT_BLOCKS)
        score = jnp.where(forced, BIG, jnp.where(valid, imp, NEG))
        sel = _topk_mask(score, blk, n_past_blk + 1, 1)
        selneg = jnp.where(sel, 0.0, NEG)
        selneg = jnp.concatenate(
            [selneg[g * TOK_PAD:(g + 1) * TOK_PAD] for g in range(KV_HEADS) for _ in range(HPG)], axis=0)
        low = _iota((R, LANES), 1) < SEL_HALF
        qaug = []
        for hf in range(n_past_blk // SEL_HALF):
            slab = selneg[:, (hf // 2) * LANES:(hf // 2 + 1) * LANES]
            if hf % 2:
                slab = pltpu.roll(slab, SEL_HALF, 1)
            qaug.append(jnp.concatenate([qa, jnp.where(low, slab, 0.0).astype(BF16)], axis=1))

        _wait_pages((pk_ref, pv_ref), (kb, vb), (sem.at[2 * si], sem.at[2 * si + 1]), n_pages)

        pages_per_half = SEL_HALF * SEL_BLOCK // PAGE_SIZE

        def half(buf, hf):
            return jnp.concatenate(
                [buf[hf * pages_per_half + j] for j in range(pages_per_half)], axis=1).astype(BF16)

        col = _iota((R, LANES), 1)
        ss = [_dot(qaug[hf], jnp.concatenate([half(kb, hf), oh_ref[...]], axis=0)) for hf in range(len(qaug))]
        ss.append(jnp.where(col <= t_row, _nt(qa, kt_ref[slot]), NEG))
        m = functools.reduce(jnp.maximum, [jnp.max(s, axis=-1, keepdims=True) for s in ss])
        es = [jnp.exp(s - m) for s in ss]
        l = functools.reduce(jnp.add, [jnp.sum(e, axis=-1, keepdims=True) for e in es])
        acc = _dot(es[-1].astype(BF16), vt_ref[slot])
        for hf in range(len(qaug)):
            acc = acc + _nt(es[hf].astype(BF16), half(vb, hf))
        o_sel = acc / l

        sa = jnp.where(_iota((R, wc), 1) > t_row, _dot(qa, ckw_ref[slot].astype(BF16)), NEG)
        sb = jnp.where(col <= t_row, _nt(qa, kwn_ref[slot]), NEG)
        m = jnp.maximum(jnp.max(sa, axis=-1, keepdims=True), jnp.max(sb, axis=-1, keepdims=True))
        ea, eb = jnp.exp(sa - m), jnp.exp(sb - m)
        den = jnp.sum(ea, axis=-1, keepdims=True) + jnp.sum(eb, axis=-1, keepdims=True)
        o_win = (_nt(ea.astype(BF16), cvw_ref[slot].astype(BF16)) + _dot(eb.astype(BF16), vwn_ref[slot])) / den

        o_ref[slot, 0] = o_cmp
        o_ref[slot, 1] = o_sel
        o_ref[slot, 2] = o_win

        newest = _iota((LANES, LANES), 1) >= LANES - tn
        for src, new, dst in ((ckw_ref, kwt_ref, okw_ref), (cvw_ref, vwt_ref, ovw_ref)):
            shifted = pltpu.roll(src[slot], wc - tn, 1)
            dst[slot, :, 0:wc - LANES] = shifted[:, 0:wc - LANES]
            dst[slot, :, wc - LANES:wc] = jnp.where(newest, new[slot], shifted[:, wc - LANES:wc])

    @pl.when(s_id == 0)
    def _():
        start(0, 0)

    start(2 * s_id + 1, 1)
    finish(0)

    @pl.when(s_id + 1 < pl.num_programs(0))
    def _():
        start(2 * s_id + 2, 0)

    finish(1)


def _attn_sample(page_table, pool_k, pool_v, tn, qa, kct, vc, ovl, ktail, vtail, ckw, cvw, kwn, vwn, kwt, vwt):
    Bd, n_pages = page_table.shape
    R = N_HEADS * TOK_PAD
    wc = ckw.shape[2]
    const = lambda *shape: pl.BlockSpec(shape, lambda s, pt: (0,) * len(shape))
    pair = lambda *shape: pl.BlockSpec((2,) + shape, lambda s, pt: (s,) + (0,) * len(shape))
    anyspec = pl.BlockSpec(memory_space=pl.ANY)
    half_keys = SEL_HALF * SEL_BLOCK
    onehot = np.zeros((LANES, half_keys), np.float32)
    onehot[np.arange(half_keys) // SEL_BLOCK, np.arange(half_keys)] = 1.0
    grid_spec = pltpu.PrefetchScalarGridSpec(
        num_scalar_prefetch=1,
        grid=(Bd // 2,),
        in_specs=[anyspec, anyspec, pair(R, LANES), pair(LANES, kct.shape[2]), pair(vc.shape[1], LANES),
                  const(*ovl.shape), const(LANES, half_keys), pair(LANES, LANES), pair(LANES, LANES),
                  pair(LANES, wc), pair(LANES, wc), pair(LANES, LANES), pair(LANES, LANES),
                  pair(LANES, LANES), pair(LANES, LANES)],
        out_specs=[pair(3, R, LANES), pair(LANES, wc), pair(LANES, wc)],
        scratch_shapes=[pltpu.VMEM((n_pages, LANES, PAGE_SIZE), F32)] * 4 + [pltpu.SemaphoreType.DMA((4,))],
    )
    return pl.pallas_call(
        functools.partial(_attn_sample_body, n_pages, tn),
        grid_spec=grid_spec,
        out_shape=[jax.ShapeDtypeStruct((Bd, 3, R, LANES), F32)] + [jax.ShapeDtypeStruct((Bd, LANES, wc), F32)] * 2,
        compiler_params=pltpu.CompilerParams(dimension_semantics=("arbitrary",), vmem_limit_bytes=VMEM_LIMIT),
        name="attn_sample",
    )(page_table, pool_k, pool_v, qa, kct, vc, ovl, jnp.asarray(onehot, BF16), ktail, vtail, ckw, cvw, kwn, vwn,
      kwt, vwt)


def _mixer_sample_body(o3_ref, g3_ref, sza_ref, sra_ref, gpb_ref, x_ref, wpa_ref, wo_ref, fg_ref, y_ref):
    o = g3_ref[0] * o3_ref[0] + g3_ref[1] * o3_ref[1] + g3_ref[2] * o3_ref[2]
    y_ref[...] = _mixer_tail(o, sza_ref[...], sra_ref[...], gpb_ref[...], x_ref[...], wpa_ref, wo_ref, fg_ref)


def _mixer_sample(o3, g3, sza, sra, gpb, x2d, prm):
    T = x2d.shape[0]
    full = lambda *shape: pl.BlockSpec(shape, lambda i: (0,) * len(shape))
    return pl.pallas_call(
        _mixer_sample_body,
        grid=(1,),
        in_specs=[full(3, T, ATTN_WIDTH), full(3, T, ATTN_WIDTH), full(T, ATTN_WIDTH), full(T, D_MODEL),
                  full(T, D_MODEL), full(T, D_MODEL), full(ATTN_WIDTH, D_MODEL), full(D_MODEL, D_MODEL),
                  full(1, D_MODEL)],
        out_specs=full(T, D_MODEL),
        out_shape=jax.ShapeDtypeStruct((T, D_MODEL), F32),
        compiler_params=pltpu.CompilerParams(dimension_semantics=("arbitrary",), vmem_limit_bytes=VMEM_LIMIT),
        name="mixer_sample",
    )(o3, g3, sza, sra, gpb, x2d, prm["w_pa"], prm["w_o"], prm["final_g"])


def _overlap(n_cmp, n_sel):
    cs = np.arange(n_cmp)[:, None] * CMP_STRIDE
    ss = np.arange(n_sel)[None, :] * SEL_BLOCK
    ov = np.minimum(cs + CMP_BLOCK, ss + SEL_BLOCK) - np.maximum(cs, ss)
    return np.clip(ov, 0, None).astype(np.float32) / CMP_BLOCK


def _cmp_weights(w1, b1, w2, b2):
    w1r = w1.reshape(2, CMP_STRIDE, HEAD_DIM, CMP_HIDDEN).transpose(1, 2, 0, 3)
    w1r = w1r.reshape(CMP_STRIDE, 1, HEAD_DIM, 2 * CMP_HIDDEN).astype(BF16)
    w2r = w2.reshape(1, CMP_HIDDEN, HEAD_DIM).astype(BF16)
    big = jnp.concatenate(
        [jnp.pad(w1r, ((0, 0), (0, 0), (0, 0), (g * 2 * CMP_HIDDEN, (KV_HEADS - 1 - g) * 2 * CMP_HIDDEN)))
         for g in range(KV_HEADS)], axis=1)
    w2b = jnp.concatenate(
        [jnp.pad(w2r, ((0, 0), (0, 0), (g * HEAD_DIM, (KV_HEADS - 1 - g) * HEAD_DIM))) for g in range(KV_HEADS)],
        axis=0)
    return (big.reshape(CMP_STRIDE * LANES, 4 * LANES), jnp.tile(b1, KV_HEADS)[None],
            w2b.reshape(KV_HEADS * CMP_HIDDEN, LANES), jnp.tile(b2, KV_HEADS)[None])


def _seq_minor(t):
    lead = t.shape[:-3]
    n = len(lead)
    return t.transpose(*range(n), n + 1, n + 2, n).reshape(*lead, KV_WIDTH, t.shape[-3])


def _seq_major(t):
    lead = t.shape[:-2]
    n = len(lead)
    return t.reshape(*lead, KV_HEADS, HEAD_DIM, t.shape[-1]).transpose(*range(n), n + 2, n, n + 1)


def kernel(x_prompt, x_sample, cache_k_cmp, cache_v_cmp, cache_k_sel, cache_v_sel, cache_k_win, cache_v_win, page_table, norm_g, w_in, cmp_k_w1, cmp_k_b1, cmp_k_w2, cmp_k_b2, cmp_v_w1, cmp_v_b1, cmp_v_w2, cmp_v_b2, v_norm_g, v_norm_b, w_spatial, b_spatial, w_pa, w_pb, w_o, final_g):
    B, S, _ = x_prompt.shape
    Bd, tn, _ = x_sample.shape
    depth = w_in.shape[0]
    assert depth == 1, "single-layer step"
    assert Bd * tn == CHUNK, "the sample tokens form one 128-row tile"
    n_pages = page_table.shape[1]
    past = n_pages * PAGE_SIZE

    split = OFF_GL + 3 * N_HEADS
    w = w_in[0]
    prm = {
        "norm_g": norm_g,
        "w_in": jnp.concatenate([w.T[:split], jnp.zeros((GL_PAD, D_MODEL), F32), w.T[split:]], axis=0).astype(BF16),
        "w_sp": w_spatial[0],
        "b_sp": jnp.repeat(b_spatial[0].T, LANES, axis=1),
        "w_sp_s": jnp.stack([jnp.kron(jnp.eye(CHUNK // tn, dtype=F32), w_spatial[0, g, :tn, :tn])
                             for g in range(GMLP_GROUPS)]),
        "b_sp_s": jnp.tile(jnp.repeat(b_spatial[0, :, :tn].T, LANES, axis=1), (CHUNK // tn, 1)),
        "v_norm_g": v_norm_g, "v_norm_b": v_norm_b,
        "w_pb": w_pb[0].astype(BF16), "w_pa": w_pa[0].astype(BF16), "w_o": w_o[0].astype(BF16),
        "final_g": final_g[None],
        "cmp_k": _cmp_weights(cmp_k_w1[0], cmp_k_b1[0], cmp_k_w2[0], cmp_k_b2[0]),
        "cmp_v": _cmp_weights(cmp_v_w1[0], cmp_v_b1[0], cmp_v_w2[0], cmp_v_b2[0]),
    }

    xp = x_prompt.reshape(B * S, D_MODEL)
    (q_hm, p_kcmp, p_vcmp, p_ksel, p_vsel, p_kwin, p_vwin, kcr, vcr, kaug, kwt, vsb, vwb,
     gates, sza, sra, gpb) = _in_project(xp, _rope_tables(np.arange(S)), prm, TM_PROMPT, False, S)
    n_half = S // CMP_STRIDE
    n_cmp = (S - CMP_BLOCK) // CMP_STRIDE + 1
    n_sel = S // SEL_BLOCK
    assert n_sel <= SEL_HALF
    ovl = np.zeros((n_half, LANES), np.float32)
    ovl[:n_cmp, HEAD_DIM:HEAD_DIM + n_sel] = _overlap(n_cmp, n_sel)
    cend_tables = _rope_tables(np.arange(n_half) * CMP_STRIDE + CMP_BLOCK - 1)
    kct, vco = _compress_prompt(kcr.reshape(B, S, LANES), vcr.reshape(B, S, LANES), prm, cend_tables,
                                jnp.asarray(ovl, BF16))
    y_prompt = _attn_prompt(B, S, q_hm, kaug, vsb, kwt, vwb, kct, vco, gates, sza, sra, gpb, xp, prm)

    xs = x_sample.reshape(Bd * tn, D_MODEL)
    pos_s = np.tile(past + np.arange(tn), Bd)
    (q_s, s_kcmp, s_vcmp, s_ksel, s_vsel, s_kwin, s_vwin, gates_s, sza_s, sra_s, gpb_s, vn_s) = _in_project(
        xs, _rope_tables(pos_s), prm, Bd * tn, True, tn)
    n_half_s = past // CMP_STRIDE
    pools = [_seq_minor(c[0]) for c in (cache_k_cmp, cache_v_cmp, cache_k_sel, cache_v_sel)]
    cend_s = _rope_tables(np.arange(n_half_s) * CMP_STRIDE + CMP_BLOCK - 1)
    kct_s, vc_s = _compress_sample(page_table, pools[0], pools[1], prm, cend_s)

    n_cmp_s = (past + tn - CMP_BLOCK) // CMP_STRIDE + 1
    n_blk_s = past // SEL_BLOCK + -(-tn // SEL_BLOCK)
    assert n_cmp_s == n_half_s - 1 and tn <= TOK_PAD and (past // SEL_BLOCK) % SEL_HALF == 0
    ovl_s = np.zeros((n_half_s, 2 * LANES), np.float32)
    ovl_s[:n_cmp_s, :n_blk_s] = _overlap(n_cmp_s, n_blk_s)
    q5 = q_s.reshape(Bd, tn, KV_HEADS, HPG, HEAD_DIM).transpose(0, 2, 3, 1, 4)
    q5 = jnp.pad(q5, ((0, 0), (0, 0), (0, 0), (0, TOK_PAD - tn), (0, 0))).reshape(Bd, KV_HEADS, HPG * TOK_PAD, HEAD_DIM)
    qa = jnp.concatenate(
        [jnp.pad(q5[:, g], ((0, 0), (0, 0), (g * HEAD_DIM, (KV_HEADS - 1 - g) * HEAD_DIM))) for g in range(KV_HEADS)],
        axis=1).astype(BF16)
    pad_rows = lambda t: jnp.pad(t.reshape(Bd, tn, LANES), ((0, 0), (0, LANES - tn), (0, 0))).astype(BF16)
    ckw = _seq_minor(cache_k_win[0])
    cvw = _seq_minor(cache_v_win[0])
    new_t = lambda t: jnp.pad(t.reshape(Bd, tn, LANES).transpose(0, 2, 1), ((0, 0), (0, 0), (LANES - tn, 0)))
    o3, s_k_win, s_v_win = _attn_sample(
        page_table, pools[2], pools[3], tn, qa, kct_s, vc_s, jnp.asarray(ovl_s, BF16),
        pad_rows(s_ksel), pad_rows(s_vsel), ckw, cvw, pad_rows(s_kwin), pad_rows(s_vwin), new_t(s_kwin), new_t(s_vwin))
    s_k_win = _seq_major(s_k_win)[None]
    s_v_win = _seq_major(s_v_win)[None]
    first_group = jnp.arange(N_HEADS * TOK_PAD)[:, None] < HPG * TOK_PAD
    o3r = jnp.where(first_group, o3[..., :HEAD_DIM], o3[..., HEAD_DIM:])
    o3r = o3r.reshape(Bd, 3, N_HEADS, TOK_PAD, HEAD_DIM)[:, :, :, :tn]
    o3r = o3r.transpose(1, 0, 3, 2, 4).reshape(3, Bd * tn, ATTN_WIDTH)
    g3 = gates_s[:, :3 * N_HEADS].reshape(Bd * tn, N_HEADS, 3).transpose(2, 0, 1)
    g3 = jnp.repeat(g3, HEAD_DIM, axis=2)
    y_sample = _mixer_sample(o3r, g3, sza_s, sra_s, gpb_s, xs, prm)

    kv5 = lambda t, b, n: t.reshape(1, b, n, KV_HEADS, HEAD_DIM)
    pw = min(WINDOW, S)
    return (y_prompt.reshape(B, S, D_MODEL), y_sample.reshape(Bd, tn, D_MODEL),
            _seq_major(p_kcmp)[None], _seq_major(p_vcmp)[None], _seq_major(p_ksel)[None], _seq_major(p_vsel)[None],
            _seq_major(p_kwin[:, :, S - pw:])[None], _seq_major(p_vwin[:, :, S - pw:])[None],
            kv5(s_kcmp, Bd, tn), kv5(s_vcmp, Bd, tn), kv5(s_ksel, Bd, tn), kv5(s_vsel, Bd, tn),
            s_k_win, s_v_win, vn_s.reshape(1, Bd, tn, GMLP_WIDTH))
```

```python
import functools

import numpy as np
import jax
import jax.numpy as jnp
from jax import lax
from jax.experimental import pallas as pl
from jax.experimental.pallas import tpu as pltpu

F32 = jnp.float32
BF16 = jnp.bfloat16

D_MODEL = 1024
HEAD_DIM = 64
N_HEADS = 8
KV_HEADS = 2
HPG = N_HEADS // KV_HEADS
ATTN_WIDTH = N_HEADS * HEAD_DIM
KV_WIDTH = KV_HEADS * HEAD_DIM
ROT_DIM = HEAD_DIM // 4
ROT_HALF = ROT_DIM // 2
ROPE_THETA = 500000.0
CMP_BLOCK = 32
CMP_STRIDE = 16
CMP_HIDDEN = 128
SEL_BLOCK = 64
TOP_N = 16
N_INIT_BLOCKS = 1
N_LOCAL_BLOCKS = 2
WINDOW = 512
CHUNK = 128
GMLP_GROUPS = 4
GMLP_WIDTH = 512
PAGE_SIZE = 128
NEG = -1e30
BIG = 1e30
EPS = 1e-6
LOG2E = 1.4426950408889634

LANES = 128
GL_PAD = LANES - 3 * N_HEADS

OFF_Q = 0
OFF_KV = ATTN_WIDTH
OFF_GL = OFF_KV + 6 * KV_WIDTH
OFF_ZA = OFF_GL + LANES
OFF_U = OFF_ZA + ATTN_WIDTH
OFF_V = OFF_U + GMLP_WIDTH
OFF_ZB = OFF_V + GMLP_WIDTH
OFF_RA = OFF_ZB + GMLP_WIDTH
OFF_RB = OFF_RA + D_MODEL
W_TOT = OFF_RB + D_MODEL

VMEM_LIMIT = 56 * 1024 * 1024

TM_PROMPT = 512
TQ = 256
HU = 2
KT = 512
WIN_KEYS = WINDOW + TQ
SEL_HALF = 64
TOK_PAD = 8
RANK_GROUP = 8


def _nt(a, b):
    return lax.dot_general(a, b, (((1,), (1,)), ((), ())), preferred_element_type=F32)


def _dot(a, b):
    return jnp.dot(a, b, preferred_element_type=F32)


def _iota(shape, dim):
    return lax.broadcasted_iota(jnp.int32, shape, dim)


def _rope(slab, cos, s1, s2):
    return slab * cos + pltpu.roll(slab, LANES - ROT_HALF, 1) * s1 + pltpu.roll(slab, ROT_HALF, 1) * s2


def _rope_tables(pos):
    pos = np.asarray(pos, np.float64)
    n = pos.shape[0]
    inv = np.power(np.float64(ROPE_THETA), -np.arange(0, ROT_DIM, 2, dtype=np.float64) / ROT_DIM)
    ang = pos[:, None] * inv[None, :]
    cos, sin = np.cos(ang), np.sin(ang)
    rest = HEAD_DIM - ROT_DIM
    c = np.concatenate([cos, cos, np.ones((n, rest))], axis=1)
    s1 = np.concatenate([-sin, np.zeros((n, HEAD_DIM - ROT_HALF))], axis=1)
    s2 = np.concatenate([np.zeros((n, ROT_HALF)), sin, np.zeros((n, rest))], axis=1)
    return tuple(jnp.asarray(np.tile(t, (1, LANES // HEAD_DIM)), F32) for t in (c, s1, s2))


def _sigmoid(x):
    return 1.0 / (1.0 + jnp.exp(-x))


def _gelu(x):
    return jax.nn.gelu(x, approximate=True)


def _inproj_body(tm, sample, pos_tiles, x_ref, ng_ref, w_ref, cos_ref, s1_ref, s2_ref, wsp_ref, bsp_ref,
                 vng_ref, vnb_ref, wpb_ref, *outs):
    if sample:
        (q_ref, kcmp_ref, vcmp_ref, ksel_ref, vsel_ref, kwin_ref, vwin_ref,
         gat_ref, sza_ref, sra_ref, gpb_ref, vn_ref) = outs
    else:
        (q_ref, kcmp_ref, vcmp_ref, ksel_ref, vsel_ref, kwin_ref, vwin_ref,
         kcr_ref, vcr_ref, kaug_ref, kwt_ref, vsb_ref, vwb_ref, gat_ref, sza_ref, sra_ref, gpb_ref) = outs

    x = x_ref[...]
    r = lax.rsqrt(jnp.mean(x * x, axis=-1, keepdims=True) + EPS)
    h = ((x * r) * ng_ref[...]).astype(BF16)

    def proj(lo, hi):
        return _nt(h, w_ref[lo:hi, :])

    cos, s1, s2 = cos_ref[...], s1_ref[...], s2_ref[...]
    low = _iota((tm, LANES), 1) < HEAD_DIM

    q = proj(OFF_Q, OFF_Q + ATTN_WIDTH)
    q_scale = HEAD_DIM ** -0.5 if sample else HEAD_DIM ** -0.5 * LOG2E
    for pp in range(N_HEADS // 2):
        slab = _rope(q[:, pp * LANES:(pp + 1) * LANES], cos, s1, s2) * q_scale
        if sample:
            q_ref[:, pp * LANES:(pp + 1) * LANES] = slab
        elif (2 * pp) // HPG == 0:
            q_ref[2 * pp] = jnp.where(low, slab, 0.0).astype(BF16)
            q_ref[2 * pp + 1] = jnp.where(low, pltpu.roll(slab, HEAD_DIM, 1), 0.0).astype(BF16)
        else:
            q_ref[2 * pp] = jnp.where(low, 0.0, pltpu.roll(slab, HEAD_DIM, 1)).astype(BF16)
            q_ref[2 * pp + 1] = jnp.where(low, 0.0, slab).astype(BF16)

    kv = proj(OFF_KV, OFF_KV + 6 * KV_WIDTH)
    kcmp = kv[:, 0:LANES]
    vcmp = kv[:, LANES:2 * LANES]
    ksel = _rope(kv[:, 2 * LANES:3 * LANES], cos, s1, s2)
    vsel = kv[:, 3 * LANES:4 * LANES]
    kwin = _rope(kv[:, 4 * LANES:5 * LANES], cos, s1, s2)
    vwin = kv[:, 5 * LANES:6 * LANES]
    if sample:
        for ref, val in ((kcmp_ref, kcmp), (vcmp_ref, vcmp), (ksel_ref, ksel), (vsel_ref, vsel),
                         (kwin_ref, kwin), (vwin_ref, vwin)):
            ref[...] = val
    else:
        ksel_t = ksel.T
        kwin_t = kwin.T
        for ref, val in ((kcmp_ref, kcmp.T), (vcmp_ref, vcmp.T), (ksel_ref, ksel_t), (vsel_ref, vsel.T),
                         (kwin_ref, kwin_t), (vwin_ref, vwin.T)):
            ref[0] = val
        kcr_ref[...] = kcmp
        vcr_ref[...] = vcmp
        base = (pl.program_id(0) % pos_tiles) * tm
        blk = (base + _iota((LANES, tm), 1)) // SEL_BLOCK
        onehot = jnp.where(_iota((LANES, tm), 0) == blk, 1.0, 0.0)
        kaug_ref[0, 0:LANES] = ksel_t.astype(BF16)
        kaug_ref[0, LANES:2 * LANES] = onehot.astype(BF16)
        kwt_ref[0] = kwin_t.astype(BF16)
        for ref, val in ((vsb_ref, vsel), (vwb_ref, vwin)):
            ref[0] = jnp.where(low, val, 1.0).astype(BF16)
            ref[1] = jnp.where(low, 1.0, val).astype(BF16)

    gat_ref[...] = _sigmoid(proj(OFF_GL, OFF_GL + LANES))
    za = proj(OFF_ZA, OFF_ZA + ATTN_WIDTH)
    sza_ref[...] = za * _sigmoid(za)
    sra_ref[...] = _sigmoid(proj(OFF_RA, OFF_RA + D_MODEL))

    v = proj(OFF_V, OFF_V + GMLP_WIDTH)
    gv = _gelu(v)
    mu = jnp.mean(gv, axis=-1, keepdims=True)
    var = jnp.mean(jnp.square(gv - mu), axis=-1, keepdims=True)
    vn = ((gv - mu) * lax.rsqrt(var + EPS)) * vng_ref[...] + vnb_ref[...]
    if sample:
        vn_ref[...] = vn
    vnb16 = vn.astype(BF16)
    n_chunk = tm // CHUNK
    tri = _iota((CHUNK, CHUNK), 0) >= _iota((CHUNK, CHUNK), 1)
    mixed = []
    for g in range(GMLP_GROUPS):
        wm = jnp.where(tri, wsp_ref[g], 0.0).astype(BF16)
        cat = jnp.concatenate(
            [vnb16[c * CHUNK:(c + 1) * CHUNK, g * LANES:(g + 1) * LANES] for c in range(n_chunk)], axis=1)
        mixed.append(_dot(wm, cat))
    bsp = bsp_ref[...]
    sg = jnp.concatenate(
        [jnp.concatenate([mixed[g][:, c * LANES:(c + 1) * LANES] for g in range(GMLP_GROUPS)], axis=1) + bsp
         for c in range(n_chunk)], axis=0)
    u = proj(OFF_U, OFF_U + GMLP_WIDTH)
    zb = proj(OFF_ZB, OFF_ZB + GMLP_WIDTH)
    t = (_gelu(u) * sg) * (zb * _sigmoid(zb))
    pb = _dot(t.astype(BF16), wpb_ref[...])
    gpb_ref[...] = _sigmoid(proj(OFF_RB, OFF_RB + D_MODEL)) * pb


def _in_project(x2d, tables, prm, tm, sample, seq):
    T = x2d.shape[0]
    nt = T // tm
    pos_tiles = tables[0].shape[0] // tm
    const = lambda *shape: pl.BlockSpec(shape, lambda i: (0,) * len(shape))
    row = lambda width: pl.BlockSpec((tm, width), lambda i: (i, 0))
    tab = pl.BlockSpec((tm, LANES), lambda i: (i % pos_tiles, 0))
    in_specs = [
        row(D_MODEL), const(1, D_MODEL),
        pl.BlockSpec((W_TOT, D_MODEL), lambda i: (0, 0), pipeline_mode=pl.Buffered(1)),
        tab, tab, tab,
        const(GMLP_GROUPS, CHUNK, CHUNK), const(CHUNK, GMLP_WIDTH), const(1, GMLP_WIDTH), const(1, GMLP_WIDTH),
        const(GMLP_WIDTH, D_MODEL),
    ]
    f32rows = lambda width: jax.ShapeDtypeStruct((T, width), F32)
    tail_shapes = [f32rows(LANES), f32rows(ATTN_WIDTH), f32rows(D_MODEL), f32rows(D_MODEL)]
    tail_specs = [row(LANES), row(ATTN_WIDTH), row(D_MODEL), row(D_MODEL)]
    if sample:
        out_shape = [f32rows(ATTN_WIDTH)] + [f32rows(KV_WIDTH)] * 6 + tail_shapes + [f32rows(GMLP_WIDTH)]
        out_specs = [row(ATTN_WIDTH)] + [row(KV_WIDTH)] * 6 + tail_specs + [row(GMLP_WIDTH)]
    else:
        nb = T // seq
        bf = lambda *shape: jax.ShapeDtypeStruct(shape, BF16)
        tposed = lambda rows: pl.BlockSpec((1, rows, tm), lambda i: (i // pos_tiles, 0, i % pos_tiles))
        out_shape = ([bf(N_HEADS, T, LANES)] + [jax.ShapeDtypeStruct((nb, KV_WIDTH, seq), F32)] * 6
                     + [f32rows(KV_WIDTH)] * 2
                     + [bf(nb, 2 * LANES, seq), bf(nb, LANES, seq), bf(KV_HEADS, T, LANES), bf(KV_HEADS, T, LANES)]
                     + tail_shapes)
        heads = lambda n: pl.BlockSpec((n, tm, LANES), lambda i: (0, i, 0))
        out_specs = ([heads(N_HEADS)] + [tposed(KV_WIDTH)] * 6 + [row(KV_WIDTH)] * 2
                     + [tposed(2 * LANES), tposed(LANES), heads(KV_HEADS), heads(KV_HEADS)] + tail_specs)
    return pl.pallas_call(
        functools.partial(_inproj_body, tm, sample, pos_tiles),
        grid=(nt,),
        in_specs=in_specs,
        out_specs=out_specs,
        out_shape=out_shape,
        compiler_params=pltpu.CompilerParams(dimension_semantics=("arbitrary",), vmem_limit_bytes=VMEM_LIMIT),
        name="in_project_sample" if sample else "in_project_prompt",
    )(x2d, prm["norm_g"], prm["w_in"], *tables, prm["w_sp_s" if sample else "w_sp"],
      prm["b_sp_s" if sample else "b_sp"], prm["v_norm_g"], prm["v_norm_b"], prm["w_pb"])


def _strided_halfblocks(load_rows):
    return jnp.concatenate([load_rows(t) for t in range(CMP_STRIDE)], axis=1).astype(BF16)


def _compress_mlp(xcat, n, w1_ref, b1_ref, w2_ref, b2_ref):
    hh = _dot(xcat, w1_ref[...])
    hid = jnp.concatenate(
        [hh[:, 0:LANES] + pltpu.roll(hh[:, LANES:2 * LANES], n - 1, 0),
         hh[:, 2 * LANES:3 * LANES] + pltpu.roll(hh[:, 3 * LANES:4 * LANES], n - 1, 0)], axis=1) + b1_ref[...]
    return _dot(_gelu(hid).astype(BF16), w2_ref[...]) + b2_ref[...]


def _compress_mlp_grouped(x_ref, n, w1_ref, b1_ref, w2_ref, b2_ref):
    hid = []
    for g in range(KV_HEADS):
        hh = _dot(x_ref[g], w1_ref[...])
        hid.append(hh[:, 0:LANES] + pltpu.roll(hh[:, LANES:2 * LANES], n - 1, 0))
    hid = jnp.concatenate(hid, axis=1) + b1_ref[...]
    return _dot(_gelu(hid).astype(BF16), w2_ref[...]) + b2_ref[...]


def _compress_prompt_body(n, kr_ref, vr_ref, w1k, b1k, w2k, b2k, w1v, b1v, w2v, b2v, cos_ref, s1_ref, s2_ref,
                          ovl_ref, kct_ref, vco_ref):
    kc = _compress_mlp(_strided_halfblocks(lambda t: kr_ref[0, pl.ds(t, n, stride=CMP_STRIDE), :]),
                       n, w1k, b1k, w2k, b2k)
    kct_ref[0] = _rope(kc, cos_ref[...], s1_ref[...], s2_ref[...]).T.astype(BF16)
    vc = _compress_mlp(_strided_halfblocks(lambda t: vr_ref[0, pl.ds(t, n, stride=CMP_STRIDE), :]),
                       n, w1v, b1v, w2v, b2v)
    vco_ref[0, :, 0:LANES] = vc.astype(BF16)
    vco_ref[0, :, LANES:2 * LANES] = ovl_ref[...]


def _compress_prompt(kr, vr, prm, tables, ovl):
    B, S, _ = kr.shape
    n = S // CMP_STRIDE
    const = lambda *shape: pl.BlockSpec(shape, lambda b: (0,) * len(shape))
    rows = pl.BlockSpec((1, S, LANES), lambda b: (b, 0, 0))
    wspecs = [const(CMP_STRIDE * LANES, 4 * LANES), const(1, 2 * LANES), const(2 * LANES, LANES), const(1, LANES)]
    return pl.pallas_call(
        functools.partial(_compress_prompt_body, n),
        grid=(B,),
        in_specs=[rows, rows] + wspecs + wspecs + [const(n, LANES)] * 3 + [const(n, LANES)],
        out_specs=[pl.BlockSpec((1, LANES, n), lambda b: (b, 0, 0)),
                   pl.BlockSpec((1, n, 2 * LANES), lambda b: (b, 0, 0))],
        out_shape=[jax.ShapeDtypeStruct((B, LANES, n), BF16),
                   jax.ShapeDtypeStruct((B, n, 2 * LANES), BF16)],
        compiler_params=pltpu.CompilerParams(dimension_semantics=("arbitrary",), vmem_limit_bytes=VMEM_LIMIT),
        name="compress_prompt",
    )(kr, vr, *prm["cmp_k"], *prm["cmp_v"], *tables, ovl)


def _start_pages(pt_ref, pools, bufs, sems, b, n_pages):
    def body(p, carry):
        page = pt_ref[b, p]
        for i, (pool, buf, sem) in enumerate(zip(pools, bufs, sems)):
            pltpu.make_async_copy(pool.at[page], buf.at[p], sem).start(priority=i % 2)
        return carry
    lax.fori_loop(0, n_pages, body, 0)


def _wait_pages(pools, bufs, sems, n_pages):
    for pool, buf, sem in zip(pools, bufs, sems):
        pltpu.make_async_copy(pool.at[pl.ds(0, n_pages)], buf, sem).wait()


def _compress_sample_body(n, n_pages, pt_ref, pk_ref, pv_ref, w1k, b1k, w2k, b2k, w1v, b1v, w2v, b2v,
                          cos_ref, s1_ref, s2_ref, perm_ref, kct_ref, vc_ref, kb0, kb1, vb0, vb1, xk, xv, sem):
    s = pl.program_id(0)
    bufs = ((kb0, vb0, 0), (kb1, vb1, 1))

    def start(b, slot):
        kb, vb, si = bufs[slot]
        _start_pages(pt_ref, (pk_ref, pv_ref), (kb, vb), (sem.at[2 * si], sem.at[2 * si + 1]), b, n_pages)

    hb_per_page = PAGE_SIZE // CMP_STRIDE

    def halfblocks(buf, xcat):
        pairs = jnp.concatenate([buf[pl.ds(0, n_pages // 2, stride=2)], buf[pl.ds(1, n_pages // 2, stride=2)]], axis=2)
        y = _dot(pairs.reshape(n_pages // 2 * LANES, 2 * PAGE_SIZE).astype(BF16), perm_ref[...]).astype(BF16)
        low = _iota((2 * hb_per_page, LANES), 1) < HEAD_DIM
        for i2 in range(n_pages // 2):
            z = y[i2 * LANES:(i2 + 1) * LANES].T
            rows = slice(i2 * 2 * hb_per_page, (i2 + 1) * 2 * hb_per_page)
            for u in range(CMP_STRIDE // 2):
                za = z[2 * u * 2 * hb_per_page:(2 * u + 1) * 2 * hb_per_page]
                zb = z[(2 * u + 1) * 2 * hb_per_page:(2 * u + 2) * 2 * hb_per_page]
                xcat[0, rows, u * LANES:(u + 1) * LANES] = jnp.where(low, za, pltpu.roll(zb, HEAD_DIM, 1))
                xcat[1, rows, u * LANES:(u + 1) * LANES] = jnp.where(low, pltpu.roll(za, HEAD_DIM, 1), zb)

    def finish(slot):
        kb, vb, si = bufs[slot]
        _wait_pages((pk_ref, pv_ref), (kb, vb), (sem.at[2 * si], sem.at[2 * si + 1]), n_pages)
        halfblocks(kb, xk)
        halfblocks(vb, xv)
        kc = _compress_mlp_grouped(xk, n, w1k, b1k, w2k, b2k)
        kct_ref[slot] = _rope(kc, cos_ref[...], s1_ref[...], s2_ref[...]).T.astype(BF16)
        vc_ref[slot] = _compress_mlp_grouped(xv, n, w1v, b1v, w2v, b2v).astype(BF16)

    @pl.when(s == 0)
    def _():
        start(0, 0)

    start(2 * s + 1, 1)
    finish(0)

    @pl.when(s + 1 < pl.num_programs(0))
    def _():
        start(2 * s + 2, 0)

    finish(1)


def _compress_sample(page_table, pool_k, pool_v, prm, tables):
    Bd, n_pages = page_table.shape
    past = n_pages * PAGE_SIZE
    n = past // CMP_STRIDE
    const = lambda *shape: pl.BlockSpec(shape, lambda s, pt: (0,) * len(shape))
    wspecs = [const(CMP_STRIDE * HEAD_DIM, 2 * LANES), const(1, 2 * LANES), const(2 * LANES, LANES), const(1, LANES)]
    anyspec = pl.BlockSpec(memory_space=pl.ANY)
    grid_spec = pltpu.PrefetchScalarGridSpec(
        num_scalar_prefetch=1,
        grid=(Bd // 2,),
        in_specs=[anyspec, anyspec] + wspecs + wspecs + [const(n, LANES)] * 3 + [const(2 * PAGE_SIZE, 2 * PAGE_SIZE)],
        out_specs=[pl.BlockSpec((2, LANES, n), lambda s, pt: (s, 0, 0)),
                   pl.BlockSpec((2, n, LANES), lambda s, pt: (s, 0, 0))],
        scratch_shapes=[pltpu.VMEM((n_pages, LANES, PAGE_SIZE), F32)] * 4
        + [pltpu.VMEM((KV_HEADS, n, CMP_STRIDE * HEAD_DIM), BF16)] * 2 + [pltpu.SemaphoreType.DMA((4,))],
    )
    hb = PAGE_SIZE // CMP_STRIDE
    perm = np.zeros((2 * PAGE_SIZE, 2 * PAGE_SIZE), np.float32)
    for side in range(2):
        for jj in range(hb):
            for t in range(CMP_STRIDE):
                perm[side * PAGE_SIZE + CMP_STRIDE * jj + t, t * 2 * hb + side * hb + jj] = 1.0
    return pl.pallas_call(
        functools.partial(_compress_sample_body, n, n_pages),
        grid_spec=grid_spec,
        out_shape=[jax.ShapeDtypeStruct((Bd, LANES, n), BF16), jax.ShapeDtypeStruct((Bd, n, LANES), BF16)],
        compiler_params=pltpu.CompilerParams(dimension_semantics=("arbitrary",), vmem_limit_bytes=VMEM_LIMIT),
        name="compress_sample",
    )(page_table, pool_k, pool_v, *prm["cmp_k_s"], *prm["cmp_v_s"], *tables, jnp.asarray(perm, BF16))


def _mixer_tail(o, sza, sra, gpb, x, wpa_ref, wo_ref, fg_ref):
    pa = _dot((o * sza).astype(BF16), wpa_ref[...])
    merged = sra * pa + gpb
    hn = x + _dot(merged.astype(BF16), wo_ref[...])
    r = lax.rsqrt(jnp.mean(hn * hn, axis=-1, keepdims=True) + EPS)
    return (hn * r) * fg_ref[...]


def _topk_mask(score, blk, n_blocks, axis):
    cnt = jnp.zeros(score.shape, jnp.int32)
    for sp in range(n_blocks):
        b = lax.slice_in_dim(score, sp, sp + 1, axis=axis)
        ge = jnp.where(b >= score, 1, 0)
        gt = jnp.where(b > score, 1, 0)
        cnt = cnt + jnp.where(blk > sp, ge, gt)
    return cnt < TOP_N


def _attn_prompt_body(n_sel, q_ref, ka_ref, vs_ref, kw_ref, vw_ref, kc_ref, vco_ref, band_ref, rowhot_ref, gat_ref,
                      sza_ref, sra_ref, gpb_ref, x_ref, wpa_ref, wo_ref, fg_ref, y_ref):
    i = pl.program_id(1)
    M = HU * TQ
    G = range(KV_HEADS)
    units = [(g, u) for g in G for u in range(HPG // HU)]
    qpos = i * TQ + (_iota((M, 1), 0) & (TQ - 1))
    n_cmp_pad = kc_ref.shape[2]
    gt = gat_ref[...]
    qs = [q_ref[HPG * g + HU * u:HPG * g + HU * (u + 1)].reshape(M, LANES) for g, u in units]

    mk = (_iota((M, n_cmp_pad), 1) * CMP_STRIDE + (CMP_BLOCK - 1)) <= qpos
    o_cmp = []
    imps = [None] * KV_HEADS
    for n, (g, u) in enumerate(units):
        s = jnp.where(mk, _dot(qs[n], kc_ref[0]), NEG)
        e = jnp.exp2(s - jnp.max(s, axis=-1, keepdims=True))
        p = jnp.where(mk, e / jnp.sum(e, axis=-1, keepdims=True), 0.0)
        r = _dot(p.astype(BF16), vco_ref[0])
        o_cmp.append(r[:, 0:LANES])
        impc = r[:, LANES:2 * LANES]
        for j in range(HU):
            part = impc[j * TQ:(j + 1) * TQ]
            imps[g] = part if imps[g] is None else imps[g] + part
    blk = _iota((n_sel, TQ), 0)
    cur = (i * TQ + _iota((n_sel, TQ), 1)) // SEL_BLOCK
    valid = blk <= cur
    forced = (valid & (blk > cur - N_LOCAL_BLOCKS)) | (blk < N_INIT_BLOCKS)
    scores = [jnp.where(forced, BIG, jnp.where(valid, imps[g].T[HEAD_DIM:HEAD_DIM + n_sel], NEG)) for g in G]

    last_blk = ((i + 1) * TQ - 1) // SEL_BLOCK

    def count_group(k, cnts):
        out = []
        for g in G:
            c = cnts[g]
            for sp in range(k * RANK_GROUP, (k + 1) * RANK_GROUP):
                b = scores[g][sp:sp + 1]
                c = c + jnp.where(blk > sp, jnp.where(b >= scores[g], 1, 0), jnp.where(b > scores[g], 1, 0))
            out.append(c)
        return tuple(out)

    cnts = (jnp.zeros((n_sel, TQ), jnp.int32),) * KV_HEADS
    for k in range(n_sel // RANK_GROUP):
        cnts = lax.cond(k * RANK_GROUP <= last_blk, functools.partial(count_group, k), lambda c: c, cnts)
    seln = []
    for g in G:
        selneg = jnp.where(cnts[g] < TOP_N, 0.0, NEG)
        seln_t = jnp.concatenate([selneg, jnp.zeros((LANES - n_sel, TQ), F32)], axis=0).T.astype(BF16)
        seln.append(jnp.concatenate([seln_t] * HU, axis=0))
    qa = [jnp.concatenate([qs[n], seln[g]], axis=1) for n, (g, u) in enumerate(units)]

    def update(carry, s, v):
        m, acc = carry
        mn = jnp.maximum(m, jnp.max(s, axis=-1, keepdims=True))
        acc = jnp.exp2(m - mn) * acc + _dot(jnp.exp2((s - mn).astype(BF16)), v)
        return mn, acc

    def step(kt, carries, causal):
        off = pl.multiple_of(kt * KT, KT)
        k = ka_ref[0, :, pl.ds(off, KT)]
        ss = [_dot(q, k) for q in qa]
        if causal:
            keep = kt * KT + _iota((M, KT), 1) <= qpos
            ss = [jnp.where(keep, s, NEG) for s in ss]
        return tuple(update(carries[n], ss[n], vs_ref[g, pl.ds(off, KT), :]) for n, (g, u) in enumerate(units))

    n_full = i // (KT // TQ)
    init = (jnp.full((M, 1), NEG, F32), jnp.zeros((M, LANES), F32))
    carries = lax.fori_loop(0, n_full // 2, lambda kp, c: step(2 * kp + 1, step(2 * kp, c, False), False),
                            (init,) * len(units))
    carries = lax.cond(n_full % 2 == 1, lambda c: step(n_full - 1, c, False), lambda c: c, carries)
    carries = step(n_full, carries, True)
    sum_lane = [(KV_HEADS - 1 - g) * HEAD_DIM for g in G]
    o_sel = [carries[n][1] / carries[n][1][:, sum_lane[g]:sum_lane[g] + 1] for n, (g, u) in enumerate(units)]

    st = pl.multiple_of(jnp.maximum(i * TQ - WINDOW, 0), TQ)
    kw = jnp.concatenate([kw_ref[0, :, pl.ds(st, WIN_KEYS)], band_ref[jnp.minimum(i, WINDOW // TQ)]], axis=0)
    o_units = []
    for n, (g, u) in enumerate(units):
        s = _dot(jnp.concatenate([qs[n], rowhot_ref[...]], axis=1), kw)
        e = jnp.exp2(s - jnp.max(s, axis=-1, keepdims=True))
        r = _dot(e.astype(BF16), vw_ref[g, pl.ds(st, WIN_KEYS), :])
        o_win = r / r[:, sum_lane[g]:sum_lane[g] + 1]

        def gcol(c, g=g, u=u):
            return jnp.concatenate(
                [gt[:, 3 * (HPG * g + HU * u + j) + c:3 * (HPG * g + HU * u + j) + c + 1] for j in range(HU)], axis=0)

        o_units.append(gcol(0) * o_cmp[n] + gcol(1) * o_sel[n] + gcol(2) * o_win)

    low = _iota((TQ, LANES), 1) < HEAD_DIM
    slabs = []
    for pp in range(N_HEADS // 2):
        h0 = 2 * pp
        g, n, j0 = h0 // HPG, h0 // HU, h0 % HU
        a = o_units[n][j0 * TQ:(j0 + 1) * TQ]
        b = o_units[n][(j0 + 1) * TQ:(j0 + 2) * TQ]
        if g == 0:
            b = pltpu.roll(b, HEAD_DIM, 1)
        else:
            a = pltpu.roll(a, HEAD_DIM, 1)
        slabs.append(jnp.where(low, a, b))
    o = jnp.concatenate(slabs, axis=1)
    y_ref[...] = _mixer_tail(o, sza_ref[...], sra_ref[...], gpb_ref[...], x_ref[...], wpa_ref, wo_ref, fg_ref)


def _attn_prompt(B, S, q_hm, kaug, vsb, kwt, vwb, kct, vco, gates, sza, sra, gpb, x2d, prm):
    nq = S // TQ
    n_sel = S // SEL_BLOCK
    n_cmp_pad = kct.shape[2]
    row = lambda width: pl.BlockSpec((TQ, width), lambda b, i: (b * nq + i, 0))
    const = lambda *shape: pl.BlockSpec(shape, lambda b, i: (0,) * len(shape))
    batch = lambda *shape: pl.BlockSpec((1,) + shape, lambda b, i: (b,) + (0,) * len(shape))
    in_specs = [
        pl.BlockSpec((N_HEADS, TQ, LANES), lambda b, i: (0, b * nq + i, 0)),
        batch(2 * LANES, S),
        pl.BlockSpec((KV_HEADS, S, LANES), lambda b, i: (0, b, 0)),
        batch(LANES, S),
        pl.BlockSpec((KV_HEADS, S, LANES), lambda b, i: (0, b, 0)),
        batch(LANES, n_cmp_pad),
        batch(n_cmp_pad, 2 * LANES),
        const(WINDOW // TQ + 1, TQ, WIN_KEYS), const(HU * TQ, TQ),
        row(LANES), row(ATTN_WIDTH), row(D_MODEL), row(D_MODEL), row(D_MODEL),
        const(ATTN_WIDTH, D_MODEL), const(D_MODEL, D_MODEL), const(1, D_MODEL),
    ]
    d = np.arange(WIN_KEYS)[None, :] - np.arange(TQ)[:, None]
    band = np.stack([np.where((d <= min(WINDOW, v * TQ)) & (d > min(WINDOW, v * TQ) - WINDOW), 0.0, NEG)
                     for v in range(WINDOW // TQ + 1)]).astype(np.float32)
    rowhot = np.tile(np.eye(TQ, dtype=np.float32), (HU, 1))
    return pl.pallas_call(
        functools.partial(_attn_prompt_body, n_sel),
        grid=(B, nq),
        in_specs=in_specs,
        out_specs=row(D_MODEL),
        out_shape=jax.ShapeDtypeStruct((B * S, D_MODEL), F32),
        compiler_params=pltpu.CompilerParams(dimension_semantics=("arbitrary", "arbitrary"),
                                             vmem_limit_bytes=VMEM_LIMIT),
        name="attn_prompt",
    )(q_hm, kaug, vsb, kwt, vwb, kct, vco, jnp.asarray(band, BF16), jnp.asarray(rowhot, BF16), gates, sza, sra, gpb,
      x2d, prm["w_pa"], prm["w_o"], prm["final_g"])


def _attn_sample_body(n_pages, tn, pt_ref, pk_ref, pv_ref, qa_ref, kc_ref, vc_ref, ovl_ref, oh_ref, kt_ref, vt_ref,
                      ckw_ref, cvw_ref, kwn_ref, vwn_ref, kwt_ref, vwt_ref, o_ref, okw_ref, ovw_ref,
                      kb0, kb1, vb0, vb1, sem):
    s_id = pl.program_id(0)
    past = n_pages * PAGE_SIZE
    n_past_blk = past // SEL_BLOCK
    n_blk_pad = ovl_ref.shape[1]
    R = N_HEADS * TOK_PAD
    GR = HPG * TOK_PAD
    wc = ckw_ref.shape[2]
    bufs = ((kb0, vb0, 0), (kb1, vb1, 1))
    t_row = _iota((R, 1), 0) & (TOK_PAD - 1)

    def start(b, slot):
        kb, vb, si = bufs[slot]
        _start_pages(pt_ref, (pk_ref, pv_ref), (kb, vb), (sem.at[2 * si], sem.at[2 * si + 1]), b, n_pages)

    def finish(slot):
        kb, vb, si = bufs[slot]
        qa = qa_ref[slot]

        n_cmp_pad = kc_ref.shape[2]
        s = _dot(qa, kc_ref[slot])
        mk = _iota((R, n_cmp_pad), 1) < n_cmp_pad - 1
        s = jnp.where(mk, s, NEG)
        e = jnp.exp(s - jnp.max(s, axis=-1, keepdims=True))
        p = jnp.where(mk, e / jnp.sum(e, axis=-1, keepdims=True), 0.0).astype(BF16)
        o_cmp = _dot(p, vc_ref[slot])
        impc = _dot(p, ovl_ref[...])
        imp = jnp.concatenate(
            [impc[g * GR:g * GR + TOK_PAD] + impc[g * GR + TOK_PAD:g * GR + 2 * TOK_PAD]
             + impc[g * GR + 2 * TOK_PAD:g * GR + 3 * TOK_PAD] + impc[g * GR + 3 * TOK_PAD:g * GR + 4 * TOK_PAD]
             for g in range(KV_HEADS)], axis=0)
        nr = KV_HEADS * TOK_PAD
        blk = _iota((nr, n_blk_pad), 1)
        cur = (past + jnp.minimum(_iota((nr, n_blk_pad), 0) & (TOK_PAD - 1), tn - 1)) // SEL_BLOCK
        valid = blk <= cur
        forced = (valid & (blk > cur - N_LOCAL_BLOCKS)) | (blk < N_INIT_BLOCKS)
        score = jnp.where(forced, BIG, jnp.where(valid, imp, NEG))
        sel = _topk_mask(score, blk, n_past_blk + 1, 1)
        selneg = jnp.where(sel, 0.0, NEG)
        selneg = jnp.concatenate(
            [selneg[g * TOK_PAD:(g + 1) * TOK_PAD] for g in range(KV_HEADS) for _ in range(HPG)], axis=0)
        low = _iota((R, LANES), 1) < SEL_HALF
        qaug = []
        for hf in range(n_past_blk // SEL_HALF):
            slab = selneg[:, (hf // 2) * LANES:(hf // 2 + 1) * LANES]
            if hf % 2:
                slab = pltpu.roll(slab, SEL_HALF, 1)
            qaug.append(jnp.concatenate([qa, jnp.where(low, slab, 0.0).astype(BF16)], axis=1))

        _wait_pages((pk_ref, pv_ref), (kb, vb), (sem.at[2 * si], sem.at[2 * si + 1]), n_pages)

        pages_per_half = SEL_HALF * SEL_BLOCK // PAGE_SIZE

        def half(buf, hf):
            return jnp.concatenate(
                [buf[hf * pages_per_half + j] for j in range(pages_per_half)], axis=1).astype(BF16)

        col = _iota((R, LANES), 1)
        ss = [_dot(qaug[hf], jnp.concatenate([half(kb, hf), oh_ref[...]], axis=0)) for hf in range(len(qaug))]
        ss.append(jnp.where(col <= t_row, _nt(qa, kt_ref[slot]), NEG))
        m = functools.reduce(jnp.maximum, [jnp.max(s, axis=-1, keepdims=True) for s in ss])
        es = [jnp.exp(s - m) for s in ss]
        l = functools.reduce(jnp.add, [jnp.sum(e, axis=-1, keepdims=True) for e in es])
        acc = _dot(es[-1].astype(BF16), vt_ref[slot])
        for hf in range(len(qaug)):
            acc = acc + _nt(es[hf].astype(BF16), half(vb, hf))
        o_sel = acc / l

        sa = jnp.where(_iota((R, wc), 1) > t_row, _dot(qa, ckw_ref[slot].astype(BF16)), NEG)
        sb = jnp.where(col <= t_row, _nt(qa, kwn_ref[slot]), NEG)
        m = jnp.maximum(jnp.max(sa, axis=-1, keepdims=True), jnp.max(sb, axis=-1, keepdims=True))
        ea, eb = jnp.exp(sa - m), jnp.exp(sb - m)
        den = jnp.sum(ea, axis=-1, keepdims=True) + jnp.sum(eb, axis=-1, keepdims=True)
        o_win = (_nt(ea.astype(BF16), cvw_ref[slot].astype(BF16)) + _dot(eb.astype(BF16), vwn_ref[slot])) / den

        o_ref[slot, 0] = o_cmp
        o_ref[slot, 1] = o_sel
        o_ref[slot, 2] = o_win

        newest = _iota((LANES, LANES), 1) >= LANES - tn
        for src, new, dst in ((ckw_ref, kwt_ref, okw_ref), (cvw_ref, vwt_ref, ovw_ref)):
            shifted = pltpu.roll(src[slot], wc - tn, 1)
            dst[slot, :, 0:wc - LANES] = shifted[:, 0:wc - LANES]
            dst[slot, :, wc - LANES:wc] = jnp.where(newest, new[slot], shifted[:, wc - LANES:wc])

    @pl.when(s_id == 0)
    def _():
        start(0, 0)

    start(2 * s_id + 1, 1)
    finish(0)

    @pl.when(s_id + 1 < pl.num_programs(0))
    def _():
        start(2 * s_id + 2, 0)

    finish(1)


def _attn_sample(page_table, pool_k, pool_v, tn, qa, kct, vc, ovl, ktail, vtail, ckw, cvw, kwn, vwn, kwt, vwt):
    Bd, n_pages = page_table.shape
    R = N_HEADS * TOK_PAD
    wc = ckw.shape[2]
    const = lambda *shape: pl.BlockSpec(shape, lambda s, pt: (0,) * len(shape))
    pair = lambda *shape: pl.BlockSpec((2,) + shape, lambda s, pt: (s,) + (0,) * len(shape))
    anyspec = pl.BlockSpec(memory_space=pl.ANY)
    half_keys = SEL_HALF * SEL_BLOCK
    onehot = np.zeros((LANES, half_keys), np.float32)
    onehot[np.arange(half_keys) // SEL_BLOCK, np.arange(half_keys)] = 1.0
    grid_spec = pltpu.PrefetchScalarGridSpec(
        num_scalar_prefetch=1,
        grid=(Bd // 2,),
        in_specs=[anyspec, anyspec, pair(R, LANES), pair(LANES, kct.shape[2]), pair(vc.shape[1], LANES),
                  const(*ovl.shape), const(LANES, half_keys), pair(LANES, LANES), pair(LANES, LANES),
                  pair(LANES, wc), pair(LANES, wc), pair(LANES, LANES), pair(LANES, LANES),
                  pair(LANES, LANES), pair(LANES, LANES)],
        out_specs=[pair(3, R, LANES), pair(LANES, wc), pair(LANES, wc)],
        scratch_shapes=[pltpu.VMEM((n_pages, LANES, PAGE_SIZE), F32)] * 4 + [pltpu.SemaphoreType.DMA((4,))],
    )
    return pl.pallas_call(
        functools.partial(_attn_sample_body, n_pages, tn),
        grid_spec=grid_spec,
        out_shape=[jax.ShapeDtypeStruct((Bd, 3, R, LANES), F32)] + [jax.ShapeDtypeStruct((Bd, LANES, wc), F32)] * 2,
        compiler_params=pltpu.CompilerParams(dimension_semantics=("arbitrary",), vmem_limit_bytes=VMEM_LIMIT),
        name="attn_sample",
    )(page_table, pool_k, pool_v, qa, kct, vc, ovl, jnp.asarray(onehot, BF16), ktail, vtail, ckw, cvw, kwn, vwn,
      kwt, vwt)


def _mixer_sample_body(o3_ref, g3_ref, sza_ref, sra_ref, gpb_ref, x_ref, wpa_ref, wo_ref, fg_ref, y_ref):
    o = g3_ref[0] * o3_ref[0] + g3_ref[1] * o3_ref[1] + g3_ref[2] * o3_ref[2]
    y_ref[...] = _mixer_tail(o, sza_ref[...], sra_ref[...], gpb_ref[...], x_ref[...], wpa_ref, wo_ref, fg_ref)


def _mixer_sample(o3, g3, sza, sra, gpb, x2d, prm):
    T = x2d.shape[0]
    full = lambda *shape: pl.BlockSpec(shape, lambda i: (0,) * len(shape))
    return pl.pallas_call(
        _mixer_sample_body,
        grid=(1,),
        in_specs=[full(3, T, ATTN_WIDTH), full(3, T, ATTN_WIDTH), full(T, ATTN_WIDTH), full(T, D_MODEL),
                  full(T, D_MODEL), full(T, D_MODEL), full(ATTN_WIDTH, D_MODEL), full(D_MODEL, D_MODEL),
                  full(1, D_MODEL)],
        out_specs=full(T, D_MODEL),
        out_shape=jax.ShapeDtypeStruct((T, D_MODEL), F32),
        compiler_params=pltpu.CompilerParams(dimension_semantics=("arbitrary",), vmem_limit_bytes=VMEM_LIMIT),
        name="mixer_sample",
    )(o3, g3, sza, sra, gpb, x2d, prm["w_pa"], prm["w_o"], prm["final_g"])


def _overlap(n_cmp, n_sel):
    cs = np.arange(n_cmp)[:, None] * CMP_STRIDE
    ss = np.arange(n_sel)[None, :] * SEL_BLOCK
    ov = np.minimum(cs + CMP_BLOCK, ss + SEL_BLOCK) - np.maximum(cs, ss)
    return np.clip(ov, 0, None).astype(np.float32) / CMP_BLOCK


def _cmp_weights(w1, b1, w2, b2):
    w1r = w1.reshape(2, CMP_STRIDE, HEAD_DIM, CMP_HIDDEN).transpose(1, 2, 0, 3)
    w1r = w1r.reshape(CMP_STRIDE, 1, HEAD_DIM, 2 * CMP_HIDDEN).astype(BF16)
    w2r = w2.reshape(1, CMP_HIDDEN, HEAD_DIM).astype(BF16)
    big = jnp.concatenate(
        [jnp.pad(w1r, ((0, 0), (0, 0), (0, 0), (g * 2 * CMP_HIDDEN, (KV_HEADS - 1 - g) * 2 * CMP_HIDDEN)))
         for g in range(KV_HEADS)], axis=1)
    w2b = jnp.concatenate(
        [jnp.pad(w2r, ((0, 0), (0, 0), (g * HEAD_DIM, (KV_HEADS - 1 - g) * HEAD_DIM))) for g in range(KV_HEADS)],
        axis=0)
    w2b = w2b.reshape(KV_HEADS * CMP_HIDDEN, LANES)
    b1t, b2t = jnp.tile(b1, KV_HEADS)[None], jnp.tile(b2, KV_HEADS)[None]
    return ((big.reshape(CMP_STRIDE * LANES, 4 * LANES), b1t, w2b, b2t),
            (w1r.reshape(CMP_STRIDE * HEAD_DIM, 2 * CMP_HIDDEN), b1t, w2b, b2t))


def _seq_minor(t):
    lead = t.shape[:-3]
    n = len(lead)
    return t.transpose(*range(n), n + 1, n + 2, n).reshape(*lead, KV_WIDTH, t.shape[-3])


def _seq_major(t):
    lead = t.shape[:-2]
    n = len(lead)
    return t.reshape(*lead, KV_HEADS, HEAD_DIM, t.shape[-1]).transpose(*range(n), n + 2, n, n + 1)


def kernel(x_prompt, x_sample, cache_k_cmp, cache_v_cmp, cache_k_sel, cache_v_sel, cache_k_win, cache_v_win, page_table, norm_g, w_in, cmp_k_w1, cmp_k_b1, cmp_k_w2, cmp_k_b2, cmp_v_w1, cmp_v_b1, cmp_v_w2, cmp_v_b2, v_norm_g, v_norm_b, w_spatial, b_spatial, w_pa, w_pb, w_o, final_g):
    B, S, _ = x_prompt.shape
    Bd, tn, _ = x_sample.shape
    depth = w_in.shape[0]
    assert depth == 1, "single-layer step"
    assert Bd * tn == CHUNK, "the sample tokens form one 128-row tile"
    n_pages = page_table.shape[1]
    past = n_pages * PAGE_SIZE

    split = OFF_GL + 3 * N_HEADS
    w = w_in[0]
    prm = {
        "norm_g": norm_g,
        "w_in": jnp.concatenate([w.T[:split], jnp.zeros((GL_PAD, D_MODEL), F32), w.T[split:]], axis=0).astype(BF16),
        "w_sp": w_spatial[0],
        "b_sp": jnp.repeat(b_spatial[0].T, LANES, axis=1),
        "w_sp_s": jnp.stack([jnp.kron(jnp.eye(CHUNK // tn, dtype=F32), w_spatial[0, g, :tn, :tn])
                             for g in range(GMLP_GROUPS)]),
        "b_sp_s": jnp.tile(jnp.repeat(b_spatial[0, :, :tn].T, LANES, axis=1), (CHUNK // tn, 1)),
        "v_norm_g": v_norm_g, "v_norm_b": v_norm_b,
        "w_pb": w_pb[0].astype(BF16), "w_pa": w_pa[0].astype(BF16), "w_o": w_o[0].astype(BF16),
        "final_g": final_g[None],
    }
    prm["cmp_k"], prm["cmp_k_s"] = _cmp_weights(cmp_k_w1[0], cmp_k_b1[0], cmp_k_w2[0], cmp_k_b2[0])
    prm["cmp_v"], prm["cmp_v_s"] = _cmp_weights(cmp_v_w1[0], cmp_v_b1[0], cmp_v_w2[0], cmp_v_b2[0])

    xp = x_prompt.reshape(B * S, D_MODEL)
    (q_hm, p_kcmp, p_vcmp, p_ksel, p_vsel, p_kwin, p_vwin, kcr, vcr, kaug, kwt, vsb, vwb,
     gates, sza, sra, gpb) = _in_project(xp, _rope_tables(np.arange(S)), prm, TM_PROMPT, False, S)
    n_half = S // CMP_STRIDE
    n_cmp = (S - CMP_BLOCK) // CMP_STRIDE + 1
    n_sel = S // SEL_BLOCK
    assert n_sel <= SEL_HALF
    ovl = np.zeros((n_half, LANES), np.float32)
    ovl[:n_cmp, HEAD_DIM:HEAD_DIM + n_sel] = _overlap(n_cmp, n_sel)
    cend_tables = _rope_tables(np.arange(n_half) * CMP_STRIDE + CMP_BLOCK - 1)
    kct, vco = _compress_prompt(kcr.reshape(B, S, LANES), vcr.reshape(B, S, LANES), prm, cend_tables,
                                jnp.asarray(ovl, BF16))
    y_prompt = _attn_prompt(B, S, q_hm, kaug, vsb, kwt, vwb, kct, vco, gates, sza, sra, gpb, xp, prm)

    xs = x_sample.reshape(Bd * tn, D_MODEL)
    pos_s = np.tile(past + np.arange(tn), Bd)
    (q_s, s_kcmp, s_vcmp, s_ksel, s_vsel, s_kwin, s_vwin, gates_s, sza_s, sra_s, gpb_s, vn_s) = _in_project(
        xs, _rope_tables(pos_s), prm, Bd * tn, True, tn)
    n_half_s = past // CMP_STRIDE
    pools = [_seq_minor(c[0]) for c in (cache_k_cmp, cache_v_cmp, cache_k_sel, cache_v_sel)]
    cend_s = _rope_tables(np.arange(n_half_s) * CMP_STRIDE + CMP_BLOCK - 1)
    kct_s, vc_s = _compress_sample(page_table, pools[0], pools[1], prm, cend_s)

    n_cmp_s = (past + tn - CMP_BLOCK) // CMP_STRIDE + 1
    n_blk_s = past // SEL_BLOCK + -(-tn // SEL_BLOCK)
    assert n_cmp_s == n_half_s - 1 and tn <= TOK_PAD and (past // SEL_BLOCK) % SEL_HALF == 0
    ovl_s = np.zeros((n_half_s, 2 * LANES), np.float32)
    ovl_s[:n_cmp_s, :n_blk_s] = _overlap(n_cmp_s, n_blk_s)
    q5 = q_s.reshape(Bd, tn, KV_HEADS, HPG, HEAD_DIM).transpose(0, 2, 3, 1, 4)
    q5 = jnp.pad(q5, ((0, 0), (0, 0), (0, 0), (0, TOK_PAD - tn), (0, 0))).reshape(Bd, KV_HEADS, HPG * TOK_PAD, HEAD_DIM)
    qa = jnp.concatenate(
        [jnp.pad(q5[:, g], ((0, 0), (0, 0), (g * HEAD_DIM, (KV_HEADS - 1 - g) * HEAD_DIM))) for g in range(KV_HEADS)],
        axis=1).astype(BF16)
    pad_rows = lambda t: jnp.pad(t.reshape(Bd, tn, LANES), ((0, 0), (0, LANES - tn), (0, 0))).astype(BF16)
    ckw = _seq_minor(cache_k_win[0])
    cvw = _seq_minor(cache_v_win[0])
    new_t = lambda t: jnp.pad(t.reshape(Bd, tn, LANES).transpose(0, 2, 1), ((0, 0), (0, 0), (LANES - tn, 0)))
    o3, s_k_win, s_v_win = _attn_sample(
        page_table, pools[2], pools[3], tn, qa, kct_s, vc_s, jnp.asarray(ovl_s, BF16),
        pad_rows(s_ksel), pad_rows(s_vsel), ckw, cvw, pad_rows(s_kwin), pad_rows(s_vwin), new_t(s_kwin), new_t(s_vwin))
    s_k_win = _seq_major(s_k_win)[None]
    s_v_win = _seq_major(s_v_win)[None]
    first_group = jnp.arange(N_HEADS * TOK_PAD)[:, None] < HPG * TOK_PAD
    o3r = jnp.where(first_group, o3[..., :HEAD_DIM], o3[..., HEAD_DIM:])
    o3r = o3r.reshape(Bd, 3, N_HEADS, TOK_PAD, HEAD_DIM)[:, :, :, :tn]
    o3r = o3r.transpose(1, 0, 3, 2, 4).reshape(3, Bd * tn, ATTN_WIDTH)
    g3 = gates_s[:, :3 * N_HEADS].reshape(Bd * tn, N_HEADS, 3).transpose(2, 0, 1)
    g3 = jnp.repeat(g3, HEAD_DIM, axis=2)
    y_sample = _mixer_sample(o3r, g3, sza_s, sra_s, gpb_s, xs, prm)

    kv5 = lambda t, b, n: t.reshape(1, b, n, KV_HEADS, HEAD_DIM)
    pw = min(WINDOW, S)
    return (y_prompt.reshape(B, S, D_MODEL), y_sample.reshape(Bd, tn, D_MODEL),
            _seq_major(p_kcmp)[None], _seq_major(p_vcmp)[None], _seq_major(p_ksel)[None], _seq_major(p_vsel)[None],
            _seq_major(p_kwin[:, :, S - pw:])[None], _seq_major(p_vwin[:, :, S - pw:])[None],
            kv5(s_kcmp, Bd, tn), kv5(s_vcmp, Bd, tn), kv5(s_ksel, Bd, tn), kv5(s_vsel, Bd, tn),
            s_k_win, s_v_win, vn_s.reshape(1, Bd, tn, GMLP_WIDTH))
```

```python
import functools

import numpy as np
import jax
import jax.numpy as jnp
from jax import lax
from jax.experimental import pallas as pl
from jax.experimental.pallas import tpu as pltpu

F32 = jnp.float32
BF16 = jnp.bfloat16

D_MODEL = 1024
HEAD_DIM = 64
N_HEADS = 8
KV_HEADS = 2
HPG = N_HEADS // KV_HEADS
ATTN_WIDTH = N_HEADS * HEAD_DIM
KV_WIDTH = KV_HEADS * HEAD_DIM
ROT_DIM = HEAD_DIM // 4
ROT_HALF = ROT_DIM // 2
ROPE_THETA = 500000.0
CMP_BLOCK = 32
CMP_STRIDE = 16
CMP_HIDDEN = 128
SEL_BLOCK = 64
TOP_N = 16
N_INIT_BLOCKS = 1
N_LOCAL_BLOCKS = 2
WINDOW = 512
CHUNK = 128
GMLP_GROUPS = 4
GMLP_WIDTH = 512
PAGE_SIZE = 128
NEG = -1e30
BIG = 1e30
EPS = 1e-6
LOG2E = 1.4426950408889634

LANES = 128
GL_PAD = LANES - 3 * N_HEADS

OFF_Q = 0
OFF_KV = ATTN_WIDTH
OFF_GL = OFF_KV + 6 * KV_WIDTH
OFF_ZA = OFF_GL + LANES
OFF_U = OFF_ZA + ATTN_WIDTH
OFF_V = OFF_U + GMLP_WIDTH
OFF_ZB = OFF_V + GMLP_WIDTH
OFF_RA = OFF_ZB + GMLP_WIDTH
OFF_RB = OFF_RA + D_MODEL
W_TOT = OFF_RB + D_MODEL

VMEM_LIMIT = 56 * 1024 * 1024

TM_PROMPT = 512
TQ = 256
HU = 2
KT = 512
WIN_KEYS = WINDOW + TQ
SEL_HALF = 64
TOK_PAD = 8
RANK_GROUP = 8


def _nt(a, b):
    return lax.dot_general(a, b, (((1,), (1,)), ((), ())), preferred_element_type=F32)


def _dot(a, b):
    return jnp.dot(a, b, preferred_element_type=F32)


def _iota(shape, dim):
    return lax.broadcasted_iota(jnp.int32, shape, dim)


def _rope(slab, cos, s1, s2):
    return slab * cos + pltpu.roll(slab, LANES - ROT_HALF, 1) * s1 + pltpu.roll(slab, ROT_HALF, 1) * s2


def _rope_tables(pos):
    pos = np.asarray(pos, np.float64)
    n = pos.shape[0]
    inv = np.power(np.float64(ROPE_THETA), -np.arange(0, ROT_DIM, 2, dtype=np.float64) / ROT_DIM)
    ang = pos[:, None] * inv[None, :]
    cos, sin = np.cos(ang), np.sin(ang)
    rest = HEAD_DIM - ROT_DIM
    c = np.concatenate([cos, cos, np.ones((n, rest))], axis=1)
    s1 = np.concatenate([-sin, np.zeros((n, HEAD_DIM - ROT_HALF))], axis=1)
    s2 = np.concatenate([np.zeros((n, ROT_HALF)), sin, np.zeros((n, rest))], axis=1)
    return tuple(jnp.asarray(np.tile(t, (1, LANES // HEAD_DIM)), F32) for t in (c, s1, s2))


def _sigmoid(x):
    return 1.0 / (1.0 + jnp.exp(-x))


def _gelu(x):
    return jax.nn.gelu(x, approximate=True)


def _inproj_body(tm, sample, pos_tiles, x_ref, ng_ref, w_ref, cos_ref, s1_ref, s2_ref, wsp_ref, bsp_ref,
                 vng_ref, vnb_ref, wpb_ref, *outs):
    if sample:
        (q_ref, kcmp_ref, vcmp_ref, ksel_ref, vsel_ref, kwin_ref, vwin_ref,
         gat_ref, sza_ref, sra_ref, gpb_ref, vn_ref) = outs
    else:
        (q_ref, kcmp_ref, vcmp_ref, ksel_ref, vsel_ref, kwin_ref, vwin_ref,
         kcr_ref, vcr_ref, kaug_ref, kwt_ref, vsb_ref, vwb_ref, gat_ref, sza_ref, sra_ref, gpb_ref) = outs

    x = x_ref[...]
    r = lax.rsqrt(jnp.mean(x * x, axis=-1, keepdims=True) + EPS)
    h = ((x * r) * ng_ref[...]).astype(BF16)

    def proj(lo, hi):
        return _nt(h, w_ref[lo:hi, :])

    cos, s1, s2 = cos_ref[...], s1_ref[...], s2_ref[...]
    low = _iota((tm, LANES), 1) < HEAD_DIM

    head = proj(OFF_Q, OFF_ZA)
    mid = proj(OFF_ZA, OFF_RA)
    q = head[:, OFF_Q:OFF_Q + ATTN_WIDTH]
    q_scale = HEAD_DIM ** -0.5 if sample else HEAD_DIM ** -0.5 * LOG2E
    for pp in range(N_HEADS // 2):
        slab = _rope(q[:, pp * LANES:(pp + 1) * LANES], cos, s1, s2) * q_scale
        if sample:
            q_ref[:, pp * LANES:(pp + 1) * LANES] = slab
        elif (2 * pp) // HPG == 0:
            q_ref[2 * pp] = jnp.where(low, slab, 0.0).astype(BF16)
            q_ref[2 * pp + 1] = jnp.where(low, pltpu.roll(slab, HEAD_DIM, 1), 0.0).astype(BF16)
        else:
            q_ref[2 * pp] = jnp.where(low, 0.0, pltpu.roll(slab, HEAD_DIM, 1)).astype(BF16)
            q_ref[2 * pp + 1] = jnp.where(low, 0.0, slab).astype(BF16)

    kv = head[:, OFF_KV:OFF_GL]
    kcmp = kv[:, 0:LANES]
    vcmp = kv[:, LANES:2 * LANES]
    ksel = _rope(kv[:, 2 * LANES:3 * LANES], cos, s1, s2)
    vsel = kv[:, 3 * LANES:4 * LANES]
    kwin = _rope(kv[:, 4 * LANES:5 * LANES], cos, s1, s2)
    vwin = kv[:, 5 * LANES:6 * LANES]
    if sample:
        for ref, val in ((kcmp_ref, kcmp), (vcmp_ref, vcmp), (ksel_ref, ksel), (vsel_ref, vsel),
                         (kwin_ref, kwin), (vwin_ref, vwin)):
            ref[...] = val
    else:
        ksel_t = ksel.T
        kwin_t = kwin.T
        for ref, val in ((kcmp_ref, kcmp.T), (vcmp_ref, vcmp.T), (ksel_ref, ksel_t), (vsel_ref, vsel.T),
                         (kwin_ref, kwin_t), (vwin_ref, vwin.T)):
            ref[0] = val
        kcr_ref[...] = kcmp
        vcr_ref[...] = vcmp
        base = (pl.program_id(0) % pos_tiles) * tm
        blk = (base + _iota((LANES, tm), 1)) // SEL_BLOCK
        onehot = jnp.where(_iota((LANES, tm), 0) == blk, 1.0, 0.0)
        kaug_ref[0, 0:LANES] = ksel_t.astype(BF16)
        kaug_ref[0, LANES:2 * LANES] = onehot.astype(BF16)
        kwt_ref[0] = kwin_t.astype(BF16)
        for ref, val in ((vsb_ref, vsel), (vwb_ref, vwin)):
            ref[0] = jnp.where(low, val, 1.0).astype(BF16)
            ref[1] = jnp.where(low, 1.0, val).astype(BF16)

    gat_ref[...] = _sigmoid(head[:, OFF_GL:OFF_ZA])
    za = mid[:, 0:ATTN_WIDTH]
    sza_ref[...] = za * _sigmoid(za)
    sra_ref[...] = _sigmoid(proj(OFF_RA, OFF_RA + D_MODEL))

    v = mid[:, OFF_V - OFF_ZA:OFF_ZB - OFF_ZA]
    gv = _gelu(v)
    mu = jnp.mean(gv, axis=-1, keepdims=True)
    var = jnp.mean(jnp.square(gv - mu), axis=-1, keepdims=True)
    vn = ((gv - mu) * lax.rsqrt(var + EPS)) * vng_ref[...] + vnb_ref[...]
    if sample:
        vn_ref[...] = vn
    vnb16 = vn.astype(BF16)
    n_chunk = tm // CHUNK
    tri = _iota((CHUNK, CHUNK), 0) >= _iota((CHUNK, CHUNK), 1)
    mixed = []
    for g in range(GMLP_GROUPS):
        wm = jnp.where(tri, wsp_ref[g], 0.0).astype(BF16)
        cat = jnp.concatenate(
            [vnb16[c * CHUNK:(c + 1) * CHUNK, g * LANES:(g + 1) * LANES] for c in range(n_chunk)], axis=1)
        mixed.append(_dot(wm, cat))
    bsp = bsp_ref[...]
    sg = jnp.concatenate(
        [jnp.concatenate([mixed[g][:, c * LANES:(c + 1) * LANES] for g in range(GMLP_GROUPS)], axis=1) + bsp
         for c in range(n_chunk)], axis=0)
    u = mid[:, OFF_U - OFF_ZA:OFF_V - OFF_ZA]
    zb = mid[:, OFF_ZB - OFF_ZA:OFF_RA - OFF_ZA]
    t = (_gelu(u) * sg) * (zb * _sigmoid(zb))
    pb = _dot(t.astype(BF16), wpb_ref[...])
    gpb_ref[...] = _sigmoid(proj(OFF_RB, OFF_RB + D_MODEL)) * pb


def _in_project(x2d, tables, prm, tm, sample, seq):
    T = x2d.shape[0]
    nt = T // tm
    pos_tiles = tables[0].shape[0] // tm
    const = lambda *shape: pl.BlockSpec(shape, lambda i: (0,) * len(shape))
    row = lambda width: pl.BlockSpec((tm, width), lambda i: (i, 0))
    tab = pl.BlockSpec((tm, LANES), lambda i: (i % pos_tiles, 0))
    in_specs = [
        row(D_MODEL), const(1, D_MODEL),
        pl.BlockSpec((W_TOT, D_MODEL), lambda i: (0, 0), pipeline_mode=pl.Buffered(1)),
        tab, tab, tab,
        const(GMLP_GROUPS, CHUNK, CHUNK), const(CHUNK, GMLP_WIDTH), const(1, GMLP_WIDTH), const(1, GMLP_WIDTH),
        const(GMLP_WIDTH, D_MODEL),
    ]
    f32rows = lambda width: jax.ShapeDtypeStruct((T, width), F32)
    tail_shapes = [f32rows(LANES), f32rows(ATTN_WIDTH), f32rows(D_MODEL), f32rows(D_MODEL)]
    tail_specs = [row(LANES), row(ATTN_WIDTH), row(D_MODEL), row(D_MODEL)]
    if sample:
        out_shape = [f32rows(ATTN_WIDTH)] + [f32rows(KV_WIDTH)] * 6 + tail_shapes + [f32rows(GMLP_WIDTH)]
        out_specs = [row(ATTN_WIDTH)] + [row(KV_WIDTH)] * 6 + tail_specs + [row(GMLP_WIDTH)]
    else:
        nb = T // seq
        bf = lambda *shape: jax.ShapeDtypeStruct(shape, BF16)
        tposed = lambda rows: pl.BlockSpec((1, rows, tm), lambda i: (i // pos_tiles, 0, i % pos_tiles))
        out_shape = ([bf(N_HEADS, T, LANES)] + [jax.ShapeDtypeStruct((nb, KV_WIDTH, seq), F32)] * 6
                     + [f32rows(KV_WIDTH)] * 2
                     + [bf(nb, 2 * LANES, seq), bf(nb, LANES, seq), bf(KV_HEADS, T, LANES), bf(KV_HEADS, T, LANES)]
                     + tail_shapes)
        heads = lambda n: pl.BlockSpec((n, tm, LANES), lambda i: (0, i, 0))
        out_specs = ([heads(N_HEADS)] + [tposed(KV_WIDTH)] * 6 + [row(KV_WIDTH)] * 2
                     + [tposed(2 * LANES), tposed(LANES), heads(KV_HEADS), heads(KV_HEADS)] + tail_specs)
    return pl.pallas_call(
        functools.partial(_inproj_body, tm, sample, pos_tiles),
        grid=(nt,),
        in_specs=in_specs,
        out_specs=out_specs,
        out_shape=out_shape,
        compiler_params=pltpu.CompilerParams(dimension_semantics=("arbitrary",), vmem_limit_bytes=VMEM_LIMIT),
        name="in_project_sample" if sample else "in_project_prompt",
    )(x2d, prm["norm_g"], prm["w_in"], *tables, prm["w_sp_s" if sample else "w_sp"],
      prm["b_sp_s" if sample else "b_sp"], prm["v_norm_g"], prm["v_norm_b"], prm["w_pb"])


def _strided_halfblocks(load_rows):
    return jnp.concatenate([load_rows(t) for t in range(CMP_STRIDE)], axis=1).astype(BF16)


def _compress_mlp(xcat, n, w1_ref, b1_ref, w2_ref, b2_ref):
    hh = _dot(xcat, w1_ref[...])
    hid = jnp.concatenate(
        [hh[:, 0:LANES] + pltpu.roll(hh[:, LANES:2 * LANES], n - 1, 0),
         hh[:, 2 * LANES:3 * LANES] + pltpu.roll(hh[:, 3 * LANES:4 * LANES], n - 1, 0)], axis=1) + b1_ref[...]
    return _dot(_gelu(hid).astype(BF16), w2_ref[...]) + b2_ref[...]


def _compress_mlp_grouped(x_ref, n, w1_ref, b1_ref, w2_ref, b2_ref):
    hid = []
    for g in range(KV_HEADS):
        hh = _dot(x_ref[g], w1_ref[...])
        hid.append(hh[:, 0:LANES] + pltpu.roll(hh[:, LANES:2 * LANES], n - 1, 0))
    hid = jnp.concatenate(hid, axis=1) + b1_ref[...]
    return _dot(_gelu(hid).astype(BF16), w2_ref[...]) + b2_ref[...]


def _compress_prompt_body(n, kr_ref, vr_ref, w1k, b1k, w2k, b2k, w1v, b1v, w2v, b2v, cos_ref, s1_ref, s2_ref,
                          ovl_ref, kct_ref, vco_ref):
    kc = _compress_mlp(_strided_halfblocks(lambda t: kr_ref[0, pl.ds(t, n, stride=CMP_STRIDE), :]),
                       n, w1k, b1k, w2k, b2k)
    kct_ref[0] = _rope(kc, cos_ref[...], s1_ref[...], s2_ref[...]).T.astype(BF16)
    vc = _compress_mlp(_strided_halfblocks(lambda t: vr_ref[0, pl.ds(t, n, stride=CMP_STRIDE), :]),
                       n, w1v, b1v, w2v, b2v)
    vco_ref[0, :, 0:LANES] = vc.astype(BF16)
    vco_ref[0, :, LANES:2 * LANES] = ovl_ref[...]


def _compress_prompt(kr, vr, prm, tables, ovl):
    B, S, _ = kr.shape
    n = S // CMP_STRIDE
    const = lambda *shape: pl.BlockSpec(shape, lambda b: (0,) * len(shape))
    rows = pl.BlockSpec((1, S, LANES), lambda b: (b, 0, 0))
    wspecs = [const(CMP_STRIDE * LANES, 4 * LANES), const(1, 2 * LANES), const(2 * LANES, LANES), const(1, LANES)]
    return pl.pallas_call(
        functools.partial(_compress_prompt_body, n),
        grid=(B,),
        in_specs=[rows, rows] + wspecs + wspecs + [const(n, LANES)] * 3 + [const(n, LANES)],
        out_specs=[pl.BlockSpec((1, LANES, n), lambda b: (b, 0, 0)),
                   pl.BlockSpec((1, n, 2 * LANES), lambda b: (b, 0, 0))],
        out_shape=[jax.ShapeDtypeStruct((B, LANES, n), BF16),
                   jax.ShapeDtypeStruct((B, n, 2 * LANES), BF16)],
        compiler_params=pltpu.CompilerParams(dimension_semantics=("arbitrary",), vmem_limit_bytes=VMEM_LIMIT),
        name="compress_prompt",
    )(kr, vr, *prm["cmp_k"], *prm["cmp_v"], *tables, ovl)


def _start_pages(pt_ref, pools, bufs, sems, b, n_pages):
    def body(p, carry):
        page = pt_ref[b, p]
        for i, (pool, buf, sem) in enumerate(zip(pools, bufs, sems)):
            pltpu.make_async_copy(pool.at[page], buf.at[p], sem).start(priority=i % 2)
        return carry
    lax.fori_loop(0, n_pages, body, 0)


def _wait_pages(pools, bufs, sems, n_pages):
    for pool, buf, sem in zip(pools, bufs, sems):
        pltpu.make_async_copy(pool.at[pl.ds(0, n_pages)], buf, sem).wait()


def _compress_sample_body(n, n_pages, pt_ref, pk_ref, pv_ref, w1k, b1k, w2k, b2k, w1v, b1v, w2v, b2v,
                          cos_ref, s1_ref, s2_ref, perm_ref, kct_ref, vc_ref, kb0, kb1, vb0, vb1, xk, xv, sem):
    s = pl.program_id(0)
    bufs = ((kb0, vb0, 0), (kb1, vb1, 1))

    def start(b, slot):
        kb, vb, si = bufs[slot]
        _start_pages(pt_ref, (pk_ref, pv_ref), (kb, vb), (sem.at[2 * si], sem.at[2 * si + 1]), b, n_pages)

    hb_per_page = PAGE_SIZE // CMP_STRIDE

    def halfblocks(buf, xcat):
        pairs = jnp.concatenate([buf[pl.ds(0, n_pages // 2, stride=2)], buf[pl.ds(1, n_pages // 2, stride=2)]], axis=2)
        y = _dot(pairs.reshape(n_pages // 2 * LANES, 2 * PAGE_SIZE).astype(BF16), perm_ref[...]).astype(BF16)
        low = _iota((2 * hb_per_page, LANES), 1) < HEAD_DIM
        for i2 in range(n_pages // 2):
            z = y[i2 * LANES:(i2 + 1) * LANES].T
            rows = slice(i2 * 2 * hb_per_page, (i2 + 1) * 2 * hb_per_page)
            for u in range(CMP_STRIDE // 2):
                za = z[2 * u * 2 * hb_per_page:(2 * u + 1) * 2 * hb_per_page]
                zb = z[(2 * u + 1) * 2 * hb_per_page:(2 * u + 2) * 2 * hb_per_page]
                xcat[0, rows, u * LANES:(u + 1) * LANES] = jnp.where(low, za, pltpu.roll(zb, HEAD_DIM, 1))
                xcat[1, rows, u * LANES:(u + 1) * LANES] = jnp.where(low, pltpu.roll(za, HEAD_DIM, 1), zb)

    def finish(slot):
        kb, vb, si = bufs[slot]
        _wait_pages((pk_ref, pv_ref), (kb, vb), (sem.at[2 * si], sem.at[2 * si + 1]), n_pages)
        halfblocks(kb, xk)
        halfblocks(vb, xv)
        kc = _compress_mlp_grouped(xk, n, w1k, b1k, w2k, b2k)
        kct_ref[slot] = _rope(kc, cos_ref[...], s1_ref[...], s2_ref[...]).T.astype(BF16)
        vc_ref[slot] = _compress_mlp_grouped(xv, n, w1v, b1v, w2v, b2v).astype(BF16)

    @pl.when(s == 0)
    def _():
        start(0, 0)

    start(2 * s + 1, 1)
    finish(0)

    @pl.when(s + 1 < pl.num_programs(0))
    def _():
        start(2 * s + 2, 0)

    finish(1)


def _compress_sample(page_table, pool_k, pool_v, prm, tables):
    Bd, n_pages = page_table.shape
    past = n_pages * PAGE_SIZE
    n = past // CMP_STRIDE
    const = lambda *shape: pl.BlockSpec(shape, lambda s, pt: (0,) * len(shape))
    wspecs = [const(CMP_STRIDE * HEAD_DIM, 2 * LANES), const(1, 2 * LANES), const(2 * LANES, LANES), const(1, LANES)]
    anyspec = pl.BlockSpec(memory_space=pl.ANY)
    grid_spec = pltpu.PrefetchScalarGridSpec(
        num_scalar_prefetch=1,
        grid=(Bd // 2,),
        in_specs=[anyspec, anyspec] + wspecs + wspecs + [const(n, LANES)] * 3 + [const(2 * PAGE_SIZE, 2 * PAGE_SIZE)],
        out_specs=[pl.BlockSpec((2, LANES, n), lambda s, pt: (s, 0, 0)),
                   pl.BlockSpec((2, n, LANES), lambda s, pt: (s, 0, 0))],
        scratch_shapes=[pltpu.VMEM((n_pages, LANES, PAGE_SIZE), F32)] * 4
        + [pltpu.VMEM((KV_HEADS, n, CMP_STRIDE * HEAD_DIM), BF16)] * 2 + [pltpu.SemaphoreType.DMA((4,))],
    )
    hb = PAGE_SIZE // CMP_STRIDE
    perm = np.zeros((2 * PAGE_SIZE, 2 * PAGE_SIZE), np.float32)
    for side in range(2):
        for jj in range(hb):
            for t in range(CMP_STRIDE):
                perm[side * PAGE_SIZE + CMP_STRIDE * jj + t, t * 2 * hb + side * hb + jj] = 1.0
    return pl.pallas_call(
        functools.partial(_compress_sample_body, n, n_pages),
        grid_spec=grid_spec,
        out_shape=[jax.ShapeDtypeStruct((Bd, LANES, n), BF16), jax.ShapeDtypeStruct((Bd, n, LANES), BF16)],
        compiler_params=pltpu.CompilerParams(dimension_semantics=("arbitrary",), vmem_limit_bytes=VMEM_LIMIT),
        name="compress_sample",
    )(page_table, pool_k, pool_v, *prm["cmp_k_s"], *prm["cmp_v_s"], *tables, jnp.asarray(perm, BF16))


def _mixer_tail(o, sza, sra, gpb, x, wpa_ref, wo_ref, fg_ref):
    pa = _dot((o * sza).astype(BF16), wpa_ref[...])
    merged = sra * pa + gpb
    hn = x + _dot(merged.astype(BF16), wo_ref[...])
    r = lax.rsqrt(jnp.mean(hn * hn, axis=-1, keepdims=True) + EPS)
    return (hn * r) * fg_ref[...]


def _topk_mask(score, blk, n_blocks, axis):
    cnt = jnp.zeros(score.shape, jnp.int32)
    for sp in range(n_blocks):
        b = lax.slice_in_dim(score, sp, sp + 1, axis=axis)
        ge = jnp.where(b >= score, 1, 0)
        gt = jnp.where(b > score, 1, 0)
        cnt = cnt + jnp.where(blk > sp, ge, gt)
    return cnt < TOP_N


def _attn_prompt_body(n_sel, q_ref, ka_ref, vs_ref, kw_ref, vw_ref, kc_ref, vco_ref, band_ref, rowhot_ref, gat_ref,
                      sza_ref, sra_ref, gpb_ref, x_ref, wpa_ref, wo_ref, fg_ref, y_ref):
    i = pl.program_id(1)
    M = HU * TQ
    G = range(KV_HEADS)
    units = [(g, u) for g in G for u in range(HPG // HU)]
    qpos = i * TQ + (_iota((M, 1), 0) & (TQ - 1))
    n_cmp_pad = kc_ref.shape[2]
    gt = gat_ref[...]
    qs = [q_ref[HPG * g + HU * u:HPG * g + HU * (u + 1)].reshape(M, LANES) for g, u in units]

    mk = (_iota((M, n_cmp_pad), 1) * CMP_STRIDE + (CMP_BLOCK - 1)) <= qpos
    o_cmp = []
    imps = [None] * KV_HEADS
    for n, (g, u) in enumerate(units):
        s = jnp.where(mk, _dot(qs[n], kc_ref[0]), NEG)
        e = jnp.exp2(s - jnp.max(s, axis=-1, keepdims=True))
        p = jnp.where(mk, e / jnp.sum(e, axis=-1, keepdims=True), 0.0)
        r = _dot(p.astype(BF16), vco_ref[0])
        o_cmp.append(r[:, 0:LANES])
        impc = r[:, LANES:2 * LANES]
        for j in range(HU):
            part = impc[j * TQ:(j + 1) * TQ]
            imps[g] = part if imps[g] is None else imps[g] + part
    blk = _iota((n_sel, TQ), 0)
    cur = (i * TQ + _iota((n_sel, TQ), 1)) // SEL_BLOCK
    valid = blk <= cur
    forced = (valid & (blk > cur - N_LOCAL_BLOCKS)) | (blk < N_INIT_BLOCKS)
    scores = [jnp.where(forced, BIG, jnp.where(valid, imps[g].T[HEAD_DIM:HEAD_DIM + n_sel], NEG)) for g in G]

    last_blk = ((i + 1) * TQ - 1) // SEL_BLOCK

    def count_group(k, cnts):
        out = []
        for g in G:
            c = cnts[g]
            for sp in range(k * RANK_GROUP, (k + 1) * RANK_GROUP):
                b = scores[g][sp:sp + 1]
                c = c + jnp.where(blk > sp, jnp.where(b >= scores[g], 1, 0), jnp.where(b > scores[g], 1, 0))
            out.append(c)
        return tuple(out)

    cnts = (jnp.zeros((n_sel, TQ), jnp.int32),) * KV_HEADS
    for k in range(n_sel // RANK_GROUP):
        cnts = lax.cond(k * RANK_GROUP <= last_blk, functools.partial(count_group, k), lambda c: c, cnts)
    seln = []
    for g in G:
        selneg = jnp.where(cnts[g] < TOP_N, 0.0, NEG)
        seln_t = jnp.concatenate([selneg, jnp.zeros((LANES - n_sel, TQ), F32)], axis=0).T.astype(BF16)
        seln.append(jnp.concatenate([seln_t] * HU, axis=0))
    qa = [jnp.concatenate([qs[n], seln[g]], axis=1) for n, (g, u) in enumerate(units)]

    def update(carry, s, v):
        m, acc = carry
        mn = jnp.maximum(m, jnp.max(s, axis=-1, keepdims=True))
        acc = jnp.exp2(m - mn) * acc + _dot(jnp.exp2((s - mn).astype(BF16)), v)
        return mn, acc

    def step(kt, carries, causal):
        off = pl.multiple_of(kt * KT, KT)
        k = ka_ref[0, :, pl.ds(off, KT)]
        ss = [_dot(q, k) for q in qa]
        if causal:
            keep = kt * KT + _iota((M, KT), 1) <= qpos
            ss = [jnp.where(keep, s, NEG) for s in ss]
        return tuple(update(carries[n], ss[n], vs_ref[g, pl.ds(off, KT), :]) for n, (g, u) in enumerate(units))

    n_full = i // (KT // TQ)
    init = (jnp.full((M, 1), NEG, F32), jnp.zeros((M, LANES), F32))
    carries = lax.fori_loop(0, n_full // 2, lambda kp, c: step(2 * kp + 1, step(2 * kp, c, False), False),
                            (init,) * len(units))
    carries = lax.cond(n_full % 2 == 1, lambda c: step(n_full - 1, c, False), lambda c: c, carries)
    carries = step(n_full, carries, True)
    sum_lane = [(KV_HEADS - 1 - g) * HEAD_DIM for g in G]
    o_sel = [carries[n][1] / carries[n][1][:, sum_lane[g]:sum_lane[g] + 1] for n, (g, u) in enumerate(units)]

    st = pl.multiple_of(jnp.maximum(i * TQ - WINDOW, 0), TQ)
    kw = jnp.concatenate([kw_ref[0, :, pl.ds(st, WIN_KEYS)], band_ref[jnp.minimum(i, WINDOW // TQ)]], axis=0)
    o_units = []
    for n, (g, u) in enumerate(units):
        s = _dot(jnp.concatenate([qs[n], rowhot_ref[...]], axis=1), kw)
        e = jnp.exp2(s - jnp.max(s, axis=-1, keepdims=True))
        r = _dot(e.astype(BF16), vw_ref[g, pl.ds(st, WIN_KEYS), :])
        o_win = r / r[:, sum_lane[g]:sum_lane[g] + 1]

        def gcol(c, g=g, u=u):
            return jnp.concatenate(
                [gt[:, 3 * (HPG * g + HU * u + j) + c:3 * (HPG * g + HU * u + j) + c + 1] for j in range(HU)], axis=0)

        o_units.append(gcol(0) * o_cmp[n] + gcol(1) * o_sel[n] + gcol(2) * o_win)

    low = _iota((TQ, LANES), 1) < HEAD_DIM
    slabs = []
    for pp in range(N_HEADS // 2):
        h0 = 2 * pp
        g, n, j0 = h0 // HPG, h0 // HU, h0 % HU
        a = o_units[n][j0 * TQ:(j0 + 1) * TQ]
        b = o_units[n][(j0 + 1) * TQ:(j0 + 2) * TQ]
        if g == 0:
            b = pltpu.roll(b, HEAD_DIM, 1)
        else:
            a = pltpu.roll(a, HEAD_DIM, 1)
        slabs.append(jnp.where(low, a, b))
    o = jnp.concatenate(slabs, axis=1)
    y_ref[...] = _mixer_tail(o, sza_ref[...], sra_ref[...], gpb_ref[...], x_ref[...], wpa_ref, wo_ref, fg_ref)


def _attn_prompt(B, S, q_hm, kaug, vsb, kwt, vwb, kct, vco, gates, sza, sra, gpb, x2d, prm):
    nq = S // TQ
    n_sel = S // SEL_BLOCK
    n_cmp_pad = kct.shape[2]
    row = lambda width: pl.BlockSpec((TQ, width), lambda b, i: (b * nq + i, 0))
    const = lambda *shape: pl.BlockSpec(shape, lambda b, i: (0,) * len(shape))
    batch = lambda *shape: pl.BlockSpec((1,) + shape, lambda b, i: (b,) + (0,) * len(shape))
    in_specs = [
        pl.BlockSpec((N_HEADS, TQ, LANES), lambda b, i: (0, b * nq + i, 0)),
        batch(2 * LANES, S),
        pl.BlockSpec((KV_HEADS, S, LANES), lambda b, i: (0, b, 0)),
        batch(LANES, S),
        pl.BlockSpec((KV_HEADS, S, LANES), lambda b, i: (0, b, 0)),
        batch(LANES, n_cmp_pad),
        batch(n_cmp_pad, 2 * LANES),
        const(WINDOW // TQ + 1, TQ, WIN_KEYS), const(HU * TQ, TQ),
        row(LANES), row(ATTN_WIDTH), row(D_MODEL), row(D_MODEL), row(D_MODEL),
        const(ATTN_WIDTH, D_MODEL), const(D_MODEL, D_MODEL), const(1, D_MODEL),
    ]
    d = np.arange(WIN_KEYS)[None, :] - np.arange(TQ)[:, None]
    band = np.stack([np.where((d <= min(WINDOW, v * TQ)) & (d > min(WINDOW, v * TQ) - WINDOW), 0.0, NEG)
                     for v in range(WINDOW // TQ + 1)]).astype(np.float32)
    rowhot = np.tile(np.eye(TQ, dtype=np.float32), (HU, 1))
    return pl.pallas_call(
        functools.partial(_attn_prompt_body, n_sel),
        grid=(B, nq),
        in_specs=in_specs,
        out_specs=row(D_MODEL),
        out_shape=jax.ShapeDtypeStruct((B * S, D_MODEL), F32),
        compiler_params=pltpu.CompilerParams(dimension_semantics=("arbitrary", "arbitrary"),
                                             vmem_limit_bytes=VMEM_LIMIT),
        name="attn_prompt",
    )(q_hm, kaug, vsb, kwt, vwb, kct, vco, jnp.asarray(band, BF16), jnp.asarray(rowhot, BF16), gates, sza, sra, gpb,
      x2d, prm["w_pa"], prm["w_o"], prm["final_g"])


def _attn_sample_body(n_pages, tn, pt_ref, pk_ref, pv_ref, qa_ref, kc_ref, vc_ref, ovl_ref, oh_ref, kt_ref, vt_ref,
                      ckw_ref, cvw_ref, kwn_ref, vwn_ref, kwt_ref, vwt_ref, o_ref, okw_ref, ovw_ref,
                      kb0, kb1, vb0, vb1, sem):
    s_id = pl.program_id(0)
    past = n_pages * PAGE_SIZE
    n_past_blk = past // SEL_BLOCK
    n_blk_pad = ovl_ref.shape[1]
    R = N_HEADS * TOK_PAD
    GR = HPG * TOK_PAD
    wc = ckw_ref.shape[2]
    bufs = ((kb0, vb0, 0), (kb1, vb1, 1))
    t_row = _iota((R, 1), 0) & (TOK_PAD - 1)

    def start(b, slot):
        kb, vb, si = bufs[slot]
        _start_pages(pt_ref, (pk_ref, pv_ref), (kb, vb), (sem.at[2 * si], sem.at[2 * si + 1]), b, n_pages)

    def finish(slot):
        kb, vb, si = bufs[slot]
        qa = qa_ref[slot]

        n_cmp_pad = kc_ref.shape[2]
        s = _dot(qa, kc_ref[slot])
        mk = _iota((R, n_cmp_pad), 1) < n_cmp_pad - 1
        s = jnp.where(mk, s, NEG)
        e = jnp.exp(s - jnp.max(s, axis=-1, keepdims=True))
        p = jnp.where(mk, e / jnp.sum(e, axis=-1, keepdims=True), 0.0).astype(BF16)
        o_cmp = _dot(p, vc_ref[slot])
        impc = _dot(p, ovl_ref[...])
        imp = jnp.concatenate(
            [impc[g * GR:g * GR + TOK_PAD] + impc[g * GR + TOK_PAD:g * GR + 2 * TOK_PAD]
             + impc[g * GR + 2 * TOK_PAD:g * GR + 3 * TOK_PAD] + impc[g * GR + 3 * TOK_PAD:g * GR + 4 * TOK_PAD]
             for g in range(KV_HEADS)], axis=0)
        nr = KV_HEADS * TOK_PAD
        blk = _iota((nr, n_blk_pad), 1)
        cur = (past + jnp.minimum(_iota((nr, n_blk_pad), 0) & (TOK_PAD - 1), tn - 1)) // SEL_BLOCK
        valid = blk <= cur
        forced = (valid & (blk > cur - N_LOCAL_BLOCKS)) | (blk < N_INIT_BLOCKS)
        score = jnp.where(forced, BIG, jnp.where(valid, imp, NEG))
        sel = _topk_mask(score, blk, n_past_blk + 1, 1)
        selneg = jnp.where(sel, 0.0, NEG)
        selneg = jnp.concatenate(
            [selneg[g * TOK_PAD:(g + 1) * TOK_PAD] for g in range(KV_HEADS) for _ in range(HPG)], axis=0)
        low = _iota((R, LANES), 1) < SEL_HALF
        qaug = []
        for hf in range(n_past_blk // SEL_HALF):
            slab = selneg[:, (hf // 2) * LANES:(hf // 2 + 1) * LANES]
            if hf % 2:
                slab = pltpu.roll(slab, SEL_HALF, 1)
            qaug.append(jnp.concatenate([qa, jnp.where(low, slab, 0.0).astype(BF16)], axis=1))

        _wait_pages((pk_ref, pv_ref), (kb, vb), (sem.at[2 * si], sem.at[2 * si + 1]), n_pages)

        pages_per_half = SEL_HALF * SEL_BLOCK // PAGE_SIZE

        def half(buf, hf):
            return jnp.concatenate(
                [buf[hf * pages_per_half + j] for j in range(pages_per_half)], axis=1).astype(BF16)

        col = _iota((R, LANES), 1)
        ss = [_dot(qaug[hf], jnp.concatenate([half(kb, hf), oh_ref[...]], axis=0)) for hf in range(len(qaug))]
        ss.append(jnp.where(col <= t_row, _nt(qa, kt_ref[slot]), NEG))
        m = functools.reduce(jnp.maximum, [jnp.max(s, axis=-1, keepdims=True) for s in ss])
        es = [jnp.exp(s - m) for s in ss]
        l = functools.reduce(jnp.add, [jnp.sum(e, axis=-1, keepdims=True) for e in es])
        acc = _dot(es[-1].astype(BF16), vt_ref[slot])
        for hf in range(len(qaug)):
            acc = acc + _nt(es[hf].astype(BF16), half(vb, hf))
        o_sel = acc / l

        sa = jnp.where(_iota((R, wc), 1) > t_row, _dot(qa, ckw_ref[slot].astype(BF16)), NEG)
        sb = jnp.where(col <= t_row, _nt(qa, kwn_ref[slot]), NEG)
        m = jnp.maximum(jnp.max(sa, axis=-1, keepdims=True), jnp.max(sb, axis=-1, keepdims=True))
        ea, eb = jnp.exp(sa - m), jnp.exp(sb - m)
        den = jnp.sum(ea, axis=-1, keepdims=True) + jnp.sum(eb, axis=-1, keepdims=True)
        o_win = (_nt(ea.astype(BF16), cvw_ref[slot].astype(BF16)) + _dot(eb.astype(BF16), vwn_ref[slot])) / den

        o_ref[slot, 0] = o_cmp
        o_ref[slot, 1] = o_sel
        o_ref[slot, 2] = o_win

        newest = _iota((LANES, LANES), 1) >= LANES - tn
        for src, new, dst in ((ckw_ref, kwt_ref, okw_ref), (cvw_ref, vwt_ref, ovw_ref)):
            shifted = pltpu.roll(src[slot], wc - tn, 1)
            dst[slot, :, 0:wc - LANES] = shifted[:, 0:wc - LANES]
            dst[slot, :, wc - LANES:wc] = jnp.where(newest, new[slot], shifted[:, wc - LANES:wc])

    @pl.when(s_id == 0)
    def _():
        start(0, 0)

    start(2 * s_id + 1, 1)
    finish(0)

    @pl.when(s_id + 1 < pl.num_programs(0))
    def _():
        start(2 * s_id + 2, 0)

    finish(1)


def _attn_sample(page_table, pool_k, pool_v, tn, qa, kct, vc, ovl, ktail, vtail, ckw, cvw, kwn, vwn, kwt, vwt):
    Bd, n_pages = page_table.shape
    R = N_HEADS * TOK_PAD
    wc = ckw.shape[2]
    const = lambda *shape: pl.BlockSpec(shape, lambda s, pt: (0,) * len(shape))
    pair = lambda *shape: pl.BlockSpec((2,) + shape, lambda s, pt: (s,) + (0,) * len(shape))
    anyspec = pl.BlockSpec(memory_space=pl.ANY)
    half_keys = SEL_HALF * SEL_BLOCK
    onehot = np.zeros((LANES, half_keys), np.float32)
    onehot[np.arange(half_keys) // SEL_BLOCK, np.arange(half_keys)] = 1.0
    grid_spec = pltpu.PrefetchScalarGridSpec(
        num_scalar_prefetch=1,
        grid=(Bd // 2,),
        in_specs=[anyspec, anyspec, pair(R, LANES), pair(LANES, kct.shape[2]), pair(vc.shape[1], LANES),
                  const(*ovl.shape), const(LANES, half_keys), pair(LANES, LANES), pair(LANES, LANES),
                  pair(LANES, wc), pair(LANES, wc), pair(LANES, LANES), pair(LANES, LANES),
                  pair(LANES, LANES), pair(LANES, LANES)],
        out_specs=[pair(3, R, LANES), pair(LANES, wc), pair(LANES, wc)],
        scratch_shapes=[pltpu.VMEM((n_pages, LANES, PAGE_SIZE), F32)] * 4 + [pltpu.SemaphoreType.DMA((4,))],
    )
    return pl.pallas_call(
        functools.partial(_attn_sample_body, n_pages, tn),
        grid_spec=grid_spec,
        out_shape=[jax.ShapeDtypeStruct((Bd, 3, R, LANES), F32)] + [jax.ShapeDtypeStruct((Bd, LANES, wc), F32)] * 2,
        compiler_params=pltpu.CompilerParams(dimension_semantics=("arbitrary",), vmem_limit_bytes=VMEM_LIMIT),
        name="attn_sample",
    )(page_table, pool_k, pool_v, qa, kct, vc, ovl, jnp.asarray(onehot, BF16), ktail, vtail, ckw, cvw, kwn, vwn,
      kwt, vwt)


def _mixer_sample_body(o3_ref, g3_ref, sza_ref, sra_ref, gpb_ref, x_ref, wpa_ref, wo_ref, fg_ref, y_ref):
    o = g3_ref[0] * o3_ref[0] + g3_ref[1] * o3_ref[1] + g3_ref[2] * o3_ref[2]
    y_ref[...] = _mixer_tail(o, sza_ref[...], sra_ref[...], gpb_ref[...], x_ref[...], wpa_ref, wo_ref, fg_ref)


def _mixer_sample(o3, g3, sza, sra, gpb, x2d, prm):
    T = x2d.shape[0]
    full = lambda *shape: pl.BlockSpec(shape, lambda i: (0,) * len(shape))
    return pl.pallas_call(
        _mixer_sample_body,
        grid=(1,),
        in_specs=[full(3, T, ATTN_WIDTH), full(3, T, ATTN_WIDTH), full(T, ATTN_WIDTH), full(T, D_MODEL),
                  full(T, D_MODEL), full(T, D_MODEL), full(ATTN_WIDTH, D_MODEL), full(D_MODEL, D_MODEL),
                  full(1, D_MODEL)],
        out_specs=full(T, D_MODEL),
        out_shape=jax.ShapeDtypeStruct((T, D_MODEL), F32),
        compiler_params=pltpu.CompilerParams(dimension_semantics=("arbitrary",), vmem_limit_bytes=VMEM_LIMIT),
        name="mixer_sample",
    )(o3, g3, sza, sra, gpb, x2d, prm["w_pa"], prm["w_o"], prm["final_g"])


def _overlap(n_cmp, n_sel):
    cs = np.arange(n_cmp)[:, None] * CMP_STRIDE
    ss = np.arange(n_sel)[None, :] * SEL_BLOCK
    ov = np.minimum(cs + CMP_BLOCK, ss + SEL_BLOCK) - np.maximum(cs, ss)
    return np.clip(ov, 0, None).astype(np.float32) / CMP_BLOCK


def _cmp_weights(w1, b1, w2, b2):
    w1r = w1.reshape(2, CMP_STRIDE, HEAD_DIM, CMP_HIDDEN).transpose(1, 2, 0, 3)
    w1r = w1r.reshape(CMP_STRIDE, 1, HEAD_DIM, 2 * CMP_HIDDEN).astype(BF16)
    w2r = w2.reshape(1, CMP_HIDDEN, HEAD_DIM).astype(BF16)
    big = jnp.concatenate(
        [jnp.pad(w1r, ((0, 0), (0, 0), (0, 0), (g * 2 * CMP_HIDDEN, (KV_HEADS - 1 - g) * 2 * CMP_HIDDEN)))
         for g in range(KV_HEADS)], axis=1)
    w2b = jnp.concatenate(
        [jnp.pad(w2r, ((0, 0), (0, 0), (g * HEAD_DIM, (KV_HEADS - 1 - g) * HEAD_DIM))) for g in range(KV_HEADS)],
        axis=0)
    w2b = w2b.reshape(KV_HEADS * CMP_HIDDEN, LANES)
    b1t, b2t = jnp.tile(b1, KV_HEADS)[None], jnp.tile(b2, KV_HEADS)[None]
    return ((big.reshape(CMP_STRIDE * LANES, 4 * LANES), b1t, w2b, b2t),
            (w1r.reshape(CMP_STRIDE * HEAD_DIM, 2 * CMP_HIDDEN), b1t, w2b, b2t))


def _seq_minor(t):
    lead = t.shape[:-3]
    n = len(lead)
    return t.transpose(*range(n), n + 1, n + 2, n).reshape(*lead, KV_WIDTH, t.shape[-3])


def _seq_major(t):
    lead = t.shape[:-2]
    n = len(lead)
    return t.reshape(*lead, KV_HEADS, HEAD_DIM, t.shape[-1]).transpose(*range(n), n + 2, n, n + 1)


def kernel(x_prompt, x_sample, cache_k_cmp, cache_v_cmp, cache_k_sel, cache_v_sel, cache_k_win, cache_v_win, page_table, norm_g, w_in, cmp_k_w1, cmp_k_b1, cmp_k_w2, cmp_k_b2, cmp_v_w1, cmp_v_b1, cmp_v_w2, cmp_v_b2, v_norm_g, v_norm_b, w_spatial, b_spatial, w_pa, w_pb, w_o, final_g):
    B, S, _ = x_prompt.shape
    Bd, tn, _ = x_sample.shape
    depth = w_in.shape[0]
    assert depth == 1, "single-layer step"
    assert Bd * tn == CHUNK, "the sample tokens form one 128-row tile"
    n_pages = page_table.shape[1]
    past = n_pages * PAGE_SIZE

    split = OFF_GL + 3 * N_HEADS
    w = w_in[0]
    prm = {
        "norm_g": norm_g,
        "w_in": jnp.concatenate([w.T[:split], jnp.zeros((GL_PAD, D_MODEL), F32), w.T[split:]], axis=0).astype(BF16),
        "w_sp": w_spatial[0],
        "b_sp": jnp.repeat(b_spatial[0].T, LANES, axis=1),
        "w_sp_s": jnp.stack([jnp.kron(jnp.eye(CHUNK // tn, dtype=F32), w_spatial[0, g, :tn, :tn])
                             for g in range(GMLP_GROUPS)]),
        "b_sp_s": jnp.tile(jnp.repeat(b_spatial[0, :, :tn].T, LANES, axis=1), (CHUNK // tn, 1)),
        "v_norm_g": v_norm_g, "v_norm_b": v_norm_b,
        "w_pb": w_pb[0].astype(BF16), "w_pa": w_pa[0].astype(BF16), "w_o": w_o[0].astype(BF16),
        "final_g": final_g[None],
    }
    prm["cmp_k"], prm["cmp_k_s"] = _cmp_weights(cmp_k_w1[0], cmp_k_b1[0], cmp_k_w2[0], cmp_k_b2[0])
    prm["cmp_v"], prm["cmp_v_s"] = _cmp_weights(cmp_v_w1[0], cmp_v_b1[0], cmp_v_w2[0], cmp_v_b2[0])

    xp = x_prompt.reshape(B * S, D_MODEL)
    (q_hm, p_kcmp, p_vcmp, p_ksel, p_vsel, p_kwin, p_vwin, kcr, vcr, kaug, kwt, vsb, vwb,
     gates, sza, sra, gpb) = _in_project(xp, _rope_tables(np.arange(S)), prm, TM_PROMPT, False, S)
    n_half = S // CMP_STRIDE
    n_cmp = (S - CMP_BLOCK) // CMP_STRIDE + 1
    n_sel = S // SEL_BLOCK
    assert n_sel <= SEL_HALF
    ovl = np.zeros((n_half, LANES), np.float32)
    ovl[:n_cmp, HEAD_DIM:HEAD_DIM + n_sel] = _overlap(n_cmp, n_sel)
    cend_tables = _rope_tables(np.arange(n_half) * CMP_STRIDE + CMP_BLOCK - 1)
    kct, vco = _compress_prompt(kcr.reshape(B, S, LANES), vcr.reshape(B, S, LANES), prm, cend_tables,
                                jnp.asarray(ovl, BF16))
    y_prompt = _attn_prompt(B, S, q_hm, kaug, vsb, kwt, vwb, kct, vco, gates, sza, sra, gpb, xp, prm)

    xs = x_sample.reshape(Bd * tn, D_MODEL)
    pos_s = np.tile(past + np.arange(tn), Bd)
    (q_s, s_kcmp, s_vcmp, s_ksel, s_vsel, s_kwin, s_vwin, gates_s, sza_s, sra_s, gpb_s, vn_s) = _in_project(
        xs, _rope_tables(pos_s), prm, Bd * tn, True, tn)
    n_half_s = past // CMP_STRIDE
    pools = [_seq_minor(c[0]) for c in (cache_k_cmp, cache_v_cmp, cache_k_sel, cache_v_sel)]
    cend_s = _rope_tables(np.arange(n_half_s) * CMP_STRIDE + CMP_BLOCK - 1)
    kct_s, vc_s = _compress_sample(page_table, pools[0], pools[1], prm, cend_s)

    n_cmp_s = (past + tn - CMP_BLOCK) // CMP_STRIDE + 1
    n_blk_s = past // SEL_BLOCK + -(-tn // SEL_BLOCK)
    assert n_cmp_s == n_half_s - 1 and tn <= TOK_PAD and (past // SEL_BLOCK) % SEL_HALF == 0
    ovl_s = np.zeros((n_half_s, 2 * LANES), np.float32)
    ovl_s[:n_cmp_s, :n_blk_s] = _overlap(n_cmp_s, n_blk_s)
    q5 = q_s.reshape(Bd, tn, KV_HEADS, HPG, HEAD_DIM).transpose(0, 2, 3, 1, 4)
    q5 = jnp.pad(q5, ((0, 0), (0, 0), (0, 0), (0, TOK_PAD - tn), (0, 0))).reshape(Bd, KV_HEADS, HPG * TOK_PAD, HEAD_DIM)
    qa = jnp.concatenate(
        [jnp.pad(q5[:, g], ((0, 0), (0, 0), (g * HEAD_DIM, (KV_HEADS - 1 - g) * HEAD_DIM))) for g in range(KV_HEADS)],
        axis=1).astype(BF16)
    pad_rows = lambda t: jnp.pad(t.reshape(Bd, tn, LANES), ((0, 0), (0, LANES - tn), (0, 0))).astype(BF16)
    ckw = _seq_minor(cache_k_win[0])
    cvw = _seq_minor(cache_v_win[0])
    new_t = lambda t: jnp.pad(t.reshape(Bd, tn, LANES).transpose(0, 2, 1), ((0, 0), (0, 0), (LANES - tn, 0)))
    o3, s_k_win, s_v_win = _attn_sample(
        page_table, pools[2], pools[3], tn, qa, kct_s, vc_s, jnp.asarray(ovl_s, BF16),
        pad_rows(s_ksel), pad_rows(s_vsel), ckw, cvw, pad_rows(s_kwin), pad_rows(s_vwin), new_t(s_kwin), new_t(s_vwin))
    s_k_win = _seq_major(s_k_win)[None]
    s_v_win = _seq_major(s_v_win)[None]
    first_group = jnp.arange(N_HEADS * TOK_PAD)[:, None] < HPG * TOK_PAD
    o3r = jnp.where(first_group, o3[..., :HEAD_DIM], o3[..., HEAD_DIM:])
    o3r = o3r.reshape(Bd, 3, N_HEADS, TOK_PAD, HEAD_DIM)[:, :, :, :tn]
    o3r = o3r.transpose(1, 0, 3, 2, 4).reshape(3, Bd * tn, ATTN_WIDTH)
    g3 = gates_s[:, :3 * N_HEADS].reshape(Bd * tn, N_HEADS, 3).transpose(2, 0, 1)
    g3 = jnp.repeat(g3, HEAD_DIM, axis=2)
    y_sample = _mixer_sample(o3r, g3, sza_s, sra_s, gpb_s, xs, prm)

    kv5 = lambda t, b, n: t.reshape(1, b, n, KV_HEADS, HEAD_DIM)
    pw = min(WINDOW, S)
    return (y_prompt.reshape(B, S, D_MODEL), y_sample.reshape(Bd, tn, D_MODEL),
            _seq_major(p_kcmp)[None], _seq_major(p_vcmp)[None], _seq_major(p_ksel)[None], _seq_major(p_vsel)[None],
            _seq_major(p_kwin[:, :, S - pw:])[None], _seq_major(p_vwin[:, :, S - pw:])[None],
            kv5(s_kcmp, Bd, tn), kv5(s_vcmp, Bd, tn), kv5(s_ksel, Bd, tn), kv5(s_vsel, Bd, tn),
            s_k_win, s_v_win, vn_s.reshape(1, Bd, tn, GMLP_WIDTH))
```
